```python
import math
import jax, jax.numpy as jnp
from jax import lax
import numpy as np

D_MODEL = 2048
BATCH = 2
SEQ = 16384
DEPTH = 4

MLA_HEADS = 6
MLA_Q_RANK = 512
MLA_KV_RANK = 256
MLA_NOPE = 128
MLA_ROPE = 64
MLA_V = 128
GLA_HEADS = 4
GLA_DK = 64
GLA_DV = 128
GLA_GATE_RANK = 16
GLA_TAU = 16.0
GLA_CHUNK = 64
DSWA_GROUPS = ((128, 1), (512, 4), (2048, 16))
DSWA_HEADS_PER_GROUP = 2
DSWA_HEADS = 6
DSWA_HEAD_DIM = 128
D_FF = 128 * ((8 * D_MODEL // 3 + 127) // 128)
ROPE_THETA = 10000.0
ALIBI_MAX_EXP = 8.0
Q_BLOCK = 128
NORM_EPS = 1e-6
NEG_INF = -1e30
IN_SPLITS = (MLA_Q_RANK, MLA_KV_RANK, MLA_ROPE,
             GLA_HEADS * GLA_DK, GLA_HEADS * GLA_DK, GLA_HEADS * GLA_DV, GLA_GATE_RANK, GLA_HEADS * GLA_DV,
             DSWA_HEADS * DSWA_HEAD_DIM, DSWA_HEADS * DSWA_HEAD_DIM, DSWA_HEADS * DSWA_HEAD_DIM,
             D_MODEL, D_MODEL, D_MODEL)
D_IN = sum(IN_SPLITS)

kernel_name = 'hybrid_mla_gla_dilated_swa_macaron'


def rms_norm(x, g):
    xf = x.astype(jnp.float32)
    y = xf * lax.rsqrt(jnp.mean(xf * xf, axis=-1, keepdims=True) + NORM_EPS)
    return (y * g.astype(jnp.float32)).astype(x.dtype)


def swiglu(h, w_gate, w_up, w_down):
    return (jax.nn.silu(h @ w_gate) * (h @ w_up)) @ w_down


def split_columns(u):
    idx, acc = [], 0
    for s in IN_SPLITS[:-1]:
        acc += s
        idx.append(acc)
    return jnp.split(u, idx, axis=-1)


def apply_rope(x, positions):
    half = x.shape[-1] // 2
    inv_freq = ROPE_THETA ** (-jnp.arange(half, dtype=jnp.float32) / half)
    ang = positions.astype(jnp.float32)[..., None] * inv_freq
    cos = jnp.cos(ang)[:, :, None, :]
    sin = jnp.sin(ang)[:, :, None, :]
    xf = x.astype(jnp.float32)
    x1, x2 = xf[..., :half], xf[..., half:]
    return jnp.concatenate([x1 * cos - x2 * sin, x1 * sin + x2 * cos], axis=-1).astype(x.dtype)


def causal_dense_attention(q, k, v, scale):
    B, S, H, Dk = q.shape
    Dv = v.shape[-1]
    nb = S // Q_BLOCK
    qb = q.reshape(B, nb, Q_BLOCK, H, Dk).transpose(1, 0, 3, 2, 4)
    kt = k.transpose(0, 2, 1, 3)
    vt = v.transpose(0, 2, 1, 3)
    key_pos = jnp.arange(S)

    def one_block(args):
        i, q_blk = args
        s = jnp.einsum('bhqd,bhkd->bhqk', q_blk, kt, preferred_element_type=jnp.float32) * scale
        q_pos = i * Q_BLOCK + jnp.arange(Q_BLOCK)
        s = jnp.where(key_pos[None, :] <= q_pos[:, None], s, NEG_INF)
        p = jax.nn.softmax(s, axis=-1)
        return jnp.einsum('bhqk,bhkd->bhqd', p.astype(v.dtype), vt)

    out = lax.map(one_block, (jnp.arange(nb), qb))
    return out.transpose(1, 0, 3, 2, 4).reshape(B, S, H, Dv)


def mla_attention(c_q, c_kv, k_rope, positions, g_cq, g_ckv, w_uq, w_ukv, g_q, g_k):
    B, S, _ = c_q.shape
    c_q = rms_norm(c_q, g_cq)
    c_kv = rms_norm(c_kv, g_ckv)
    q = (c_q @ w_uq).reshape(B, S, MLA_HEADS, MLA_NOPE + MLA_ROPE)
    kv = (c_kv @ w_ukv).reshape(B, S, MLA_HEADS, MLA_NOPE + MLA_V)
    k_nope, v = kv[..., :MLA_NOPE], kv[..., MLA_NOPE:]
    k = jnp.concatenate([k_nope, jnp.broadcast_to(k_rope[:, :, None, :], (B, S, MLA_HEADS, MLA_ROPE))], axis=-1)
    q = rms_norm(q, g_q)
    k = rms_norm(k, g_k)
    q = jnp.concatenate([q[..., :MLA_NOPE], apply_rope(q[..., MLA_NOPE:], positions)], axis=-1)
    k = jnp.concatenate([k[..., :MLA_NOPE], apply_rope(k[..., MLA_NOPE:], positions)], axis=-1)
    out = causal_dense_attention(q, k, v, (MLA_NOPE + MLA_ROPE) ** -0.5)
    return out.reshape(B, S, MLA_HEADS * MLA_V)


def gla_mixer(q, k, v, gate_lr, r, w_gate2, b_gate2, g_o):
    B, S, _ = q.shape
    H, DK, DV, C = GLA_HEADS, GLA_DK, GLA_DV, GLA_CHUNK
    n = S // C
    log_a = jax.nn.log_sigmoid((gate_lr @ w_gate2 + b_gate2).astype(jnp.float32)) / GLA_TAU

    def chunks(t, d):
        return t.astype(jnp.float32).reshape(B, n, C, H, d).transpose(1, 0, 3, 2, 4)

    qc = chunks(q, DK) * (DK ** -0.5)
    kc, vc, ac = chunks(k, DK), chunks(v, DV), chunks(log_a, DK)
    causal = jnp.tril(jnp.ones((C, C), dtype=bool))[:, :, None]

    def step(state, inp):
        qi, ki, vi, ai = inp
        b = jnp.cumsum(ai, axis=-2)
        diff = b[:, :, :, None, :] - b[:, :, None, :, :]
        decay = jnp.exp(jnp.where(causal, diff, -jnp.inf))
        attn = jnp.einsum('bhtd,bhsd,bhtsd->bhts', qi, ki, decay)
        o = attn @ vi + jnp.einsum('bhtd,bhde->bhte', qi * jnp.exp(b), state)
        b_end = b[:, :, -1:, :]
        new_state = jnp.exp(b_end[:, :, 0, :])[..., None] * state + jnp.einsum('bhsd,bhse->bhde', ki * jnp.exp(b_end - b), vi)
        return new_state, o

    state0 = jnp.zeros((B, H, DK, DV), jnp.float32)
    _, o = lax.scan(step, state0, (qc, kc, vc, ac))
    o = o.transpose(1, 0, 3, 2, 4).reshape(B, S, H, DV)
    o = rms_norm(o, g_o).reshape(B, S, H * DV)
    return (o * jax.nn.silu(r.astype(jnp.float32))).astype(v.dtype)


def dilated_group_attention(q, k, v, slopes, window, dilation):
    B, S, H, D = q.shape
    w = window // dilation
    nblk = -(-S // window)
    s_pad = nblk * window

    def to_blocks(t):
        t = jnp.pad(t, ((0, 0), (0, s_pad - S), (0, 0), (0, 0)))
        return t.reshape(B, nblk, w, dilation, H, D)

    def with_prev(t):
        prev = jnp.pad(t[:, :-1], ((0, 0), (1, 0), (0, 0), (0, 0), (0, 0), (0, 0)))
        return jnp.concatenate([prev, t], axis=2)

    qb = to_blocks(q)
    kk = with_prev(to_blocks(k))
    vv = with_prev(to_blocks(v))
    scores = jnp.einsum('bnirhd,bnjrhd->bnrhij', qb, kk, preferred_element_type=jnp.float32) * (D ** -0.5)
    i = jnp.arange(w)[:, None]
    j = jnp.arange(2 * w)[None, :]
    steps = w + i - j
    blk = jnp.arange(nblk)[:, None, None]
    valid = (steps >= 0) & (steps <= w) & ((blk > 0) | (j >= w))
    bias = -slopes.astype(jnp.float32)[:, None, None] * (steps * dilation).astype(jnp.float32)
    scores = jnp.where(valid[None, :, None, None], scores + bias[None, None, None], NEG_INF)
    lse = jax.nn.logsumexp(scores, axis=-1)
    probs = jnp.exp(scores - lse[..., None])
    out = jnp.einsum('bnrhij,bnjrhd->bnirhd', probs.astype(v.dtype), vv)
    out = out.reshape(B, s_pad, H, D)[:, :S]
    lse = lse.transpose(0, 1, 4, 2, 3).reshape(B, s_pad, H)[:, :S]
    return out, lse


def dilated_swa_mixer(q, k, v, g_q, g_k):
    B, S, _ = q.shape
    q = rms_norm(q.reshape(B, S, DSWA_HEADS, DSWA_HEAD_DIM), g_q)
    k = rms_norm(k.reshape(B, S, DSWA_HEADS, DSWA_HEAD_DIM), g_k)
    v = v.reshape(B, S, DSWA_HEADS, DSWA_HEAD_DIM)
    slopes = 2.0 ** (-ALIBI_MAX_EXP * jnp.arange(1, DSWA_HEADS + 1, dtype=jnp.float32) / DSWA_HEADS)
    outs, lses = [], []
    for gi, (window, dilation) in enumerate(DSWA_GROUPS):
        sl = slice(gi * DSWA_HEADS_PER_GROUP, (gi + 1) * DSWA_HEADS_PER_GROUP)
        o, l = dilated_group_attention(q[:, :, sl], k[:, :, sl], v[:, :, sl], slopes[sl], window, dilation)
        outs.append(o)
        lses.append(l)
    wts = jax.nn.softmax(jnp.stack(lses, axis=0), axis=0)
    y = jnp.einsum('gbsh,gbshd->bshd', wts.astype(v.dtype), jnp.stack(outs, axis=0))
    return y.reshape(B, S, DSWA_HEADS_PER_GROUP * DSWA_HEAD_DIM)


def setup_inputs(seed: int = 0) -> dict:
    key = jax.random.key(seed)
    ks = jax.random.split(key, 32)

    def w(k, fan_in, fan_out):
        return jax.random.normal(k, (DEPTH, fan_in, fan_out), jnp.float32) * (fan_in ** -0.5)

    def gain(k, n):
        return 1.0 + 0.02 * jax.random.normal(k, (DEPTH, n), jnp.float32)

    x = jax.random.normal(ks[0], (BATCH, SEQ, D_MODEL), jnp.float32)
    offsets = jax.random.randint(ks[1], (BATCH, 1), 0, 4096, dtype=jnp.int32)
    positions = (offsets + jnp.arange(SEQ, dtype=jnp.int32)[None, :]).astype(jnp.int32)
    return {
        'x': x,
        'positions': positions,
        'ffn1_norm': gain(ks[2], D_MODEL),
        'ffn1_w_gate': w(ks[3], D_MODEL, D_FF),
        'ffn1_w_up': w(ks[4], D_MODEL, D_FF),
        'ffn1_w_down': w(ks[5], D_FF, D_MODEL),
        'mix_norm': gain(ks[6], D_MODEL),
        'w_in': w(ks[7], D_MODEL, D_IN),
        'mla_cq_norm': gain(ks[8], MLA_Q_RANK),
        'mla_ckv_norm': gain(ks[9], MLA_KV_RANK),
        'mla_w_uq': w(ks[10], MLA_Q_RANK, MLA_HEADS * (MLA_NOPE + MLA_ROPE)),
        'mla_w_ukv': w(ks[11], MLA_KV_RANK, MLA_HEADS * (MLA_NOPE + MLA_V)),
        'mla_q_norm': gain(ks[12], MLA_NOPE + MLA_ROPE),
        'mla_k_norm': gain(ks[13], MLA_NOPE + MLA_ROPE),
        'gla_w_gate2': w(ks[14], GLA_GATE_RANK, GLA_HEADS * GLA_DK),
        'gla_b_gate2': 0.1 * jax.random.normal(ks[15], (DEPTH, GLA_HEADS * GLA_DK), jnp.float32),
        'gla_o_norm': gain(ks[16], GLA_DV),
        'dswa_q_norm': gain(ks[17], DSWA_HEAD_DIM),
        'dswa_k_norm': gain(ks[18], DSWA_HEAD_DIM),
        'w_branch_a': w(ks[19], MLA_HEADS * MLA_V, D_MODEL),
        'w_branch_b': w(ks[20], GLA_HEADS * GLA_DV, D_MODEL),
        'w_branch_c': w(ks[21], DSWA_HEADS_PER_GROUP * DSWA_HEAD_DIM, D_MODEL),
        'w_out': w(ks[22], D_MODEL, D_MODEL),
        'ffn2_norm': gain(ks[23], D_MODEL),
        'ffn2_w_gate': w(ks[24], D_MODEL, D_FF),
        'ffn2_w_up': w(ks[25], D_MODEL, D_FF),
        'ffn2_w_down': w(ks[26], D_FF, D_MODEL),
    }


def reference(x, positions, ffn1_norm, ffn1_w_gate, ffn1_w_up, ffn1_w_down, mix_norm, w_in,
              mla_cq_norm, mla_ckv_norm, mla_w_uq, mla_w_ukv, mla_q_norm, mla_k_norm,
              gla_w_gate2, gla_b_gate2, gla_o_norm, dswa_q_norm, dswa_k_norm,
              w_branch_a, w_branch_b, w_branch_c, w_out,
              ffn2_norm, ffn2_w_gate, ffn2_w_up, ffn2_w_down):
    for l in range(DEPTH):
        x = x + 0.5 * swiglu(rms_norm(x, ffn1_norm[l]), ffn1_w_gate[l], ffn1_w_up[l], ffn1_w_down[l])
        h = rms_norm(x, mix_norm[l])
        (c_q, c_kv, k_rope, q_b, k_b, v_b, gate_lr, r_b,
         q_c, k_c, v_c, g_a, g_b, g_c) = split_columns(h @ w_in[l])
        y_a = mla_attention(c_q, c_kv, k_rope, positions, mla_cq_norm[l], mla_ckv_norm[l],
                            mla_w_uq[l], mla_w_ukv[l], mla_q_norm[l], mla_k_norm[l])
        y_b = gla_mixer(q_b, k_b, v_b, gate_lr, r_b, gla_w_gate2[l], gla_b_gate2[l], gla_o_norm[l])
        y_c = dilated_swa_mixer(q_c, k_c, v_c, dswa_q_norm[l], dswa_k_norm[l])
        merged = (jax.nn.sigmoid(g_a) * (y_a @ w_branch_a[l])
                  + jax.nn.sigmoid(g_b) * (y_b @ w_branch_b[l])
                  + jax.nn.sigmoid(g_c) * (y_c @ w_branch_c[l]))
        x = x + merged @ w_out[l]
        x = x + 0.5 * swiglu(rms_norm(x, ffn2_norm[l]), ffn2_w_gate[l], ffn2_w_up[l], ffn2_w_down[l])
    return x
```

```python
import functools

import numpy as np
import jax
import jax.numpy as jnp
from jax import lax
from jax.experimental import pallas as pl
from jax.experimental.pallas import tpu as pltpu

F32 = jnp.float32
BF16 = jnp.bfloat16

MLA_HEADS = 6
MLA_Q_RANK = 512
MLA_KV_RANK = 256
MLA_NOPE = 128
MLA_ROPE = 64
MLA_V = 128
MLA_QK = MLA_NOPE + MLA_ROPE
MLA_QK_PAD = 256
GLA_HEADS = 4
GLA_DK = 64
GLA_DV = 128
GLA_GATE_RANK = 16
GLA_TAU = 16.0
GLA_CHUNK = 64
GLA_SUB = 8
GLA_LEVELS = (32, 16, 8)
DSWA_GROUPS = ((128, 1), (512, 4), (2048, 16))
DSWA_HEADS_PER_GROUP = 2
DSWA_HEADS = 6
DSWA_HEAD_DIM = 128
DSWA_W = 128
ROPE_THETA = 10000.0
ALIBI_MAX_EXP = 8.0
NORM_EPS = 1e-6
NEG_INF = -1e30

LANE = 128
MIB = 1024 * 1024


def _cparams(sem, vmem_mib):
    return pltpu.CompilerParams(dimension_semantics=sem, vmem_limit_bytes=int(vmem_mib * MIB))


def _inv_rms(xf, n):
    return lax.rsqrt(jnp.sum(xf * xf, axis=-1, keepdims=True) / n + NORM_EPS)


def _const_spec(shape):
    nd = len(shape)
    return pl.BlockSpec(shape, lambda *_: (0,) * nd)


def _ffn_kernel(x_ref, g_ref, wg_ref, wu_ref, wd_ref, o_ref, h_scr):
    @pl.when(pl.program_id(1) == 0)
    def _():
        xf = x_ref[...]
        h_scr[...] = (xf * _inv_rms(xf, xf.shape[-1]) * g_ref[...]).astype(BF16)
        o_ref[...] = xf

    h = h_scr[...]
    gate = jnp.dot(h, wg_ref[...], preferred_element_type=F32)
    up = jnp.dot(h, wu_ref[...], preferred_element_type=F32)
    inter = (gate * jax.nn.sigmoid(gate) * up).astype(BF16)
    o_ref[...] += 0.5 * jnp.dot(inter, wd_ref[...], preferred_element_type=F32)


def _ffn(x, g, wg, wu, wd, *, tm, tf):
    t, d = x.shape
    fp = wg.shape[1]
    return pl.pallas_call(
        _ffn_kernel,
        grid=(t // tm, fp // tf),
        in_specs=[
            pl.BlockSpec((tm, d), lambda i, j: (i, 0)),
            pl.BlockSpec((1, d), lambda i, j: (0, 0)),
            pl.BlockSpec((d, tf), lambda i, j: (0, j)),
            pl.BlockSpec((d, tf), lambda i, j: (0, j)),
            pl.BlockSpec((tf, d), lambda i, j: (j, 0)),
        ],
        out_specs=pl.BlockSpec((tm, d), lambda i, j: (i, 0)),
        out_shape=jax.ShapeDtypeStruct((t, d), F32),
        scratch_shapes=[pltpu.VMEM((tm, d), BF16)],
        compiler_params=_cparams(("parallel", "arbitrary"), 52),
        name="ffn",
    )(x, g, wg, wu, wd)


def _rope_kernel(pos_ref, inv_ref, cos_ref, sin_ref):
    ang = pos_ref[...].astype(F32) * inv_ref[...]
    cos_ref[...] = jnp.cos(ang)
    sin_ref[...] = jnp.sin(ang)


def _rope_tables(pos_col, inv_pad, *, tm):
    t = pos_col.shape[0]
    return pl.pallas_call(
        _rope_kernel,
        grid=(t // tm,),
        in_specs=[pl.BlockSpec((tm, 1), lambda i: (i, 0)), _const_spec((1, LANE))],
        out_specs=[pl.BlockSpec((tm, LANE), lambda i: (i, 0))] * 2,
        out_shape=[jax.ShapeDtypeStruct((t, LANE), F32)] * 2,
        compiler_params=_cparams(("parallel",), 32),
        name="rope_tables",
    )(pos_col, inv_pad)


def _mla_prep_kernel(x_ref, g_ref, cos_ref, sin_ref, w1_ref, gcq_ref, gckv_ref, wuq_ref, wukv_ref,
                     gq_ref, gk_ref, q_ref, k_ref, v_ref):
    xf = x_ref[...]
    h = (xf * _inv_rms(xf, xf.shape[-1]) * g_ref[...]).astype(BF16)
    u = jnp.dot(h, w1_ref[...], preferred_element_type=F32)
    c_q = u[:, :MLA_Q_RANK]
    c_kv = u[:, MLA_Q_RANK:MLA_Q_RANK + MLA_KV_RANK]
    kr = u[:, MLA_Q_RANK + MLA_KV_RANK:MLA_Q_RANK + MLA_KV_RANK + LANE]
    krot = u[:, MLA_Q_RANK + MLA_KV_RANK + LANE:]
    c_q = (c_q * _inv_rms(c_q, MLA_Q_RANK) * gcq_ref[...]).astype(BF16)
    c_kv = (c_kv * _inv_rms(c_kv, MLA_KV_RANK) * gckv_ref[...]).astype(BF16)
    qall = jnp.dot(c_q, wuq_ref[...], preferred_element_type=F32)
    kv = jnp.dot(c_kv, wukv_ref[...], preferred_element_type=F32)
    cos = cos_ref[...]
    sin = sin_ref[...]
    scale = MLA_QK ** -0.5
    gq_n, gq_r, gq_t = gq_ref[0:1, :], gq_ref[1:2, :], gq_ref[2:3, :]
    gk_n, gk_r, gk_t = gk_ref[0:1, :], gk_ref[1:2, :], gk_ref[2:3, :]
    k_roped = kr * gk_r * cos + krot * gk_t * sin
    kr_ss = jnp.sum(kr * kr, axis=-1, keepdims=True)
    for hh in range(MLA_HEADS):
        base = hh * 3 * LANE
        nope = qall[:, base:base + LANE]
        rope = qall[:, base + LANE:base + 2 * LANE]
        rot = qall[:, base + 2 * LANE:base + 3 * LANE]
        ss = jnp.sum(nope * nope, axis=-1, keepdims=True) + jnp.sum(rope * rope, axis=-1, keepdims=True)
        r = lax.rsqrt(ss / MLA_QK + NORM_EPS) * scale
        q_ref[hh, :, 0:LANE] = (nope * r * gq_n).astype(BF16)
        q_ref[hh, :, LANE:2 * LANE] = ((rope * gq_r * cos + rot * gq_t * sin) * r).astype(BF16)
        kn = kv[:, hh * LANE:(hh + 1) * LANE]
        rk = lax.rsqrt((jnp.sum(kn * kn, axis=-1, keepdims=True) + kr_ss) / MLA_QK + NORM_EPS)
        k_ref[hh, :, 0:LANE] = (kn * rk * gk_n).astype(BF16)
        k_ref[hh, :, LANE:2 * LANE] = (k_roped * rk).astype(BF16)
        v_ref[hh] = kv[:, (MLA_HEADS + hh) * LANE:(MLA_HEADS + hh + 1) * LANE].astype(BF16)


def _mla_prep(x, g, cos, sin, w1, gcq, gckv, wuq, wukv, gq3, gk3, *, tm):
    t, d = x.shape
    row = lambda i: (i, 0)
    hrow = lambda i: (0, i, 0)
    return pl.pallas_call(
        _mla_prep_kernel,
        grid=(t // tm,),
        in_specs=[
            pl.BlockSpec((tm, d), row), _const_spec((1, d)),
            pl.BlockSpec((tm, LANE), row), pl.BlockSpec((tm, LANE), row),
            _const_spec(w1.shape), _const_spec(gcq.shape), _const_spec(gckv.shape),
            _const_spec(wuq.shape), _const_spec(wukv.shape), _const_spec(gq3.shape), _const_spec(gk3.shape),
        ],
        out_specs=[
            pl.BlockSpec((MLA_HEADS, tm, MLA_QK_PAD), hrow),
            pl.BlockSpec((MLA_HEADS, tm, MLA_QK_PAD), hrow),
            pl.BlockSpec((MLA_HEADS, tm, MLA_V), hrow),
        ],
        out_shape=[
            jax.ShapeDtypeStruct((MLA_HEADS, t, MLA_QK_PAD), BF16),
            jax.ShapeDtypeStruct((MLA_HEADS, t, MLA_QK_PAD), BF16),
            jax.ShapeDtypeStruct((MLA_HEADS, t, MLA_V), BF16),
        ],
        compiler_params=_cparams(("parallel",), 52),
        name="mla_prep",
    )(x, g, cos, sin, w1, gcq, gckv, wuq, wukv, gq3, gk3)


def _flash_kernel(qt_ref, kt_ref, q_ref, k_ref, v_ref, o_ref, m_scr, l_scr, acc_scr):
    s = pl.program_id(2)
    qi = qt_ref[s]
    ki = kt_ref[s]

    @pl.when(ki == 0)
    def _():
        m_scr[...] = jnp.full_like(m_scr, NEG_INF)
        l_scr[...] = jnp.zeros_like(l_scr)
        acc_scr[...] = jnp.zeros_like(acc_scr)

    def step(masked):
        sc = lax.dot_general(q_ref[...], k_ref[...], (((1,), (1,)), ((), ())), preferred_element_type=F32)
        if masked:
            row = lax.broadcasted_iota(jnp.int32, sc.shape, 0)
            col = lax.broadcasted_iota(jnp.int32, sc.shape, 1)
            sc = jnp.where(col <= row, sc, NEG_INF)
        m_prev = m_scr[...]
        m_new = jnp.maximum(m_prev, jnp.max(sc, axis=-1, keepdims=True))
        alpha = jnp.exp(m_prev - m_new)
        p = jnp.exp(sc - m_new[:, 0:1])
        l_scr[...] = alpha * l_scr[...] + jnp.sum(p, axis=-1, keepdims=True)
        acc_scr[...] = alpha * acc_scr[...] + jnp.dot(p.astype(BF16), v_ref[...], preferred_element_type=F32)
        m_scr[...] = m_new

    @pl.when(ki < qi)
    def _():
        step(False)

    @pl.when(ki == qi)
    def _():
        step(True)
        o_ref[...] = (acc_scr[...] / l_scr[...]).astype(o_ref.dtype)


def _flash(q, k, v, *, batch, seq, tq):
    nq = seq // tq
    pairs = [(a, b) for a in range(nq) for b in range(a + 1)]
    qt = jnp.asarray(np.array([p[0] for p in pairs], np.int32))
    kt = jnp.asarray(np.array([p[1] for p in pairs], np.int32))
    t = batch * seq
    grid_spec = pltpu.PrefetchScalarGridSpec(
        num_scalar_prefetch=2,
        grid=(batch, MLA_HEADS, len(pairs)),
        in_specs=[
            pl.BlockSpec((None, tq, MLA_QK_PAD), lambda b, h, s, qt, kt: (h, b * nq + qt[s], 0)),
            pl.BlockSpec((None, tq, MLA_QK_PAD), lambda b, h, s, qt, kt: (h, b * nq + kt[s], 0)),
            pl.BlockSpec((None, tq, MLA_V), lambda b, h, s, qt, kt: (h, b * nq + kt[s], 0)),
        ],
        out_specs=pl.BlockSpec((tq, MLA_V), lambda b, h, s, qt, kt: (b * nq + qt[s], h)),
        scratch_shapes=[pltpu.VMEM((tq, MLA_V), F32)] * 3,
    )
    return pl.pallas_call(
        _flash_kernel,
        grid_spec=grid_spec,
        out_shape=jax.ShapeDtypeStruct((t, MLA_HEADS * MLA_V), BF16),
        compiler_params=_cparams(("parallel", "parallel", "arbitrary"), 32),
        name="mla_flash",
    )(qt, kt, q, k, v)


def _gla_proj_kernel(x_ref, g_ref, w_ref, w2_ref, b2_ref, q_ref, k_ref, v_ref, la_ref, sr_ref):
    xf = x_ref[...]
    h = (xf * _inv_rms(xf, xf.shape[-1]) * g_ref[...]).astype(BF16)
    u = jnp.dot(h, w_ref[...], preferred_element_type=F32)
    nk = GLA_HEADS * GLA_DK
    nv = GLA_HEADS * GLA_DV
    q_ref[...] = u[:, 0:nk] * (GLA_DK ** -0.5)
    k_ref[...] = u[:, nk:2 * nk]
    v_ref[...] = u[:, 2 * nk:2 * nk + nv]
    gate_lr = u[:, 2 * nk + nv:2 * nk + nv + LANE].astype(BF16)
    z = jnp.dot(gate_lr, w2_ref[...], preferred_element_type=F32) + b2_ref[...]
    la_ref[...] = -(jnp.maximum(-z, 0.0) + jnp.log1p(jnp.exp(-jnp.abs(z)))) / GLA_TAU
    r = u[:, 2 * nk + nv + LANE:]
    sr_ref[...] = r * jax.nn.sigmoid(r)


def _gla_proj(x, g, w, w2, b2, *, tm):
    t, d = x.shape
    nk = GLA_HEADS * GLA_DK
    nv = GLA_HEADS * GLA_DV
    row = lambda i: (i, 0)
    return pl.pallas_call(
        _gla_proj_kernel,
        grid=(t // tm,),
        in_specs=[pl.BlockSpec((tm, d), row), _const_spec((1, d)), _const_spec(w.shape),
                  _const_spec(w2.shape), _const_spec(b2.shape)],
        out_specs=[pl.BlockSpec((tm, nk), row), pl.BlockSpec((tm, nk), row), pl.BlockSpec((tm, nv), row),
                   pl.BlockSpec((tm, nk), row), pl.BlockSpec((tm, nv), row)],
        out_shape=[jax.ShapeDtypeStruct((t, nk), F32), jax.ShapeDtypeStruct((t, nk), F32),
                   jax.ShapeDtypeStruct((t, nv), F32), jax.ShapeDtypeStruct((t, nk), F32),
                   jax.ShapeDtypeStruct((t, nv), F32)],
        compiler_params=_cparams(("parallel",), 52),
        name="gla_proj",
    )(x, g, w, w2, b2)


def _gla_constants():
    c = GLA_CHUNK
    idx = np.arange(c)
    mats = [(idx[None, :] <= idx[:, None])]
    for lvl in GLA_LEVELS:
        p = ((idx // lvl) | 1) * lvl
        qrow = (idx >= p)[:, None] & (idx[None, :] >= p[:, None]) & (idx[None, :] <= idx[:, None])
        krow = (idx < p)[:, None] & (idx[None, :] > idx[:, None]) & (idx[None, :] < p[:, None])
        mats.append(qrow | krow)
    mats.append(idx[None, :] > idx[:, None])
    mstack = np.concatenate(mats, axis=0).astype(np.float32)
    lmask = []
    for lvl in GLA_LEVELS:
        blk = idx // lvl
        m = ((blk[:, None] % 2) == 1) & (blk[None, :] == blk[:, None] - 1)
        lmask.append(np.tile(m.astype(np.float32), (GLA_HEADS, 1)))
    lmask = np.stack(lmask, axis=0)
    hk = np.arange(GLA_HEADS * GLA_DK) // GLA_DK
    hv = np.arange(GLA_HEADS * GLA_DV) // GLA_DV
    hrow = np.arange(GLA_HEADS * c) // c
    headmask = (hrow[:, None] == hk[None, :]).astype(np.float32)
    e_ind = (hk[:, None] == hv[None, :]).astype(np.float32)
    return mstack, lmask, headmask, e_ind


def _gla_kernel(q_ref, k_ref, v_ref, la_ref, sr_ref, go_ref, mst_ref, lmask_ref, hmask_ref, eind_ref, eindt_ref,
                y_ref, st_scr, *, chunks):
    c = GLA_CHUNK
    nk = GLA_HEADS * GLA_DK
    nv = GLA_HEADS * GLA_DV
    nsub = c // GLA_SUB

    @pl.when(pl.program_id(1) == 0)
    def _():
        st_scr[...] = jnp.zeros_like(st_scr)

    def chunk(ci, carry):
        r0 = pl.multiple_of(ci * c, c)
        q = q_ref[pl.ds(r0, c), :]
        k = k_ref[pl.ds(r0, c), :]
        v = v_ref[pl.ds(r0, c), :]
        a = la_ref[pl.ds(r0, c), :]
        a1 = a.astype(BF16)
        r1 = a - a1.astype(F32)
        a2 = r1.astype(BF16)
        a3 = (r1 - a2.astype(F32)).astype(BF16)
        mst = mst_ref[...]
        ex = (jnp.dot(mst, a1, preferred_element_type=F32) + jnp.dot(mst, a2, preferred_element_type=F32)
              + jnp.dot(mst, a3, preferred_element_type=F32))
        b = ex[0:c]
        f = jnp.exp(ex)
        v_bf = v.astype(BF16)

        amat = jnp.zeros((GLA_HEADS * c, c), F32)
        hmask = hmask_ref[...]
        for li in range(len(GLA_LEVELS)):
            fl = f[(li + 1) * c:(li + 2) * c]
            qt = q * fl
            kt = (k * fl).astype(BF16)
            qs = (jnp.concatenate([qt] * GLA_HEADS, axis=0) * hmask).astype(BF16)
            sc = lax.dot_general(qs, kt, (((1,), (1,)), ((), ())), preferred_element_type=F32)
            amat = amat + sc * lmask_ref[li]
        a_bf = amat.astype(BF16)
        o = jnp.concatenate(
            [jnp.dot(a_bf[hh * c:(hh + 1) * c], v_bf[:, hh * GLA_DV:(hh + 1) * GLA_DV], preferred_element_type=F32)
             for hh in range(GLA_HEADS)], axis=1)

        qb = (q * f[0:c]).astype(BF16)
        st = st_scr[...]
        o = o + lax.dot_general(qb, st.astype(BF16), (((1,), (1,)), ((), ())), preferred_element_type=F32)

        q3 = q.reshape(nsub, GLA_SUB, nk)
        k3 = k.reshape(nsub, GLA_SUB, nk)
        b3 = b.reshape(nsub, GLA_SUB, nk)
        v3 = v.reshape(nsub, GLA_SUB, nv)
        tt = lax.broadcasted_iota(jnp.int32, (nsub, GLA_SUB, nk), 1)
        od = jnp.zeros((nsub, GLA_SUB, nv), F32)
        eind = eind_ref[...]
        for s in range(GLA_SUB):
            w = jnp.exp(jnp.minimum(b3 - b3[:, s:s + 1, :], 0.0))
            x = jnp.where(tt >= s, q3 * k3[:, s:s + 1, :] * w, 0.0)
            rr = jnp.dot(x.reshape(c, nk).astype(BF16), eind, preferred_element_type=F32)
            od = od + rr.reshape(nsub, GLA_SUB, nv) * v3[:, s:s + 1, :]
        o = o + od.reshape(c, nv)

        kend = (k * f[(len(GLA_LEVELS) + 1) * c:(len(GLA_LEVELS) + 2) * c]).astype(BF16)
        upd = lax.dot_general(v_bf, kend, (((0,), (0,)), ((), ())), preferred_element_type=F32)
        st_scr[...] = st * f[c - 1:c] + upd * eindt_ref[...]

        go = go_ref[...]
        sr = sr_ref[pl.ds(r0, c), :]
        ys = []
        for hh in range(GLA_HEADS):
            oh = o[:, hh * GLA_DV:(hh + 1) * GLA_DV]
            ys.append(oh * _inv_rms(oh, GLA_DV) * go)
        y_ref[pl.ds(r0, c), :] = (jnp.concatenate(ys, axis=1) * sr).astype(y_ref.dtype)
        return carry

    lax.fori_loop(0, chunks, chunk, 0)


def _gla(q, k, v, la, sr, go, *, batch, seq, tb):
    nk = GLA_HEADS * GLA_DK
    nv = GLA_HEADS * GLA_DV
    nb = seq // tb
    mstack, lmask, headmask, e_ind = _gla_constants()
    consts = [jnp.asarray(mstack, BF16), jnp.asarray(lmask, F32), jnp.asarray(headmask, F32),
              jnp.asarray(e_ind, BF16), jnp.asarray(e_ind.T, F32)]
    row = lambda b, i: (b * nb + i, 0)
    return pl.pallas_call(
        functools.partial(_gla_kernel, chunks=tb // GLA_CHUNK),
        grid=(batch, nb),
        in_specs=[pl.BlockSpec((tb, nk), row), pl.BlockSpec((tb, nk), row), pl.BlockSpec((tb, nv), row),
                  pl.BlockSpec((tb, nk), row), pl.BlockSpec((tb, nv), row), _const_spec(go.shape)]
                 + [_const_spec(cst.shape) for cst in consts],
        out_specs=pl.BlockSpec((tb, nv), row),
        out_shape=jax.ShapeDtypeStruct((batch * seq, nv), BF16),
        scratch_shapes=[pltpu.VMEM((nv, nk), F32)],
        compiler_params=_cparams(("parallel", "arbitrary"), 40),
        name="gla_scan",
    )(q, k, v, la, sr, go, *consts)


def _dswa_proj_kernel(x_ref, g_ref, w_ref, gq_ref, gk_ref, q_ref, k_ref, v_ref):
    xf = x_ref[...]
    h = (xf * _inv_rms(xf, xf.shape[-1]) * g_ref[...]).astype(BF16)
    u = jnp.dot(h, w_ref[...], preferred_element_type=F32)
    n = DSWA_HEADS * DSWA_HEAD_DIM
    gq = gq_ref[...]
    gk = gk_ref[...]
    scale = DSWA_HEAD_DIM ** -0.5
    for hh in range(DSWA_HEADS):
        sl = slice(hh * DSWA_HEAD_DIM, (hh + 1) * DSWA_HEAD_DIM)
        qh = u[:, sl]
        q_ref[:, sl] = (qh * _inv_rms(qh, DSWA_HEAD_DIM) * gq * scale).astype(BF16)
        kh = u[:, n + hh * DSWA_HEAD_DIM:n + (hh + 1) * DSWA_HEAD_DIM]
        k_ref[:, sl] = (kh * _inv_rms(kh, DSWA_HEAD_DIM) * gk).astype(BF16)
    v_ref[...] = u[:, 2 * n:].astype(BF16)


def _dswa_proj(x, g, w, gq, gk, *, tm):
    t, d = x.shape
    n = DSWA_HEADS * DSWA_HEAD_DIM
    row = lambda i: (i, 0)
    return pl.pallas_call(
        _dswa_proj_kernel,
        grid=(t // tm,),
        in_specs=[pl.BlockSpec((tm, d), row), _const_spec((1, d)), _const_spec(w.shape),
                  _const_spec(gq.shape), _const_spec(gk.shape)],
        out_specs=[pl.BlockSpec((tm, n), row)] * 3,
        out_shape=[jax.ShapeDtypeStruct((t, n), BF16)] * 3,
        compiler_params=_cparams(("parallel",), 52),
        name="dswa_proj",
    )(x, g, w, gq, gk)


def _dswa_kernel(slope_ref, q_ref, kc_ref, kp_ref, vc_ref, vp_ref, o_ref, l_ref, *, dilation, group, nb):
    w = DSWA_W
    hh = pl.program_id(1)
    ib = pl.program_id(3)
    slope = slope_ref[group * DSWA_HEADS_PER_GROUP + hh]
    i = lax.broadcasted_iota(jnp.int32, (w, 2 * w), 0)
    j = lax.broadcasted_iota(jnp.int32, (w, 2 * w), 1)
    steps = w + i - j
    bias = -slope * (steps * dilation).astype(F32)
    bias_all = jnp.where(steps >= 0, jnp.where(steps <= w, bias, NEG_INF), NEG_INF)
    bias_first = jnp.where(j >= w, bias_all, NEG_INF)
    for jb in range(nb):
        rows = slice(jb * w, (jb + 1) * w)
        qj = q_ref[0, rows, :]
        if jb == 0:
            kprev, vprev = kp_ref[0], vp_ref[0]
            bias_j = jnp.where(ib > 0, bias_all, bias_first)
        else:
            prows = slice((jb - 1) * w, jb * w)
            kprev, vprev = kc_ref[0, prows, :], vc_ref[0, prows, :]
            bias_j = bias_all
        keys = jnp.concatenate([kprev, kc_ref[0, rows, :]], axis=0)
        vals = jnp.concatenate([vprev, vc_ref[0, rows, :]], axis=0)
        sc = lax.dot_general(qj, keys, (((1,), (1,)), ((), ())), preferred_element_type=F32)
        sc = jnp.where(bias_j > 0.5 * NEG_INF, sc + bias_j, NEG_INF)
        m = jnp.max(sc, axis=-1, keepdims=True)
        p = jnp.exp(sc - m)
        lsum = jnp.sum(p, axis=-1, keepdims=True)
        out = jnp.dot(p.astype(BF16), vals, preferred_element_type=F32) / lsum
        o_ref[0, rows, :] = out
        l_ref[0, rows, :] = jnp.broadcast_to(m + jnp.log(lsum), (w, DSWA_HEAD_DIM))


def _dswa_group(slopes, qc, kc, vc, *, batch, seq, group):
    window, dilation = DSWA_GROUPS[group]
    assert window // dilation == DSWA_W
    n = DSWA_HEADS * DSWA_HEAD_DIM
    no = DSWA_HEADS_PER_GROUP * DSWA_HEAD_DIM
    sd = seq // dilation
    tq = min(512, sd)
    nb = tq // DSWA_W
    qv = qc.reshape(batch, sd, dilation * n)
    kv = kc.reshape(batch, sd, dilation * n)
    vv = vc.reshape(batch, sd, dilation * n)
    col = lambda b, h, r, i: (b, i, r * DSWA_HEADS + group * DSWA_HEADS_PER_GROUP + h)
    pcol = lambda b, h, r, i: (b, jnp.maximum(i * nb - 1, 0), r * DSWA_HEADS + group * DSWA_HEADS_PER_GROUP + h)
    ocol = lambda b, h, r, i: (b, i, r * DSWA_HEADS_PER_GROUP + h)
    cur = pl.BlockSpec((1, tq, DSWA_HEAD_DIM), col)
    prev = pl.BlockSpec((1, DSWA_W, DSWA_HEAD_DIM), pcol)
    out, lse = pl.pallas_call(
        functools.partial(_dswa_kernel, dilation=dilation, group=group, nb=nb),
        grid=(batch, DSWA_HEADS_PER_GROUP, dilation, sd // tq),
        in_specs=[pl.BlockSpec(memory_space=pltpu.SMEM), cur, cur, prev, cur, prev],
        out_specs=[pl.BlockSpec((1, tq, DSWA_HEAD_DIM), ocol)] * 2,
        out_shape=[jax.ShapeDtypeStruct((batch, sd, dilation * no), F32)] * 2,
        compiler_params=_cparams(("parallel", "parallel", "parallel", "arbitrary"), 32),
        name="dswa_g%d" % group,
    )(slopes, qv, kv, kv, vv, vv)
    return out.reshape(batch * seq, no), lse.reshape(batch * seq, no)


def _merge_kernel(x_ref, g_ref, ya_ref, yb_ref, o0_ref, o1_ref, o2_ref, l0_ref, l1_ref, l2_ref,
                  wga_ref, wgb_ref, wgc_ref, wa_ref, wb_ref, wc_ref, wo_ref, out_ref, h_scr, yc_scr):
    @pl.when(pl.program_id(1) == 0)
    def _():
        xf = x_ref[...]
        h_scr[...] = (xf * _inv_rms(xf, xf.shape[-1]) * g_ref[...]).astype(BF16)
        out_ref[...] = xf
        l0, l1, l2 = l0_ref[...], l1_ref[...], l2_ref[...]
        mx = jnp.maximum(jnp.maximum(l0, l1), l2)
        e0, e1, e2 = jnp.exp(l0 - mx), jnp.exp(l1 - mx), jnp.exp(l2 - mx)
        den = e0 + e1 + e2
        yc = (e0 / den) * o0_ref[...] + (e1 / den) * o1_ref[...] + (e2 / den) * o2_ref[...]
        yc_scr[...] = yc.astype(BF16)

    h = h_scr[...]
    dot = lambda a, b: jnp.dot(a, b, preferred_element_type=F32)
    inter = (jax.nn.sigmoid(dot(h, wga_ref[...])) * dot(ya_ref[...], wa_ref[...])
             + jax.nn.sigmoid(dot(h, wgb_ref[...])) * dot(yb_ref[...], wb_ref[...])
             + jax.nn.sigmoid(dot(h, wgc_ref[...])) * dot(yc_scr[...], wc_ref[...]))
    out_ref[...] += dot(inter.astype(BF16), wo_ref[...])


def _merge(x, g, ya, yb, outs, lses, wga, wgb, wgc, wa, wb, wc, wo, *, tm, tn):
    t, d = x.shape
    row = lambda i, j: (i, 0)
    colw = lambda i, j: (0, j)
    no = DSWA_HEADS_PER_GROUP * DSWA_HEAD_DIM
    return pl.pallas_call(
        _merge_kernel,
        grid=(t // tm, d // tn),
        in_specs=[pl.BlockSpec((tm, d), row), pl.BlockSpec((1, d), lambda i, j: (0, 0)),
                  pl.BlockSpec((tm, ya.shape[1]), row), pl.BlockSpec((tm, yb.shape[1]), row)]
                 + [pl.BlockSpec((tm, no), row)] * 6
                 + [pl.BlockSpec((d, tn), colw)] * 3
                 + [pl.BlockSpec((wa.shape[0], tn), colw), pl.BlockSpec((wb.shape[0], tn), colw),
                    pl.BlockSpec((wc.shape[0], tn), colw), pl.BlockSpec((tn, d), lambda i, j: (j, 0))],
        out_specs=pl.BlockSpec((tm, d), row),
        out_shape=jax.ShapeDtypeStruct((t, d), F32),
        scratch_shapes=[pltpu.VMEM((tm, d), BF16), pltpu.VMEM((tm, no), BF16)],
        compiler_params=_cparams(("parallel", "arbitrary"), 52),
        name="merge",
    )(x, g, ya, yb, *outs, *lses, wga, wgb, wgc, wa, wb, wc, wo)


def _pad_cols(w, n):
    return jnp.pad(w, ((0, 0), (0, n - w.shape[1])))


def _rot_half_cols(w):
    half = w.shape[-1] // 2
    return jnp.concatenate([-w[..., half:], w[..., :half]], axis=-1)


def _swap_half(g):
    half = g.shape[-1] // 2
    return jnp.concatenate([g[..., half:], g[..., :half]], axis=-1)


def _gain3(g):
    pad = lambda v: jnp.pad(v, (0, LANE - v.shape[0]))
    return jnp.stack([g[:MLA_NOPE], pad(g[MLA_NOPE:]), pad(_swap_half(g[MLA_NOPE:]))], axis=0)


def kernel(x, positions, ffn1_norm, ffn1_w_gate, ffn1_w_up, ffn1_w_down, mix_norm, w_in, mla_cq_norm, mla_ckv_norm, mla_w_uq, mla_w_ukv, mla_q_norm, mla_k_norm, gla_w_gate2, gla_b_gate2, gla_o_norm, dswa_q_norm, dswa_k_norm, w_branch_a, w_branch_b, w_branch_c, w_out, ffn2_norm, ffn2_w_gate, ffn2_w_up, ffn2_w_down):
    batch, seq, d = x.shape
    depth = w_in.shape[0]
    d_ff = ffn1_w_gate.shape[-1]
    t = batch * seq
    tm = 512
    tf = 512
    ff_pad = -(-d_ff // tf) * tf
    assert seq % max(wd for wd, _ in DSWA_GROUPS) == 0 and seq % tm == 0

    xf = x.reshape(t, d)
    half = MLA_ROPE // 2
    inv_freq = ROPE_THETA ** (-jnp.arange(half, dtype=F32) / half)
    inv_pad = jnp.pad(jnp.concatenate([inv_freq, inv_freq]), (0, LANE - MLA_ROPE)).reshape(1, LANE)
    cos, sin = _rope_tables(positions.reshape(t, 1), inv_pad, tm=tm)
    slopes = 2.0 ** (-ALIBI_MAX_EXP * jnp.arange(1, DSWA_HEADS + 1, dtype=F32) / DSWA_HEADS)

    o_cq, o_ckv, o_kr = 0, MLA_Q_RANK, MLA_Q_RANK + MLA_KV_RANK
    o_gla = o_kr + MLA_ROPE
    n_qkv_b = GLA_HEADS * (2 * GLA_DK + GLA_DV)
    o_glr = o_gla + n_qkv_b
    o_rb = o_glr + GLA_GATE_RANK
    o_c = o_rb + GLA_HEADS * GLA_DV
    o_ga = o_c + 3 * DSWA_HEADS * DSWA_HEAD_DIM
    o_gb, o_gc = o_ga + d, o_ga + 2 * d

    def ffn_weights(wg, wu, wdn):
        padc = lambda w: jnp.pad(w, ((0, 0), (0, ff_pad - d_ff))).astype(BF16)
        return padc(wg), padc(wu), jnp.pad(wdn, ((0, ff_pad - d_ff), (0, 0))).astype(BF16)

    for l in range(depth):
        wg, wu, wdn = ffn_weights(ffn1_w_gate[l], ffn1_w_up[l], ffn1_w_down[l])
        xf = _ffn(xf, ffn1_norm[l].reshape(1, d), wg, wu, wdn, tm=tm, tf=tf)

        wl = w_in[l]
        gmix = mix_norm[l].reshape(1, d)
        w_kr = wl[:, o_kr:o_kr + MLA_ROPE]
        w1 = jnp.concatenate([wl[:, o_cq:o_kr], _pad_cols(w_kr, LANE), _pad_cols(_rot_half_cols(w_kr), LANE)],
                             axis=1).astype(BF16)
        wq3 = mla_w_uq[l].reshape(MLA_Q_RANK, MLA_HEADS, MLA_QK)
        wq_rope = wq3[..., MLA_NOPE:]
        zpad = jnp.zeros((MLA_Q_RANK, MLA_HEADS, LANE - MLA_ROPE), F32)
        wuq = jnp.concatenate([wq3[..., :MLA_NOPE], wq_rope, zpad, _rot_half_cols(wq_rope), zpad], axis=-1)
        wuq = wuq.reshape(MLA_Q_RANK, MLA_HEADS * 3 * LANE).astype(BF16)
        wkv3 = mla_w_ukv[l].reshape(MLA_KV_RANK, MLA_HEADS, MLA_NOPE + MLA_V)
        wukv = jnp.concatenate([wkv3[..., :MLA_NOPE].reshape(MLA_KV_RANK, -1),
                                wkv3[..., MLA_NOPE:].reshape(MLA_KV_RANK, -1)], axis=1).astype(BF16)
        qa, ka, va = _mla_prep(xf, gmix, cos, sin, w1, mla_cq_norm[l].reshape(1, -1), mla_ckv_norm[l].reshape(1, -1),
                               wuq, wukv, _gain3(mla_q_norm[l]), _gain3(mla_k_norm[l]), tm=tm)
        y_a = _flash(qa, ka, va, batch=batch, seq=seq, tq=512)

        w_gla = jnp.concatenate([wl[:, o_gla:o_glr], _pad_cols(wl[:, o_glr:o_rb], LANE), wl[:, o_rb:o_c]],
                                axis=1).astype(BF16)
        w2 = jnp.pad(gla_w_gate2[l], ((0, LANE - GLA_GATE_RANK), (0, 0))).astype(BF16)
        qb, kb, vb, la, sr = _gla_proj(xf, gmix, w_gla, w2, gla_b_gate2[l].reshape(1, -1), tm=tm)
        y_b = _gla(qb, kb, vb, la, sr, gla_o_norm[l].reshape(1, -1), batch=batch, seq=seq, tb=512)

        qc, kc, vc = _dswa_proj(xf, gmix, wl[:, o_c:o_ga].astype(BF16), dswa_q_norm[l].reshape(1, -1),
                                dswa_k_norm[l].reshape(1, -1), tm=tm)
        outs, lses = [], []
        for gi in range(len(DSWA_GROUPS)):
            og, lg = _dswa_group(slopes, qc, kc, vc, batch=batch, seq=seq, group=gi)
            outs.append(og)
            lses.append(lg)

        xf = _merge(xf, gmix, y_a, y_b, outs, lses,
                    wl[:, o_ga:o_gb].astype(BF16), wl[:, o_gb:o_gc].astype(BF16), wl[:, o_gc:o_gc + d].astype(BF16),
                    w_branch_a[l].astype(BF16), w_branch_b[l].astype(BF16), w_branch_c[l].astype(BF16),
                    w_out[l].astype(BF16), tm=tm, tn=256)

        wg, wu, wdn = ffn_weights(ffn2_w_gate[l], ffn2_w_up[l], ffn2_w_down[l])
        xf = _ffn(xf, ffn2_norm[l].reshape(1, d), wg, wu, wdn, tm=tm, tf=tf)
    return xf.reshape(batch, seq, d)
```

```python
import functools

import numpy as np
import jax
import jax.numpy as jnp
from jax import lax
from jax.experimental import pallas as pl
from jax.experimental.pallas import tpu as pltpu

F32 = jnp.float32
BF16 = jnp.bfloat16

MLA_HEADS = 6
MLA_Q_RANK = 512
MLA_KV_RANK = 256
MLA_NOPE = 128
MLA_ROPE = 64
MLA_V = 128
MLA_QK = MLA_NOPE + MLA_ROPE
MLA_QK_PAD = 256
GLA_HEADS = 4
GLA_DK = 64
GLA_DV = 128
GLA_GATE_RANK = 16
GLA_TAU = 16.0
GLA_CHUNK = 64
GLA_SUB = 8
GLA_LEVELS = (32, 16, 8)
DSWA_GROUPS = ((128, 1), (512, 4), (2048, 16))
DSWA_HEADS_PER_GROUP = 2
DSWA_HEADS = 6
DSWA_HEAD_DIM = 128
DSWA_W = 128
ROPE_THETA = 10000.0
ALIBI_MAX_EXP = 8.0
NORM_EPS = 1e-6
NEG_INF = -1e30
LOG2E = 1.4426950408889634

LANE = 128
MIB = 1024 * 1024


def _cparams(sem, vmem_mib):
    return pltpu.CompilerParams(dimension_semantics=sem, vmem_limit_bytes=int(vmem_mib * MIB))


def _inv_rms(xf, n):
    return lax.rsqrt(jnp.sum(xf * xf, axis=-1, keepdims=True) / n + NORM_EPS)


def _const_spec(shape):
    nd = len(shape)
    return pl.BlockSpec(shape, lambda *_: (0,) * nd)


def _ffn_kernel(x_ref, g_ref, wg_ref, wu_ref, wd_ref, o_ref, h_scr):
    @pl.when(pl.program_id(1) == 0)
    def _():
        xf = x_ref[...]
        h_scr[...] = (xf * _inv_rms(xf, xf.shape[-1]) * g_ref[...]).astype(BF16)
        o_ref[...] = xf

    h = h_scr[...]
    gate = jnp.dot(h, wg_ref[...], preferred_element_type=F32)
    up = jnp.dot(h, wu_ref[...], preferred_element_type=F32)
    inter = (gate * jax.nn.sigmoid(gate) * up).astype(BF16)
    o_ref[...] += 0.5 * jnp.dot(inter, wd_ref[...], preferred_element_type=F32)


def _ffn(x, g, wg, wu, wd, *, tm, tf):
    t, d = x.shape
    fp = wg.shape[1]
    return pl.pallas_call(
        _ffn_kernel,
        grid=(t // tm, fp // tf),
        in_specs=[
            pl.BlockSpec((tm, d), lambda i, j: (i, 0)),
            pl.BlockSpec((1, d), lambda i, j: (0, 0)),
            pl.BlockSpec((d, tf), lambda i, j: (0, j)),
            pl.BlockSpec((d, tf), lambda i, j: (0, j)),
            pl.BlockSpec((tf, d), lambda i, j: (j, 0)),
        ],
        out_specs=pl.BlockSpec((tm, d), lambda i, j: (i, 0)),
        out_shape=jax.ShapeDtypeStruct((t, d), F32),
        scratch_shapes=[pltpu.VMEM((tm, d), BF16)],
        compiler_params=_cparams(("parallel", "arbitrary"), 52),
        name="ffn",
    )(x, g, wg, wu, wd)


def _rope_kernel(pos_ref, inv_ref, cos_ref, sin_ref):
    ang = pos_ref[...].astype(F32) * inv_ref[...]
    cos_ref[...] = jnp.cos(ang)
    sin_ref[...] = jnp.sin(ang)


def _rope_tables(pos_col, inv_pad, *, tm):
    t = pos_col.shape[0]
    return pl.pallas_call(
        _rope_kernel,
        grid=(t // tm,),
        in_specs=[pl.BlockSpec((tm, 1), lambda i: (i, 0)), _const_spec((1, LANE))],
        out_specs=[pl.BlockSpec((tm, LANE), lambda i: (i, 0))] * 2,
        out_shape=[jax.ShapeDtypeStruct((t, LANE), F32)] * 2,
        compiler_params=_cparams(("parallel",), 32),
        name="rope_tables",
    )(pos_col, inv_pad)


def _mla_prep_kernel(x_ref, g_ref, cos_ref, sin_ref, w1_ref, gcq_ref, gckv_ref, wuq_ref, wukv_ref,
                     gq_ref, gk_ref, q_ref, k_ref, v_ref):
    xf = x_ref[...]
    h = (xf * _inv_rms(xf, xf.shape[-1]) * g_ref[...]).astype(BF16)
    u = jnp.dot(h, w1_ref[...], preferred_element_type=F32)
    c_q = u[:, :MLA_Q_RANK]
    c_kv = u[:, MLA_Q_RANK:MLA_Q_RANK + MLA_KV_RANK]
    kr = u[:, MLA_Q_RANK + MLA_KV_RANK:MLA_Q_RANK + MLA_KV_RANK + LANE]
    krot = u[:, MLA_Q_RANK + MLA_KV_RANK + LANE:]
    c_q = (c_q * _inv_rms(c_q, MLA_Q_RANK) * gcq_ref[...]).astype(BF16)
    c_kv = (c_kv * _inv_rms(c_kv, MLA_KV_RANK) * gckv_ref[...]).astype(BF16)
    qall = jnp.dot(c_q, wuq_ref[...], preferred_element_type=F32)
    kv = jnp.dot(c_kv, wukv_ref[...], preferred_element_type=F32)
    cos = cos_ref[...]
    sin = sin_ref[...]
    scale = MLA_QK ** -0.5 * LOG2E
    gq_n, gq_r, gq_t = gq_ref[0:1, :], gq_ref[1:2, :], gq_ref[2:3, :]
    gk_n, gk_r, gk_t = gk_ref[0:1, :], gk_ref[1:2, :], gk_ref[2:3, :]
    k_roped = kr * gk_r * cos + krot * gk_t * sin
    kr_ss = jnp.sum(kr * kr, axis=-1, keepdims=True)
    for hh in range(MLA_HEADS):
        base = hh * 3 * LANE
        nope = qall[:, base:base + LANE]
        rope = qall[:, base + LANE:base + 2 * LANE]
        rot = qall[:, base + 2 * LANE:base + 3 * LANE]
        ss = jnp.sum(nope * nope, axis=-1, keepdims=True) + jnp.sum(rope * rope, axis=-1, keepdims=True)
        r = lax.rsqrt(ss / MLA_QK + NORM_EPS) * scale
        q_ref[hh, :, 0:LANE] = (nope * r * gq_n).astype(BF16)
        q_ref[hh, :, LANE:2 * LANE] = ((rope * gq_r * cos + rot * gq_t * sin) * r).astype(BF16)
        kn = kv[:, hh * LANE:(hh + 1) * LANE]
        rk = lax.rsqrt((jnp.sum(kn * kn, axis=-1, keepdims=True) + kr_ss) / MLA_QK + NORM_EPS)
        k_ref[hh, :, 0:LANE] = (kn * rk * gk_n).astype(BF16)
        k_ref[hh, :, LANE:2 * LANE] = (k_roped * rk).astype(BF16)
        v_ref[hh, :, 0:MLA_V] = kv[:, (MLA_HEADS + hh) * LANE:(MLA_HEADS + hh + 1) * LANE].astype(BF16)
        v_ref[hh, :, MLA_V:] = jnp.ones((kv.shape[0], MLA_V), BF16)


def _mla_prep(x, g, cos, sin, w1, gcq, gckv, wuq, wukv, gq3, gk3, *, tm):
    t, d = x.shape
    row = lambda i: (i, 0)
    hrow = lambda i: (0, i, 0)
    return pl.pallas_call(
        _mla_prep_kernel,
        grid=(t // tm,),
        in_specs=[
            pl.BlockSpec((tm, d), row), _const_spec((1, d)),
            pl.BlockSpec((tm, LANE), row), pl.BlockSpec((tm, LANE), row),
            _const_spec(w1.shape), _const_spec(gcq.shape), _const_spec(gckv.shape),
            _const_spec(wuq.shape), _const_spec(wukv.shape), _const_spec(gq3.shape), _const_spec(gk3.shape),
        ],
        out_specs=[
            pl.BlockSpec((MLA_HEADS, tm, MLA_QK_PAD), hrow),
            pl.BlockSpec((MLA_HEADS, tm, MLA_QK_PAD), hrow),
            pl.BlockSpec((MLA_HEADS, tm, 2 * MLA_V), hrow),
        ],
        out_shape=[
            jax.ShapeDtypeStruct((MLA_HEADS, t, MLA_QK_PAD), BF16),
            jax.ShapeDtypeStruct((MLA_HEADS, t, MLA_QK_PAD), BF16),
            jax.ShapeDtypeStruct((MLA_HEADS, t, 2 * MLA_V), BF16),
        ],
        compiler_params=_cparams(("parallel",), 52),
        name="mla_prep",
    )(x, g, cos, sin, w1, gcq, gckv, wuq, wukv, gq3, gk3)


def _flash_kernel(q_ref, k_ref, v_ref, o_ref, m_scr, acc_scr, *, tq, tk):
    qi = pl.program_id(2)
    ndiag = tq // tk
    m_scr[...] = jnp.full_like(m_scr, NEG_INF)
    acc_scr[...] = jnp.zeros_like(acc_scr)
    q = q_ref[...]

    def update(chain, tile, mask_off):
        r0 = pl.multiple_of(tile * tk, tk)
        sc = lax.dot_general(q, k_ref[pl.ds(r0, tk), :], (((1,), (1,)), ((), ())), preferred_element_type=F32)
        if mask_off is not None:
            row = lax.broadcasted_iota(jnp.int32, sc.shape, 0)
            col = lax.broadcasted_iota(jnp.int32, sc.shape, 1)
            sc = jnp.where(col + mask_off <= row, sc, NEG_INF)
        m_prev = m_scr[chain]
        m_new = jnp.maximum(m_prev, jnp.max(sc, axis=-1, keepdims=True))
        alpha = jnp.exp2(m_prev - m_new)
        p = jnp.exp2(sc - jnp.tile(m_new, (1, tk // LANE)))
        pv = jnp.dot(p.astype(BF16), v_ref[pl.ds(r0, tk), :], preferred_element_type=F32)
        acc_scr[chain] = jnp.tile(alpha, (1, 2)) * acc_scr[chain] + pv
        m_scr[chain] = m_new

    def body(i, carry):
        update(0, 2 * i, None)
        update(1, 2 * i + 1, None)
        return carry

    lax.fori_loop(0, qi * (ndiag // 2), body, 0)
    for dd in range(ndiag):
        update(dd % 2, qi * ndiag + dd, dd * tk)
    m0, m1 = m_scr[0], m_scr[1]
    m = jnp.maximum(m0, m1)
    w0 = jnp.exp2(m0 - m)
    w1 = jnp.exp2(m1 - m)
    num = w0 * acc_scr[0, :, 0:MLA_V] + w1 * acc_scr[1, :, 0:MLA_V]
    den = w0 * acc_scr[0, :, MLA_V:] + w1 * acc_scr[1, :, MLA_V:]
    o_ref[...] = (num / den).astype(o_ref.dtype)


def _flash(q, k, v, *, batch, seq, tq, tk):
    assert seq % tq == 0 and tq % (2 * tk) == 0
    nq = seq // tq
    t = batch * seq
    return pl.pallas_call(
        functools.partial(_flash_kernel, tq=tq, tk=tk),
        grid=(batch, MLA_HEADS, nq),
        in_specs=[
            pl.BlockSpec((None, tq, MLA_QK_PAD), lambda b, h, i: (h, b * nq + i, 0)),
            pl.BlockSpec((None, seq, MLA_QK_PAD), lambda b, h, i: (h, b, 0)),
            pl.BlockSpec((None, seq, 2 * MLA_V), lambda b, h, i: (h, b, 0)),
        ],
        out_specs=pl.BlockSpec((tq, MLA_V), lambda b, h, i: (b * nq + i, h)),
        out_shape=jax.ShapeDtypeStruct((t, MLA_HEADS * MLA_V), BF16),
        scratch_shapes=[pltpu.VMEM((2, tq, MLA_V), F32), pltpu.VMEM((2, tq, 2 * MLA_V), F32)],
        compiler_params=_cparams(("parallel", "parallel", "arbitrary"), 52),
        name="mla_flash",
    )(q, k, v)


def _gla_proj_kernel(x_ref, g_ref, w_ref, w2_ref, b2_ref, q_ref, k_ref, v_ref, la_ref, sr_ref):
    xf = x_ref[...]
    h = (xf * _inv_rms(xf, xf.shape[-1]) * g_ref[...]).astype(BF16)
    u = jnp.dot(h, w_ref[...], preferred_element_type=F32)
    nk = GLA_HEADS * GLA_DK
    nv = GLA_HEADS * GLA_DV
    q_ref[...] = u[:, 0:nk] * (GLA_DK ** -0.5)
    k_ref[...] = u[:, nk:2 * nk]
    v_ref[...] = u[:, 2 * nk:2 * nk + nv]
    gate_lr = u[:, 2 * nk + nv:2 * nk + nv + LANE].astype(BF16)
    z = jnp.dot(gate_lr, w2_ref[...], preferred_element_type=F32) + b2_ref[...]
    la_ref[...] = -(jnp.maximum(-z, 0.0) + jnp.log1p(jnp.exp(-jnp.abs(z)))) / GLA_TAU
    r = u[:, 2 * nk + nv + LANE:]
    sr_ref[...] = r * jax.nn.sigmoid(r)


def _gla_proj(x, g, w, w2, b2, *, tm):
    t, d = x.shape
    nk = GLA_HEADS * GLA_DK
    nv = GLA_HEADS * GLA_DV
    row = lambda i: (i, 0)
    return pl.pallas_call(
        _gla_proj_kernel,
        grid=(t // tm,),
        in_specs=[pl.BlockSpec((tm, d), row), _const_spec((1, d)), _const_spec(w.shape),
                  _const_spec(w2.shape), _const_spec(b2.shape)],
        out_specs=[pl.BlockSpec((tm, nk), row), pl.BlockSpec((tm, nk), row), pl.BlockSpec((tm, nv), row),
                   pl.BlockSpec((tm, nk), row), pl.BlockSpec((tm, nv), row)],
        out_shape=[jax.ShapeDtypeStruct((t, nk), F32), jax.ShapeDtypeStruct((t, nk), F32),
                   jax.ShapeDtypeStruct((t, nv), F32), jax.ShapeDtypeStruct((t, nk), F32),
                   jax.ShapeDtypeStruct((t, nv), F32)],
        compiler_params=_cparams(("parallel",), 52),
        name="gla_proj",
    )(x, g, w, w2, b2)


def _gla_constants():
    c = GLA_CHUNK
    idx = np.arange(c)
    mats = [(idx[None, :] <= idx[:, None])]
    for lvl in GLA_LEVELS:
        p = ((idx // lvl) | 1) * lvl
        qrow = (idx >= p)[:, None] & (idx[None, :] >= p[:, None]) & (idx[None, :] <= idx[:, None])
        krow = (idx < p)[:, None] & (idx[None, :] > idx[:, None]) & (idx[None, :] < p[:, None])
        mats.append(qrow | krow)
    mats.append(idx[None, :] > idx[:, None])
    mstack = np.concatenate(mats, axis=0).astype(np.float32)
    lmask = []
    for lvl in GLA_LEVELS:
        blk = idx // lvl
        m = ((blk[:, None] % 2) == 1) & (blk[None, :] == blk[:, None] - 1)
        lmask.append(np.tile(m.astype(np.float32), (GLA_HEADS, 1)))
    lmask = np.stack(lmask, axis=0)
    hk = np.arange(GLA_HEADS * GLA_DK) // GLA_DK
    hv = np.arange(GLA_HEADS * GLA_DV) // GLA_DV
    hrow = np.arange(GLA_HEADS * c) // c
    headmask = (hrow[:, None] == hk[None, :]).astype(np.float32)
    e_ind = (hk[:, None] == hv[None, :]).astype(np.float32)
    return mstack, lmask, headmask, e_ind


def _gla_kernel(q_ref, k_ref, v_ref, la_ref, sr_ref, go_ref, mst_ref, lmask_ref, hmask_ref, eind_ref, eindt_ref,
                y_ref, st_scr, *, chunks):
    c = GLA_CHUNK
    nk = GLA_HEADS * GLA_DK
    nv = GLA_HEADS * GLA_DV
    nsub = c // GLA_SUB

    @pl.when(pl.program_id(1) == 0)
    def _():
        st_scr[...] = jnp.zeros_like(st_scr)

    def chunk(ci, carry):
        r0 = pl.multiple_of(ci * c, c)
        q = q_ref[pl.ds(r0, c), :]
        k = k_ref[pl.ds(r0, c), :]
        v = v_ref[pl.ds(r0, c), :]
        a = la_ref[pl.ds(r0, c), :]
        a1 = a.astype(BF16)
        r1 = a - a1.astype(F32)
        a2 = r1.astype(BF16)
        a3 = (r1 - a2.astype(F32)).astype(BF16)
        mst = mst_ref[...]
        ex = (jnp.dot(mst, a1, preferred_element_type=F32) + jnp.dot(mst, a2, preferred_element_type=F32)
              + jnp.dot(mst, a3, preferred_element_type=F32))
        b = ex[0:c]
        f = jnp.exp(ex)
        v_bf = v.astype(BF16)

        amat = jnp.zeros((GLA_HEADS * c, c), F32)
        hmask = hmask_ref[...]
        for li in range(len(GLA_LEVELS)):
            fl = f[(li + 1) * c:(li + 2) * c]
            qt = q * fl
            kt = (k * fl).astype(BF16)
            qs = (jnp.concatenate([qt] * GLA_HEADS, axis=0) * hmask).astype(BF16)
            sc = lax.dot_general(qs, kt, (((1,), (1,)), ((), ())), preferred_element_type=F32)
            amat = amat + sc * lmask_ref[li]
        a_bf = amat.astype(BF16)
        o = jnp.concatenate(
            [jnp.dot(a_bf[hh * c:(hh + 1) * c], v_bf[:, hh * GLA_DV:(hh + 1) * GLA_DV], preferred_element_type=F32)
             for hh in range(GLA_HEADS)], axis=1)

        qb = (q * f[0:c]).astype(BF16)
        st = st_scr[...]
        o = o + lax.dot_general(qb, st.astype(BF16), (((1,), (1,)), ((), ())), preferred_element_type=F32)

        q3 = q.reshape(nsub, GLA_SUB, nk)
        k3 = k.reshape(nsub, GLA_SUB, nk)
        b3 = b.reshape(nsub, GLA_SUB, nk)
        v3 = v.reshape(nsub, GLA_SUB, nv)
        tt = lax.broadcasted_iota(jnp.int32, (nsub, GLA_SUB, nk), 1)
        od = jnp.zeros((nsub, GLA_SUB, nv), F32)
        eind = eind_ref[...]
        for s in range(GLA_SUB):
            w = jnp.exp(jnp.minimum(b3 - b3[:, s:s + 1, :], 0.0))
            x = jnp.where(tt >= s, q3 * k3[:, s:s + 1, :] * w, 0.0)
            rr = jnp.dot(x.reshape(c, nk).astype(BF16), eind, preferred_element_type=F32)
            od = od + rr.reshape(nsub, GLA_SUB, nv) * v3[:, s:s + 1, :]
        o = o + od.reshape(c, nv)

        kend = (k * f[(len(GLA_LEVELS) + 1) * c:(len(GLA_LEVELS) + 2) * c]).astype(BF16)
        upd = lax.dot_general(v_bf, kend, (((0,), (0,)), ((), ())), preferred_element_type=F32)
        st_scr[...] = st * f[c - 1:c] + upd * eindt_ref[...]

        go = go_ref[...]
        sr = sr_ref[pl.ds(r0, c), :]
        ys = []
        for hh in range(GLA_HEADS):
            oh = o[:, hh * GLA_DV:(hh + 1) * GLA_DV]
            ys.append(oh * _inv_rms(oh, GLA_DV) * go)
        y_ref[pl.ds(r0, c), :] = (jnp.concatenate(ys, axis=1) * sr).astype(y_ref.dtype)
        return carry

    lax.fori_loop(0, chunks, chunk, 0)


def _gla(q, k, v, la, sr, go, *, batch, seq, tb):
    nk = GLA_HEADS * GLA_DK
    nv = GLA_HEADS * GLA_DV
    nb = seq // tb
    mstack, lmask, headmask, e_ind = _gla_constants()
    consts = [jnp.asarray(mstack, BF16), jnp.asarray(lmask, F32), jnp.asarray(headmask, F32),
              jnp.asarray(e_ind, BF16), jnp.asarray(e_ind.T, F32)]
    row = lambda b, i: (b * nb + i, 0)
    return pl.pallas_call(
        functools.partial(_gla_kernel, chunks=tb // GLA_CHUNK),
        grid=(batch, nb),
        in_specs=[pl.BlockSpec((tb, nk), row), pl.BlockSpec((tb, nk), row), pl.BlockSpec((tb, nv), row),
                  pl.BlockSpec((tb, nk), row), pl.BlockSpec((tb, nv), row), _const_spec(go.shape)]
                 + [_const_spec(cst.shape) for cst in consts],
        out_specs=pl.BlockSpec((tb, nv), row),
        out_shape=jax.ShapeDtypeStruct((batch * seq, nv), BF16),
        scratch_shapes=[pltpu.VMEM((nv, nk), F32)],
        compiler_params=_cparams(("parallel", "arbitrary"), 40),
        name="gla_scan",
    )(q, k, v, la, sr, go, *consts)


def _dswa_proj_kernel(x_ref, g_ref, w_ref, gq_ref, gk_ref, *refs, tm):
    ng = len(DSWA_GROUPS)
    out_refs = refs[:3 * ng]
    q_scr, k_scr, v_scr = refs[3 * ng:]
    xf = x_ref[...]
    h = (xf * _inv_rms(xf, xf.shape[-1]) * g_ref[...]).astype(BF16)
    u = jnp.dot(h, w_ref[...], preferred_element_type=F32)
    n = DSWA_HEADS * DSWA_HEAD_DIM
    gq = gq_ref[...]
    gk = gk_ref[...]
    scale = DSWA_HEAD_DIM ** -0.5
    for hh in range(DSWA_HEADS):
        sl = slice(hh * DSWA_HEAD_DIM, (hh + 1) * DSWA_HEAD_DIM)
        qh = u[:, sl]
        q_scr[hh] = qh * _inv_rms(qh, DSWA_HEAD_DIM) * gq * scale
        kh = u[:, n + hh * DSWA_HEAD_DIM:n + (hh + 1) * DSWA_HEAD_DIM]
        k_scr[hh] = kh * _inv_rms(kh, DSWA_HEAD_DIM) * gk
        v_scr[hh] = u[:, 2 * n + hh * DSWA_HEAD_DIM:2 * n + (hh + 1) * DSWA_HEAD_DIM]
    for gi, (_, dil) in enumerate(DSWA_GROUPS):
        for src, dst in zip((q_scr, k_scr, v_scr), out_refs[3 * gi:3 * gi + 3]):
            for hh in range(DSWA_HEADS_PER_GROUP):
                head = gi * DSWA_HEADS_PER_GROUP + hh
                cols = slice(hh * DSWA_HEAD_DIM, (hh + 1) * DSWA_HEAD_DIM)
                for r in range(dil):
                    rows = pl.ds(r, tm // dil, stride=dil) if dil > 1 else slice(None)
                    dst[r, :, cols] = src[head, rows, :].astype(BF16)


def _dswa_proj(x, g, w, gq, gk, *, batch, seq, tm):
    t, d = x.shape
    n = DSWA_HEADS * DSWA_HEAD_DIM
    gw = DSWA_HEADS_PER_GROUP * DSWA_HEAD_DIM
    nblk = seq // tm
    out_specs, out_shape = [], []
    for _, dil in DSWA_GROUPS:
        assert tm % (16 * dil) == 0
        out_specs += [pl.BlockSpec((None, dil, tm // dil, gw), lambda b, i: (b, 0, i, 0))] * 3
        out_shape += [jax.ShapeDtypeStruct((batch, dil, seq // dil, gw), BF16)] * 3
    return pl.pallas_call(
        functools.partial(_dswa_proj_kernel, tm=tm),
        grid=(batch, nblk),
        in_specs=[pl.BlockSpec((tm, d), lambda b, i: (b * nblk + i, 0)), _const_spec((1, d)), _const_spec(w.shape),
                  _const_spec(gq.shape), _const_spec(gk.shape)],
        out_specs=out_specs,
        out_shape=out_shape,
        scratch_shapes=[pltpu.VMEM((DSWA_HEADS, tm, DSWA_HEAD_DIM), F32)] * 3,
        compiler_params=_cparams(("parallel", "parallel"), 52),
        name="dswa_proj",
    )(x, g, w, gq, gk)


def _dswa_kernel(slope_ref, q_ref, kc_ref, kp_ref, vc_ref, vp_ref, o_ref, l_ref, *, dilation, group, nb):
    w = DSWA_W
    ib = pl.program_id(2)
    i = lax.broadcasted_iota(jnp.int32, (w, 2 * w), 0)
    j = lax.broadcasted_iota(jnp.int32, (w, 2 * w), 1)
    steps = w + i - j
    dist = (steps * dilation).astype(F32)
    in_window = (steps >= 0) & (steps <= w)
    for hh in range(DSWA_HEADS_PER_GROUP):
        cols = slice(hh * DSWA_HEAD_DIM, (hh + 1) * DSWA_HEAD_DIM)
        bias_all = jnp.where(in_window, -slope_ref[group * DSWA_HEADS_PER_GROUP + hh] * dist, NEG_INF)
        bias_first = jnp.where(j >= w, bias_all, NEG_INF)
        for jb in range(nb):
            rows = slice(jb * w, (jb + 1) * w)
            if jb == 0:
                kprev, vprev = kp_ref[:, cols], vp_ref[:, cols]
                bias_j = jnp.where(ib > 0, bias_all, bias_first)
            else:
                prows = slice((jb - 1) * w, jb * w)
                kprev, vprev = kc_ref[prows, cols], vc_ref[prows, cols]
                bias_j = bias_all
            keys = jnp.concatenate([kprev, kc_ref[rows, cols]], axis=0)
            vals = jnp.concatenate([vprev, vc_ref[rows, cols]], axis=0)
            sc = lax.dot_general(q_ref[rows, cols], keys, (((1,), (1,)), ((), ())), preferred_element_type=F32)
            sc = jnp.where(bias_j > 0.5 * NEG_INF, sc + bias_j, NEG_INF)
            m = jnp.max(sc, axis=-1, keepdims=True)
            p = jnp.exp(sc - m)
            lsum = jnp.sum(p, axis=-1, keepdims=True)
            o_ref[rows, cols] = jnp.dot(p.astype(BF16), vals, preferred_element_type=F32) / lsum
            l_ref[rows, cols] = jnp.broadcast_to(m + jnp.log(lsum), (w, DSWA_HEAD_DIM))


def _dswa_group(slopes, qg, kg, vg, *, group):
    window, dilation = DSWA_GROUPS[group]
    assert window // dilation == DSWA_W
    batch, _, sd, gw = qg.shape
    tq = min(512, sd)
    nb = tq // DSWA_W
    cur = pl.BlockSpec((None, None, tq, gw), lambda b, r, i: (b, r, i, 0))
    prev = pl.BlockSpec((None, None, DSWA_W, gw), lambda b, r, i: (b, r, jnp.maximum(i * nb - 1, 0), 0))
    return pl.pallas_call(
        functools.partial(_dswa_kernel, dilation=dilation, group=group, nb=nb),
        grid=(batch, dilation, sd // tq),
        in_specs=[pl.BlockSpec(memory_space=pltpu.SMEM), cur, cur, prev, cur, prev],
        out_specs=[cur] * 2,
        out_shape=[jax.ShapeDtypeStruct(qg.shape, F32)] * 2,
        compiler_params=_cparams(("parallel", "parallel", "arbitrary"), 32),
        name="dswa_g%d" % group,
    )(slopes, qg, kg, kg, vg, vg)


def _dswa_combine_kernel(*refs, tm):
    ng = len(DSWA_GROUPS)
    in_refs, y_ref, scr = refs[:2 * ng], refs[2 * ng], refs[2 * ng + 1]
    vals = []
    for gi, (_, dil) in enumerate(DSWA_GROUPS):
        for which in range(2):
            src = in_refs[2 * gi + which]
            if dil == 1:
                vals.append(src[0])
            else:
                halves = []
                for hh in range(DSWA_HEADS_PER_GROUP):
                    slot = (2 * gi + which) * DSWA_HEADS_PER_GROUP + hh
                    for r in range(dil):
                        scr[slot, pl.ds(r, tm // dil, stride=dil), :] = (
                            src[r, :, hh * DSWA_HEAD_DIM:(hh + 1) * DSWA_HEAD_DIM])
                    halves.append(scr[slot])
                vals.append(jnp.concatenate(halves, axis=1))
    lses = vals[1::2]
    mx = functools.reduce(jnp.maximum, lses)
    es = [jnp.exp(l - mx) for l in lses]
    den = functools.reduce(lambda a, b: a + b, es)
    y = functools.reduce(lambda a, b: a + b, [(e / den) * o for e, o in zip(es, vals[0::2])])
    y_ref[...] = y.astype(y_ref.dtype)


def _dswa_combine(outs_lses, *, batch, seq, tm):
    gw = DSWA_HEADS_PER_GROUP * DSWA_HEAD_DIM
    nblk = seq // tm
    in_specs = []
    for _, dil in DSWA_GROUPS:
        in_specs += [pl.BlockSpec((None, dil, tm // dil, gw), lambda b, i: (b, 0, i, 0))] * 2
    return pl.pallas_call(
        functools.partial(_dswa_combine_kernel, tm=tm),
        grid=(batch, nblk),
        in_specs=in_specs,
        out_specs=pl.BlockSpec((tm, gw), lambda b, i: (b * nblk + i, 0)),
        out_shape=jax.ShapeDtypeStruct((batch * seq, gw), BF16),
        scratch_shapes=[pltpu.VMEM((2 * len(DSWA_GROUPS) * DSWA_HEADS_PER_GROUP, tm, DSWA_HEAD_DIM), F32)],
        compiler_params=_cparams(("parallel", "parallel"), 32),
        name="dswa_combine",
    )(*outs_lses)


def _merge_kernel(x_ref, g_ref, ya_ref, yb_ref, yc_ref, wga_ref, wgb_ref, wgc_ref, wa_ref, wb_ref, wc_ref, wo_ref,
                  out_ref, h_scr):
    @pl.when(pl.program_id(1) == 0)
    def _():
        xf = x_ref[...]
        h_scr[...] = (xf * _inv_rms(xf, xf.shape[-1]) * g_ref[...]).astype(BF16)
        out_ref[...] = xf

    h = h_scr[...]
    dot = lambda a, b: jnp.dot(a, b, preferred_element_type=F32)
    inter = (jax.nn.sigmoid(dot(h, wga_ref[...])) * dot(ya_ref[...], wa_ref[...])
             + jax.nn.sigmoid(dot(h, wgb_ref[...])) * dot(yb_ref[...], wb_ref[...])
             + jax.nn.sigmoid(dot(h, wgc_ref[...])) * dot(yc_ref[...], wc_ref[...]))
    out_ref[...] += dot(inter.astype(BF16), wo_ref[...])


def _merge(x, g, ya, yb, yc, wga, wgb, wgc, wa, wb, wc, wo, *, tm, tn):
    t, d = x.shape
    row = lambda i, j: (i, 0)
    colw = lambda i, j: (0, j)
    return pl.pallas_call(
        _merge_kernel,
        grid=(t // tm, d // tn),
        in_specs=[pl.BlockSpec((tm, d), row), pl.BlockSpec((1, d), lambda i, j: (0, 0)),
                  pl.BlockSpec((tm, ya.shape[1]), row), pl.BlockSpec((tm, yb.shape[1]), row),
                  pl.BlockSpec((tm, yc.shape[1]), row)]
                 + [pl.BlockSpec((d, tn), colw)] * 3
                 + [pl.BlockSpec((wa.shape[0], tn), colw), pl.BlockSpec((wb.shape[0], tn), colw),
                    pl.BlockSpec((wc.shape[0], tn), colw), pl.BlockSpec((tn, d), lambda i, j: (j, 0))],
        out_specs=pl.BlockSpec((tm, d), row),
        out_shape=jax.ShapeDtypeStruct((t, d), F32),
        scratch_shapes=[pltpu.VMEM((tm, d), BF16)],
        compiler_params=_cparams(("parallel", "arbitrary"), 52),
        name="merge",
    )(x, g, ya, yb, yc, wga, wgb, wgc, wa, wb, wc, wo)


def _pad_cols(w, n):
    return jnp.pad(w, ((0, 0), (0, n - w.shape[1])))


def _rot_half_cols(w):
    half = w.shape[-1] // 2
    return jnp.concatenate([-w[..., half:], w[..., :half]], axis=-1)


def _swap_half(g):
    half = g.shape[-1] // 2
    return jnp.concatenate([g[..., half:], g[..., :half]], axis=-1)


def _gain3(g):
    pad = lambda v: jnp.pad(v, (0, LANE - v.shape[0]))
    return jnp.stack([g[:MLA_NOPE], pad(g[MLA_NOPE:]), pad(_swap_half(g[MLA_NOPE:]))], axis=0)


def kernel(x, positions, ffn1_norm, ffn1_w_gate, ffn1_w_up, ffn1_w_down, mix_norm, w_in, mla_cq_norm, mla_ckv_norm, mla_w_uq, mla_w_ukv, mla_q_norm, mla_k_norm, gla_w_gate2, gla_b_gate2, gla_o_norm, dswa_q_norm, dswa_k_norm, w_branch_a, w_branch_b, w_branch_c, w_out, ffn2_norm, ffn2_w_gate, ffn2_w_up, ffn2_w_down):
    batch, seq, d = x.shape
    depth = w_in.shape[0]
    d_ff = ffn1_w_gate.shape[-1]
    t = batch * seq
    tm = 512
    tf = 512
    ff_pad = -(-d_ff // tf) * tf
    assert seq % max(wd for wd, _ in DSWA_GROUPS) == 0 and seq % tm == 0

    xf = x.reshape(t, d)
    half = MLA_ROPE // 2
    inv_freq = ROPE_THETA ** (-jnp.arange(half, dtype=F32) / half)
    inv_pad = jnp.pad(jnp.concatenate([inv_freq, inv_freq]), (0, LANE - MLA_ROPE)).reshape(1, LANE)
    cos, sin = _rope_tables(positions.reshape(t, 1), inv_pad, tm=tm)
    slopes = 2.0 ** (-ALIBI_MAX_EXP * jnp.arange(1, DSWA_HEADS + 1, dtype=F32) / DSWA_HEADS)

    o_cq, o_ckv, o_kr = 0, MLA_Q_RANK, MLA_Q_RANK + MLA_KV_RANK
    o_gla = o_kr + MLA_ROPE
    n_qkv_b = GLA_HEADS * (2 * GLA_DK + GLA_DV)
    o_glr = o_gla + n_qkv_b
    o_rb = o_glr + GLA_GATE_RANK
    o_c = o_rb + GLA_HEADS * GLA_DV
    o_ga = o_c + 3 * DSWA_HEADS * DSWA_HEAD_DIM
    o_gb, o_gc = o_ga + d, o_ga + 2 * d

    def ffn_weights(wg, wu, wdn):
        padc = lambda w: jnp.pad(w, ((0, 0), (0, ff_pad - d_ff))).astype(BF16)
        return padc(wg), padc(wu), jnp.pad(wdn, ((0, ff_pad - d_ff), (0, 0))).astype(BF16)

    for l in range(depth):
        wg, wu, wdn = ffn_weights(ffn1_w_gate[l], ffn1_w_up[l], ffn1_w_down[l])
        xf = _ffn(xf, ffn1_norm[l].reshape(1, d), wg, wu, wdn, tm=tm, tf=tf)

        wl = w_in[l]
        gmix = mix_norm[l].reshape(1, d)
        w_kr = wl[:, o_kr:o_kr + MLA_ROPE]
        w1 = jnp.concatenate([wl[:, o_cq:o_kr], _pad_cols(w_kr, LANE), _pad_cols(_rot_half_cols(w_kr), LANE)],
                             axis=1).astype(BF16)
        wq3 = mla_w_uq[l].reshape(MLA_Q_RANK, MLA_HEADS, MLA_QK)
        wq_rope = wq3[..., MLA_NOPE:]
        zpad = jnp.zeros((MLA_Q_RANK, MLA_HEADS, LANE - MLA_ROPE), F32)
        wuq = jnp.concatenate([wq3[..., :MLA_NOPE], wq_rope, zpad, _rot_half_cols(wq_rope), zpad], axis=-1)
        wuq = wuq.reshape(MLA_Q_RANK, MLA_HEADS * 3 * LANE).astype(BF16)
        wkv3 = mla_w_ukv[l].reshape(MLA_KV_RANK, MLA_HEADS, MLA_NOPE + MLA_V)
        wukv = jnp.concatenate([wkv3[..., :MLA_NOPE].reshape(MLA_KV_RANK, -1),
                                wkv3[..., MLA_NOPE:].reshape(MLA_KV_RANK, -1)], axis=1).astype(BF16)
        qa, ka, va = _mla_prep(xf, gmix, cos, sin, w1, mla_cq_norm[l].reshape(1, -1), mla_ckv_norm[l].reshape(1, -1),
                               wuq, wukv, _gain3(mla_q_norm[l]), _gain3(mla_k_norm[l]), tm=tm)
        y_a = _flash(qa, ka, va, batch=batch, seq=seq, tq=1024, tk=512)

        w_gla = jnp.concatenate([wl[:, o_gla:o_glr], _pad_cols(wl[:, o_glr:o_rb], LANE), wl[:, o_rb:o_c]],
                                axis=1).astype(BF16)
        w2 = jnp.pad(gla_w_gate2[l], ((0, LANE - GLA_GATE_RANK), (0, 0))).astype(BF16)
        qb, kb, vb, la, sr = _gla_proj(xf, gmix, w_gla, w2, gla_b_gate2[l].reshape(1, -1), tm=tm)
        y_b = _gla(qb, kb, vb, la, sr, gla_o_norm[l].reshape(1, -1), batch=batch, seq=seq, tb=512)

        qkv_c = _dswa_proj(xf, gmix, wl[:, o_c:o_ga].astype(BF16), dswa_q_norm[l].reshape(1, -1),
                           dswa_k_norm[l].reshape(1, -1), batch=batch, seq=seq, tm=tm)
        outs_lses = []
        for gi in range(len(DSWA_GROUPS)):
            outs_lses += _dswa_group(slopes, *qkv_c[3 * gi:3 * gi + 3], group=gi)
        y_c = _dswa_combine(outs_lses, batch=batch, seq=seq, tm=tm)

        xf = _merge(xf, gmix, y_a, y_b, y_c,
                    wl[:, o_ga:o_gb].astype(BF16), wl[:, o_gb:o_gc].astype(BF16), wl[:, o_gc:o_gc + d].astype(BF16),
                    w_branch_a[l].astype(BF16), w_branch_b[l].astype(BF16), w_branch_c[l].astype(BF16),
                    w_out[l].astype(BF16), tm=tm, tn=512)

        wg, wu, wdn = ffn_weights(ffn2_w_gate[l], ffn2_w_up[l], ffn2_w_down[l])
        xf = _ffn(xf, ffn2_norm[l].reshape(1, d), wg, wu, wdn, tm=tm, tf=tf)
    return xf.reshape(batch, seq, d)
```

```python
import functools

import numpy as np
import jax
import jax.numpy as jnp
from jax import lax
from jax.experimental import pallas as pl
from jax.experimental.pallas import tpu as pltpu

F32 = jnp.float32
BF16 = jnp.bfloat16

MLA_HEADS = 6
MLA_Q_RANK = 512
MLA_KV_RANK = 256
MLA_NOPE = 128
MLA_ROPE = 64
MLA_V = 128
MLA_QK = MLA_NOPE + MLA_ROPE
MLA_QK_PAD = 256
GLA_HEADS = 4
GLA_DK = 64
GLA_DV = 128
GLA_GATE_RANK = 16
GLA_TAU = 16.0
GLA_CHUNK = 64
GLA_SUB = 8
GLA_LEVELS = (32, 16, 8)
DSWA_GROUPS = ((128, 1), (512, 4), (2048, 16))
DSWA_HEADS_PER_GROUP = 2
DSWA_HEADS = 6
DSWA_HEAD_DIM = 128
DSWA_W = 128
ROPE_THETA = 10000.0
ALIBI_MAX_EXP = 8.0
NORM_EPS = 1e-6
NEG_INF = -1e30
LOG2E = 1.4426950408889634

LANE = 128
MIB = 1024 * 1024


def _cparams(sem, vmem_mib):
    return pltpu.CompilerParams(dimension_semantics=sem, vmem_limit_bytes=int(vmem_mib * MIB))


def _inv_rms(xf, n):
    return lax.rsqrt(jnp.sum(xf * xf, axis=-1, keepdims=True) / n + NORM_EPS)


def _const_spec(shape):
    nd = len(shape)
    return pl.BlockSpec(shape, lambda *_: (0,) * nd)


def _ffn_kernel(x_ref, g_ref, wg_ref, wu_ref, wd_ref, o_ref, h_scr):
    @pl.when(pl.program_id(1) == 0)
    def _():
        xf = x_ref[...]
        h_scr[...] = (xf * _inv_rms(xf, xf.shape[-1]) * g_ref[...]).astype(BF16)
        o_ref[...] = xf

    h = h_scr[...]
    gate = jnp.dot(h, wg_ref[...], preferred_element_type=F32)
    up = jnp.dot(h, wu_ref[...], preferred_element_type=F32)
    inter = (gate * jax.nn.sigmoid(gate) * up).astype(BF16)
    o_ref[...] += 0.5 * jnp.dot(inter, wd_ref[...], preferred_element_type=F32)


def _ffn(x, g, wg, wu, wd, *, tm, tf):
    t, d = x.shape
    fp = wg.shape[1]
    return pl.pallas_call(
        _ffn_kernel,
        grid=(t // tm, fp // tf),
        in_specs=[
            pl.BlockSpec((tm, d), lambda i, j: (i, 0)),
            pl.BlockSpec((1, d), lambda i, j: (0, 0)),
            pl.BlockSpec((d, tf), lambda i, j: (0, j)),
            pl.BlockSpec((d, tf), lambda i, j: (0, j)),
            pl.BlockSpec((tf, d), lambda i, j: (j, 0)),
        ],
        out_specs=pl.BlockSpec((tm, d), lambda i, j: (i, 0)),
        out_shape=jax.ShapeDtypeStruct((t, d), F32),
        scratch_shapes=[pltpu.VMEM((tm, d), BF16)],
        compiler_params=_cparams(("parallel", "arbitrary"), 52),
        name="ffn",
    )(x, g, wg, wu, wd)


def _rope_kernel(pos_ref, inv_ref, cos_ref, sin_ref):
    ang = pos_ref[...].astype(F32) * inv_ref[...]
    cos_ref[...] = jnp.cos(ang)
    sin_ref[...] = jnp.sin(ang)


def _rope_tables(pos_col, inv_pad, *, tm):
    t = pos_col.shape[0]
    return pl.pallas_call(
        _rope_kernel,
        grid=(t // tm,),
        in_specs=[pl.BlockSpec((tm, 1), lambda i: (i, 0)), _const_spec((1, LANE))],
        out_specs=[pl.BlockSpec((tm, LANE), lambda i: (i, 0))] * 2,
        out_shape=[jax.ShapeDtypeStruct((t, LANE), F32)] * 2,
        compiler_params=_cparams(("parallel",), 32),
        name="rope_tables",
    )(pos_col, inv_pad)


def _mla_prep_kernel(x_ref, g_ref, cos_ref, sin_ref, w1_ref, gcq_ref, gckv_ref, wuq_ref, wukv_ref,
                     gq_ref, gk_ref, q_ref, k_ref, v_ref):
    xf = x_ref[...]
    h = (xf * _inv_rms(xf, xf.shape[-1]) * g_ref[...]).astype(BF16)
    u = jnp.dot(h, w1_ref[...], preferred_element_type=F32)
    c_q = u[:, :MLA_Q_RANK]
    c_kv = u[:, MLA_Q_RANK:MLA_Q_RANK + MLA_KV_RANK]
    kr = u[:, MLA_Q_RANK + MLA_KV_RANK:MLA_Q_RANK + MLA_KV_RANK + LANE]
    krot = u[:, MLA_Q_RANK + MLA_KV_RANK + LANE:]
    c_q = (c_q * _inv_rms(c_q, MLA_Q_RANK) * gcq_ref[...]).astype(BF16)
    c_kv = (c_kv * _inv_rms(c_kv, MLA_KV_RANK) * gckv_ref[...]).astype(BF16)
    qall = jnp.dot(c_q, wuq_ref[...], preferred_element_type=F32)
    kv = jnp.dot(c_kv, wukv_ref[...], preferred_element_type=F32)
    cos = cos_ref[...]
    sin = sin_ref[...]
    scale = MLA_QK ** -0.5 * LOG2E
    gq_n, gq_r, gq_t = gq_ref[0:1, :], gq_ref[1:2, :], gq_ref[2:3, :]
    gk_n, gk_r, gk_t = gk_ref[0:1, :], gk_ref[1:2, :], gk_ref[2:3, :]
    k_roped = kr * gk_r * cos + krot * gk_t * sin
    kr_ss = jnp.sum(kr * kr, axis=-1, keepdims=True)
    for hh in range(MLA_HEADS):
        base = hh * 3 * LANE
        nope = qall[:, base:base + LANE]
        rope = qall[:, base + LANE:base + 2 * LANE]
        rot = qall[:, base + 2 * LANE:base + 3 * LANE]
        ss = jnp.sum(nope * nope, axis=-1, keepdims=True) + jnp.sum(rope * rope, axis=-1, keepdims=True)
        r = lax.rsqrt(ss / MLA_QK + NORM_EPS) * scale
        q_ref[hh, :, 0:LANE] = (nope * r * gq_n).astype(BF16)
        q_ref[hh, :, LANE:2 * LANE] = ((rope * gq_r * cos + rot * gq_t * sin) * r).astype(BF16)
        kn = kv[:, hh * LANE:(hh + 1) * LANE]
        rk = lax.rsqrt((jnp.sum(kn * kn, axis=-1, keepdims=True) + kr_ss) / MLA_QK + NORM_EPS)
        k_ref[hh, :, 0:LANE] = (kn * rk * gk_n).astype(BF16)
        k_ref[hh, :, LANE:2 * LANE] = (k_roped * rk).astype(BF16)
        v_ref[hh, :, 0:MLA_V] = kv[:, (MLA_HEADS + hh) * LANE:(MLA_HEADS + hh + 1) * LANE].astype(BF16)
        v_ref[hh, :, MLA_V:] = jnp.ones((kv.shape[0], MLA_V), BF16)


def _mla_prep(x, g, cos, sin, w1, gcq, gckv, wuq, wukv, gq3, gk3, *, tm):
    t, d = x.shape
    row = lambda i: (i, 0)
    hrow = lambda i: (0, i, 0)
    return pl.pallas_call(
        _mla_prep_kernel,
        grid=(t // tm,),
        in_specs=[
            pl.BlockSpec((tm, d), row), _const_spec((1, d)),
            pl.BlockSpec((tm, LANE), row), pl.BlockSpec((tm, LANE), row),
            _const_spec(w1.shape), _const_spec(gcq.shape), _const_spec(gckv.shape),
            _const_spec(wuq.shape), _const_spec(wukv.shape), _const_spec(gq3.shape), _const_spec(gk3.shape),
        ],
        out_specs=[
            pl.BlockSpec((MLA_HEADS, tm, MLA_QK_PAD), hrow),
            pl.BlockSpec((MLA_HEADS, tm, MLA_QK_PAD), hrow),
            pl.BlockSpec((MLA_HEADS, tm, 2 * MLA_V), hrow),
        ],
        out_shape=[
            jax.ShapeDtypeStruct((MLA_HEADS, t, MLA_QK_PAD), BF16),
            jax.ShapeDtypeStruct((MLA_HEADS, t, MLA_QK_PAD), BF16),
            jax.ShapeDtypeStruct((MLA_HEADS, t, 2 * MLA_V), BF16),
        ],
        compiler_params=_cparams(("parallel",), 52),
        name="mla_prep",
    )(x, g, cos, sin, w1, gcq, gckv, wuq, wukv, gq3, gk3)


def _flash_kernel(q_ref, k_ref, v_ref, o_ref, sa_scr, sb_scr, m_scr, acc_scr, *, tq):
    qi = pl.program_id(2)
    m_scr[...] = jnp.full_like(m_scr, NEG_INF)
    acc_scr[...] = jnp.zeros_like(acc_scr)
    q = q_ref[...]

    def scores(tile, s_scr):
        r0 = pl.multiple_of(tile * tq, tq)
        s_scr[...] = lax.dot_general(q, k_ref[pl.ds(r0, tq), :], (((1,), (1,)), ((), ())),
                                     preferred_element_type=F32)

    def accumulate(tile, s_scr, diagonal):
        r0 = pl.multiple_of(tile * tq, tq)
        sc = s_scr[...]
        if diagonal:
            row = lax.broadcasted_iota(jnp.int32, sc.shape, 0)
            col = lax.broadcasted_iota(jnp.int32, sc.shape, 1)
            sc = jnp.where(col <= row, sc, NEG_INF)
        m_prev = m_scr[...]
        m_new = jnp.maximum(m_prev, jnp.max(sc, axis=-1, keepdims=True))
        alpha = jnp.exp2(m_prev - m_new)
        p = jnp.exp2(sc - jnp.tile(m_new, (1, tq // LANE)))
        pv = jnp.dot(p.astype(BF16), v_ref[pl.ds(r0, tq), :], preferred_element_type=F32)
        acc_scr[...] = jnp.tile(alpha, (1, 2)) * acc_scr[...] + pv
        m_scr[...] = m_new

    scores(0, sa_scr)

    def body(j, carry):
        scores(2 * j + 1, sb_scr)
        accumulate(2 * j, sa_scr, False)
        scores(2 * j + 2, sa_scr)
        accumulate(2 * j + 1, sb_scr, False)
        return carry

    lax.fori_loop(0, qi // 2, body, 0)

    @pl.when(qi % 2 == 0)
    def _():
        accumulate(qi, sa_scr, True)

    @pl.when(qi % 2 == 1)
    def _():
        scores(qi, sb_scr)
        accumulate(qi - 1, sa_scr, False)
        accumulate(qi, sb_scr, True)

    acc = acc_scr[...]
    o_ref[...] = (acc[:, 0:MLA_V] / acc[:, MLA_V:]).astype(o_ref.dtype)


def _flash(q, k, v, *, batch, seq, tq):
    assert seq % tq == 0
    nq = seq // tq
    t = batch * seq
    resident = dict(pipeline_mode=pl.Buffered(1))
    return pl.pallas_call(
        functools.partial(_flash_kernel, tq=tq),
        grid=(batch, MLA_HEADS, nq),
        in_specs=[
            pl.BlockSpec((None, tq, MLA_QK_PAD), lambda b, h, i: (h, b * nq + i, 0)),
            pl.BlockSpec((None, seq, MLA_QK_PAD), lambda b, h, i: (h, b, 0), **resident),
            pl.BlockSpec((None, seq, 2 * MLA_V), lambda b, h, i: (h, b, 0), **resident),
        ],
        out_specs=pl.BlockSpec((tq, MLA_V), lambda b, h, i: (b * nq + i, h)),
        out_shape=jax.ShapeDtypeStruct((t, MLA_HEADS * MLA_V), BF16),
        scratch_shapes=[pltpu.VMEM((tq, tq), F32), pltpu.VMEM((tq, tq), F32),
                        pltpu.VMEM((tq, MLA_V), F32), pltpu.VMEM((tq, 2 * MLA_V), F32)],
        compiler_params=_cparams(("parallel", "parallel", "arbitrary"), 52),
        name="mla_flash",
    )(q, k, v)


def _gla_proj_kernel(x_ref, g_ref, w_ref, w2_ref, b2_ref, q_ref, k_ref, v_ref, la_ref, sr_ref):
    xf = x_ref[...]
    h = (xf * _inv_rms(xf, xf.shape[-1]) * g_ref[...]).astype(BF16)
    u = jnp.dot(h, w_ref[...], preferred_element_type=F32)
    nk = GLA_HEADS * GLA_DK
    nv = GLA_HEADS * GLA_DV
    q_ref[...] = u[:, 0:nk] * (GLA_DK ** -0.5)
    k_ref[...] = u[:, nk:2 * nk]
    v_ref[...] = u[:, 2 * nk:2 * nk + nv]
    gate_lr = u[:, 2 * nk + nv:2 * nk + nv + LANE].astype(BF16)
    z = jnp.dot(gate_lr, w2_ref[...], preferred_element_type=F32) + b2_ref[...]
    la_ref[...] = -(jnp.maximum(-z, 0.0) + jnp.log1p(jnp.exp(-jnp.abs(z)))) / GLA_TAU
    r = u[:, 2 * nk + nv + LANE:]
    sr_ref[...] = r * jax.nn.sigmoid(r)


def _gla_proj(x, g, w, w2, b2, *, tm):
    t, d = x.shape
    nk = GLA_HEADS * GLA_DK
    nv = GLA_HEADS * GLA_DV
    row = lambda i: (i, 0)
    return pl.pallas_call(
        _gla_proj_kernel,
        grid=(t // tm,),
        in_specs=[pl.BlockSpec((tm, d), row), _const_spec((1, d)), _const_spec(w.shape),
                  _const_spec(w2.shape), _const_spec(b2.shape)],
        out_specs=[pl.BlockSpec((tm, nk), row), pl.BlockSpec((tm, nk), row), pl.BlockSpec((tm, nv), row),
                   pl.BlockSpec((tm, nk), row), pl.BlockSpec((tm, nv), row)],
        out_shape=[jax.ShapeDtypeStruct((t, nk), F32), jax.ShapeDtypeStruct((t, nk), F32),
                   jax.ShapeDtypeStruct((t, nv), F32), jax.ShapeDtypeStruct((t, nk), F32),
                   jax.ShapeDtypeStruct((t, nv), F32)],
        compiler_params=_cparams(("parallel",), 52),
        name="gla_proj",
    )(x, g, w, w2, b2)


def _gla_constants():
    c = GLA_CHUNK
    idx = np.arange(c)
    mats = [(idx[None, :] <= idx[:, None])]
    for lvl in GLA_LEVELS:
        p = ((idx // lvl) | 1) * lvl
        qrow = (idx >= p)[:, None] & (idx[None, :] >= p[:, None]) & (idx[None, :] <= idx[:, None])
        krow = (idx < p)[:, None] & (idx[None, :] > idx[:, None]) & (idx[None, :] < p[:, None])
        mats.append(qrow | krow)
    mats.append(idx[None, :] > idx[:, None])
    mstack = np.concatenate(mats, axis=0).astype(np.float32)
    lmask = []
    for lvl in GLA_LEVELS:
        blk = idx // lvl
        m = ((blk[:, None] % 2) == 1) & (blk[None, :] == blk[:, None] - 1)
        lmask.append(np.tile(m.astype(np.float32), (GLA_HEADS, 1)))
    lmask = np.stack(lmask, axis=0)
    hk = np.arange(GLA_HEADS * GLA_DK) // GLA_DK
    hv = np.arange(GLA_HEADS * GLA_DV) // GLA_DV
    hrow = np.arange(GLA_HEADS * c) // c
    headmask = (hrow[:, None] == hk[None, :]).astype(np.float32)
    e_ind = (hk[:, None] == hv[None, :]).astype(np.float32)
    return mstack, lmask, headmask, e_ind


def _gla_kernel(q_ref, k_ref, v_ref, la_ref, sr_ref, go_ref, mst_ref, lmask_ref, hmask_ref, eind_ref, eindt_ref,
                y_ref, st_scr, *, chunks):
    c = GLA_CHUNK
    nk = GLA_HEADS * GLA_DK
    nv = GLA_HEADS * GLA_DV
    nsub = c // GLA_SUB

    @pl.when(pl.program_id(1) == 0)
    def _():
        st_scr[...] = jnp.zeros_like(st_scr)

    def chunk(ci, carry):
        r0 = pl.multiple_of(ci * c, c)
        q = q_ref[pl.ds(r0, c), :]
        k = k_ref[pl.ds(r0, c), :]
        v = v_ref[pl.ds(r0, c), :]
        a = la_ref[pl.ds(r0, c), :]
        a1 = a.astype(BF16)
        r1 = a - a1.astype(F32)
        a2 = r1.astype(BF16)
        a3 = (r1 - a2.astype(F32)).astype(BF16)
        mst = mst_ref[...]
        ex = (jnp.dot(mst, a1, preferred_element_type=F32) + jnp.dot(mst, a2, preferred_element_type=F32)
              + jnp.dot(mst, a3, preferred_element_type=F32))
        b = ex[0:c]
        f = jnp.exp(ex)
        v_bf = v.astype(BF16)

        amat = jnp.zeros((GLA_HEADS * c, c), F32)
        hmask = hmask_ref[...]
        for li in range(len(GLA_LEVELS)):
            fl = f[(li + 1) * c:(li + 2) * c]
            qt = q * fl
            kt = (k * fl).astype(BF16)
            qs = (jnp.concatenate([qt] * GLA_HEADS, axis=0) * hmask).astype(BF16)
            sc = lax.dot_general(qs, kt, (((1,), (1,)), ((), ())), preferred_element_type=F32)
            amat = amat + sc * lmask_ref[li]
        a_bf = amat.astype(BF16)
        o = jnp.concatenate(
            [jnp.dot(a_bf[hh * c:(hh + 1) * c], v_bf[:, hh * GLA_DV:(hh + 1) * GLA_DV], preferred_element_type=F32)
             for hh in range(GLA_HEADS)], axis=1)

        qb = (q * f[0:c]).astype(BF16)
        st = st_scr[...]
        o = o + lax.dot_general(qb, st.astype(BF16), (((1,), (1,)), ((), ())), preferred_element_type=F32)

        q3 = q.reshape(nsub, GLA_SUB, nk)
        k3 = k.reshape(nsub, GLA_SUB, nk)
        b3 = b.reshape(nsub, GLA_SUB, nk)
        v3 = v.reshape(nsub, GLA_SUB, nv)
        tt = lax.broadcasted_iota(jnp.int32, (nsub, GLA_SUB, nk), 1)
        od = jnp.zeros((nsub, GLA_SUB, nv), F32)
        eind = eind_ref[...]
        for s in range(GLA_SUB):
            w = jnp.exp(jnp.minimum(b3 - b3[:, s:s + 1, :], 0.0))
            x = jnp.where(tt >= s, q3 * k3[:, s:s + 1, :] * w, 0.0)
            rr = jnp.dot(x.reshape(c, nk).astype(BF16), eind, preferred_element_type=F32)
            od = od + rr.reshape(nsub, GLA_SUB, nv) * v3[:, s:s + 1, :]
        o = o + od.reshape(c, nv)

        kend = (k * f[(len(GLA_LEVELS) + 1) * c:(len(GLA_LEVELS) + 2) * c]).astype(BF16)
        upd = lax.dot_general(v_bf, kend, (((0,), (0,)), ((), ())), preferred_element_type=F32)
        st_scr[...] = st * f[c - 1:c] + upd * eindt_ref[...]

        go = go_ref[...]
        sr = sr_ref[pl.ds(r0, c), :]
        ys = []
        for hh in range(GLA_HEADS):
            oh = o[:, hh * GLA_DV:(hh + 1) * GLA_DV]
            ys.append(oh * _inv_rms(oh, GLA_DV) * go)
        y_ref[pl.ds(r0, c), :] = (jnp.concatenate(ys, axis=1) * sr).astype(y_ref.dtype)
        return carry

    lax.fori_loop(0, chunks, chunk, 0, unroll=2)


def _gla(q, k, v, la, sr, go, *, batch, seq, tb):
    nk = GLA_HEADS * GLA_DK
    nv = GLA_HEADS * GLA_DV
    nb = seq // tb
    mstack, lmask, headmask, e_ind = _gla_constants()
    consts = [jnp.asarray(mstack, BF16), jnp.asarray(lmask, F32), jnp.asarray(headmask, F32),
              jnp.asarray(e_ind, BF16), jnp.asarray(e_ind.T, F32)]
    row = lambda b, i: (b * nb + i, 0)
    return pl.pallas_call(
        functools.partial(_gla_kernel, chunks=tb // GLA_CHUNK),
        grid=(batch, nb),
        in_specs=[pl.BlockSpec((tb, nk), row), pl.BlockSpec((tb, nk), row), pl.BlockSpec((tb, nv), row),
                  pl.BlockSpec((tb, nk), row), pl.BlockSpec((tb, nv), row), _const_spec(go.shape)]
                 + [_const_spec(cst.shape) for cst in consts],
        out_specs=pl.BlockSpec((tb, nv), row),
        out_shape=jax.ShapeDtypeStruct((batch * seq, nv), BF16),
        scratch_shapes=[pltpu.VMEM((nv, nk), F32)],
        compiler_params=_cparams(("parallel", "arbitrary"), 40),
        name="gla_scan",
    )(q, k, v, la, sr, go, *consts)


def _dswa_proj_kernel(x_ref, g_ref, w_ref, gq_ref, gk_ref, *refs, tm):
    ng = len(DSWA_GROUPS)
    out_refs = refs[:3 * ng]
    q_scr, k_scr, v_scr = refs[3 * ng:]
    xf = x_ref[...]
    h = (xf * _inv_rms(xf, xf.shape[-1]) * g_ref[...]).astype(BF16)
    u = jnp.dot(h, w_ref[...], preferred_element_type=F32)
    n = DSWA_HEADS * DSWA_HEAD_DIM
    gq = gq_ref[...]
    gk = gk_ref[...]
    scale = DSWA_HEAD_DIM ** -0.5
    for hh in range(DSWA_HEADS):
        sl = slice(hh * DSWA_HEAD_DIM, (hh + 1) * DSWA_HEAD_DIM)
        qh = u[:, sl]
        q_scr[hh] = qh * _inv_rms(qh, DSWA_HEAD_DIM) * gq * scale
        kh = u[:, n + hh * DSWA_HEAD_DIM:n + (hh + 1) * DSWA_HEAD_DIM]
        k_scr[hh] = kh * _inv_rms(kh, DSWA_HEAD_DIM) * gk
        v_scr[hh] = u[:, 2 * n + hh * DSWA_HEAD_DIM:2 * n + (hh + 1) * DSWA_HEAD_DIM]
    for gi, (_, dil) in enumerate(DSWA_GROUPS):
        for src, dst in zip((q_scr, k_scr, v_scr), out_refs[3 * gi:3 * gi + 3]):
            for hh in range(DSWA_HEADS_PER_GROUP):
                head = gi * DSWA_HEADS_PER_GROUP + hh
                cols = slice(hh * DSWA_HEAD_DIM, (hh + 1) * DSWA_HEAD_DIM)
                for r in range(dil):
                    rows = pl.ds(r, tm // dil, stride=dil) if dil > 1 else slice(None)
                    dst[r, :, cols] = src[head, rows, :].astype(BF16)


def _dswa_proj(x, g, w, gq, gk, *, batch, seq, tm):
    t, d = x.shape
    n = DSWA_HEADS * DSWA_HEAD_DIM
    gw = DSWA_HEADS_PER_GROUP * DSWA_HEAD_DIM
    nblk = seq // tm
    out_specs, out_shape = [], []
    for _, dil in DSWA_GROUPS:
        assert tm % (16 * dil) == 0
        out_specs += [pl.BlockSpec((None, dil, tm // dil, gw), lambda b, i: (b, 0, i, 0))] * 3
        out_shape += [jax.ShapeDtypeStruct((batch, dil, seq // dil, gw), BF16)] * 3
    return pl.pallas_call(
        functools.partial(_dswa_proj_kernel, tm=tm),
        grid=(batch, nblk),
        in_specs=[pl.BlockSpec((tm, d), lambda b, i: (b * nblk + i, 0)), _const_spec((1, d)), _const_spec(w.shape),
                  _const_spec(gq.shape), _const_spec(gk.shape)],
        out_specs=out_specs,
        out_shape=out_shape,
        scratch_shapes=[pltpu.VMEM((DSWA_HEADS, tm, DSWA_HEAD_DIM), F32)] * 3,
        compiler_params=_cparams(("parallel", "parallel"), 52),
        name="dswa_proj",
    )(x, g, w, gq, gk)


def _dswa_kernel(slope_ref, q_ref, kc_ref, kp_ref, vc_ref, vp_ref, o_ref, l_ref, *, dilation, group, nb):
    w = DSWA_W
    ib = pl.program_id(2)
    i = lax.broadcasted_iota(jnp.int32, (w, 2 * w), 0)
    j = lax.broadcasted_iota(jnp.int32, (w, 2 * w), 1)
    steps = w + i - j
    dist = (steps * dilation).astype(F32)
    in_window = (steps >= 0) & (steps <= w)
    for hh in range(DSWA_HEADS_PER_GROUP):
        cols = slice(hh * DSWA_HEAD_DIM, (hh + 1) * DSWA_HEAD_DIM)
        bias_all = jnp.where(in_window, -slope_ref[group * DSWA_HEADS_PER_GROUP + hh] * dist, NEG_INF)
        bias_first = jnp.where(j >= w, bias_all, NEG_INF)
        for jb in range(nb):
            rows = slice(jb * w, (jb + 1) * w)
            if jb == 0:
                kprev, vprev = kp_ref[:, cols], vp_ref[:, cols]
                bias_j = jnp.where(ib > 0, bias_all, bias_first)
            else:
                prows = slice((jb - 1) * w, jb * w)
                kprev, vprev = kc_ref[prows, cols], vc_ref[prows, cols]
                bias_j = bias_all
            keys = jnp.concatenate([kprev, kc_ref[rows, cols]], axis=0)
            vals = jnp.concatenate([vprev, vc_ref[rows, cols]], axis=0)
            sc = lax.dot_general(q_ref[rows, cols], keys, (((1,), (1,)), ((), ())), preferred_element_type=F32)
            sc = jnp.where(bias_j > 0.5 * NEG_INF, sc + bias_j, NEG_INF)
            m = jnp.max(sc, axis=-1, keepdims=True)
            p = jnp.exp(sc - m)
            lsum = jnp.sum(p, axis=-1, keepdims=True)
            o_ref[rows, cols] = jnp.dot(p.astype(BF16), vals, preferred_element_type=F32) / lsum
            l_ref[rows, cols] = jnp.broadcast_to(m + jnp.log(lsum), (w, DSWA_HEAD_DIM))


def _dswa_group(slopes, qg, kg, vg, *, group):
    window, dilation = DSWA_GROUPS[group]
    assert window // dilation == DSWA_W
    batch, _, sd, gw = qg.shape
    tq = min(512, sd)
    nb = tq // DSWA_W
    cur = pl.BlockSpec((None, None, tq, gw), lambda b, r, i: (b, r, i, 0))
    prev = pl.BlockSpec((None, None, DSWA_W, gw), lambda b, r, i: (b, r, jnp.maximum(i * nb - 1, 0), 0))
    return pl.pallas_call(
        functools.partial(_dswa_kernel, dilation=dilation, group=group, nb=nb),
        grid=(batch, dilation, sd // tq),
        in_specs=[pl.BlockSpec(memory_space=pltpu.SMEM), cur, cur, prev, cur, prev],
        out_specs=[cur] * 2,
        out_shape=[jax.ShapeDtypeStruct(qg.shape, F32)] * 2,
        compiler_params=_cparams(("parallel", "parallel", "arbitrary"), 32),
        name="dswa_g%d" % group,
    )(slopes, qg, kg, kg, vg, vg)


def _dswa_combine_kernel(*refs, tm):
    ng = len(DSWA_GROUPS)
    in_refs, y_ref, scr = refs[:2 * ng], refs[2 * ng], refs[2 * ng + 1]
    vals = []
    for gi, (_, dil) in enumerate(DSWA_GROUPS):
        for which in range(2):
            src = in_refs[2 * gi + which]
            if dil == 1:
                vals.append(src[0])
            else:
                halves = []
                for hh in range(DSWA_HEADS_PER_GROUP):
                    slot = (2 * gi + which) * DSWA_HEADS_PER_GROUP + hh
                    for r in range(dil):
                        scr[slot, pl.ds(r, tm // dil, stride=dil), :] = (
                            src[r, :, hh * DSWA_HEAD_DIM:(hh + 1) * DSWA_HEAD_DIM])
                    halves.append(scr[slot])
                vals.append(jnp.concatenate(halves, axis=1))
    lses = vals[1::2]
    mx = functools.reduce(jnp.maximum, lses)
    es = [jnp.exp(l - mx) for l in lses]
    den = functools.reduce(lambda a, b: a + b, es)
    y = functools.reduce(lambda a, b: a + b, [(e / den) * o for e, o in zip(es, vals[0::2])])
    y_ref[...] = y.astype(y_ref.dtype)


def _dswa_combine(outs_lses, *, batch, seq, tm):
    gw = DSWA_HEADS_PER_GROUP * DSWA_HEAD_DIM
    nblk = seq // tm
    in_specs = []
    for _, dil in DSWA_GROUPS:
        in_specs += [pl.BlockSpec((None, dil, tm // dil, gw), lambda b, i: (b, 0, i, 0))] * 2
    return pl.pallas_call(
        functools.partial(_dswa_combine_kernel, tm=tm),
        grid=(batch, nblk),
        in_specs=in_specs,
        out_specs=pl.BlockSpec((tm, gw), lambda b, i: (b * nblk + i, 0)),
        out_shape=jax.ShapeDtypeStruct((batch * seq, gw), BF16),
        scratch_shapes=[pltpu.VMEM((2 * len(DSWA_GROUPS) * DSWA_HEADS_PER_GROUP, tm, DSWA_HEAD_DIM), F32)],
        compiler_params=_cparams(("parallel", "parallel"), 32),
        name="dswa_combine",
    )(*outs_lses)


def _merge_kernel(x_ref, g_ref, ya_ref, yb_ref, yc_ref, wga_ref, wgb_ref, wgc_ref, wa_ref, wb_ref, wc_ref, wo_ref,
                  out_ref, h_scr):
    @pl.when(pl.program_id(1) == 0)
    def _():
        xf = x_ref[...]
        h_scr[...] = (xf * _inv_rms(xf, xf.shape[-1]) * g_ref[...]).astype(BF16)
        out_ref[...] = xf

    h = h_scr[...]
    dot = lambda a, b: jnp.dot(a, b, preferred_element_type=F32)
    inter = (jax.nn.sigmoid(dot(h, wga_ref[...])) * dot(ya_ref[...], wa_ref[...])
             + jax.nn.sigmoid(dot(h, wgb_ref[...])) * dot(yb_ref[...], wb_ref[...])
             + jax.nn.sigmoid(dot(h, wgc_ref[...])) * dot(yc_ref[...], wc_ref[...]))
    out_ref[...] += dot(inter.astype(BF16), wo_ref[...])


def _merge(x, g, ya, yb, yc, wga, wgb, wgc, wa, wb, wc, wo, *, tm, tn):
    t, d = x.shape
    row = lambda i, j: (i, 0)
    colw = lambda i, j: (0, j)
    return pl.pallas_call(
        _merge_kernel,
        grid=(t // tm, d // tn),
        in_specs=[pl.BlockSpec((tm, d), row), pl.BlockSpec((1, d), lambda i, j: (0, 0)),
                  pl.BlockSpec((tm, ya.shape[1]), row), pl.BlockSpec((tm, yb.shape[1]), row),
                  pl.BlockSpec((tm, yc.shape[1]), row)]
                 + [pl.BlockSpec((d, tn), colw)] * 3
                 + [pl.BlockSpec((wa.shape[0], tn), colw), pl.BlockSpec((wb.shape[0], tn), colw),
                    pl.BlockSpec((wc.shape[0], tn), colw), pl.BlockSpec((tn, d), lambda i, j: (j, 0))],
        out_specs=pl.BlockSpec((tm, d), row),
        out_shape=jax.ShapeDtypeStruct((t, d), F32),
        scratch_shapes=[pltpu.VMEM((tm, d), BF16)],
        compiler_params=_cparams(("parallel", "arbitrary"), 52),
        name="merge",
    )(x, g, ya, yb, yc, wga, wgb, wgc, wa, wb, wc, wo)


def _pad_cols(w, n):
    return jnp.pad(w, ((0, 0), (0, n - w.shape[1])))


def _rot_half_cols(w):
    half = w.shape[-1] // 2
    return jnp.concatenate([-w[..., half:], w[..., :half]], axis=-1)


def _swap_half(g):
    half = g.shape[-1] // 2
    return jnp.concatenate([g[..., half:], g[..., :half]], axis=-1)


def _gain3(g):
    pad = lambda v: jnp.pad(v, (0, LANE - v.shape[0]))
    return jnp.stack([g[:MLA_NOPE], pad(g[MLA_NOPE:]), pad(_swap_half(g[MLA_NOPE:]))], axis=0)


def kernel(x, positions, ffn1_norm, ffn1_w_gate, ffn1_w_up, ffn1_w_down, mix_norm, w_in, mla_cq_norm, mla_ckv_norm, mla_w_uq, mla_w_ukv, mla_q_norm, mla_k_norm, gla_w_gate2, gla_b_gate2, gla_o_norm, dswa_q_norm, dswa_k_norm, w_branch_a, w_branch_b, w_branch_c, w_out, ffn2_norm, ffn2_w_gate, ffn2_w_up, ffn2_w_down):
    batch, seq, d = x.shape
    depth = w_in.shape[0]
    d_ff = ffn1_w_gate.shape[-1]
    t = batch * seq
    tm = 512
    tf = 512
    ff_pad = -(-d_ff // tf) * tf
    assert seq % max(wd for wd, _ in DSWA_GROUPS) == 0 and seq % tm == 0

    xf = x.reshape(t, d)
    half = MLA_ROPE // 2
    inv_freq = ROPE_THETA ** (-jnp.arange(half, dtype=F32) / half)
    inv_pad = jnp.pad(jnp.concatenate([inv_freq, inv_freq]), (0, LANE - MLA_ROPE)).reshape(1, LANE)
    cos, sin = _rope_tables(positions.reshape(t, 1), inv_pad, tm=tm)
    slopes = 2.0 ** (-ALIBI_MAX_EXP * jnp.arange(1, DSWA_HEADS + 1, dtype=F32) / DSWA_HEADS)

    o_cq, o_ckv, o_kr = 0, MLA_Q_RANK, MLA_Q_RANK + MLA_KV_RANK
    o_gla = o_kr + MLA_ROPE
    n_qkv_b = GLA_HEADS * (2 * GLA_DK + GLA_DV)
    o_glr = o_gla + n_qkv_b
    o_rb = o_glr + GLA_GATE_RANK
    o_c = o_rb + GLA_HEADS * GLA_DV
    o_ga = o_c + 3 * DSWA_HEADS * DSWA_HEAD_DIM
    o_gb, o_gc = o_ga + d, o_ga + 2 * d

    def ffn_weights(wg, wu, wdn):
        padc = lambda w: jnp.pad(w, ((0, 0), (0, ff_pad - d_ff))).astype(BF16)
        return padc(wg), padc(wu), jnp.pad(wdn, ((0, ff_pad - d_ff), (0, 0))).astype(BF16)

    for l in range(depth):
        wg, wu, wdn = ffn_weights(ffn1_w_gate[l], ffn1_w_up[l], ffn1_w_down[l])
        xf = _ffn(xf, ffn1_norm[l].reshape(1, d), wg, wu, wdn, tm=tm, tf=tf)

        wl = w_in[l]
        gmix = mix_norm[l].reshape(1, d)
        w_kr = wl[:, o_kr:o_kr + MLA_ROPE]
        w1 = jnp.concatenate([wl[:, o_cq:o_kr], _pad_cols(w_kr, LANE), _pad_cols(_rot_half_cols(w_kr), LANE)],
                             axis=1).astype(BF16)
        wq3 = mla_w_uq[l].reshape(MLA_Q_RANK, MLA_HEADS, MLA_QK)
        wq_rope = wq3[..., MLA_NOPE:]
        zpad = jnp.zeros((MLA_Q_RANK, MLA_HEADS, LANE - MLA_ROPE), F32)
        wuq = jnp.concatenate([wq3[..., :MLA_NOPE], wq_rope, zpad, _rot_half_cols(wq_rope), zpad], axis=-1)
        wuq = wuq.reshape(MLA_Q_RANK, MLA_HEADS * 3 * LANE).astype(BF16)
        wkv3 = mla_w_ukv[l].reshape(MLA_KV_RANK, MLA_HEADS, MLA_NOPE + MLA_V)
        wukv = jnp.concatenate([wkv3[..., :MLA_NOPE].reshape(MLA_KV_RANK, -1),
                                wkv3[..., MLA_NOPE:].reshape(MLA_KV_RANK, -1)], axis=1).astype(BF16)
        qa, ka, va = _mla_prep(xf, gmix, cos, sin, w1, mla_cq_norm[l].reshape(1, -1), mla_ckv_norm[l].reshape(1, -1),
                               wuq, wukv, _gain3(mla_q_norm[l]), _gain3(mla_k_norm[l]), tm=tm)
        y_a = _flash(qa, ka, va, batch=batch, seq=seq, tq=1024)

        w_gla = jnp.concatenate([wl[:, o_gla:o_glr], _pad_cols(wl[:, o_glr:o_rb], LANE), wl[:, o_rb:o_c]],
                                axis=1).astype(BF16)
        w2 = jnp.pad(gla_w_gate2[l], ((0, LANE - GLA_GATE_RANK), (0, 0))).astype(BF16)
        qb, kb, vb, la, sr = _gla_proj(xf, gmix, w_gla, w2, gla_b_gate2[l].reshape(1, -1), tm=tm)
        y_b = _gla(qb, kb, vb, la, sr, gla_o_norm[l].reshape(1, -1), batch=batch, seq=seq, tb=512)

        qkv_c = _dswa_proj(xf, gmix, wl[:, o_c:o_ga].astype(BF16), dswa_q_norm[l].reshape(1, -1),
                           dswa_k_norm[l].reshape(1, -1), batch=batch, seq=seq, tm=tm)
        outs_lses = []
        for gi in range(len(DSWA_GROUPS)):
            outs_lses += _dswa_group(slopes, *qkv_c[3 * gi:3 * gi + 3], group=gi)
        y_c = _dswa_combine(outs_lses, batch=batch, seq=seq, tm=tm)

        xf = _merge(xf, gmix, y_a, y_b, y_c,
                    wl[:, o_ga:o_gb].astype(BF16), wl[:, o_gb:o_gc].astype(BF16), wl[:, o_gc:o_gc + d].astype(BF16),
                    w_branch_a[l].astype(BF16), w_branch_b[l].astype(BF16), w_branch_c[l].astype(BF16),
                    w_out[l].astype(BF16), tm=tm, tn=512)

        wg, wu, wdn = ffn_weights(ffn2_w_gate[l], ffn2_w_up[l], ffn2_w_down[l])
        xf = _ffn(xf, ffn2_norm[l].reshape(1, d), wg, wu, wdn, tm=tm, tf=tf)
    return xf.reshape(batch, seq, d)
```

```python
import functools

import numpy as np
import jax
import jax.numpy as jnp
from jax import lax
from jax.experimental import pallas as pl
from jax.experimental.pallas import tpu as pltpu

F32 = jnp.float32
BF16 = jnp.bfloat16

MLA_HEADS = 6
MLA_Q_RANK = 512
MLA_KV_RANK = 256
MLA_NOPE = 128
MLA_ROPE = 64
MLA_V = 128
MLA_QK = MLA_NOPE + MLA_ROPE
MLA_QK_PAD = 256
GLA_HEADS = 4
GLA_DK = 64
GLA_DV = 128
GLA_GATE_RANK = 16
GLA_TAU = 16.0
GLA_CHUNK = 64
GLA_SUB = 8
GLA_LEVELS = (32, 16, 8)
DSWA_GROUPS = ((128, 1), (512, 4), (2048, 16))
DSWA_HEADS_PER_GROUP = 2
DSWA_HEADS = 6
DSWA_HEAD_DIM = 128
DSWA_W = 128
ROPE_THETA = 10000.0
ALIBI_MAX_EXP = 8.0
NORM_EPS = 1e-6
NEG_INF = -1e30
LOG2E = 1.4426950408889634

LANE = 128
MIB = 1024 * 1024


def _cparams(sem, vmem_mib):
    return pltpu.CompilerParams(dimension_semantics=sem, vmem_limit_bytes=int(vmem_mib * MIB))


def _inv_rms(xf, n):
    return lax.rsqrt(jnp.sum(xf * xf, axis=-1, keepdims=True) / n + NORM_EPS)


def _const_spec(shape):
    nd = len(shape)
    return pl.BlockSpec(shape, lambda *_: (0,) * nd)


def _ffn_kernel(x_ref, g_ref, wg_ref, wu_ref, wd_ref, o_ref, h_scr):
    @pl.when(pl.program_id(1) == 0)
    def _():
        xf = x_ref[...]
        h_scr[...] = (xf * _inv_rms(xf, xf.shape[-1]) * g_ref[...]).astype(BF16)
        o_ref[...] = xf

    h = h_scr[...]
    gate = jnp.dot(h, wg_ref[...], preferred_element_type=F32)
    up = jnp.dot(h, wu_ref[...], preferred_element_type=F32)
    inter = (gate * jax.nn.sigmoid(gate) * up).astype(BF16)
    o_ref[...] += 0.5 * jnp.dot(inter, wd_ref[...], preferred_element_type=F32)


def _ffn(x, g, wg, wu, wd, *, tm, tf, blocked=False, nbuf=2):
    t, d = x.shape
    fp = wg.shape[1]
    mode = {} if nbuf == 2 else dict(pipeline_mode=pl.Buffered(nbuf))
    if blocked:
        wg = wg.reshape(d, fp // tf, tf).transpose(1, 0, 2)
        wu = wu.reshape(d, fp // tf, tf).transpose(1, 0, 2)
        wspec = pl.BlockSpec((None, d, tf), lambda i, j: (j, 0, 0), **mode)
    else:
        wspec = pl.BlockSpec((d, tf), lambda i, j: (0, j), **mode)
    return pl.pallas_call(
        _ffn_kernel,
        grid=(t // tm, fp // tf),
        in_specs=[
            pl.BlockSpec((tm, d), lambda i, j: (i, 0)),
            pl.BlockSpec((1, d), lambda i, j: (0, 0)),
            wspec,
            wspec,
            pl.BlockSpec((tf, d), lambda i, j: (j, 0), **mode),
        ],
        out_specs=pl.BlockSpec((tm, d), lambda i, j: (i, 0)),
        out_shape=jax.ShapeDtypeStruct((t, d), F32),
        scratch_shapes=[pltpu.VMEM((tm, d), BF16)],
        compiler_params=_cparams(("parallel", "arbitrary"), 57 if tm > 512 else 52),
        name="ffn",
    )(x, g, wg, wu, wd)


def _rope_kernel(pos_ref, inv_ref, cos_ref, sin_ref):
    ang = pos_ref[...].astype(F32) * inv_ref[...]
    cos_ref[...] = jnp.cos(ang)
    sin_ref[...] = jnp.sin(ang)


def _rope_tables(pos_col, inv_pad, *, tm):
    t = pos_col.shape[0]
    return pl.pallas_call(
        _rope_kernel,
        grid=(t // tm,),
        in_specs=[pl.BlockSpec((tm, 1), lambda i: (i, 0)), _const_spec((1, LANE))],
        out_specs=[pl.BlockSpec((tm, LANE), lambda i: (i, 0))] * 2,
        out_shape=[jax.ShapeDtypeStruct((t, LANE), F32)] * 2,
        compiler_params=_cparams(("parallel",), 32),
        name="rope_tables",
    )(pos_col, inv_pad)


def _mla_prep_kernel(x_ref, g_ref, cos_ref, sin_ref, w1_ref, gcq_ref, gckv_ref, wuq_ref, wukv_ref,
                     gq_ref, gk_ref, q_ref, k_ref, v_ref):
    xf = x_ref[...]
    h = (xf * _inv_rms(xf, xf.shape[-1]) * g_ref[...]).astype(BF16)
    u = jnp.dot(h, w1_ref[...], preferred_element_type=F32)
    c_q = u[:, :MLA_Q_RANK]
    c_kv = u[:, MLA_Q_RANK:MLA_Q_RANK + MLA_KV_RANK]
    kr = u[:, MLA_Q_RANK + MLA_KV_RANK:MLA_Q_RANK + MLA_KV_RANK + LANE]
    krot = u[:, MLA_Q_RANK + MLA_KV_RANK + LANE:]
    c_q = (c_q * _inv_rms(c_q, MLA_Q_RANK) * gcq_ref[...]).astype(BF16)
    c_kv = (c_kv * _inv_rms(c_kv, MLA_KV_RANK) * gckv_ref[...]).astype(BF16)
    qall = jnp.dot(c_q, wuq_ref[...], preferred_element_type=F32)
    kv = jnp.dot(c_kv, wukv_ref[...], preferred_element_type=F32)
    cos = cos_ref[...]
    sin = sin_ref[...]
    scale = MLA_QK ** -0.5 * LOG2E
    gq_n, gq_r, gq_t = gq_ref[0:1, :], gq_ref[1:2, :], gq_ref[2:3, :]
    gk_n, gk_r, gk_t = gk_ref[0:1, :], gk_ref[1:2, :], gk_ref[2:3, :]
    k_roped = kr * gk_r * cos + krot * gk_t * sin
    kr_ss = jnp.sum(kr * kr, axis=-1, keepdims=True)
    for hh in range(MLA_HEADS):
        base = hh * 3 * LANE
        nope = qall[:, base:base + LANE]
        rope = qall[:, base + LANE:base + 2 * LANE]
        rot = qall[:, base + 2 * LANE:base + 3 * LANE]
        ss = jnp.sum(nope * nope, axis=-1, keepdims=True) + jnp.sum(rope * rope, axis=-1, keepdims=True)
        r = lax.rsqrt(ss / MLA_QK + NORM_EPS) * scale
        q_ref[hh, :, 0:LANE] = (nope * r * gq_n).astype(BF16)
        q_ref[hh, :, LANE:2 * LANE] = ((rope * gq_r * cos + rot * gq_t * sin) * r).astype(BF16)
        kn = kv[:, hh * LANE:(hh + 1) * LANE]
        rk = lax.rsqrt((jnp.sum(kn * kn, axis=-1, keepdims=True) + kr_ss) / MLA_QK + NORM_EPS)
        k_ref[hh, :, 0:LANE] = (kn * rk * gk_n).astype(BF16)
        k_ref[hh, :, LANE:2 * LANE] = (k_roped * rk).astype(BF16)
        v_ref[hh] = kv[:, (MLA_HEADS + hh) * LANE:(MLA_HEADS + hh + 1) * LANE].astype(BF16)


def _mla_prep(x, g, cos, sin, w1, gcq, gckv, wuq, wukv, gq3, gk3, *, tm):
    t, d = x.shape
    row = lambda i: (i, 0)
    hrow = lambda i: (0, i, 0)
    return pl.pallas_call(
        _mla_prep_kernel,
        grid=(t // tm,),
        in_specs=[
            pl.BlockSpec((tm, d), row), _const_spec((1, d)),
            pl.BlockSpec((tm, LANE), row), pl.BlockSpec((tm, LANE), row),
            _const_spec(w1.shape), _const_spec(gcq.shape), _const_spec(gckv.shape),
            _const_spec(wuq.shape), _const_spec(wukv.shape), _const_spec(gq3.shape), _const_spec(gk3.shape),
        ],
        out_specs=[
            pl.BlockSpec((MLA_HEADS, tm, MLA_QK_PAD), hrow),
            pl.BlockSpec((MLA_HEADS, tm, MLA_QK_PAD), hrow),
            pl.BlockSpec((MLA_HEADS, tm, MLA_V), hrow),
        ],
        out_shape=[
            jax.ShapeDtypeStruct((MLA_HEADS, t, MLA_QK_PAD), BF16),
            jax.ShapeDtypeStruct((MLA_HEADS, t, MLA_QK_PAD), BF16),
            jax.ShapeDtypeStruct((MLA_HEADS, t, MLA_V), BF16),
        ],
        compiler_params=_cparams(("parallel",), 52),
        name="mla_prep",
    )(x, g, cos, sin, w1, gcq, gckv, wuq, wukv, gq3, gk3)


def _flash_kernel(q_ref, k_ref, v_ref, o_ref, sa_scr, sb_scr, m_scr, acc_scr, *, tq):
    qi = pl.program_id(2)
    m_scr[...] = jnp.full_like(m_scr, NEG_INF)
    acc_scr[...] = jnp.zeros_like(acc_scr)
    q = q_ref[...]
    ones = jnp.ones((tq, MLA_V), BF16)

    def scores(tile, s_scr):
        r0 = pl.multiple_of(tile * tq, tq)
        s_scr[...] = lax.dot_general(q, k_ref[pl.ds(r0, tq), :], (((1,), (1,)), ((), ())),
                                     preferred_element_type=F32)

    def accumulate(tile, s_scr, diagonal):
        r0 = pl.multiple_of(tile * tq, tq)
        sc = s_scr[...]
        if diagonal:
            row = lax.broadcasted_iota(jnp.int32, sc.shape, 0)
            col = lax.broadcasted_iota(jnp.int32, sc.shape, 1)
            sc = jnp.where(col <= row, sc, NEG_INF)
        m_prev = m_scr[...]
        m_new = jnp.maximum(m_prev, jnp.max(sc, axis=-1, keepdims=True))
        alpha = jnp.exp2(m_prev - m_new)
        p = jnp.exp2(sc - jnp.tile(m_new, (1, tq // LANE)))
        v1 = jnp.concatenate([v_ref[pl.ds(r0, tq), :], ones], axis=1)
        pv = jnp.dot(p.astype(BF16), v1, preferred_element_type=F32)
        acc_scr[...] = jnp.tile(alpha, (1, 2)) * acc_scr[...] + pv
        m_scr[...] = m_new

    scores(0, sa_scr)

    def body(j, carry):
        scores(2 * j + 1, sb_scr)
        accumulate(2 * j, sa_scr, False)
        scores(2 * j + 2, sa_scr)
        accumulate(2 * j + 1, sb_scr, False)
        return carry

    lax.fori_loop(0, qi // 2, body, 0)

    @pl.when(qi % 2 == 0)
    def _():
        accumulate(qi, sa_scr, True)

    @pl.when(qi % 2 == 1)
    def _():
        scores(qi, sb_scr)
        accumulate(qi - 1, sa_scr, False)
        accumulate(qi, sb_scr, True)

    acc = acc_scr[...]
    o_ref[...] = (acc[:, 0:MLA_V] / acc[:, MLA_V:]).astype(o_ref.dtype)


def _flash(q, k, v, *, batch, seq, tq):
    assert seq % tq == 0
    nq = seq // tq
    t = batch * seq
    return pl.pallas_call(
        functools.partial(_flash_kernel, tq=tq),
        grid=(batch, MLA_HEADS, nq),
        in_specs=[
            pl.BlockSpec((None, tq, MLA_QK_PAD), lambda b, h, i: (h, b * nq + i, 0)),
            pl.BlockSpec((None, seq, MLA_QK_PAD), lambda b, h, i: (h, b, 0)),
            pl.BlockSpec((None, seq, MLA_V), lambda b, h, i: (h, b, 0)),
        ],
        out_specs=pl.BlockSpec((tq, MLA_V), lambda b, h, i: (b * nq + i, h)),
        out_shape=jax.ShapeDtypeStruct((t, MLA_HEADS * MLA_V), BF16),
        scratch_shapes=[pltpu.VMEM((tq, tq), F32), pltpu.VMEM((tq, tq), F32),
                        pltpu.VMEM((tq, MLA_V), F32), pltpu.VMEM((tq, 2 * MLA_V), F32)],
        compiler_params=_cparams(("parallel", "parallel", "arbitrary"), 52),
        name="mla_flash",
    )(q, k, v)


def _gla_proj_kernel(x_ref, g_ref, w_ref, w2_ref, b2_ref, q_ref, k_ref, v_ref, la_ref, sr_ref):
    xf = x_ref[...]
    h = (xf * _inv_rms(xf, xf.shape[-1]) * g_ref[...]).astype(BF16)
    u = jnp.dot(h, w_ref[...], preferred_element_type=F32)
    nk = GLA_HEADS * GLA_DK
    nv = GLA_HEADS * GLA_DV
    q_ref[...] = u[:, 0:nk] * (GLA_DK ** -0.5)
    k_ref[...] = u[:, nk:2 * nk]
    v_ref[...] = u[:, 2 * nk:2 * nk + nv]
    gate_lr = u[:, 2 * nk + nv:2 * nk + nv + LANE].astype(BF16)
    z = jnp.dot(gate_lr, w2_ref[...], preferred_element_type=F32) + b2_ref[...]
    la_ref[...] = -(jnp.maximum(-z, 0.0) + jnp.log1p(jnp.exp(-jnp.abs(z)))) / GLA_TAU
    r = u[:, 2 * nk + nv + LANE:]
    sr_ref[...] = r * jax.nn.sigmoid(r)


def _gla_proj(x, g, w, w2, b2, *, tm):
    t, d = x.shape
    nk = GLA_HEADS * GLA_DK
    nv = GLA_HEADS * GLA_DV
    row = lambda i: (i, 0)
    return pl.pallas_call(
        _gla_proj_kernel,
        grid=(t // tm,),
        in_specs=[pl.BlockSpec((tm, d), row), _const_spec((1, d)), _const_spec(w.shape),
                  _const_spec(w2.shape), _const_spec(b2.shape)],
        out_specs=[pl.BlockSpec((tm, nk), row), pl.BlockSpec((tm, nk), row), pl.BlockSpec((tm, nv), row),
                   pl.BlockSpec((tm, nk), row), pl.BlockSpec((tm, nv), row)],
        out_shape=[jax.ShapeDtypeStruct((t, nk), F32), jax.ShapeDtypeStruct((t, nk), F32),
                   jax.ShapeDtypeStruct((t, nv), F32), jax.ShapeDtypeStruct((t, nk), F32),
                   jax.ShapeDtypeStruct((t, nv), F32)],
        compiler_params=_cparams(("parallel",), 52),
        name="gla_proj",
    )(x, g, w, w2, b2)


def _gla_constants():
    c = GLA_CHUNK
    idx = np.arange(c)
    mats = [(idx[None, :] <= idx[:, None])]
    for lvl in GLA_LEVELS:
        p = ((idx // lvl) | 1) * lvl
        qrow = (idx >= p)[:, None] & (idx[None, :] >= p[:, None]) & (idx[None, :] <= idx[:, None])
        krow = (idx < p)[:, None] & (idx[None, :] > idx[:, None]) & (idx[None, :] < p[:, None])
        mats.append(qrow | krow)
    mats.append(idx[None, :] > idx[:, None])
    mstack = np.concatenate(mats, axis=0).astype(np.float32)
    lmask = []
    for lvl in GLA_LEVELS:
        blk = idx // lvl
        m = ((blk[:, None] % 2) == 1) & (blk[None, :] == blk[:, None] - 1)
        lmask.append(np.tile(m.astype(np.float32), (GLA_HEADS, 1)))
    lmask = np.stack(lmask, axis=0)
    hk = np.arange(GLA_HEADS * GLA_DK) // GLA_DK
    hv = np.arange(GLA_HEADS * GLA_DV) // GLA_DV
    hrow = np.arange(GLA_HEADS * c) // c
    headmask = (hrow[:, None] == hk[None, :]).astype(np.float32)
    e_ind = (hk[:, None] == hv[None, :]).astype(np.float32)
    return mstack, lmask, headmask, e_ind


def _gla_kernel(q_ref, k_ref, v_ref, la_ref, sr_ref, go_ref, mst_ref, lmask_ref, hmask_ref, eind_ref, eindt_ref,
                y_ref, st_scr, *, chunks):
    c = GLA_CHUNK
    nk = GLA_HEADS * GLA_DK
    nv = GLA_HEADS * GLA_DV
    nsub = c // GLA_SUB

    @pl.when(pl.program_id(1) == 0)
    def _():
        st_scr[...] = jnp.zeros_like(st_scr)

    def chunk(ci, carry):
        r0 = pl.multiple_of(ci * c, c)
        q = q_ref[pl.ds(r0, c), :]
        k = k_ref[pl.ds(r0, c), :]
        v = v_ref[pl.ds(r0, c), :]
        a = la_ref[pl.ds(r0, c), :]
        a1 = a.astype(BF16)
        r1 = a - a1.astype(F32)
        a2 = r1.astype(BF16)
        a3 = (r1 - a2.astype(F32)).astype(BF16)
        mst = mst_ref[...]
        ex = (jnp.dot(mst, a1, preferred_element_type=F32) + jnp.dot(mst, a2, preferred_element_type=F32)
              + jnp.dot(mst, a3, preferred_element_type=F32))
        b = ex[0:c]
        f = jnp.exp(ex)
        v_bf = v.astype(BF16)

        amat = jnp.zeros((GLA_HEADS * c, c), F32)
        hmask = hmask_ref[...]
        for li in range(len(GLA_LEVELS)):
            fl = f[(li + 1) * c:(li + 2) * c]
            qt = q * fl
            kt = (k * fl).astype(BF16)
            qs = (jnp.concatenate([qt] * GLA_HEADS, axis=0) * hmask).astype(BF16)
            sc = lax.dot_general(qs, kt, (((1,), (1,)), ((), ())), preferred_element_type=F32)
            amat = amat + sc * lmask_ref[li]
        a_bf = amat.astype(BF16)
        o = jnp.concatenate(
            [jnp.dot(a_bf[hh * c:(hh + 1) * c], v_bf[:, hh * GLA_DV:(hh + 1) * GLA_DV], preferred_element_type=F32)
             for hh in range(GLA_HEADS)], axis=1)

        qb = (q * f[0:c]).astype(BF16)
        st = st_scr[...]
        o = o + lax.dot_general(qb, st.astype(BF16), (((1,), (1,)), ((), ())), preferred_element_type=F32)

        q3 = q.reshape(nsub, GLA_SUB, nk)
        k3 = k.reshape(nsub, GLA_SUB, nk)
        b3 = b.reshape(nsub, GLA_SUB, nk)
        v3 = v.reshape(nsub, GLA_SUB, nv)
        tt = lax.broadcasted_iota(jnp.int32, (nsub, GLA_SUB, nk), 1)
        od = jnp.zeros((nsub, GLA_SUB, nv), F32)
        eind = eind_ref[...]
        for s in range(GLA_SUB):
            w = jnp.exp(jnp.minimum(b3 - b3[:, s:s + 1, :], 0.0))
            x = jnp.where(tt >= s, q3 * k3[:, s:s + 1, :] * w, 0.0)
            rr = jnp.dot(x.reshape(c, nk).astype(BF16), eind, preferred_element_type=F32)
            od = od + rr.reshape(nsub, GLA_SUB, nv) * v3[:, s:s + 1, :]
        o = o + od.reshape(c, nv)

        kend = (k * f[(len(GLA_LEVELS) + 1) * c:(len(GLA_LEVELS) + 2) * c]).astype(BF16)
        upd = lax.dot_general(v_bf, kend, (((0,), (0,)), ((), ())), preferred_element_type=F32)
        st_scr[...] = st * f[c - 1:c] + upd * eindt_ref[...]

        go = go_ref[...]
        sr = sr_ref[pl.ds(r0, c), :]
        ys = []
        for hh in range(GLA_HEADS):
            oh = o[:, hh * GLA_DV:(hh + 1) * GLA_DV]
            ys.append(oh * _inv_rms(oh, GLA_DV) * go)
        y_ref[pl.ds(r0, c), :] = (jnp.concatenate(ys, axis=1) * sr).astype(y_ref.dtype)
        return carry

    lax.fori_loop(0, chunks, chunk, 0, unroll=2)


def _gla(q, k, v, la, sr, go, *, batch, seq, tb):
    nk = GLA_HEADS * GLA_DK
    nv = GLA_HEADS * GLA_DV
    nb = seq // tb
    mstack, lmask, headmask, e_ind = _gla_constants()
    consts = [jnp.asarray(mstack, BF16), jnp.asarray(lmask, F32), jnp.asarray(headmask, F32),
              jnp.asarray(e_ind, BF16), jnp.asarray(e_ind.T, F32)]
    row = lambda b, i: (b * nb + i, 0)
    return pl.pallas_call(
        functools.partial(_gla_kernel, chunks=tb // GLA_CHUNK),
        grid=(batch, nb),
        in_specs=[pl.BlockSpec((tb, nk), row), pl.BlockSpec((tb, nk), row), pl.BlockSpec((tb, nv), row),
                  pl.BlockSpec((tb, nk), row), pl.BlockSpec((tb, nv), row), _const_spec(go.shape)]
                 + [_const_spec(cst.shape) for cst in consts],
        out_specs=pl.BlockSpec((tb, nv), row),
        out_shape=jax.ShapeDtypeStruct((batch * seq, nv), BF16),
        scratch_shapes=[pltpu.VMEM((nv, nk), F32)],
        compiler_params=_cparams(("parallel", "arbitrary"), 40),
        name="gla_scan",
    )(q, k, v, la, sr, go, *consts)


def _dswa_proj_kernel(x_ref, g_ref, w_ref, gq_ref, gk_ref, *refs, tm):
    ng = len(DSWA_GROUPS)
    out_refs = refs[:3 * ng]
    q_scr, k_scr, v_scr = refs[3 * ng:]
    xf = x_ref[...]
    h = (xf * _inv_rms(xf, xf.shape[-1]) * g_ref[...]).astype(BF16)
    u = jnp.dot(h, w_ref[...], preferred_element_type=F32)
    n = DSWA_HEADS * DSWA_HEAD_DIM
    gq = gq_ref[...]
    gk = gk_ref[...]
    scale = DSWA_HEAD_DIM ** -0.5
    for hh in range(DSWA_HEADS):
        sl = slice(hh * DSWA_HEAD_DIM, (hh + 1) * DSWA_HEAD_DIM)
        qh = u[:, sl]
        q_scr[hh] = qh * _inv_rms(qh, DSWA_HEAD_DIM) * gq * scale
        kh = u[:, n + hh * DSWA_HEAD_DIM:n + (hh + 1) * DSWA_HEAD_DIM]
        k_scr[hh] = kh * _inv_rms(kh, DSWA_HEAD_DIM) * gk
        v_scr[hh] = u[:, 2 * n + hh * DSWA_HEAD_DIM:2 * n + (hh + 1) * DSWA_HEAD_DIM]
    for gi, (_, dil) in enumerate(DSWA_GROUPS):
        for src, dst in zip((q_scr, k_scr, v_scr), out_refs[3 * gi:3 * gi + 3]):
            for hh in range(DSWA_HEADS_PER_GROUP):
                head = gi * DSWA_HEADS_PER_GROUP + hh
                cols = slice(hh * DSWA_HEAD_DIM, (hh + 1) * DSWA_HEAD_DIM)
                for r in range(dil):
                    rows = pl.ds(r, tm // dil, stride=dil) if dil > 1 else slice(None)
                    dst[r, :, cols] = src[head, rows, :].astype(BF16)


def _dswa_proj(x, g, w, gq, gk, *, batch, seq, tm):
    t, d = x.shape
    n = DSWA_HEADS * DSWA_HEAD_DIM
    gw = DSWA_HEADS_PER_GROUP * DSWA_HEAD_DIM
    nblk = seq // tm
    out_specs, out_shape = [], []
    for _, dil in DSWA_GROUPS:
        assert tm % (16 * dil) == 0
        out_specs += [pl.BlockSpec((None, dil, tm // dil, gw), lambda b, i: (b, 0, i, 0))] * 3
        out_shape += [jax.ShapeDtypeStruct((batch, dil, seq // dil, gw), BF16)] * 3
    return pl.pallas_call(
        functools.partial(_dswa_proj_kernel, tm=tm),
        grid=(batch, nblk),
        in_specs=[pl.BlockSpec((tm, d), lambda b, i: (b * nblk + i, 0)), _const_spec((1, d)), _const_spec(w.shape),
                  _const_spec(gq.shape), _const_spec(gk.shape)],
        out_specs=out_specs,
        out_shape=out_shape,
        scratch_shapes=[pltpu.VMEM((DSWA_HEADS, tm, DSWA_HEAD_DIM), F32)] * 3,
        compiler_params=_cparams(("parallel", "parallel"), 52),
        name="dswa_proj",
    )(x, g, w, gq, gk)


def _dswa_kernel(slope_ref, q_ref, kc_ref, kp_ref, vc_ref, vp_ref, o_ref, l_ref, *, dilation, group, nb):
    w = DSWA_W
    ib = pl.program_id(2)
    i = lax.broadcasted_iota(jnp.int32, (w, 2 * w), 0)
    j = lax.broadcasted_iota(jnp.int32, (w, 2 * w), 1)
    steps = w + i - j
    dist = (steps * dilation).astype(F32)
    in_window = (steps >= 0) & (steps <= w)
    for hh in range(DSWA_HEADS_PER_GROUP):
        cols = slice(hh * DSWA_HEAD_DIM, (hh + 1) * DSWA_HEAD_DIM)
        bias_all = jnp.where(in_window, -slope_ref[group * DSWA_HEADS_PER_GROUP + hh] * dist, NEG_INF)
        bias_first = jnp.where(j >= w, bias_all, NEG_INF)
        for jb in range(nb):
            rows = slice(jb * w, (jb + 1) * w)
            if jb == 0:
                kprev, vprev = kp_ref[:, cols], vp_ref[:, cols]
                bias_j = jnp.where(ib > 0, bias_all, bias_first)
            else:
                prows = slice((jb - 1) * w, jb * w)
                kprev, vprev = kc_ref[prows, cols], vc_ref[prows, cols]
                bias_j = bias_all
            keys = jnp.concatenate([kprev, kc_ref[rows, cols]], axis=0)
            vals = jnp.concatenate([vprev, vc_ref[rows, cols]], axis=0)
            sc = lax.dot_general(q_ref[rows, cols], keys, (((1,), (1,)), ((), ())), preferred_element_type=F32)
            sc = jnp.where(bias_j > 0.5 * NEG_INF, sc + bias_j, NEG_INF)
            m = jnp.max(sc, axis=-1, keepdims=True)
            p = jnp.exp(sc - m)
            lsum = jnp.sum(p, axis=-1, keepdims=True)
            o_ref[rows, cols] = jnp.dot(p.astype(BF16), vals, preferred_element_type=F32) / lsum
            l_ref[rows, cols] = jnp.broadcast_to(m + jnp.log(lsum), (w, DSWA_HEAD_DIM))


def _dswa_group(slopes, qg, kg, vg, *, group):
    window, dilation = DSWA_GROUPS[group]
    assert window // dilation == DSWA_W
    batch, _, sd, gw = qg.shape
    tq = min(1024, sd)
    nb = tq // DSWA_W
    cur = pl.BlockSpec((None, None, tq, gw), lambda b, r, i: (b, r, i, 0))
    prev = pl.BlockSpec((None, None, DSWA_W, gw), lambda b, r, i: (b, r, jnp.maximum(i * nb - 1, 0), 0))
    return pl.pallas_call(
        functools.partial(_dswa_kernel, dilation=dilation, group=group, nb=nb),
        grid=(batch, dilation, sd // tq),
        in_specs=[pl.BlockSpec(memory_space=pltpu.SMEM), cur, cur, prev, cur, prev],
        out_specs=[cur] * 2,
        out_shape=[jax.ShapeDtypeStruct(qg.shape, F32)] * 2,
        compiler_params=_cparams(("parallel", "parallel", "arbitrary"), 32),
        name="dswa_g%d" % group,
    )(slopes, qg, kg, kg, vg, vg)


def _dswa_combine_kernel(*refs, tm):
    ng = len(DSWA_GROUPS)
    in_refs, y_ref, scr = refs[:2 * ng], refs[2 * ng], refs[2 * ng + 1]
    vals = []
    for gi, (_, dil) in enumerate(DSWA_GROUPS):
        for which in range(2):
            src = in_refs[2 * gi + which]
            if dil == 1:
                vals.append(src[0])
            else:
                halves = []
                for hh in range(DSWA_HEADS_PER_GROUP):
                    slot = (2 * gi + which) * DSWA_HEADS_PER_GROUP + hh
                    for r in range(dil):
                        scr[slot, pl.ds(r, tm // dil, stride=dil), :] = (
                            src[r, :, hh * DSWA_HEAD_DIM:(hh + 1) * DSWA_HEAD_DIM])
                    halves.append(scr[slot])
                vals.append(jnp.concatenate(halves, axis=1))
    lses = vals[1::2]
    mx = functools.reduce(jnp.maximum, lses)
    es = [jnp.exp(l - mx) for l in lses]
    den = functools.reduce(lambda a, b: a + b, es)
    y = functools.reduce(lambda a, b: a + b, [(e / den) * o for e, o in zip(es, vals[0::2])])
    y_ref[...] = y.astype(y_ref.dtype)


def _dswa_combine(outs_lses, *, batch, seq, tm):
    gw = DSWA_HEADS_PER_GROUP * DSWA_HEAD_DIM
    nblk = seq // tm
    in_specs = []
    for _, dil in DSWA_GROUPS:
        in_specs += [pl.BlockSpec((None, dil, tm // dil, gw), lambda b, i: (b, 0, i, 0))] * 2
    return pl.pallas_call(
        functools.partial(_dswa_combine_kernel, tm=tm),
        grid=(batch, nblk),
        in_specs=in_specs,
        out_specs=pl.BlockSpec((tm, gw), lambda b, i: (b * nblk + i, 0)),
        out_shape=jax.ShapeDtypeStruct((batch * seq, gw), BF16),
        scratch_shapes=[pltpu.VMEM((2 * len(DSWA_GROUPS) * DSWA_HEADS_PER_GROUP, tm, DSWA_HEAD_DIM), F32)],
        compiler_params=_cparams(("parallel", "parallel"), 32),
        name="dswa_combine",
    )(*outs_lses)


def _merge_kernel(x_ref, g_ref, ya_ref, yb_ref, yc_ref, wga_ref, wgb_ref, wgc_ref, wa_ref, wb_ref, wc_ref, wo_ref,
                  out_ref, h_scr, inter_scr):
    j = pl.program_id(1)
    nj = pl.num_programs(1) - 1
    dot = lambda a, b: jnp.dot(a, b, preferred_element_type=F32)

    def gated_chunk():
        h = h_scr[...]
        return (jax.nn.sigmoid(dot(h, wga_ref[...])) * dot(ya_ref[...], wa_ref[...])
                + jax.nn.sigmoid(dot(h, wgb_ref[...])) * dot(yb_ref[...], wb_ref[...])
                + jax.nn.sigmoid(dot(h, wgc_ref[...])) * dot(yc_ref[...], wc_ref[...])).astype(BF16)

    @pl.when(j == 0)
    def _():
        xf = x_ref[...]
        h_scr[...] = (xf * _inv_rms(xf, xf.shape[-1]) * g_ref[...]).astype(BF16)
        out_ref[...] = xf
        inter_scr[...] = gated_chunk()

    @pl.when((j > 0) & (j < nj))
    def _():
        out_ref[...] += dot(inter_scr[...], wo_ref[...])
        inter_scr[...] = gated_chunk()

    @pl.when(j == nj)
    def _():
        out_ref[...] += dot(inter_scr[...], wo_ref[...])


def _merge(x, g, ya, yb, yc, wga, wgb, wgc, wa, wb, wc, wo, *, tm, tn):
    t, d = x.shape
    nj = d // tn
    row = lambda i, j: (i, 0)
    colw = lambda i, j: (0, jnp.minimum(j, nj - 1))
    return pl.pallas_call(
        _merge_kernel,
        grid=(t // tm, nj + 1),
        in_specs=[pl.BlockSpec((tm, d), row), pl.BlockSpec((1, d), lambda i, j: (0, 0)),
                  pl.BlockSpec((tm, ya.shape[1]), row), pl.BlockSpec((tm, yb.shape[1]), row),
                  pl.BlockSpec((tm, yc.shape[1]), row)]
                 + [pl.BlockSpec((d, tn), colw)] * 3
                 + [pl.BlockSpec((wa.shape[0], tn), colw), pl.BlockSpec((wb.shape[0], tn), colw),
                    pl.BlockSpec((wc.shape[0], tn), colw),
                    pl.BlockSpec((tn, d), lambda i, j: (jnp.maximum(j - 1, 0), 0))],
        out_specs=pl.BlockSpec((tm, d), row),
        out_shape=jax.ShapeDtypeStruct((t, d), F32),
        scratch_shapes=[pltpu.VMEM((tm, d), BF16), pltpu.VMEM((tm, tn), BF16)],
        compiler_params=_cparams(("parallel", "arbitrary"), 52),
        name="merge",
    )(x, g, ya, yb, yc, wga, wgb, wgc, wa, wb, wc, wo)


def _pad_cols(w, n):
    return jnp.pad(w, ((0, 0), (0, n - w.shape[1])))


def _rot_half_cols(w):
    half = w.shape[-1] // 2
    return jnp.concatenate([-w[..., half:], w[..., :half]], axis=-1)


def _swap_half(g):
    half = g.shape[-1] // 2
    return jnp.concatenate([g[..., half:], g[..., :half]], axis=-1)


def _gain3(g):
    pad = lambda v: jnp.pad(v, (0, LANE - v.shape[0]))
    return jnp.stack([g[:MLA_NOPE], pad(g[MLA_NOPE:]), pad(_swap_half(g[MLA_NOPE:]))], axis=0)


def kernel(x, positions, ffn1_norm, ffn1_w_gate, ffn1_w_up, ffn1_w_down, mix_norm, w_in, mla_cq_norm, mla_ckv_norm, mla_w_uq, mla_w_ukv, mla_q_norm, mla_k_norm, gla_w_gate2, gla_b_gate2, gla_o_norm, dswa_q_norm, dswa_k_norm, w_branch_a, w_branch_b, w_branch_c, w_out, ffn2_norm, ffn2_w_gate, ffn2_w_up, ffn2_w_down):
    batch, seq, d = x.shape
    depth = w_in.shape[0]
    d_ff = ffn1_w_gate.shape[-1]
    t = batch * seq
    tm = 512
    tf = 512
    ff_pad = -(-d_ff // tf) * tf
    assert seq % max(wd for wd, _ in DSWA_GROUPS) == 0 and seq % tm == 0

    xf = x.reshape(t, d)
    half = MLA_ROPE // 2
    inv_freq = ROPE_THETA ** (-jnp.arange(half, dtype=F32) / half)
    inv_pad = jnp.pad(jnp.concatenate([inv_freq, inv_freq]), (0, LANE - MLA_ROPE)).reshape(1, LANE)
    cos, sin = _rope_tables(positions.reshape(t, 1), inv_pad, tm=tm)
    slopes = 2.0 ** (-ALIBI_MAX_EXP * jnp.arange(1, DSWA_HEADS + 1, dtype=F32) / DSWA_HEADS)

    o_cq, o_ckv, o_kr = 0, MLA_Q_RANK, MLA_Q_RANK + MLA_KV_RANK
    o_gla = o_kr + MLA_ROPE
    n_qkv_b = GLA_HEADS * (2 * GLA_DK + GLA_DV)
    o_glr = o_gla + n_qkv_b
    o_rb = o_glr + GLA_GATE_RANK
    o_c = o_rb + GLA_HEADS * GLA_DV
    o_ga = o_c + 3 * DSWA_HEADS * DSWA_HEAD_DIM
    o_gb, o_gc = o_ga + d, o_ga + 2 * d

    def ffn_weights(wg, wu, wdn):
        padc = lambda w: jnp.pad(w, ((0, 0), (0, ff_pad - d_ff))).astype(BF16)
        return padc(wg), padc(wu), jnp.pad(wdn, ((0, ff_pad - d_ff), (0, 0))).astype(BF16)

    for l in range(depth):
        wg, wu, wdn = ffn_weights(ffn1_w_gate[l], ffn1_w_up[l], ffn1_w_down[l])
        xf = _ffn(xf, ffn1_norm[l].reshape(1, d), wg, wu, wdn, tm=tm, tf=tf, blocked=(l % 2 == 1))

        wl = w_in[l]
        gmix = mix_norm[l].reshape(1, d)
        w_kr = wl[:, o_kr:o_kr + MLA_ROPE]
        w1 = jnp.concatenate([wl[:, o_cq:o_kr], _pad_cols(w_kr, LANE), _pad_cols(_rot_half_cols(w_kr), LANE)],
                             axis=1).astype(BF16)
        wq3 = mla_w_uq[l].reshape(MLA_Q_RANK, MLA_HEADS, MLA_QK)
        wq_rope = wq3[..., MLA_NOPE:]
        zpad = jnp.zeros((MLA_Q_RANK, MLA_HEADS, LANE - MLA_ROPE), F32)
        wuq = jnp.concatenate([wq3[..., :MLA_NOPE], wq_rope, zpad, _rot_half_cols(wq_rope), zpad], axis=-1)
        wuq = wuq.reshape(MLA_Q_RANK, MLA_HEADS * 3 * LANE).astype(BF16)
        wkv3 = mla_w_ukv[l].reshape(MLA_KV_RANK, MLA_HEADS, MLA_NOPE + MLA_V)
        wukv = jnp.concatenate([wkv3[..., :MLA_NOPE].reshape(MLA_KV_RANK, -1),
                                wkv3[..., MLA_NOPE:].reshape(MLA_KV_RANK, -1)], axis=1).astype(BF16)
        qa, ka, va = _mla_prep(xf, gmix, cos, sin, w1, mla_cq_norm[l].reshape(1, -1), mla_ckv_norm[l].reshape(1, -1),
                               wuq, wukv, _gain3(mla_q_norm[l]), _gain3(mla_k_norm[l]), tm=tm)
        y_a = _flash(qa, ka, va, batch=batch, seq=seq, tq=1024)

        w_gla = jnp.concatenate([wl[:, o_gla:o_glr], _pad_cols(wl[:, o_glr:o_rb], LANE), wl[:, o_rb:o_c]],
                                axis=1).astype(BF16)
        w2 = jnp.pad(gla_w_gate2[l], ((0, LANE - GLA_GATE_RANK), (0, 0))).astype(BF16)
        qb, kb, vb, la, sr = _gla_proj(xf, gmix, w_gla, w2, gla_b_gate2[l].reshape(1, -1), tm=tm)
        y_b = _gla(qb, kb, vb, la, sr, gla_o_norm[l].reshape(1, -1), batch=batch, seq=seq, tb=512)

        qkv_c = _dswa_proj(xf, gmix, wl[:, o_c:o_ga].astype(BF16), dswa_q_norm[l].reshape(1, -1),
                           dswa_k_norm[l].reshape(1, -1), batch=batch, seq=seq, tm=tm)
        outs_lses = []
        for gi in range(len(DSWA_GROUPS)):
            outs_lses += _dswa_group(slopes, *qkv_c[3 * gi:3 * gi + 3], group=gi)
        y_c = _dswa_combine(outs_lses, batch=batch, seq=seq, tm=tm)

        xf = _merge(xf, gmix, y_a, y_b, y_c,
                    wl[:, o_ga:o_gb].astype(BF16), wl[:, o_gb:o_gc].astype(BF16), wl[:, o_gc:o_gc + d].astype(BF16),
                    w_branch_a[l].astype(BF16), w_branch_b[l].astype(BF16), w_branch_c[l].astype(BF16),
                    w_out[l].astype(BF16), tm=tm, tn=512)

        wg, wu, wdn = ffn_weights(ffn2_w_gate[l], ffn2_w_up[l], ffn2_w_down[l])
        xf = _ffn(xf, ffn2_norm[l].reshape(1, d), wg, wu, wdn, tm=(1024 if l == 1 else 256 if l == 2 else tm), tf=tf)
    return xf.reshape(batch, seq, d)
```

```python
import functools

import numpy as np
import jax
import jax.numpy as jnp
from jax import lax
from jax.experimental import pallas as pl
from jax.experimental.pallas import tpu as pltpu

F32 = jnp.float32
BF16 = jnp.bfloat16

MLA_HEADS = 6
MLA_Q_RANK = 512
MLA_KV_RANK = 256
MLA_NOPE = 128
MLA_ROPE = 64
MLA_V = 128
MLA_QK = MLA_NOPE + MLA_ROPE
MLA_QK_PAD = 256
GLA_HEADS = 4
GLA_DK = 64
GLA_DV = 128
GLA_GATE_RANK = 16
GLA_TAU = 16.0
GLA_CHUNK = 64
GLA_SUB = 8
GLA_LEVELS = (32, 16, 8)
DSWA_GROUPS = ((128, 1), (512, 4), (2048, 16))
DSWA_HEADS_PER_GROUP = 2
DSWA_HEADS = 6
DSWA_HEAD_DIM = 128
DSWA_W = 128
ROPE_THETA = 10000.0
ALIBI_MAX_EXP = 8.0
NORM_EPS = 1e-6
NEG_INF = -1e30
LOG2E = 1.4426950408889634

LANE = 128
MIB = 1024 * 1024


def _cparams(sem, vmem_mib):
    return pltpu.CompilerParams(dimension_semantics=sem, vmem_limit_bytes=int(vmem_mib * MIB))


def _inv_rms(xf, n):
    return lax.rsqrt(jnp.sum(xf * xf, axis=-1, keepdims=True) / n + NORM_EPS)


def _const_spec(shape):
    nd = len(shape)
    return pl.BlockSpec(shape, lambda *_: (0,) * nd)


def _ffn_kernel(x_ref, g_ref, wg_ref, wu_ref, wd_ref, o_ref, h_scr):
    @pl.when(pl.program_id(1) == 0)
    def _():
        xf = x_ref[...]
        h_scr[...] = (xf * _inv_rms(xf, xf.shape[-1]) * g_ref[...]).astype(BF16)
        o_ref[...] = xf

    h = h_scr[...]
    gate = jnp.dot(h, wg_ref[...], preferred_element_type=F32)
    up = jnp.dot(h, wu_ref[...], preferred_element_type=F32)
    inter = (gate * jax.nn.sigmoid(gate) * up).astype(BF16)
    o_ref[...] += 0.5 * jnp.dot(inter, wd_ref[...], preferred_element_type=F32)


def _ffn(x, g, wg, wu, wd, *, tm, tf):
    t, d = x.shape
    fp = wg.shape[1]
    return pl.pallas_call(
        _ffn_kernel,
        grid=(t // tm, fp // tf),
        in_specs=[
            pl.BlockSpec((tm, d), lambda i, j: (i, 0)),
            pl.BlockSpec((1, d), lambda i, j: (0, 0)),
            pl.BlockSpec((d, tf), lambda i, j: (0, j)),
            pl.BlockSpec((d, tf), lambda i, j: (0, j)),
            pl.BlockSpec((tf, d), lambda i, j: (j, 0)),
        ],
        out_specs=pl.BlockSpec((tm, d), lambda i, j: (i, 0)),
        out_shape=jax.ShapeDtypeStruct((t, d), F32),
        scratch_shapes=[pltpu.VMEM((tm, d), BF16)],
        compiler_params=_cparams(("parallel", "arbitrary"), 57),
        name="ffn",
    )(x, g, wg, wu, wd)


def _rope_kernel(pos_ref, inv_ref, cos_ref, sin_ref):
    ang = pos_ref[...].astype(F32) * inv_ref[...]
    cos_ref[...] = jnp.cos(ang)
    sin_ref[...] = jnp.sin(ang)


def _rope_tables(pos_col, inv_pad, *, tm):
    t = pos_col.shape[0]
    return pl.pallas_call(
        _rope_kernel,
        grid=(t // tm,),
        in_specs=[pl.BlockSpec((tm, 1), lambda i: (i, 0)), _const_spec((1, LANE))],
        out_specs=[pl.BlockSpec((tm, LANE), lambda i: (i, 0))] * 2,
        out_shape=[jax.ShapeDtypeStruct((t, LANE), F32)] * 2,
        compiler_params=_cparams(("parallel",), 32),
        name="rope_tables",
    )(pos_col, inv_pad)


def _mla_prep_kernel(x_ref, g_ref, cos_ref, sin_ref, w1_ref, gcq_ref, gckv_ref, wuq_ref, wukv_ref,
                     gq_ref, gk_ref, q_ref, k_ref, v_ref):
    xf = x_ref[...]
    h = (xf * _inv_rms(xf, xf.shape[-1]) * g_ref[...]).astype(BF16)
    u = jnp.dot(h, w1_ref[...], preferred_element_type=F32)
    c_q = u[:, :MLA_Q_RANK]
    c_kv = u[:, MLA_Q_RANK:MLA_Q_RANK + MLA_KV_RANK]
    kr = u[:, MLA_Q_RANK + MLA_KV_RANK:MLA_Q_RANK + MLA_KV_RANK + LANE]
    krot = u[:, MLA_Q_RANK + MLA_KV_RANK + LANE:]
    c_q = (c_q * _inv_rms(c_q, MLA_Q_RANK) * gcq_ref[...]).astype(BF16)
    c_kv = (c_kv * _inv_rms(c_kv, MLA_KV_RANK) * gckv_ref[...]).astype(BF16)
    qall = jnp.dot(c_q, wuq_ref[...], preferred_element_type=F32)
    kv = jnp.dot(c_kv, wukv_ref[...], preferred_element_type=F32)
    cos = cos_ref[...]
    sin = sin_ref[...]
    scale = MLA_QK ** -0.5 * LOG2E
    gq_n, gq_r, gq_t = gq_ref[0:1, :], gq_ref[1:2, :], gq_ref[2:3, :]
    gk_n, gk_r, gk_t = gk_ref[0:1, :], gk_ref[1:2, :], gk_ref[2:3, :]
    k_roped = kr * gk_r * cos + krot * gk_t * sin
    kr_ss = jnp.sum(kr * kr, axis=-1, keepdims=True)
    for hh in range(MLA_HEADS):
        base = hh * 3 * LANE
        nope = qall[:, base:base + LANE]
        rope = qall[:, base + LANE:base + 2 * LANE]
        rot = qall[:, base + 2 * LANE:base + 3 * LANE]
        ss = jnp.sum(nope * nope, axis=-1, keepdims=True) + jnp.sum(rope * rope, axis=-1, keepdims=True)
        r = lax.rsqrt(ss / MLA_QK + NORM_EPS) * scale
        q_ref[hh, :, 0:LANE] = (nope * r * gq_n).astype(BF16)
        q_ref[hh, :, LANE:2 * LANE] = ((rope * gq_r * cos + rot * gq_t * sin) * r).astype(BF16)
        kn = kv[:, hh * LANE:(hh + 1) * LANE]
        rk = lax.rsqrt((jnp.sum(kn * kn, axis=-1, keepdims=True) + kr_ss) / MLA_QK + NORM_EPS)
        k_ref[hh, :, 0:LANE] = (kn * rk * gk_n).astype(BF16)
        k_ref[hh, :, LANE:2 * LANE] = (k_roped * rk).astype(BF16)
        v_ref[hh] = kv[:, (MLA_HEADS + hh) * LANE:(MLA_HEADS + hh + 1) * LANE].astype(BF16)


def _mla_prep(x, g, cos, sin, w1, gcq, gckv, wuq, wukv, gq3, gk3, *, tm):
    t, d = x.shape
    row = lambda i: (i, 0)
    hrow = lambda i: (0, i, 0)
    return pl.pallas_call(
        _mla_prep_kernel,
        grid=(t // tm,),
        in_specs=[
            pl.BlockSpec((tm, d), row), _const_spec((1, d)),
            pl.BlockSpec((tm, LANE), row), pl.BlockSpec((tm, LANE), row),
            _const_spec(w1.shape), _const_spec(gcq.shape), _const_spec(gckv.shape),
            _const_spec(wuq.shape), _const_spec(wukv.shape), _const_spec(gq3.shape), _const_spec(gk3.shape),
        ],
        out_specs=[
            pl.BlockSpec((MLA_HEADS, tm, MLA_QK_PAD), hrow),
            pl.BlockSpec((MLA_HEADS, tm, MLA_QK_PAD), hrow),
            pl.BlockSpec((MLA_HEADS, tm, MLA_V), hrow),
        ],
        out_shape=[
            jax.ShapeDtypeStruct((MLA_HEADS, t, MLA_QK_PAD), BF16),
            jax.ShapeDtypeStruct((MLA_HEADS, t, MLA_QK_PAD), BF16),
            jax.ShapeDtypeStruct((MLA_HEADS, t, MLA_V), BF16),
        ],
        compiler_params=_cparams(("parallel",), 52),
        name="mla_prep",
    )(x, g, cos, sin, w1, gcq, gckv, wuq, wukv, gq3, gk3)


def _flash_kernel(q_ref, k_ref, v_ref, o_ref, sa_scr, sb_scr, m_scr, acc_scr, *, tq):
    qi = pl.program_id(2)
    m_scr[...] = jnp.full_like(m_scr, NEG_INF)
    acc_scr[...] = jnp.zeros_like(acc_scr)
    q = q_ref[...]
    ones = jnp.ones((tq, MLA_V), BF16)

    def scores(tile, s_scr):
        r0 = pl.multiple_of(tile * tq, tq)
        s_scr[...] = lax.dot_general(q, k_ref[pl.ds(r0, tq), :], (((1,), (1,)), ((), ())),
                                     preferred_element_type=F32)

    def accumulate(tile, s_scr, diagonal):
        r0 = pl.multiple_of(tile * tq, tq)
        sc = s_scr[...]
        if diagonal:
            row = lax.broadcasted_iota(jnp.int32, sc.shape, 0)
            col = lax.broadcasted_iota(jnp.int32, sc.shape, 1)
            sc = jnp.where(col <= row, sc, NEG_INF)
        m_prev = m_scr[...]
        m_new = jnp.maximum(m_prev, jnp.max(sc, axis=-1, keepdims=True))
        alpha = jnp.exp2(m_prev - m_new)
        p = jnp.exp2(sc - jnp.tile(m_new, (1, tq // LANE)))
        v1 = jnp.concatenate([v_ref[pl.ds(r0, tq), :], ones], axis=1)
        pv = jnp.dot(p.astype(BF16), v1, preferred_element_type=F32)
        acc_scr[...] = jnp.tile(alpha, (1, 2)) * acc_scr[...] + pv
        m_scr[...] = m_new

    scores(0, sa_scr)

    def body(j, carry):
        scores(2 * j + 1, sb_scr)
        accumulate(2 * j, sa_scr, False)
        scores(2 * j + 2, sa_scr)
        accumulate(2 * j + 1, sb_scr, False)
        return carry

    lax.fori_loop(0, qi // 2, body, 0)

    @pl.when(qi % 2 == 0)
    def _():
        accumulate(qi, sa_scr, True)

    @pl.when(qi % 2 == 1)
    def _():
        scores(qi, sb_scr)
        accumulate(qi - 1, sa_scr, False)
        accumulate(qi, sb_scr, True)

    acc = acc_scr[...]
    o_ref[...] = (acc[:, 0:MLA_V] / acc[:, MLA_V:]).astype(o_ref.dtype)


def _flash(q, k, v, *, batch, seq, tq):
    assert seq % tq == 0
    nq = seq // tq
    t = batch * seq
    return pl.pallas_call(
        functools.partial(_flash_kernel, tq=tq),
        grid=(batch, MLA_HEADS, nq),
        in_specs=[
            pl.BlockSpec((None, tq, MLA_QK_PAD), lambda b, h, i: (h, b * nq + i, 0)),
            pl.BlockSpec((None, seq, MLA_QK_PAD), lambda b, h, i: (h, b, 0)),
            pl.BlockSpec((None, seq, MLA_V), lambda b, h, i: (h, b, 0)),
        ],
        out_specs=pl.BlockSpec((tq, MLA_V), lambda b, h, i: (b * nq + i, h)),
        out_shape=jax.ShapeDtypeStruct((t, MLA_HEADS * MLA_V), BF16),
        scratch_shapes=[pltpu.VMEM((tq, tq), F32), pltpu.VMEM((tq, tq), F32),
                        pltpu.VMEM((tq, MLA_V), F32), pltpu.VMEM((tq, 2 * MLA_V), F32)],
        compiler_params=_cparams(("parallel", "parallel", "arbitrary"), 52),
        name="mla_flash",
    )(q, k, v)


def _gla_proj_kernel(x_ref, g_ref, w_ref, w2_ref, b2_ref, q_ref, k_ref, v_ref, la_ref, sr_ref):
    xf = x_ref[...]
    h = (xf * _inv_rms(xf, xf.shape[-1]) * g_ref[...]).astype(BF16)
    u = jnp.dot(h, w_ref[...], preferred_element_type=F32)
    nk = GLA_HEADS * GLA_DK
    nv = GLA_HEADS * GLA_DV
    q_ref[...] = u[:, 0:nk] * (GLA_DK ** -0.5)
    k_ref[...] = u[:, nk:2 * nk]
    v_ref[...] = u[:, 2 * nk:2 * nk + nv]
    gate_lr = u[:, 2 * nk + nv:2 * nk + nv + LANE].astype(BF16)
    z = jnp.dot(gate_lr, w2_ref[...], preferred_element_type=F32) + b2_ref[...]
    la_ref[...] = -(jnp.maximum(-z, 0.0) + jnp.log1p(jnp.exp(-jnp.abs(z)))) / GLA_TAU
    r = u[:, 2 * nk + nv + LANE:]
    sr_ref[...] = r * jax.nn.sigmoid(r)


def _gla_proj(x, g, w, w2, b2, *, tm):
    t, d = x.shape
    nk = GLA_HEADS * GLA_DK
    nv = GLA_HEADS * GLA_DV
    row = lambda i: (i, 0)
    return pl.pallas_call(
        _gla_proj_kernel,
        grid=(t // tm,),
        in_specs=[pl.BlockSpec((tm, d), row), _const_spec((1, d)), _const_spec(w.shape),
                  _const_spec(w2.shape), _const_spec(b2.shape)],
        out_specs=[pl.BlockSpec((tm, nk), row), pl.BlockSpec((tm, nk), row), pl.BlockSpec((tm, nv), row),
                   pl.BlockSpec((tm, nk), row), pl.BlockSpec((tm, nv), row)],
        out_shape=[jax.ShapeDtypeStruct((t, nk), F32), jax.ShapeDtypeStruct((t, nk), F32),
                   jax.ShapeDtypeStruct((t, nv), F32), jax.ShapeDtypeStruct((t, nk), F32),
                   jax.ShapeDtypeStruct((t, nv), F32)],
        compiler_params=_cparams(("parallel",), 52),
        name="gla_proj",
    )(x, g, w, w2, b2)


def _gla_constants():
    c = GLA_CHUNK
    idx = np.arange(c)
    mats = [(idx[None, :] <= idx[:, None])]
    for lvl in GLA_LEVELS:
        p = ((idx // lvl) | 1) * lvl
        qrow = (idx >= p)[:, None] & (idx[None, :] >= p[:, None]) & (idx[None, :] <= idx[:, None])
        krow = (idx < p)[:, None] & (idx[None, :] > idx[:, None]) & (idx[None, :] < p[:, None])
        mats.append(qrow | krow)
    mats.append(idx[None, :] > idx[:, None])
    mstack = np.concatenate(mats, axis=0).astype(np.float32)
    lmask = []
    for lvl in GLA_LEVELS:
        blk = idx // lvl
        m = ((blk[:, None] % 2) == 1) & (blk[None, :] == blk[:, None] - 1)
        lmask.append(np.tile(m.astype(np.float32), (GLA_HEADS, 1)))
    lmask = np.stack(lmask, axis=0)
    hk = np.arange(GLA_HEADS * GLA_DK) // GLA_DK
    hv = np.arange(GLA_HEADS * GLA_DV) // GLA_DV
    hrow = np.arange(GLA_HEADS * c) // c
    headmask = (hrow[:, None] == hk[None, :]).astype(np.float32)
    e_ind = (hk[:, None] == hv[None, :]).astype(np.float32)
    return mstack, lmask, headmask, e_ind


def _gla_kernel(q_ref, k_ref, v_ref, la_ref, sr_ref, go_ref, mst_ref, lmask_ref, hmask_ref, eind_ref, eindt_ref,
                y_ref, st_scr, *, chunks):
    c = GLA_CHUNK
    nk = GLA_HEADS * GLA_DK
    nv = GLA_HEADS * GLA_DV
    nsub = c // GLA_SUB

    @pl.when(pl.program_id(1) == 0)
    def _():
        st_scr[...] = jnp.zeros_like(st_scr)

    def chunk(ci, carry):
        r0 = pl.multiple_of(ci * c, c)
        q = q_ref[pl.ds(r0, c), :]
        k = k_ref[pl.ds(r0, c), :]
        v = v_ref[pl.ds(r0, c), :]
        a = la_ref[pl.ds(r0, c), :]
        a1 = a.astype(BF16)
        r1 = a - a1.astype(F32)
        a2 = r1.astype(BF16)
        a3 = (r1 - a2.astype(F32)).astype(BF16)
        mst = mst_ref[...]
        ex = (jnp.dot(mst, a1, preferred_element_type=F32) + jnp.dot(mst, a2, preferred_element_type=F32)
              + jnp.dot(mst, a3, preferred_element_type=F32))
        b = ex[0:c]
        f = jnp.exp(ex)
        v_bf = v.astype(BF16)

        amat = jnp.zeros((GLA_HEADS * c, c), F32)
        hmask = hmask_ref[...]
        for li in range(len(GLA_LEVELS)):
            fl = f[(li + 1) * c:(li + 2) * c]
            qt = q * fl
            kt = (k * fl).astype(BF16)
            qs = (jnp.concatenate([qt] * GLA_HEADS, axis=0) * hmask).astype(BF16)
            sc = lax.dot_general(qs, kt, (((1,), (1,)), ((), ())), preferred_element_type=F32)
            amat = amat + sc * lmask_ref[li]
        a_bf = amat.astype(BF16)
        o = jnp.concatenate(
            [jnp.dot(a_bf[hh * c:(hh + 1) * c], v_bf[:, hh * GLA_DV:(hh + 1) * GLA_DV], preferred_element_type=F32)
             for hh in range(GLA_HEADS)], axis=1)

        qb = (q * f[0:c]).astype(BF16)
        st = st_scr[...]
        o = o + lax.dot_general(qb, st.astype(BF16), (((1,), (1,)), ((), ())), preferred_element_type=F32)

        q3 = q.reshape(nsub, GLA_SUB, nk)
        k3 = k.reshape(nsub, GLA_SUB, nk)
        b3 = b.reshape(nsub, GLA_SUB, nk)
        v3 = v.reshape(nsub, GLA_SUB, nv)
        tt = lax.broadcasted_iota(jnp.int32, (nsub, GLA_SUB, nk), 1)
        od = jnp.zeros((nsub, GLA_SUB, nv), F32)
        eind = eind_ref[...]
        for s in range(GLA_SUB):
            w = jnp.exp(jnp.minimum(b3 - b3[:, s:s + 1, :], 0.0))
            x = jnp.where(tt >= s, q3 * k3[:, s:s + 1, :] * w, 0.0)
            rr = jnp.dot(x.reshape(c, nk).astype(BF16), eind, preferred_element_type=F32)
            od = od + rr.reshape(nsub, GLA_SUB, nv) * v3[:, s:s + 1, :]
        o = o + od.reshape(c, nv)

        kend = (k * f[(len(GLA_LEVELS) + 1) * c:(len(GLA_LEVELS) + 2) * c]).astype(BF16)
        upd = lax.dot_general(v_bf, kend, (((0,), (0,)), ((), ())), preferred_element_type=F32)
        st_scr[...] = st * f[c - 1:c] + upd * eindt_ref[...]

        go = go_ref[...]
        sr = sr_ref[pl.ds(r0, c), :]
        ys = []
        for hh in range(GLA_HEADS):
            oh = o[:, hh * GLA_DV:(hh + 1) * GLA_DV]
            ys.append(oh * _inv_rms(oh, GLA_DV) * go)
        y_ref[pl.ds(r0, c), :] = (jnp.concatenate(ys, axis=1) * sr).astype(y_ref.dtype)
        return carry

    lax.fori_loop(0, chunks, chunk, 0, unroll=2)


def _gla(q, k, v, la, sr, go, *, batch, seq, tb):
    nk = GLA_HEADS * GLA_DK
    nv = GLA_HEADS * GLA_DV
    nb = seq // tb
    mstack, lmask, headmask, e_ind = _gla_constants()
    consts = [jnp.asarray(mstack, BF16), jnp.asarray(lmask, F32), jnp.asarray(headmask, F32),
              jnp.asarray(e_ind, BF16), jnp.asarray(e_ind.T, F32)]
    row = lambda b, i: (b * nb + i, 0)
    return pl.pallas_call(
        functools.partial(_gla_kernel, chunks=tb // GLA_CHUNK),
        grid=(batch, nb),
        in_specs=[pl.BlockSpec((tb, nk), row), pl.BlockSpec((tb, nk), row), pl.BlockSpec((tb, nv), row),
                  pl.BlockSpec((tb, nk), row), pl.BlockSpec((tb, nv), row), _const_spec(go.shape)]
                 + [_const_spec(cst.shape) for cst in consts],
        out_specs=pl.BlockSpec((tb, nv), row),
        out_shape=jax.ShapeDtypeStruct((batch * seq, nv), BF16),
        scratch_shapes=[pltpu.VMEM((nv, nk), F32)],
        compiler_params=_cparams(("parallel", "arbitrary"), 40),
        name="gla_scan",
    )(q, k, v, la, sr, go, *consts)


def _dswa_proj_kernel(x_ref, g_ref, w_ref, gq_ref, gk_ref, *refs, tm):
    ng = len(DSWA_GROUPS)
    out_refs = refs[:3 * ng]
    q_scr, k_scr, v_scr = refs[3 * ng:]
    xf = x_ref[...]
    h = (xf * _inv_rms(xf, xf.shape[-1]) * g_ref[...]).astype(BF16)
    u = jnp.dot(h, w_ref[...], preferred_element_type=F32)
    n = DSWA_HEADS * DSWA_HEAD_DIM
    gq = gq_ref[...]
    gk = gk_ref[...]
    scale = DSWA_HEAD_DIM ** -0.5
    for hh in range(DSWA_HEADS):
        sl = slice(hh * DSWA_HEAD_DIM, (hh + 1) * DSWA_HEAD_DIM)
        qh = u[:, sl]
        q_scr[hh] = qh * _inv_rms(qh, DSWA_HEAD_DIM) * gq * scale
        kh = u[:, n + hh * DSWA_HEAD_DIM:n + (hh + 1) * DSWA_HEAD_DIM]
        k_scr[hh] = kh * _inv_rms(kh, DSWA_HEAD_DIM) * gk
        v_scr[hh] = u[:, 2 * n + hh * DSWA_HEAD_DIM:2 * n + (hh + 1) * DSWA_HEAD_DIM]
    for gi, (_, dil) in enumerate(DSWA_GROUPS):
        for src, dst in zip((q_scr, k_scr, v_scr), out_refs[3 * gi:3 * gi + 3]):
            for hh in range(DSWA_HEADS_PER_GROUP):
                head = gi * DSWA_HEADS_PER_GROUP + hh
                cols = slice(hh * DSWA_HEAD_DIM, (hh + 1) * DSWA_HEAD_DIM)
                for r in range(dil):
                    rows = pl.ds(r, tm // dil, stride=dil) if dil > 1 else slice(None)
                    dst[r, :, cols] = src[head, rows, :].astype(BF16)


def _dswa_proj(x, g, w, gq, gk, *, batch, seq, tm):
    t, d = x.shape
    n = DSWA_HEADS * DSWA_HEAD_DIM
    gw = DSWA_HEADS_PER_GROUP * DSWA_HEAD_DIM
    nblk = seq // tm
    out_specs, out_shape = [], []
    for _, dil in DSWA_GROUPS:
        assert tm % (16 * dil) == 0
        out_specs += [pl.BlockSpec((None, dil, tm // dil, gw), lambda b, i: (b, 0, i, 0))] * 3
        out_shape += [jax.ShapeDtypeStruct((batch, dil, seq // dil, gw), BF16)] * 3
    return pl.pallas_call(
        functools.partial(_dswa_proj_kernel, tm=tm),
        grid=(batch, nblk),
        in_specs=[pl.BlockSpec((tm, d), lambda b, i: (b * nblk + i, 0)), _const_spec((1, d)), _const_spec(w.shape),
                  _const_spec(gq.shape), _const_spec(gk.shape)],
        out_specs=out_specs,
        out_shape=out_shape,
        scratch_shapes=[pltpu.VMEM((DSWA_HEADS, tm, DSWA_HEAD_DIM), F32)] * 3,
        compiler_params=_cparams(("parallel", "parallel"), 52),
        name="dswa_proj",
    )(x, g, w, gq, gk)


def _dswa_kernel(slope_ref, q_ref, kc_ref, kp_ref, vc_ref, vp_ref, o_ref, l_ref, *, dilation, group, nb):
    w = DSWA_W
    ib = pl.program_id(2)
    i = lax.broadcasted_iota(jnp.int32, (w, 2 * w), 0)
    j = lax.broadcasted_iota(jnp.int32, (w, 2 * w), 1)
    steps = w + i - j
    dist = (steps * dilation).astype(F32)
    in_window = (steps >= 0) & (steps <= w)
    for hh in range(DSWA_HEADS_PER_GROUP):
        cols = slice(hh * DSWA_HEAD_DIM, (hh + 1) * DSWA_HEAD_DIM)
        bias_all = jnp.where(in_window, -slope_ref[group * DSWA_HEADS_PER_GROUP + hh] * dist, NEG_INF)
        bias_first = jnp.where(j >= w, bias_all, NEG_INF)
        for jb in range(nb):
            rows = slice(jb * w, (jb + 1) * w)
            if jb == 0:
                kprev, vprev = kp_ref[:, cols], vp_ref[:, cols]
                bias_j = jnp.where(ib > 0, bias_all, bias_first)
            else:
                prows = slice((jb - 1) * w, jb * w)
                kprev, vprev = kc_ref[prows, cols], vc_ref[prows, cols]
                bias_j = bias_all
            keys = jnp.concatenate([kprev, kc_ref[rows, cols]], axis=0)
            vals = jnp.concatenate([vprev, vc_ref[rows, cols]], axis=0)
            sc = lax.dot_general(q_ref[rows, cols], keys, (((1,), (1,)), ((), ())), preferred_element_type=F32)
            sc = jnp.where(bias_j > 0.5 * NEG_INF, sc + bias_j, NEG_INF)
            m = jnp.max(sc, axis=-1, keepdims=True)
            p = jnp.exp(sc - m)
            lsum = jnp.sum(p, axis=-1, keepdims=True)
            o_ref[rows, cols] = jnp.dot(p.astype(BF16), vals, preferred_element_type=F32) / lsum
            l_ref[rows, cols] = jnp.broadcast_to(m + jnp.log(lsum), (w, DSWA_HEAD_DIM))


def _dswa_group(slopes, qg, kg, vg, *, group):
    window, dilation = DSWA_GROUPS[group]
    assert window // dilation == DSWA_W
    batch, _, sd, gw = qg.shape
    tq = min(1024, sd)
    nb = tq // DSWA_W
    cur = pl.BlockSpec((None, None, tq, gw), lambda b, r, i: (b, r, i, 0))
    prev = pl.BlockSpec((None, None, DSWA_W, gw), lambda b, r, i: (b, r, jnp.maximum(i * nb - 1, 0), 0))
    return pl.pallas_call(
        functools.partial(_dswa_kernel, dilation=dilation, group=group, nb=nb),
        grid=(batch, dilation, sd // tq),
        in_specs=[pl.BlockSpec(memory_space=pltpu.SMEM), cur, cur, prev, cur, prev],
        out_specs=[cur] * 2,
        out_shape=[jax.ShapeDtypeStruct(qg.shape, F32)] * 2,
        compiler_params=_cparams(("parallel", "parallel", "arbitrary"), 32),
        name="dswa_g%d" % group,
    )(slopes, qg, kg, kg, vg, vg)


def _dswa_combine_kernel(*refs, tm):
    ng = len(DSWA_GROUPS)
    in_refs, y_ref, scr = refs[:2 * ng], refs[2 * ng], refs[2 * ng + 1]
    vals = []
    for gi, (_, dil) in enumerate(DSWA_GROUPS):
        for which in range(2):
            src = in_refs[2 * gi + which]
            if dil == 1:
                vals.append(src[0])
            else:
                halves = []
                for hh in range(DSWA_HEADS_PER_GROUP):
                    slot = (2 * gi + which) * DSWA_HEADS_PER_GROUP + hh
                    for r in range(dil):
                        scr[slot, pl.ds(r, tm // dil, stride=dil), :] = (
                            src[r, :, hh * DSWA_HEAD_DIM:(hh + 1) * DSWA_HEAD_DIM])
                    halves.append(scr[slot])
                vals.append(jnp.concatenate(halves, axis=1))
    lses = vals[1::2]
    mx = functools.reduce(jnp.maximum, lses)
    es = [jnp.exp(l - mx) for l in lses]
    den = functools.reduce(lambda a, b: a + b, es)
    y = functools.reduce(lambda a, b: a + b, [(e / den) * o for e, o in zip(es, vals[0::2])])
    y_ref[...] = y.astype(y_ref.dtype)


def _dswa_combine(outs_lses, *, batch, seq, tm):
    gw = DSWA_HEADS_PER_GROUP * DSWA_HEAD_DIM
    nblk = seq // tm
    in_specs = []
    for _, dil in DSWA_GROUPS:
        in_specs += [pl.BlockSpec((None, dil, tm // dil, gw), lambda b, i: (b, 0, i, 0))] * 2
    return pl.pallas_call(
        functools.partial(_dswa_combine_kernel, tm=tm),
        grid=(batch, nblk),
        in_specs=in_specs,
        out_specs=pl.BlockSpec((tm, gw), lambda b, i: (b * nblk + i, 0)),
        out_shape=jax.ShapeDtypeStruct((batch * seq, gw), BF16),
        scratch_shapes=[pltpu.VMEM((2 * len(DSWA_GROUPS) * DSWA_HEADS_PER_GROUP, tm, DSWA_HEAD_DIM), F32)],
        compiler_params=_cparams(("parallel", "parallel"), 32),
        name="dswa_combine",
    )(*outs_lses)


def _merge_kernel(x_ref, g_ref, ya_ref, yb_ref, yc_ref, wga_ref, wgb_ref, wgc_ref, wa_ref, wb_ref, wc_ref, wo_ref,
                  out_ref, h_scr):
    @pl.when(pl.program_id(1) == 0)
    def _():
        xf = x_ref[...]
        h_scr[...] = (xf * _inv_rms(xf, xf.shape[-1]) * g_ref[...]).astype(BF16)
        out_ref[...] = xf

    h = h_scr[...]
    dot = lambda a, b: jnp.dot(a, b, preferred_element_type=F32)
    inter = (jax.nn.sigmoid(dot(h, wga_ref[...])) * dot(ya_ref[...], wa_ref[...])
             + jax.nn.sigmoid(dot(h, wgb_ref[...])) * dot(yb_ref[...], wb_ref[...])
             + jax.nn.sigmoid(dot(h, wgc_ref[...])) * dot(yc_ref[...], wc_ref[...]))
    out_ref[...] += dot(inter.astype(BF16), wo_ref[...])


def _merge(x, g, ya, yb, yc, wga, wgb, wgc, wa, wb, wc, wo, *, tm, tn):
    t, d = x.shape
    row = lambda i, j: (i, 0)
    colw = lambda i, j: (0, j)
    return pl.pallas_call(
        _merge_kernel,
        grid=(t // tm, d // tn),
        in_specs=[pl.BlockSpec((tm, d), row), pl.BlockSpec((1, d), lambda i, j: (0, 0)),
                  pl.BlockSpec((tm, ya.shape[1]), row), pl.BlockSpec((tm, yb.shape[1]), row),
                  pl.BlockSpec((tm, yc.shape[1]), row)]
                 + [pl.BlockSpec((d, tn), colw)] * 3
                 + [pl.BlockSpec((wa.shape[0], tn), colw), pl.BlockSpec((wb.shape[0], tn), colw),
                    pl.BlockSpec((wc.shape[0], tn), colw), pl.BlockSpec((tn, d), lambda i, j: (j, 0))],
        out_specs=pl.BlockSpec((tm, d), row),
        out_shape=jax.ShapeDtypeStruct((t, d), F32),
        scratch_shapes=[pltpu.VMEM((tm, d), BF16)],
        compiler_params=_cparams(("parallel", "arbitrary"), 52),
        name="merge",
    )(x, g, ya, yb, yc, wga, wgb, wgc, wa, wb, wc, wo)


def _pad_cols(w, n):
    return jnp.pad(w, ((0, 0), (0, n - w.shape[1])))


def _rot_half_cols(w):
    half = w.shape[-1] // 2
    return jnp.concatenate([-w[..., half:], w[..., :half]], axis=-1)


def _swap_half(g):
    half = g.shape[-1] // 2
    return jnp.concatenate([g[..., half:], g[..., :half]], axis=-1)


def _gain3(g):
    pad = lambda v: jnp.pad(v, (0, LANE - v.shape[0]))
    return jnp.stack([g[:MLA_NOPE], pad(g[MLA_NOPE:]), pad(_swap_half(g[MLA_NOPE:]))], axis=0)


def kernel(x, positions, ffn1_norm, ffn1_w_gate, ffn1_w_up, ffn1_w_down, mix_norm, w_in, mla_cq_norm, mla_ckv_norm, mla_w_uq, mla_w_ukv, mla_q_norm, mla_k_norm, gla_w_gate2, gla_b_gate2, gla_o_norm, dswa_q_norm, dswa_k_norm, w_branch_a, w_branch_b, w_branch_c, w_out, ffn2_norm, ffn2_w_gate, ffn2_w_up, ffn2_w_down):
    batch, seq, d = x.shape
    depth = w_in.shape[0]
    d_ff = ffn1_w_gate.shape[-1]
    t = batch * seq
    tm = 512
    tm_ffn = 1024
    tf = 512
    ff_pad = -(-d_ff // tf) * tf
    assert seq % max(wd for wd, _ in DSWA_GROUPS) == 0 and seq % tm == 0 and t % tm_ffn == 0

    xf = x.reshape(t, d)
    half = MLA_ROPE // 2
    inv_freq = ROPE_THETA ** (-jnp.arange(half, dtype=F32) / half)
    inv_pad = jnp.pad(jnp.concatenate([inv_freq, inv_freq]), (0, LANE - MLA_ROPE)).reshape(1, LANE)
    cos, sin = _rope_tables(positions.reshape(t, 1), inv_pad, tm=tm)
    slopes = 2.0 ** (-ALIBI_MAX_EXP * jnp.arange(1, DSWA_HEADS + 1, dtype=F32) / DSWA_HEADS)

    o_cq, o_ckv, o_kr = 0, MLA_Q_RANK, MLA_Q_RANK + MLA_KV_RANK
    o_gla = o_kr + MLA_ROPE
    n_qkv_b = GLA_HEADS * (2 * GLA_DK + GLA_DV)
    o_glr = o_gla + n_qkv_b
    o_rb = o_glr + GLA_GATE_RANK
    o_c = o_rb + GLA_HEADS * GLA_DV
    o_ga = o_c + 3 * DSWA_HEADS * DSWA_HEAD_DIM
    o_gb, o_gc = o_ga + d, o_ga + 2 * d

    def ffn_weights(wg, wu, wdn):
        padc = lambda w: jnp.pad(w, ((0, 0), (0, ff_pad - d_ff))).astype(BF16)
        return padc(wg), padc(wu), jnp.pad(wdn, ((0, ff_pad - d_ff), (0, 0))).astype(BF16)

    for l in range(depth):
        wg, wu, wdn = ffn_weights(ffn1_w_gate[l], ffn1_w_up[l], ffn1_w_down[l])
        xf = _ffn(xf, ffn1_norm[l].reshape(1, d), wg, wu, wdn, tm=tm_ffn, tf=tf)

        wl = w_in[l]
        gmix = mix_norm[l].reshape(1, d)
        w_kr = wl[:, o_kr:o_kr + MLA_ROPE]
        w1 = jnp.concatenate([wl[:, o_cq:o_kr], _pad_cols(w_kr, LANE), _pad_cols(_rot_half_cols(w_kr), LANE)],
                             axis=1).astype(BF16)
        wq3 = mla_w_uq[l].reshape(MLA_Q_RANK, MLA_HEADS, MLA_QK)
        wq_rope = wq3[..., MLA_NOPE:]
        zpad = jnp.zeros((MLA_Q_RANK, MLA_HEADS, LANE - MLA_ROPE), F32)
        wuq = jnp.concatenate([wq3[..., :MLA_NOPE], wq_rope, zpad, _rot_half_cols(wq_rope), zpad], axis=-1)
        wuq = wuq.reshape(MLA_Q_RANK, MLA_HEADS * 3 * LANE).astype(BF16)
        wkv3 = mla_w_ukv[l].reshape(MLA_KV_RANK, MLA_HEADS, MLA_NOPE + MLA_V)
        wukv = jnp.concatenate([wkv3[..., :MLA_NOPE].reshape(MLA_KV_RANK, -1),
                                wkv3[..., MLA_NOPE:].reshape(MLA_KV_RANK, -1)], axis=1).astype(BF16)
        qa, ka, va = _mla_prep(xf, gmix, cos, sin, w1, mla_cq_norm[l].reshape(1, -1), mla_ckv_norm[l].reshape(1, -1),
                               wuq, wukv, _gain3(mla_q_norm[l]), _gain3(mla_k_norm[l]), tm=tm)
        y_a = _flash(qa, ka, va, batch=batch, seq=seq, tq=1024)

        w_gla = jnp.concatenate([wl[:, o_gla:o_glr], _pad_cols(wl[:, o_glr:o_rb], LANE), wl[:, o_rb:o_c]],
                                axis=1).astype(BF16)
        w2 = jnp.pad(gla_w_gate2[l], ((0, LANE - GLA_GATE_RANK), (0, 0))).astype(BF16)
        qb, kb, vb, la, sr = _gla_proj(xf, gmix, w_gla, w2, gla_b_gate2[l].reshape(1, -1), tm=tm)
        y_b = _gla(qb, kb, vb, la, sr, gla_o_norm[l].reshape(1, -1), batch=batch, seq=seq, tb=512)

        qkv_c = _dswa_proj(xf, gmix, wl[:, o_c:o_ga].astype(BF16), dswa_q_norm[l].reshape(1, -1),
                           dswa_k_norm[l].reshape(1, -1), batch=batch, seq=seq, tm=tm)
        outs_lses = []
        for gi in range(len(DSWA_GROUPS)):
            outs_lses += _dswa_group(slopes, *qkv_c[3 * gi:3 * gi + 3], group=gi)
        y_c = _dswa_combine(outs_lses, batch=batch, seq=seq, tm=tm)

        xf = _merge(xf, gmix, y_a, y_b, y_c,
                    wl[:, o_ga:o_gb].astype(BF16), wl[:, o_gb:o_gc].astype(BF16), wl[:, o_gc:o_gc + d].astype(BF16),
                    w_branch_a[l].astype(BF16), w_branch_b[l].astype(BF16), w_branch_c[l].astype(BF16),
                    w_out[l].astype(BF16), tm=tm, tn=512)

        wg, wu, wdn = ffn_weights(ffn2_w_gate[l], ffn2_w_up[l], ffn2_w_down[l])
        xf = _ffn(xf, ffn2_norm[l].reshape(1, d), wg, wu, wdn, tm=tm_ffn, tf=tf)
    return xf.reshape(batch, seq, d)
```

```python
import functools

import numpy as np
import jax
import jax.numpy as jnp
from jax import lax
from jax.experimental import pallas as pl
from jax.experimental.pallas import tpu as pltpu

F32 = jnp.float32
BF16 = jnp.bfloat16

MLA_HEADS = 6
MLA_Q_RANK = 512
MLA_KV_RANK = 256
MLA_NOPE = 128
MLA_ROPE = 64
MLA_V = 128
MLA_QK = MLA_NOPE + MLA_ROPE
MLA_QK_PAD = 256
GLA_HEADS = 4
GLA_DK = 64
GLA_DV = 128
GLA_GATE_RANK = 16
GLA_TAU = 16.0
GLA_CHUNK = 64
GLA_SUB = 8
GLA_LEVELS = (32, 16, 8)
DSWA_GROUPS = ((128, 1), (512, 4), (2048, 16))
DSWA_HEADS_PER_GROUP = 2
DSWA_HEADS = 6
DSWA_HEAD_DIM = 128
DSWA_W = 128
ROPE_THETA = 10000.0
ALIBI_MAX_EXP = 8.0
NORM_EPS = 1e-6
NEG_INF = -1e30
LOG2E = 1.4426950408889634

LANE = 128
MIB = 1024 * 1024


def _cparams(sem, vmem_mib):
    return pltpu.CompilerParams(dimension_semantics=sem, vmem_limit_bytes=int(vmem_mib * MIB))


def _inv_rms(xf, n):
    return lax.rsqrt(jnp.sum(xf * xf, axis=-1, keepdims=True) / n + NORM_EPS)


def _const_spec(shape):
    nd = len(shape)
    return pl.BlockSpec(shape, lambda *_: (0,) * nd)


def _layer_spec(arr, layer):
    nd = arr.ndim - 1
    return pl.BlockSpec((None,) + arr.shape[1:], lambda *_: (layer,) + (0,) * nd)


def _ffn_kernel(x_ref, g_ref, wg_ref, wu_ref, wd_ref, o_ref, h_scr):
    @pl.when(pl.program_id(1) == 0)
    def _():
        xf = x_ref[...]
        h_scr[...] = (xf * _inv_rms(xf, xf.shape[-1]) * g_ref[...]).astype(BF16)
        o_ref[...] = xf

    h = h_scr[...]
    gate = jnp.dot(h, wg_ref[...], preferred_element_type=F32)
    up = jnp.dot(h, wu_ref[...], preferred_element_type=F32)
    inter = (gate * jax.nn.sigmoid(gate) * up).astype(BF16)
    o_ref[...] += 0.5 * jnp.dot(inter, wd_ref[...], preferred_element_type=F32)


def _ffn(x, g, wg, wu, wd, layer, *, tm, tf):
    t, d = x.shape
    fp = wg.shape[-1]
    return pl.pallas_call(
        _ffn_kernel,
        grid=(t // tm, fp // tf),
        in_specs=[
            pl.BlockSpec((tm, d), lambda i, j: (i, 0)),
            _layer_spec(g, layer),
            pl.BlockSpec((None, d, tf), lambda i, j: (layer, 0, j)),
            pl.BlockSpec((None, d, tf), lambda i, j: (layer, 0, j)),
            pl.BlockSpec((None, tf, d), lambda i, j: (layer, j, 0)),
        ],
        out_specs=pl.BlockSpec((tm, d), lambda i, j: (i, 0)),
        out_shape=jax.ShapeDtypeStruct((t, d), F32),
        scratch_shapes=[pltpu.VMEM((tm, d), BF16)],
        compiler_params=_cparams(("parallel", "arbitrary"), 57),
        name="ffn",
    )(x, g, wg, wu, wd)


def _rope_kernel(pos_ref, inv_ref, cos_ref, sin_ref):
    ang = pos_ref[...].astype(F32) * inv_ref[...]
    cos_ref[...] = jnp.cos(ang)
    sin_ref[...] = jnp.sin(ang)


def _rope_tables(pos_col, inv_pad, *, tm):
    t = pos_col.shape[0]
    return pl.pallas_call(
        _rope_kernel,
        grid=(t // tm,),
        in_specs=[pl.BlockSpec((tm, 1), lambda i: (i, 0)), _const_spec((1, LANE))],
        out_specs=[pl.BlockSpec((tm, LANE), lambda i: (i, 0))] * 2,
        out_shape=[jax.ShapeDtypeStruct((t, LANE), F32)] * 2,
        compiler_params=_cparams(("parallel",), 32),
        name="rope_tables",
    )(pos_col, inv_pad)


def _mla_prep_kernel(x_ref, g_ref, cos_ref, sin_ref, w1_ref, gcq_ref, gckv_ref, wuq_ref, wukv_ref,
                     gq_ref, gk_ref, q_ref, k_ref, v_ref):
    xf = x_ref[...]
    h = (xf * _inv_rms(xf, xf.shape[-1]) * g_ref[...]).astype(BF16)
    u = jnp.dot(h, w1_ref[...], preferred_element_type=F32)
    c_q = u[:, :MLA_Q_RANK]
    c_kv = u[:, MLA_Q_RANK:MLA_Q_RANK + MLA_KV_RANK]
    kr = u[:, MLA_Q_RANK + MLA_KV_RANK:MLA_Q_RANK + MLA_KV_RANK + LANE]
    krot = u[:, MLA_Q_RANK + MLA_KV_RANK + LANE:]
    c_q = (c_q * _inv_rms(c_q, MLA_Q_RANK) * gcq_ref[...]).astype(BF16)
    c_kv = (c_kv * _inv_rms(c_kv, MLA_KV_RANK) * gckv_ref[...]).astype(BF16)
    qall = jnp.dot(c_q, wuq_ref[...], preferred_element_type=F32)
    kv = jnp.dot(c_kv, wukv_ref[...], preferred_element_type=F32)
    cos = cos_ref[...]
    sin = sin_ref[...]
    scale = MLA_QK ** -0.5 * LOG2E
    gq_n, gq_r, gq_t = gq_ref[0:1, :], gq_ref[1:2, :], gq_ref[2:3, :]
    gk_n, gk_r, gk_t = gk_ref[0:1, :], gk_ref[1:2, :], gk_ref[2:3, :]
    k_roped = kr * gk_r * cos + krot * gk_t * sin
    kr_ss = jnp.sum(kr * kr, axis=-1, keepdims=True)
    for hh in range(MLA_HEADS):
        base = hh * 3 * LANE
        nope = qall[:, base:base + LANE]
        rope = qall[:, base + LANE:base + 2 * LANE]
        rot = qall[:, base + 2 * LANE:base + 3 * LANE]
        ss = jnp.sum(nope * nope, axis=-1, keepdims=True) + jnp.sum(rope * rope, axis=-1, keepdims=True)
        r = lax.rsqrt(ss / MLA_QK + NORM_EPS) * scale
        q_ref[hh, :, 0:LANE] = (nope * r * gq_n).astype(BF16)
        q_ref[hh, :, LANE:2 * LANE] = ((rope * gq_r * cos + rot * gq_t * sin) * r).astype(BF16)
        kn = kv[:, hh * LANE:(hh + 1) * LANE]
        rk = lax.rsqrt((jnp.sum(kn * kn, axis=-1, keepdims=True) + kr_ss) / MLA_QK + NORM_EPS)
        k_ref[hh, :, 0:LANE] = (kn * rk * gk_n).astype(BF16)
        k_ref[hh, :, LANE:2 * LANE] = (k_roped * rk).astype(BF16)
        v_ref[hh] = kv[:, (MLA_HEADS + hh) * LANE:(MLA_HEADS + hh + 1) * LANE].astype(BF16)


def _mla_prep(x, g, cos, sin, w1, gcq, gckv, wuq, wukv, gq3, gk3, layer, *, tm):
    t, d = x.shape
    row = lambda i: (i, 0)
    hrow = lambda i: (0, i, 0)
    return pl.pallas_call(
        _mla_prep_kernel,
        grid=(t // tm,),
        in_specs=[pl.BlockSpec((tm, d), row), _layer_spec(g, layer),
                  pl.BlockSpec((tm, LANE), row), pl.BlockSpec((tm, LANE), row)]
                 + [_layer_spec(p, layer) for p in (w1, gcq, gckv, wuq, wukv, gq3, gk3)],
        out_specs=[
            pl.BlockSpec((MLA_HEADS, tm, MLA_QK_PAD), hrow),
            pl.BlockSpec((MLA_HEADS, tm, MLA_QK_PAD), hrow),
            pl.BlockSpec((MLA_HEADS, tm, MLA_V), hrow),
        ],
        out_shape=[
            jax.ShapeDtypeStruct((MLA_HEADS, t, MLA_QK_PAD), BF16),
            jax.ShapeDtypeStruct((MLA_HEADS, t, MLA_QK_PAD), BF16),
            jax.ShapeDtypeStruct((MLA_HEADS, t, MLA_V), BF16),
        ],
        compiler_params=_cparams(("parallel",), 52),
        name="mla_prep",
    )(x, g, cos, sin, w1, gcq, gckv, wuq, wukv, gq3, gk3)


def _flash_kernel(q_ref, k_ref, v_ref, o_ref, sa_scr, sb_scr, m_scr, acc_scr, *, tq):
    qi = pl.program_id(2)
    m_scr[...] = jnp.full_like(m_scr, NEG_INF)
    acc_scr[...] = jnp.zeros_like(acc_scr)
    q = q_ref[...]
    ones = jnp.ones((tq, MLA_V), BF16)

    def scores(tile, s_scr):
        r0 = pl.multiple_of(tile * tq, tq)
        s_scr[...] = lax.dot_general(q, k_ref[pl.ds(r0, tq), :], (((1,), (1,)), ((), ())),
                                     preferred_element_type=F32)

    def accumulate(tile, s_scr, diagonal):
        r0 = pl.multiple_of(tile * tq, tq)
        sc = s_scr[...]
        if diagonal:
            row = lax.broadcasted_iota(jnp.int32, sc.shape, 0)
            col = lax.broadcasted_iota(jnp.int32, sc.shape, 1)
            sc = jnp.where(col <= row, sc, NEG_INF)
        m_prev = m_scr[...]
        m_new = jnp.maximum(m_prev, jnp.max(sc, axis=-1, keepdims=True))
        alpha = jnp.exp2(m_prev - m_new)
        p = jnp.exp2(sc - jnp.tile(m_new, (1, tq // LANE)))
        v1 = jnp.concatenate([v_ref[pl.ds(r0, tq), :], ones], axis=1)
        pv = jnp.dot(p.astype(BF16), v1, preferred_element_type=F32)
        acc_scr[...] = jnp.tile(alpha, (1, 2)) * acc_scr[...] + pv
        m_scr[...] = m_new

    scores(0, sa_scr)

    def body(j, carry):
        scores(2 * j + 1, sb_scr)
        accumulate(2 * j, sa_scr, False)
        scores(2 * j + 2, sa_scr)
        accumulate(2 * j + 1, sb_scr, False)
        return carry

    lax.fori_loop(0, qi // 2, body, 0)

    @pl.when(qi % 2 == 0)
    def _():
        accumulate(qi, sa_scr, True)

    @pl.when(qi % 2 == 1)
    def _():
        scores(qi, sb_scr)
        accumulate(qi - 1, sa_scr, False)
        accumulate(qi, sb_scr, True)

    acc = acc_scr[...]
    o_ref[...] = (acc[:, 0:MLA_V] / acc[:, MLA_V:]).astype(o_ref.dtype)


def _flash(q, k, v, *, batch, seq, tq):
    assert seq % tq == 0
    nq = seq // tq
    t = batch * seq
    return pl.pallas_call(
        functools.partial(_flash_kernel, tq=tq),
        grid=(batch, MLA_HEADS, nq),
        in_specs=[
            pl.BlockSpec((None, tq, MLA_QK_PAD), lambda b, h, i: (h, b * nq + i, 0)),
            pl.BlockSpec((None, seq, MLA_QK_PAD), lambda b, h, i: (h, b, 0)),
            pl.BlockSpec((None, seq, MLA_V), lambda b, h, i: (h, b, 0)),
        ],
        out_specs=pl.BlockSpec((tq, MLA_V), lambda b, h, i: (b * nq + i, h)),
        out_shape=jax.ShapeDtypeStruct((t, MLA_HEADS * MLA_V), BF16),
        scratch_shapes=[pltpu.VMEM((tq, tq), F32), pltpu.VMEM((tq, tq), F32),
                        pltpu.VMEM((tq, MLA_V), F32), pltpu.VMEM((tq, 2 * MLA_V), F32)],
        compiler_params=_cparams(("parallel", "parallel", "arbitrary"), 52),
        name="mla_flash",
    )(q, k, v)


def _gla_proj_kernel(x_ref, g_ref, w_ref, w2_ref, b2_ref, q_ref, k_ref, v_ref, la_ref, sr_ref):
    xf = x_ref[...]
    h = (xf * _inv_rms(xf, xf.shape[-1]) * g_ref[...]).astype(BF16)
    u = jnp.dot(h, w_ref[...], preferred_element_type=F32)
    nk = GLA_HEADS * GLA_DK
    nv = GLA_HEADS * GLA_DV
    q_ref[...] = u[:, 0:nk] * (GLA_DK ** -0.5)
    k_ref[...] = u[:, nk:2 * nk]
    v_ref[...] = u[:, 2 * nk:2 * nk + nv]
    gate_lr = u[:, 2 * nk + nv:2 * nk + nv + LANE].astype(BF16)
    z = jnp.dot(gate_lr, w2_ref[...], preferred_element_type=F32) + b2_ref[...]
    la_ref[...] = -(jnp.maximum(-z, 0.0) + jnp.log1p(jnp.exp(-jnp.abs(z)))) / GLA_TAU
    r = u[:, 2 * nk + nv + LANE:]
    sr_ref[...] = r * jax.nn.sigmoid(r)


def _gla_proj(x, g, w, w2, b2, layer, *, tm):
    t, d = x.shape
    nk = GLA_HEADS * GLA_DK
    nv = GLA_HEADS * GLA_DV
    row = lambda i: (i, 0)
    return pl.pallas_call(
        _gla_proj_kernel,
        grid=(t // tm,),
        in_specs=[pl.BlockSpec((tm, d), row)] + [_layer_spec(p, layer) for p in (g, w, w2, b2)],
        out_specs=[pl.BlockSpec((tm, nk), row), pl.BlockSpec((tm, nk), row), pl.BlockSpec((tm, nv), row),
                   pl.BlockSpec((tm, nk), row), pl.BlockSpec((tm, nv), row)],
        out_shape=[jax.ShapeDtypeStruct((t, nk), F32), jax.ShapeDtypeStruct((t, nk), F32),
                   jax.ShapeDtypeStruct((t, nv), F32), jax.ShapeDtypeStruct((t, nk), F32),
                   jax.ShapeDtypeStruct((t, nv), F32)],
        compiler_params=_cparams(("parallel",), 52),
        name="gla_proj",
    )(x, g, w, w2, b2)


def _gla_constants():
    c = GLA_CHUNK
    idx = np.arange(c)
    mats = [(idx[None, :] <= idx[:, None])]
    for lvl in GLA_LEVELS:
        p = ((idx // lvl) | 1) * lvl
        qrow = (idx >= p)[:, None] & (idx[None, :] >= p[:, None]) & (idx[None, :] <= idx[:, None])
        krow = (idx < p)[:, None] & (idx[None, :] > idx[:, None]) & (idx[None, :] < p[:, None])
        mats.append(qrow | krow)
    mats.append(idx[None, :] > idx[:, None])
    mstack = np.concatenate(mats, axis=0).astype(np.float32)
    lmask = []
    for lvl in GLA_LEVELS:
        blk = idx // lvl
        m = ((blk[:, None] % 2) == 1) & (blk[None, :] == blk[:, None] - 1)
        lmask.append(np.tile(m.astype(np.float32), (GLA_HEADS, 1)))
    lmask = np.stack(lmask, axis=0)
    hk = np.arange(GLA_HEADS * GLA_DK) // GLA_DK
    hv = np.arange(GLA_HEADS * GLA_DV) // GLA_DV
    hrow = np.arange(GLA_HEADS * c) // c
    headmask = (hrow[:, None] == hk[None, :]).astype(np.float32)
    e_ind = (hk[:, None] == hv[None, :]).astype(np.float32)
    return mstack, lmask, headmask, e_ind


def _gla_kernel(q_ref, k_ref, v_ref, la_ref, sr_ref, go_ref, mst_ref, lmask_ref, hmask_ref, eind_ref, eindt_ref,
                y_ref, st_scr, *, chunks):
    c = GLA_CHUNK
    nk = GLA_HEADS * GLA_DK
    nv = GLA_HEADS * GLA_DV
    nsub = c // GLA_SUB

    @pl.when(pl.program_id(1) == 0)
    def _():
        st_scr[...] = jnp.zeros_like(st_scr)

    def chunk(ci, carry):
        r0 = pl.multiple_of(ci * c, c)
        q = q_ref[pl.ds(r0, c), :]
        k = k_ref[pl.ds(r0, c), :]
        v = v_ref[pl.ds(r0, c), :]
        a = la_ref[pl.ds(r0, c), :]
        a1 = a.astype(BF16)
        r1 = a - a1.astype(F32)
        a2 = r1.astype(BF16)
        a3 = (r1 - a2.astype(F32)).astype(BF16)
        mst = mst_ref[...]
        ex = (jnp.dot(mst, a1, preferred_element_type=F32) + jnp.dot(mst, a2, preferred_element_type=F32)
              + jnp.dot(mst, a3, preferred_element_type=F32))
        b = ex[0:c]
        f = jnp.exp(ex)
        v_bf = v.astype(BF16)

        amat = jnp.zeros((GLA_HEADS * c, c), F32)
        hmask = hmask_ref[...]
        for li in range(len(GLA_LEVELS)):
            fl = f[(li + 1) * c:(li + 2) * c]
            qt = q * fl
            kt = (k * fl).astype(BF16)
            qs = (jnp.concatenate([qt] * GLA_HEADS, axis=0) * hmask).astype(BF16)
            sc = lax.dot_general(qs, kt, (((1,), (1,)), ((), ())), preferred_element_type=F32)
            amat = amat + sc * lmask_ref[li]
        a_bf = amat.astype(BF16)
        o = jnp.concatenate(
            [jnp.dot(a_bf[hh * c:(hh + 1) * c], v_bf[:, hh * GLA_DV:(hh + 1) * GLA_DV], preferred_element_type=F32)
             for hh in range(GLA_HEADS)], axis=1)

        qb = (q * f[0:c]).astype(BF16)
        st = st_scr[...]
        o = o + lax.dot_general(qb, st.astype(BF16), (((1,), (1,)), ((), ())), preferred_element_type=F32)

        q3 = q.reshape(nsub, GLA_SUB, nk)
        k3 = k.reshape(nsub, GLA_SUB, nk)
        b3 = b.reshape(nsub, GLA_SUB, nk)
        v3 = v.reshape(nsub, GLA_SUB, nv)
        tt = lax.broadcasted_iota(jnp.int32, (nsub, GLA_SUB, nk), 1)
        od = jnp.zeros((nsub, GLA_SUB, nv), F32)
        eind = eind_ref[...]
        for s in range(GLA_SUB):
            w = jnp.exp(jnp.minimum(b3 - b3[:, s:s + 1, :], 0.0))
            x = jnp.where(tt >= s, q3 * k3[:, s:s + 1, :] * w, 0.0)
            rr = jnp.dot(x.reshape(c, nk).astype(BF16), eind, preferred_element_type=F32)
            od = od + rr.reshape(nsub, GLA_SUB, nv) * v3[:, s:s + 1, :]
        o = o + od.reshape(c, nv)

        kend = (k * f[(len(GLA_LEVELS) + 1) * c:(len(GLA_LEVELS) + 2) * c]).astype(BF16)
        upd = lax.dot_general(v_bf, kend, (((0,), (0,)), ((), ())), preferred_element_type=F32)
        st_scr[...] = st * f[c - 1:c] + upd * eindt_ref[...]

        go = go_ref[...]
        sr = sr_ref[pl.ds(r0, c), :]
        ys = []
        for hh in range(GLA_HEADS):
            oh = o[:, hh * GLA_DV:(hh + 1) * GLA_DV]
            ys.append(oh * _inv_rms(oh, GLA_DV) * go)
        y_ref[pl.ds(r0, c), :] = (jnp.concatenate(ys, axis=1) * sr).astype(y_ref.dtype)
        return carry

    lax.fori_loop(0, chunks, chunk, 0, unroll=2)


def _gla(q, k, v, la, sr, go, layer, *, batch, seq, tb):
    nk = GLA_HEADS * GLA_DK
    nv = GLA_HEADS * GLA_DV
    nb = seq // tb
    mstack, lmask, headmask, e_ind = _gla_constants()
    consts = [jnp.asarray(mstack, BF16), jnp.asarray(lmask, F32), jnp.asarray(headmask, F32),
              jnp.asarray(e_ind, BF16), jnp.asarray(e_ind.T, F32)]
    row = lambda b, i: (b * nb + i, 0)
    return pl.pallas_call(
        functools.partial(_gla_kernel, chunks=tb // GLA_CHUNK),
        grid=(batch, nb),
        in_specs=[pl.BlockSpec((tb, nk), row), pl.BlockSpec((tb, nk), row), pl.BlockSpec((tb, nv), row),
                  pl.BlockSpec((tb, nk), row), pl.BlockSpec((tb, nv), row), _layer_spec(go, layer)]
                 + [_const_spec(cst.shape) for cst in consts],
        out_specs=pl.BlockSpec((tb, nv), row),
        out_shape=jax.ShapeDtypeStruct((batch * seq, nv), BF16),
        scratch_shapes=[pltpu.VMEM((nv, nk), F32)],
        compiler_params=_cparams(("parallel", "arbitrary"), 40),
        name="gla_scan",
    )(q, k, v, la, sr, go, *consts)


def _dswa_proj_kernel(x_ref, g_ref, w_ref, gq_ref, gk_ref, *refs, tm):
    ng = len(DSWA_GROUPS)
    out_refs = refs[:3 * ng]
    q_scr, k_scr, v_scr = refs[3 * ng:]
    xf = x_ref[...]
    h = (xf * _inv_rms(xf, xf.shape[-1]) * g_ref[...]).astype(BF16)
    u = jnp.dot(h, w_ref[...], preferred_element_type=F32)
    n = DSWA_HEADS * DSWA_HEAD_DIM
    gq = gq_ref[...]
    gk = gk_ref[...]
    scale = DSWA_HEAD_DIM ** -0.5
    for hh in range(DSWA_HEADS):
        sl = slice(hh * DSWA_HEAD_DIM, (hh + 1) * DSWA_HEAD_DIM)
        qh = u[:, sl]
        q_scr[hh] = qh * _inv_rms(qh, DSWA_HEAD_DIM) * gq * scale
        kh = u[:, n + hh * DSWA_HEAD_DIM:n + (hh + 1) * DSWA_HEAD_DIM]
        k_scr[hh] = kh * _inv_rms(kh, DSWA_HEAD_DIM) * gk
        v_scr[hh] = u[:, 2 * n + hh * DSWA_HEAD_DIM:2 * n + (hh + 1) * DSWA_HEAD_DIM]
    for gi, (_, dil) in enumerate(DSWA_GROUPS):
        for src, dst in zip((q_scr, k_scr, v_scr), out_refs[3 * gi:3 * gi + 3]):
            for hh in range(DSWA_HEADS_PER_GROUP):
                head = gi * DSWA_HEADS_PER_GROUP + hh
                cols = slice(hh * DSWA_HEAD_DIM, (hh + 1) * DSWA_HEAD_DIM)
                for r in range(dil):
                    rows = pl.ds(r, tm // dil, stride=dil) if dil > 1 else slice(None)
                    dst[r, :, cols] = src[head, rows, :].astype(BF16)


def _dswa_proj(x, g, w, gq, gk, layer, *, batch, seq, tm):
    t, d = x.shape
    n = DSWA_HEADS * DSWA_HEAD_DIM
    gw = DSWA_HEADS_PER_GROUP * DSWA_HEAD_DIM
    nblk = seq // tm
    out_specs, out_shape = [], []
    for _, dil in DSWA_GROUPS:
        assert tm % (16 * dil) == 0
        out_specs += [pl.BlockSpec((None, dil, tm // dil, gw), lambda b, i: (b, 0, i, 0))] * 3
        out_shape += [jax.ShapeDtypeStruct((batch, dil, seq // dil, gw), BF16)] * 3
    return pl.pallas_call(
        functools.partial(_dswa_proj_kernel, tm=tm),
        grid=(batch, nblk),
        in_specs=[pl.BlockSpec((tm, d), lambda b, i: (b * nblk + i, 0))]
                 + [_layer_spec(p, layer) for p in (g, w, gq, gk)],
        out_specs=out_specs,
        out_shape=out_shape,
        scratch_shapes=[pltpu.VMEM((DSWA_HEADS, tm, DSWA_HEAD_DIM), F32)] * 3,
        compiler_params=_cparams(("parallel", "parallel"), 52),
        name="dswa_proj",
    )(x, g, w, gq, gk)


def _dswa_kernel(slope_ref, q_ref, kc_ref, kp_ref, vc_ref, vp_ref, o_ref, l_ref, *, dilation, group, nb):
    w = DSWA_W
    ib = pl.program_id(2)
    i = lax.broadcasted_iota(jnp.int32, (w, 2 * w), 0)
    j = lax.broadcasted_iota(jnp.int32, (w, 2 * w), 1)
    steps = w + i - j
    dist = (steps * dilation).astype(F32)
    in_window = (steps >= 0) & (steps <= w)
    for hh in range(DSWA_HEADS_PER_GROUP):
        cols = slice(hh * DSWA_HEAD_DIM, (hh + 1) * DSWA_HEAD_DIM)
        bias_all = jnp.where(in_window, -slope_ref[group * DSWA_HEADS_PER_GROUP + hh] * dist, NEG_INF)
        bias_first = jnp.where(j >= w, bias_all, NEG_INF)
        for jb in range(nb):
            rows = slice(jb * w, (jb + 1) * w)
            if jb == 0:
                kprev, vprev = kp_ref[:, cols], vp_ref[:, cols]
                bias_j = jnp.where(ib > 0, bias_all, bias_first)
            else:
                prows = slice((jb - 1) * w, jb * w)
                kprev, vprev = kc_ref[prows, cols], vc_ref[prows, cols]
                bias_j = bias_all
            keys = jnp.concatenate([kprev, kc_ref[rows, cols]], axis=0)
            vals = jnp.concatenate([vprev, vc_ref[rows, cols]], axis=0)
            sc = lax.dot_general(q_ref[rows, cols], keys, (((1,), (1,)), ((), ())), preferred_element_type=F32)
            sc = jnp.where(bias_j > 0.5 * NEG_INF, sc + bias_j, NEG_INF)
            m = jnp.max(sc, axis=-1, keepdims=True)
            p = jnp.exp(sc - m)
            lsum = jnp.sum(p, axis=-1, keepdims=True)
            o_ref[rows, cols] = jnp.dot(p.astype(BF16), vals, preferred_element_type=F32) / lsum
            l_ref[rows, cols] = jnp.broadcast_to(m + jnp.log(lsum), (w, DSWA_HEAD_DIM))


def _dswa_group(slopes, qg, kg, vg, *, group):
    window, dilation = DSWA_GROUPS[group]
    assert window // dilation == DSWA_W
    batch, _, sd, gw = qg.shape
    tq = min(1024, sd)
    nb = tq // DSWA_W
    cur = pl.BlockSpec((None, None, tq, gw), lambda b, r, i: (b, r, i, 0))
    prev = pl.BlockSpec((None, None, DSWA_W, gw), lambda b, r, i: (b, r, jnp.maximum(i * nb - 1, 0), 0))
    return pl.pallas_call(
        functools.partial(_dswa_kernel, dilation=dilation, group=group, nb=nb),
        grid=(batch, dilation, sd // tq),
        in_specs=[pl.BlockSpec(memory_space=pltpu.SMEM), cur, cur, prev, cur, prev],
        out_specs=[cur] * 2,
        out_shape=[jax.ShapeDtypeStruct(qg.shape, F32)] * 2,
        compiler_params=_cparams(("parallel", "parallel", "arbitrary"), 32),
        name="dswa_g%d" % group,
    )(slopes, qg, kg, kg, vg, vg)


def _dswa_combine_kernel(*refs, tm):
    ng = len(DSWA_GROUPS)
    in_refs, y_ref, scr = refs[:2 * ng], refs[2 * ng], refs[2 * ng + 1]
    vals = []
    for gi, (_, dil) in enumerate(DSWA_GROUPS):
        for which in range(2):
            src = in_refs[2 * gi + which]
            if dil == 1:
                vals.append(src[0])
            else:
                halves = []
                for hh in range(DSWA_HEADS_PER_GROUP):
                    slot = (2 * gi + which) * DSWA_HEADS_PER_GROUP + hh
                    for r in range(dil):
                        scr[slot, pl.ds(r, tm // dil, stride=dil), :] = (
                            src[r, :, hh * DSWA_HEAD_DIM:(hh + 1) * DSWA_HEAD_DIM])
                    halves.append(scr[slot])
                vals.append(jnp.concatenate(halves, axis=1))
    lses = vals[1::2]
    mx = functools.reduce(jnp.maximum, lses)
    es = [jnp.exp(l - mx) for l in lses]
    den = functools.reduce(lambda a, b: a + b, es)
    y = functools.reduce(lambda a, b: a + b, [(e / den) * o for e, o in zip(es, vals[0::2])])
    y_ref[...] = y.astype(y_ref.dtype)


def _dswa_combine(outs_lses, *, batch, seq, tm):
    gw = DSWA_HEADS_PER_GROUP * DSWA_HEAD_DIM
    nblk = seq // tm
    in_specs = []
    for _, dil in DSWA_GROUPS:
        in_specs += [pl.BlockSpec((None, dil, tm // dil, gw), lambda b, i: (b, 0, i, 0))] * 2
    return pl.pallas_call(
        functools.partial(_dswa_combine_kernel, tm=tm),
        grid=(batch, nblk),
        in_specs=in_specs,
        out_specs=pl.BlockSpec((tm, gw), lambda b, i: (b * nblk + i, 0)),
        out_shape=jax.ShapeDtypeStruct((batch * seq, gw), BF16),
        scratch_shapes=[pltpu.VMEM((2 * len(DSWA_GROUPS) * DSWA_HEADS_PER_GROUP, tm, DSWA_HEAD_DIM), F32)],
        compiler_params=_cparams(("parallel", "parallel"), 32),
        name="dswa_combine",
    )(*outs_lses)


def _merge_kernel(x_ref, g_ref, ya_ref, yb_ref, yc_ref, wga_ref, wgb_ref, wgc_ref, wa_ref, wb_ref, wc_ref, wo_ref,
                  out_ref, h_scr):
    @pl.when(pl.program_id(1) == 0)
    def _():
        xf = x_ref[...]
        h_scr[...] = (xf * _inv_rms(xf, xf.shape[-1]) * g_ref[...]).astype(BF16)
        out_ref[...] = xf

    h = h_scr[...]
    dot = lambda a, b: jnp.dot(a, b, preferred_element_type=F32)
    inter = (jax.nn.sigmoid(dot(h, wga_ref[...])) * dot(ya_ref[...], wa_ref[...])
             + jax.nn.sigmoid(dot(h, wgb_ref[...])) * dot(yb_ref[...], wb_ref[...])
             + jax.nn.sigmoid(dot(h, wgc_ref[...])) * dot(yc_ref[...], wc_ref[...]))
    out_ref[...] += dot(inter.astype(BF16), wo_ref[...])


def _merge(x, g, ya, yb, yc, wga, wgb, wgc, wa, wb, wc, wo, layer, *, tm, tn):
    t, d = x.shape
    row = lambda i, j: (i, 0)
    colw = lambda i, j: (layer, 0, j)
    return pl.pallas_call(
        _merge_kernel,
        grid=(t // tm, d // tn),
        in_specs=[pl.BlockSpec((tm, d), row), _layer_spec(g, layer),
                  pl.BlockSpec((tm, ya.shape[1]), row), pl.BlockSpec((tm, yb.shape[1]), row),
                  pl.BlockSpec((tm, yc.shape[1]), row)]
                 + [pl.BlockSpec((None, w.shape[1], tn), colw) for w in (wga, wgb, wgc, wa, wb, wc)]
                 + [pl.BlockSpec((None, tn, d), lambda i, j: (layer, j, 0))],
        out_specs=pl.BlockSpec((tm, d), row),
        out_shape=jax.ShapeDtypeStruct((t, d), F32),
        scratch_shapes=[pltpu.VMEM((tm, d), BF16)],
        compiler_params=_cparams(("parallel", "arbitrary"), 52),
        name="merge",
    )(x, g, ya, yb, yc, wga, wgb, wgc, wa, wb, wc, wo)


def _pad_last(w, n):
    return jnp.pad(w, [(0, 0)] * (w.ndim - 1) + [(0, n - w.shape[-1])])


def _rot_half_cols(w):
    half = w.shape[-1] // 2
    return jnp.concatenate([-w[..., half:], w[..., :half]], axis=-1)


def _swap_half(g):
    half = g.shape[-1] // 2
    return jnp.concatenate([g[..., half:], g[..., :half]], axis=-1)


def _gain3(g):
    rope = g[:, MLA_NOPE:]
    return jnp.stack([g[:, :MLA_NOPE], _pad_last(rope, LANE), _pad_last(_swap_half(rope), LANE)], axis=1)


def _row(g):
    return g[:, None, :]


def kernel(x, positions, ffn1_norm, ffn1_w_gate, ffn1_w_up, ffn1_w_down, mix_norm, w_in, mla_cq_norm, mla_ckv_norm, mla_w_uq, mla_w_ukv, mla_q_norm, mla_k_norm, gla_w_gate2, gla_b_gate2, gla_o_norm, dswa_q_norm, dswa_k_norm, w_branch_a, w_branch_b, w_branch_c, w_out, ffn2_norm, ffn2_w_gate, ffn2_w_up, ffn2_w_down):
    batch, seq, d = x.shape
    depth = w_in.shape[0]
    d_ff = ffn1_w_gate.shape[-1]
    t = batch * seq
    tm = 512
    tm_ffn = 1024
    tf = 512
    ff_pad = -(-d_ff // tf) * tf
    assert seq % max(wd for wd, _ in DSWA_GROUPS) == 0 and seq % tm == 0 and t % tm_ffn == 0

    xf = x.reshape(t, d)
    half = MLA_ROPE // 2
    inv_freq = ROPE_THETA ** (-jnp.arange(half, dtype=F32) / half)
    inv_pad = jnp.pad(jnp.concatenate([inv_freq, inv_freq]), (0, LANE - MLA_ROPE)).reshape(1, LANE)
    cos, sin = _rope_tables(positions.reshape(t, 1), inv_pad, tm=tm)
    slopes = 2.0 ** (-ALIBI_MAX_EXP * jnp.arange(1, DSWA_HEADS + 1, dtype=F32) / DSWA_HEADS)

    o_cq, o_ckv, o_kr = 0, MLA_Q_RANK, MLA_Q_RANK + MLA_KV_RANK
    o_gla = o_kr + MLA_ROPE
    n_qkv_b = GLA_HEADS * (2 * GLA_DK + GLA_DV)
    o_glr = o_gla + n_qkv_b
    o_rb = o_glr + GLA_GATE_RANK
    o_c = o_rb + GLA_HEADS * GLA_DV
    o_ga = o_c + 3 * DSWA_HEADS * DSWA_HEAD_DIM
    o_gb, o_gc = o_ga + d, o_ga + 2 * d

    bf = lambda w: w.astype(BF16)

    def ffn_weights(wg, wu, wdn):
        pad_rows = jnp.pad(wdn, ((0, 0), (0, ff_pad - d_ff), (0, 0)))
        return bf(_pad_last(wg, ff_pad)), bf(_pad_last(wu, ff_pad)), bf(pad_rows)

    ffn1 = (_row(ffn1_norm),) + ffn_weights(ffn1_w_gate, ffn1_w_up, ffn1_w_down)
    ffn2 = (_row(ffn2_norm),) + ffn_weights(ffn2_w_gate, ffn2_w_up, ffn2_w_down)
    gmix = _row(mix_norm)
    w_kr = w_in[:, :, o_kr:o_kr + MLA_ROPE]
    w1 = bf(jnp.concatenate([w_in[:, :, o_cq:o_kr], _pad_last(w_kr, LANE), _pad_last(_rot_half_cols(w_kr), LANE)],
                            axis=-1))
    wq4 = mla_w_uq.reshape(depth, MLA_Q_RANK, MLA_HEADS, MLA_QK)
    wq_rope = wq4[..., MLA_NOPE:]
    wuq = jnp.concatenate([wq4[..., :MLA_NOPE], _pad_last(wq_rope, LANE), _pad_last(_rot_half_cols(wq_rope), LANE)],
                          axis=-1)
    wuq = bf(wuq.reshape(depth, MLA_Q_RANK, MLA_HEADS * 3 * LANE))
    wkv4 = mla_w_ukv.reshape(depth, MLA_KV_RANK, MLA_HEADS, MLA_NOPE + MLA_V)
    wukv = bf(jnp.concatenate([wkv4[..., :MLA_NOPE].reshape(depth, MLA_KV_RANK, -1),
                               wkv4[..., MLA_NOPE:].reshape(depth, MLA_KV_RANK, -1)], axis=-1))
    mla_params = (w1, _row(mla_cq_norm), _row(mla_ckv_norm), wuq, wukv, _gain3(mla_q_norm), _gain3(mla_k_norm))
    w_gla = bf(jnp.concatenate([w_in[:, :, o_gla:o_glr], _pad_last(w_in[:, :, o_glr:o_rb], LANE),
                                w_in[:, :, o_rb:o_c]], axis=-1))
    w2 = bf(jnp.pad(gla_w_gate2, ((0, 0), (0, LANE - GLA_GATE_RANK), (0, 0))))
    gla_params = (w_gla, w2, _row(gla_b_gate2))
    go = _row(gla_o_norm)
    dswa_params = (bf(w_in[:, :, o_c:o_ga]), _row(dswa_q_norm), _row(dswa_k_norm))
    merge_params = (bf(w_in[:, :, o_ga:o_gb]), bf(w_in[:, :, o_gb:o_gc]), bf(w_in[:, :, o_gc:o_gc + d]),
                    bf(w_branch_a), bf(w_branch_b), bf(w_branch_c), bf(w_out))

    for l in range(depth):
        xf = _ffn(xf, *ffn1, l, tm=tm_ffn, tf=tf)

        qa, ka, va = _mla_prep(xf, gmix, cos, sin, *mla_params, l, tm=tm)
        y_a = _flash(qa, ka, va, batch=batch, seq=seq, tq=1024)

        qb, kb, vb, la, sr = _gla_proj(xf, gmix, *gla_params, l, tm=tm)
        y_b = _gla(qb, kb, vb, la, sr, go, l, batch=batch, seq=seq, tb=512)

        qkv_c = _dswa_proj(xf, gmix, *dswa_params, l, batch=batch, seq=seq, tm=tm)
        outs_lses = []
        for gi in range(len(DSWA_GROUPS)):
            outs_lses += _dswa_group(slopes, *qkv_c[3 * gi:3 * gi + 3], group=gi)
        y_c = _dswa_combine(outs_lses, batch=batch, seq=seq, tm=tm)

        xf = _merge(xf, gmix, y_a, y_b, y_c, *merge_params, l, tm=tm, tn=512)
        xf = _ffn(xf, *ffn2, l, tm=tm_ffn, tf=tf)
    return xf.reshape(batch, seq, d)
```

```python
import functools

import numpy as np
import jax
import jax.numpy as jnp
from jax import lax
from jax.experimental import pallas as pl
from jax.experimental.pallas import tpu as pltpu

F32 = jnp.float32
BF16 = jnp.bfloat16

MLA_HEADS = 6
MLA_Q_RANK = 512
MLA_KV_RANK = 256
MLA_NOPE = 128
MLA_ROPE = 64
MLA_V = 128
MLA_QK = MLA_NOPE + MLA_ROPE
MLA_QK_PAD = 256
GLA_HEADS = 4
GLA_DK = 64
GLA_DV = 128
GLA_GATE_RANK = 16
GLA_TAU = 16.0
GLA_CHUNK = 64
GLA_SUB = 8
GLA_LEVELS = (32, 16, 8)
DSWA_GROUPS = ((128, 1), (512, 4), (2048, 16))
DSWA_HEADS_PER_GROUP = 2
DSWA_HEADS = 6
DSWA_HEAD_DIM = 128
DSWA_W = 128
ROPE_THETA = 10000.0
ALIBI_MAX_EXP = 8.0
NORM_EPS = 1e-6
NEG_INF = -1e30
LOG2E = 1.4426950408889634

LANE = 128
MIB = 1024 * 1024


def _cparams(sem, vmem_mib):
    return pltpu.CompilerParams(dimension_semantics=sem, vmem_limit_bytes=int(vmem_mib * MIB))


def _inv_rms(xf, n):
    return lax.rsqrt(jnp.sum(xf * xf, axis=-1, keepdims=True) / n + NORM_EPS)


def _const_spec(shape):
    nd = len(shape)
    return pl.BlockSpec(shape, lambda *_: (0,) * nd)


def _layer_spec(arr, layer):
    nd = arr.ndim - 1
    return pl.BlockSpec((None,) + arr.shape[1:], lambda *_: (layer,) + (0,) * nd)


def _ffn_kernel(x_ref, g_ref, wg_ref, wu_ref, wd_ref, o_ref, h_scr):
    @pl.when(pl.program_id(1) == 0)
    def _():
        xf = x_ref[...]
        h_scr[...] = (xf * _inv_rms(xf, xf.shape[-1]) * g_ref[...]).astype(BF16)
        o_ref[...] = xf

    h = h_scr[...]
    gate = jnp.dot(h, wg_ref[...], preferred_element_type=F32)
    up = jnp.dot(h, wu_ref[...], preferred_element_type=F32)
    inter = (gate * jax.nn.sigmoid(gate) * up).astype(BF16)
    o_ref[...] += 0.5 * jnp.dot(inter, wd_ref[...], preferred_element_type=F32)


def _ffn(x, g, wg, wu, wd, layer, *, tm, tf):
    t, d = x.shape
    fp = wg.shape[-1]
    return pl.pallas_call(
        _ffn_kernel,
        grid=(t // tm, fp // tf),
        in_specs=[
            pl.BlockSpec((tm, d), lambda i, j: (i, 0)),
            _layer_spec(g, layer),
            pl.BlockSpec((None, d, tf), lambda i, j: (layer, 0, j)),
            pl.BlockSpec((None, d, tf), lambda i, j: (layer, 0, j)),
            pl.BlockSpec((None, tf, d), lambda i, j: (layer, j, 0)),
        ],
        out_specs=pl.BlockSpec((tm, d), lambda i, j: (i, 0)),
        out_shape=jax.ShapeDtypeStruct((t, d), F32),
        scratch_shapes=[pltpu.VMEM((tm, d), BF16)],
        compiler_params=_cparams(("parallel", "arbitrary"), 57),
        name="ffn",
    )(x, g, wg, wu, wd)


def _rope_kernel(pos_ref, inv_ref, cos_ref, sin_ref):
    ang = pos_ref[...].astype(F32) * inv_ref[...]
    cos_ref[...] = jnp.cos(ang)
    sin_ref[...] = jnp.sin(ang)


def _rope_tables(pos_col, inv_pad, *, tm):
    t = pos_col.shape[0]
    return pl.pallas_call(
        _rope_kernel,
        grid=(t // tm,),
        in_specs=[pl.BlockSpec((tm, 1), lambda i: (i, 0)), _const_spec((1, LANE))],
        out_specs=[pl.BlockSpec((tm, LANE), lambda i: (i, 0))] * 2,
        out_shape=[jax.ShapeDtypeStruct((t, LANE), F32)] * 2,
        compiler_params=_cparams(("parallel",), 32),
        name="rope_tables",
    )(pos_col, inv_pad)


def _mla_prep_kernel(x_ref, g_ref, cos_ref, sin_ref, w1_ref, gcq_ref, gckv_ref, wuq_ref, wukv_ref,
                     gq_ref, gk_ref, q_ref, k_ref, v_ref):
    xf = x_ref[...]
    h = (xf * _inv_rms(xf, xf.shape[-1]) * g_ref[...]).astype(BF16)
    u = jnp.dot(h, w1_ref[...], preferred_element_type=F32)
    c_q = u[:, :MLA_Q_RANK]
    c_kv = u[:, MLA_Q_RANK:MLA_Q_RANK + MLA_KV_RANK]
    kr = u[:, MLA_Q_RANK + MLA_KV_RANK:MLA_Q_RANK + MLA_KV_RANK + LANE]
    krot = u[:, MLA_Q_RANK + MLA_KV_RANK + LANE:]
    c_q = (c_q * _inv_rms(c_q, MLA_Q_RANK) * gcq_ref[...]).astype(BF16)
    c_kv = (c_kv * _inv_rms(c_kv, MLA_KV_RANK) * gckv_ref[...]).astype(BF16)
    qall = jnp.dot(c_q, wuq_ref[...], preferred_element_type=F32)
    kv = jnp.dot(c_kv, wukv_ref[...], preferred_element_type=F32)
    cos = cos_ref[...]
    sin = sin_ref[...]
    scale = MLA_QK ** -0.5 * LOG2E
    gq_n, gq_r, gq_t = gq_ref[0:1, :], gq_ref[1:2, :], gq_ref[2:3, :]
    gk_n, gk_r, gk_t = gk_ref[0:1, :], gk_ref[1:2, :], gk_ref[2:3, :]
    k_roped = kr * gk_r * cos + krot * gk_t * sin
    kr_ss = jnp.sum(kr * kr, axis=-1, keepdims=True)
    for hh in range(MLA_HEADS):
        base = hh * 3 * LANE
        nope = qall[:, base:base + LANE]
        rope = qall[:, base + LANE:base + 2 * LANE]
        rot = qall[:, base + 2 * LANE:base + 3 * LANE]
        ss = jnp.sum(nope * nope, axis=-1, keepdims=True) + jnp.sum(rope * rope, axis=-1, keepdims=True)
        r = lax.rsqrt(ss / MLA_QK + NORM_EPS) * scale
        q_ref[hh, :, 0:LANE] = (nope * r * gq_n).astype(BF16)
        q_ref[hh, :, LANE:2 * LANE] = ((rope * gq_r * cos + rot * gq_t * sin) * r).astype(BF16)
        kn = kv[:, hh * LANE:(hh + 1) * LANE]
        rk = lax.rsqrt((jnp.sum(kn * kn, axis=-1, keepdims=True) + kr_ss) / MLA_QK + NORM_EPS)
        k_ref[hh, :, 0:LANE] = (kn * rk * gk_n).astype(BF16)
        k_ref[hh, :, LANE:2 * LANE] = (k_roped * rk).astype(BF16)
        v_ref[hh] = kv[:, (MLA_HEADS + hh) * LANE:(MLA_HEADS + hh + 1) * LANE].astype(BF16)


def _mla_prep(x, g, cos, sin, w1, gcq, gckv, wuq, wukv, gq3, gk3, layer, *, tm):
    t, d = x.shape
    row = lambda i: (i, 0)
    hrow = lambda i: (0, i, 0)
    return pl.pallas_call(
        _mla_prep_kernel,
        grid=(t // tm,),
        in_specs=[pl.BlockSpec((tm, d), row), _layer_spec(g, layer),
                  pl.BlockSpec((tm, LANE), row), pl.BlockSpec((tm, LANE), row)]
                 + [_layer_spec(p, layer) for p in (w1, gcq, gckv, wuq, wukv, gq3, gk3)],
        out_specs=[
            pl.BlockSpec((MLA_HEADS, tm, MLA_QK_PAD), hrow),
            pl.BlockSpec((MLA_HEADS, tm, MLA_QK_PAD), hrow),
            pl.BlockSpec((MLA_HEADS, tm, MLA_V), hrow),
        ],
        out_shape=[
            jax.ShapeDtypeStruct((MLA_HEADS, t, MLA_QK_PAD), BF16),
            jax.ShapeDtypeStruct((MLA_HEADS, t, MLA_QK_PAD), BF16),
            jax.ShapeDtypeStruct((MLA_HEADS, t, MLA_V), BF16),
        ],
        compiler_params=_cparams(("parallel",), 52),
        name="mla_prep",
    )(x, g, cos, sin, w1, gcq, gckv, wuq, wukv, gq3, gk3)


def _flash_kernel(q_ref, k_ref, v_ref, o_ref, sa_scr, sb_scr, m_scr, acc_scr, *, tq):
    qi = pl.program_id(2)
    m_scr[...] = jnp.full_like(m_scr, NEG_INF)
    acc_scr[...] = jnp.zeros_like(acc_scr)
    q = q_ref[...]
    ones = jnp.ones((tq, MLA_V), BF16)

    def scores(tile, s_scr):
        r0 = pl.multiple_of(tile * tq, tq)
        s_scr[...] = lax.dot_general(q, k_ref[pl.ds(r0, tq), :], (((1,), (1,)), ((), ())),
                                     preferred_element_type=F32)

    def accumulate(tile, s_scr, diagonal):
        r0 = pl.multiple_of(tile * tq, tq)
        sc = s_scr[...]
        if diagonal:
            row = lax.broadcasted_iota(jnp.int32, sc.shape, 0)
            col = lax.broadcasted_iota(jnp.int32, sc.shape, 1)
            sc = jnp.where(col <= row, sc, NEG_INF)
        m_prev = m_scr[...]
        m_new = jnp.maximum(m_prev, jnp.max(sc, axis=-1, keepdims=True))
        alpha = jnp.exp2(m_prev - m_new)
        p = jnp.exp2(sc - jnp.tile(m_new, (1, tq // LANE)))
        v1 = jnp.concatenate([v_ref[pl.ds(r0, tq), :], ones], axis=1)
        pv = jnp.dot(p.astype(BF16), v1, preferred_element_type=F32)
        acc_scr[...] = jnp.tile(alpha, (1, 2)) * acc_scr[...] + pv
        m_scr[...] = m_new

    scores(0, sa_scr)

    def body(j, carry):
        scores(2 * j + 1, sb_scr)
        accumulate(2 * j, sa_scr, False)
        scores(2 * j + 2, sa_scr)
        accumulate(2 * j + 1, sb_scr, False)
        return carry

    lax.fori_loop(0, qi // 2, body, 0)

    @pl.when(qi % 2 == 0)
    def _():
        accumulate(qi, sa_scr, True)

    @pl.when(qi % 2 == 1)
    def _():
        scores(qi, sb_scr)
        accumulate(qi - 1, sa_scr, False)
        accumulate(qi, sb_scr, True)

    acc = acc_scr[...]
    o_ref[...] = (acc[:, 0:MLA_V] / acc[:, MLA_V:]).astype(o_ref.dtype)


def _flash(q, k, v, *, batch, seq, tq):
    assert seq % tq == 0
    nq = seq // tq
    t = batch * seq
    return pl.pallas_call(
        functools.partial(_flash_kernel, tq=tq),
        grid=(batch, MLA_HEADS, nq),
        in_specs=[
            pl.BlockSpec((None, tq, MLA_QK_PAD), lambda b, h, i: (h, b * nq + i, 0)),
            pl.BlockSpec((None, seq, MLA_QK_PAD), lambda b, h, i: (h, b, 0)),
            pl.BlockSpec((None, seq, MLA_V), lambda b, h, i: (h, b, 0)),
        ],
        out_specs=pl.BlockSpec((tq, MLA_V), lambda b, h, i: (b * nq + i, h)),
        out_shape=jax.ShapeDtypeStruct((t, MLA_HEADS * MLA_V), BF16),
        scratch_shapes=[pltpu.VMEM((tq, tq), F32), pltpu.VMEM((tq, tq), F32),
                        pltpu.VMEM((tq, MLA_V), F32), pltpu.VMEM((tq, 2 * MLA_V), F32)],
        compiler_params=_cparams(("parallel", "parallel", "arbitrary"), 52),
        name="mla_flash",
    )(q, k, v)


def _gla_proj_kernel(x_ref, g_ref, w_ref, w2_ref, b2_ref, q_ref, k_ref, v_ref, la_ref, sr_ref):
    xf = x_ref[...]
    h = (xf * _inv_rms(xf, xf.shape[-1]) * g_ref[...]).astype(BF16)
    u = jnp.dot(h, w_ref[...], preferred_element_type=F32)
    nk = GLA_HEADS * GLA_DK
    nv = GLA_HEADS * GLA_DV
    q_ref[...] = u[:, 0:nk] * (GLA_DK ** -0.5)
    k_ref[...] = u[:, nk:2 * nk]
    v_ref[...] = u[:, 2 * nk:2 * nk + nv]
    gate_lr = u[:, 2 * nk + nv:2 * nk + nv + LANE].astype(BF16)
    z = jnp.dot(gate_lr, w2_ref[...], preferred_element_type=F32) + b2_ref[...]
    la_ref[...] = -(jnp.maximum(-z, 0.0) + jnp.log1p(jnp.exp(-jnp.abs(z)))) / GLA_TAU
    r = u[:, 2 * nk + nv + LANE:]
    sr_ref[...] = r * jax.nn.sigmoid(r)


def _gla_proj(x, g, w, w2, b2, layer, *, tm):
    t, d = x.shape
    nk = GLA_HEADS * GLA_DK
    nv = GLA_HEADS * GLA_DV
    row = lambda i: (i, 0)
    return pl.pallas_call(
        _gla_proj_kernel,
        grid=(t // tm,),
        in_specs=[pl.BlockSpec((tm, d), row)] + [_layer_spec(p, layer) for p in (g, w, w2, b2)],
        out_specs=[pl.BlockSpec((tm, nk), row), pl.BlockSpec((tm, nk), row), pl.BlockSpec((tm, nv), row),
                   pl.BlockSpec((tm, nk), row), pl.BlockSpec((tm, nv), row)],
        out_shape=[jax.ShapeDtypeStruct((t, nk), F32), jax.ShapeDtypeStruct((t, nk), F32),
                   jax.ShapeDtypeStruct((t, nv), F32), jax.ShapeDtypeStruct((t, nk), F32),
                   jax.ShapeDtypeStruct((t, nv), F32)],
        compiler_params=_cparams(("parallel",), 52),
        name="gla_proj",
    )(x, g, w, w2, b2)


def _gla_constants():
    c = GLA_CHUNK
    idx = np.arange(c)
    mats = [(idx[None, :] <= idx[:, None])]
    for lvl in GLA_LEVELS:
        p = ((idx // lvl) | 1) * lvl
        qrow = (idx >= p)[:, None] & (idx[None, :] >= p[:, None]) & (idx[None, :] <= idx[:, None])
        krow = (idx < p)[:, None] & (idx[None, :] > idx[:, None]) & (idx[None, :] < p[:, None])
        mats.append(qrow | krow)
    mats.append(idx[None, :] > idx[:, None])
    mstack = np.concatenate(mats, axis=0).astype(np.float32)
    lmask = []
    for lvl in GLA_LEVELS:
        blk = idx // lvl
        m = ((blk[:, None] % 2) == 1) & (blk[None, :] == blk[:, None] - 1)
        lmask.append(np.tile(m.astype(np.float32), (GLA_HEADS, 1)))
    lmask = np.stack(lmask, axis=0)
    hk = np.arange(GLA_HEADS * GLA_DK) // GLA_DK
    hv = np.arange(GLA_HEADS * GLA_DV) // GLA_DV
    hrow = np.arange(GLA_HEADS * c) // c
    headmask = (hrow[:, None] == hk[None, :]).astype(np.float32)
    e_ind = (hk[:, None] == hv[None, :]).astype(np.float32)
    return mstack, lmask, headmask, e_ind


def _gla_kernel(q_ref, k_ref, v_ref, la_ref, sr_ref, go_ref, mst_ref, lmask_ref, hmask_ref, eind_ref, eindt_ref,
                y_ref, st_scr, *, chunks):
    c = GLA_CHUNK
    nk = GLA_HEADS * GLA_DK
    nv = GLA_HEADS * GLA_DV
    nsub = c // GLA_SUB

    @pl.when(pl.program_id(1) == 0)
    def _():
        st_scr[...] = jnp.zeros_like(st_scr)

    def chunk(ci, carry):
        r0 = pl.multiple_of(ci * c, c)
        q = q_ref[pl.ds(r0, c), :]
        k = k_ref[pl.ds(r0, c), :]
        v = v_ref[pl.ds(r0, c), :]
        a = la_ref[pl.ds(r0, c), :]
        a1 = a.astype(BF16)
        r1 = a - a1.astype(F32)
        a2 = r1.astype(BF16)
        a3 = (r1 - a2.astype(F32)).astype(BF16)
        mst = mst_ref[...]
        ex = (jnp.dot(mst, a1, preferred_element_type=F32) + jnp.dot(mst, a2, preferred_element_type=F32)
              + jnp.dot(mst, a3, preferred_element_type=F32))
        b = ex[0:c]
        f = jnp.exp(ex)
        v_bf = v.astype(BF16)

        amat = jnp.zeros((GLA_HEADS * c, c), F32)
        hmask = hmask_ref[...]
        for li in range(len(GLA_LEVELS)):
            fl = f[(li + 1) * c:(li + 2) * c]
            qt = q * fl
            kt = (k * fl).astype(BF16)
            qs = (jnp.concatenate([qt] * GLA_HEADS, axis=0) * hmask).astype(BF16)
            sc = lax.dot_general(qs, kt, (((1,), (1,)), ((), ())), preferred_element_type=F32)
            amat = amat + sc * lmask_ref[li]
        a_bf = amat.astype(BF16)
        o = jnp.concatenate(
            [jnp.dot(a_bf[hh * c:(hh + 1) * c], v_bf[:, hh * GLA_DV:(hh + 1) * GLA_DV], preferred_element_type=F32)
             for hh in range(GLA_HEADS)], axis=1)

        qb = (q * f[0:c]).astype(BF16)
        st = st_scr[...]
        o = o + lax.dot_general(qb, st.astype(BF16), (((1,), (1,)), ((), ())), preferred_element_type=F32)

        q3 = q.reshape(nsub, GLA_SUB, nk)
        k3 = k.reshape(nsub, GLA_SUB, nk)
        b3 = b.reshape(nsub, GLA_SUB, nk)
        v3 = v.reshape(nsub, GLA_SUB, nv)
        tt = lax.broadcasted_iota(jnp.int32, (nsub, GLA_SUB, nk), 1)
        od = jnp.zeros((nsub, GLA_SUB, nv), F32)
        eind = eind_ref[...]
        for s in range(GLA_SUB):
            w = jnp.exp(jnp.minimum(b3 - b3[:, s:s + 1, :], 0.0))
            x = jnp.where(tt >= s, q3 * k3[:, s:s + 1, :] * w, 0.0)
            rr = jnp.dot(x.reshape(c, nk).astype(BF16), eind, preferred_element_type=F32)
            od = od + rr.reshape(nsub, GLA_SUB, nv) * v3[:, s:s + 1, :]
        o = o + od.reshape(c, nv)

        kend = (k * f[(len(GLA_LEVELS) + 1) * c:(len(GLA_LEVELS) + 2) * c]).astype(BF16)
        upd = lax.dot_general(v_bf, kend, (((0,), (0,)), ((), ())), preferred_element_type=F32)
        st_scr[...] = st * f[c - 1:c] + upd * eindt_ref[...]

        go = go_ref[...]
        sr = sr_ref[pl.ds(r0, c), :]
        ys = []
        for hh in range(GLA_HEADS):
            oh = o[:, hh * GLA_DV:(hh + 1) * GLA_DV]
            ys.append(oh * _inv_rms(oh, GLA_DV) * go)
        y_ref[pl.ds(r0, c), :] = (jnp.concatenate(ys, axis=1) * sr).astype(y_ref.dtype)
        return carry

    lax.fori_loop(0, chunks, chunk, 0, unroll=2)


def _gla(q, k, v, la, sr, go, layer, *, batch, seq, tb):
    nk = GLA_HEADS * GLA_DK
    nv = GLA_HEADS * GLA_DV
    nb = seq // tb
    mstack, lmask, headmask, e_ind = _gla_constants()
    consts = [jnp.asarray(mstack, BF16), jnp.asarray(lmask, F32), jnp.asarray(headmask, F32),
              jnp.asarray(e_ind, BF16), jnp.asarray(e_ind.T, F32)]
    row = lambda b, i: (b * nb + i, 0)
    return pl.pallas_call(
        functools.partial(_gla_kernel, chunks=tb // GLA_CHUNK),
        grid=(batch, nb),
        in_specs=[pl.BlockSpec((tb, nk), row), pl.BlockSpec((tb, nk), row), pl.BlockSpec((tb, nv), row),
                  pl.BlockSpec((tb, nk), row), pl.BlockSpec((tb, nv), row), _layer_spec(go, layer)]
                 + [_const_spec(cst.shape) for cst in consts],
        out_specs=pl.BlockSpec((tb, nv), row),
        out_shape=jax.ShapeDtypeStruct((batch * seq, nv), BF16),
        scratch_shapes=[pltpu.VMEM((nv, nk), F32)],
        compiler_params=_cparams(("parallel", "arbitrary"), 40),
        name="gla_scan",
    )(q, k, v, la, sr, go, *consts)


def _dswa_proj_kernel(x_ref, g_ref, w_ref, gq_ref, gk_ref, *refs, tm):
    ng = len(DSWA_GROUPS)
    out_refs = refs[:3 * ng]
    q_scr, k_scr, v_scr = refs[3 * ng:]
    xf = x_ref[...]
    h = (xf * _inv_rms(xf, xf.shape[-1]) * g_ref[...]).astype(BF16)
    u = jnp.dot(h, w_ref[...], preferred_element_type=F32)
    n = DSWA_HEADS * DSWA_HEAD_DIM
    gq = gq_ref[...]
    gk = gk_ref[...]
    scale = DSWA_HEAD_DIM ** -0.5
    for hh in range(DSWA_HEADS):
        sl = slice(hh * DSWA_HEAD_DIM, (hh + 1) * DSWA_HEAD_DIM)
        qh = u[:, sl]
        q_scr[hh] = qh * _inv_rms(qh, DSWA_HEAD_DIM) * gq * scale
        kh = u[:, n + hh * DSWA_HEAD_DIM:n + (hh + 1) * DSWA_HEAD_DIM]
        k_scr[hh] = kh * _inv_rms(kh, DSWA_HEAD_DIM) * gk
        v_scr[hh] = u[:, 2 * n + hh * DSWA_HEAD_DIM:2 * n + (hh + 1) * DSWA_HEAD_DIM]
    for gi, (_, dil) in enumerate(DSWA_GROUPS):
        for src, dst in zip((q_scr, k_scr, v_scr), out_refs[3 * gi:3 * gi + 3]):
            for hh in range(DSWA_HEADS_PER_GROUP):
                head = gi * DSWA_HEADS_PER_GROUP + hh
                cols = slice(hh * DSWA_HEAD_DIM, (hh + 1) * DSWA_HEAD_DIM)
                for r in range(dil):
                    rows = pl.ds(r, tm // dil, stride=dil) if dil > 1 else slice(None)
                    dst[r, :, cols] = src[head, rows, :].astype(BF16)


def _dswa_proj(x, g, w, gq, gk, layer, *, batch, seq, tm):
    t, d = x.shape
    n = DSWA_HEADS * DSWA_HEAD_DIM
    gw = DSWA_HEADS_PER_GROUP * DSWA_HEAD_DIM
    nblk = seq // tm
    out_specs, out_shape = [], []
    for _, dil in DSWA_GROUPS:
        assert tm % (16 * dil) == 0
        out_specs += [pl.BlockSpec((None, dil, tm // dil, gw), lambda b, i: (b, 0, i, 0))] * 3
        out_shape += [jax.ShapeDtypeStruct((batch, dil, seq // dil, gw), BF16)] * 3
    return pl.pallas_call(
        functools.partial(_dswa_proj_kernel, tm=tm),
        grid=(batch, nblk),
        in_specs=[pl.BlockSpec((tm, d), lambda b, i: (b * nblk + i, 0))]
                 + [_layer_spec(p, layer) for p in (g, w, gq, gk)],
        out_specs=out_specs,
        out_shape=out_shape,
        scratch_shapes=[pltpu.VMEM((DSWA_HEADS, tm, DSWA_HEAD_DIM), F32)] * 3,
        compiler_params=_cparams(("parallel", "parallel"), 52),
        name="dswa_proj",
    )(x, g, w, gq, gk)


def _dswa_kernel(slope_ref, q_ref, kc_ref, kp_ref, vc_ref, vp_ref, o_ref, l_ref, *, dilation, group, nb):
    w = DSWA_W
    ib = pl.program_id(2)
    i = lax.broadcasted_iota(jnp.int32, (w, 2 * w), 0)
    j = lax.broadcasted_iota(jnp.int32, (w, 2 * w), 1)
    steps = w + i - j
    dist = (steps * dilation).astype(F32)
    in_window = (steps >= 0) & (steps <= w)
    for hh in range(DSWA_HEADS_PER_GROUP):
        cols = slice(hh * DSWA_HEAD_DIM, (hh + 1) * DSWA_HEAD_DIM)
        bias_all = jnp.where(in_window, -slope_ref[group * DSWA_HEADS_PER_GROUP + hh] * dist, NEG_INF)
        bias_first = jnp.where(j >= w, bias_all, NEG_INF)
        for jb in range(nb):
            rows = slice(jb * w, (jb + 1) * w)
            if jb == 0:
                kprev, vprev = kp_ref[:, cols], vp_ref[:, cols]
                bias_j = jnp.where(ib > 0, bias_all, bias_first)
            else:
                prows = slice((jb - 1) * w, jb * w)
                kprev, vprev = kc_ref[prows, cols], vc_ref[prows, cols]
                bias_j = bias_all
            keys = jnp.concatenate([kprev, kc_ref[rows, cols]], axis=0)
            vals = jnp.concatenate([vprev, vc_ref[rows, cols]], axis=0)
            sc = lax.dot_general(q_ref[rows, cols], keys, (((1,), (1,)), ((), ())), preferred_element_type=F32)
            sc = jnp.where(bias_j > 0.5 * NEG_INF, sc + bias_j, NEG_INF)
            m = jnp.max(sc, axis=-1, keepdims=True)
            p = jnp.exp(sc - m)
            lsum = jnp.sum(p, axis=-1, keepdims=True)
            o_ref[rows, cols] = jnp.dot(p.astype(BF16), vals, preferred_element_type=F32) / lsum
            l_ref[rows, cols] = jnp.broadcast_to(m + jnp.log(lsum), (w, DSWA_HEAD_DIM))


def _dswa_group(slopes, qg, kg, vg, *, group):
    window, dilation = DSWA_GROUPS[group]
    assert window // dilation == DSWA_W
    batch, _, sd, gw = qg.shape
    tq = min(1024, sd)
    nb = tq // DSWA_W
    cur = pl.BlockSpec((None, None, tq, gw), lambda b, r, i: (b, r, i, 0))
    prev = pl.BlockSpec((None, None, DSWA_W, gw), lambda b, r, i: (b, r, jnp.maximum(i * nb - 1, 0), 0))
    return pl.pallas_call(
        functools.partial(_dswa_kernel, dilation=dilation, group=group, nb=nb),
        grid=(batch, dilation, sd // tq),
        in_specs=[pl.BlockSpec(memory_space=pltpu.SMEM), cur, cur, prev, cur, prev],
        out_specs=[cur] * 2,
        out_shape=[jax.ShapeDtypeStruct(qg.shape, F32)] * 2,
        compiler_params=_cparams(("parallel", "parallel", "arbitrary"), 32),
        name="dswa_g%d" % group,
    )(slopes, qg, kg, kg, vg, vg)


def _dswa_combine_kernel(*refs, tm):
    ng = len(DSWA_GROUPS)
    in_refs, y_ref, scr = refs[:2 * ng], refs[2 * ng], refs[2 * ng + 1]
    vals = []
    for gi, (_, dil) in enumerate(DSWA_GROUPS):
        for which in range(2):
            src = in_refs[2 * gi + which]
            if dil == 1:
                vals.append(src[0])
            else:
                halves = []
                for hh in range(DSWA_HEADS_PER_GROUP):
                    slot = (2 * gi + which) * DSWA_HEADS_PER_GROUP + hh
                    for r in range(dil):
                        scr[slot, pl.ds(r, tm // dil, stride=dil), :] = (
                            src[r, :, hh * DSWA_HEAD_DIM:(hh + 1) * DSWA_HEAD_DIM])
                    halves.append(scr[slot])
                vals.append(jnp.concatenate(halves, axis=1))
    lses = vals[1::2]
    mx = functools.reduce(jnp.maximum, lses)
    es = [jnp.exp(l - mx) for l in lses]
    den = functools.reduce(lambda a, b: a + b, es)
    y = functools.reduce(lambda a, b: a + b, [(e / den) * o for e, o in zip(es, vals[0::2])])
    y_ref[...] = y.astype(y_ref.dtype)


def _dswa_combine(outs_lses, *, batch, seq, tm):
    gw = DSWA_HEADS_PER_GROUP * DSWA_HEAD_DIM
    nblk = seq // tm
    in_specs = []
    for _, dil in DSWA_GROUPS:
        in_specs += [pl.BlockSpec((None, dil, tm // dil, gw), lambda b, i: (b, 0, i, 0))] * 2
    return pl.pallas_call(
        functools.partial(_dswa_combine_kernel, tm=tm),
        grid=(batch, nblk),
        in_specs=in_specs,
        out_specs=pl.BlockSpec((tm, gw), lambda b, i: (b * nblk + i, 0)),
        out_shape=jax.ShapeDtypeStruct((batch * seq, gw), BF16),
        scratch_shapes=[pltpu.VMEM((2 * len(DSWA_GROUPS) * DSWA_HEADS_PER_GROUP, tm, DSWA_HEAD_DIM), F32)],
        compiler_params=_cparams(("parallel", "parallel"), 32),
        name="dswa_combine",
    )(*outs_lses)


def _merge_kernel(x_ref, g_ref, ya_ref, yb_ref, yc_ref, wga_ref, wgb_ref, wgc_ref, wa_ref, wb_ref, wc_ref, wo_ref,
                  out_ref, h_scr):
    @pl.when(pl.program_id(1) == 0)
    def _():
        xf = x_ref[...]
        h_scr[...] = (xf * _inv_rms(xf, xf.shape[-1]) * g_ref[...]).astype(BF16)
        out_ref[...] = xf

    h = h_scr[...]
    dot = lambda a, b: jnp.dot(a, b, preferred_element_type=F32)
    inter = (jax.nn.sigmoid(dot(h, wga_ref[...])) * dot(ya_ref[...], wa_ref[...])
             + jax.nn.sigmoid(dot(h, wgb_ref[...])) * dot(yb_ref[...], wb_ref[...])
             + jax.nn.sigmoid(dot(h, wgc_ref[...])) * dot(yc_ref[...], wc_ref[...]))
    out_ref[...] += dot(inter.astype(BF16), wo_ref[...])


def _merge(x, g, ya, yb, yc, wga, wgb, wgc, wa, wb, wc, wo, layer, *, tm, tn):
    t, d = x.shape
    row = lambda i, j: (i, 0)
    colw = lambda i, j: (layer, 0, j)
    return pl.pallas_call(
        _merge_kernel,
        grid=(t // tm, d // tn),
        in_specs=[pl.BlockSpec((tm, d), row), _layer_spec(g, layer),
                  pl.BlockSpec((tm, ya.shape[1]), row), pl.BlockSpec((tm, yb.shape[1]), row),
                  pl.BlockSpec((tm, yc.shape[1]), row)]
                 + [pl.BlockSpec((None, w.shape[1], tn), colw) for w in (wga, wgb, wgc, wa, wb, wc)]
                 + [pl.BlockSpec((None, tn, d), lambda i, j: (layer, j, 0))],
        out_specs=pl.BlockSpec((tm, d), row),
        out_shape=jax.ShapeDtypeStruct((t, d), F32),
        scratch_shapes=[pltpu.VMEM((tm, d), BF16)],
        compiler_params=_cparams(("parallel", "arbitrary"), 52),
        name="merge",
    )(x, g, ya, yb, yc, wga, wgb, wgc, wa, wb, wc, wo)


def _pad_last(w, n):
    return jnp.pad(w, [(0, 0)] * (w.ndim - 1) + [(0, n - w.shape[-1])])


def _rot_half_cols(w):
    half = w.shape[-1] // 2
    return jnp.concatenate([-w[..., half:], w[..., :half]], axis=-1)


def _swap_half(g):
    half = g.shape[-1] // 2
    return jnp.concatenate([g[..., half:], g[..., :half]], axis=-1)


def _gain3(g):
    rope = g[:, MLA_NOPE:]
    return jnp.stack([g[:, :MLA_NOPE], _pad_last(rope, LANE), _pad_last(_swap_half(rope), LANE)], axis=1)


def _row(g):
    return g[:, None, :]


def kernel(x, positions, ffn1_norm, ffn1_w_gate, ffn1_w_up, ffn1_w_down, mix_norm, w_in, mla_cq_norm, mla_ckv_norm, mla_w_uq, mla_w_ukv, mla_q_norm, mla_k_norm, gla_w_gate2, gla_b_gate2, gla_o_norm, dswa_q_norm, dswa_k_norm, w_branch_a, w_branch_b, w_branch_c, w_out, ffn2_norm, ffn2_w_gate, ffn2_w_up, ffn2_w_down):
    batch, seq, d = x.shape
    depth = w_in.shape[0]
    d_ff = ffn1_w_gate.shape[-1]
    t = batch * seq
    tm = 512
    tm_ffn = 1024
    tf = 512
    ff_pad = -(-d_ff // tf) * tf
    assert seq % max(wd for wd, _ in DSWA_GROUPS) == 0 and seq % tm == 0 and t % tm_ffn == 0

    xf = x.reshape(t, d)
    half = MLA_ROPE // 2
    inv_freq = ROPE_THETA ** (-jnp.arange(half, dtype=F32) / half)
    inv_pad = jnp.pad(jnp.concatenate([inv_freq, inv_freq]), (0, LANE - MLA_ROPE)).reshape(1, LANE)
    cos, sin = _rope_tables(positions.reshape(t, 1), inv_pad, tm=tm)
    slopes = 2.0 ** (-ALIBI_MAX_EXP * jnp.arange(1, DSWA_HEADS + 1, dtype=F32) / DSWA_HEADS)

    o_cq, o_ckv, o_kr = 0, MLA_Q_RANK, MLA_Q_RANK + MLA_KV_RANK
    o_gla = o_kr + MLA_ROPE
    n_qkv_b = GLA_HEADS * (2 * GLA_DK + GLA_DV)
    o_glr = o_gla + n_qkv_b
    o_rb = o_glr + GLA_GATE_RANK
    o_c = o_rb + GLA_HEADS * GLA_DV
    o_ga = o_c + 3 * DSWA_HEADS * DSWA_HEAD_DIM
    o_gb, o_gc = o_ga + d, o_ga + 2 * d

    bf = lambda w: w.astype(BF16)

    def ffn_weights(wg, wu, wdn):
        zc = jnp.zeros((depth, d, ff_pad - d_ff), BF16)
        zr = jnp.zeros((depth, ff_pad - d_ff, d), BF16)
        return (jnp.concatenate([bf(wg), zc], axis=2), jnp.concatenate([bf(wu), zc], axis=2),
                jnp.concatenate([bf(wdn), zr], axis=1))

    ffn1 = (_row(ffn1_norm),) + ffn_weights(ffn1_w_gate, ffn1_w_up, ffn1_w_down)
    ffn2 = (_row(ffn2_norm),) + ffn_weights(ffn2_w_gate, ffn2_w_up, ffn2_w_down)
    gmix = _row(mix_norm)
    w_in = bf(w_in)
    w_kr = w_in[:, :, o_kr:o_kr + MLA_ROPE]
    w1 = bf(jnp.concatenate([w_in[:, :, o_cq:o_kr], _pad_last(w_kr, LANE), _pad_last(_rot_half_cols(w_kr), LANE)],
                            axis=-1))
    wq4 = mla_w_uq.reshape(depth, MLA_Q_RANK, MLA_HEADS, MLA_QK)
    wq_rope = wq4[..., MLA_NOPE:]
    wuq = jnp.concatenate([wq4[..., :MLA_NOPE], _pad_last(wq_rope, LANE), _pad_last(_rot_half_cols(wq_rope), LANE)],
                          axis=-1)
    wuq = bf(wuq.reshape(depth, MLA_Q_RANK, MLA_HEADS * 3 * LANE))
    wkv4 = mla_w_ukv.reshape(depth, MLA_KV_RANK, MLA_HEADS, MLA_NOPE + MLA_V)
    wukv = bf(jnp.concatenate([wkv4[..., :MLA_NOPE].reshape(depth, MLA_KV_RANK, -1),
                               wkv4[..., MLA_NOPE:].reshape(depth, MLA_KV_RANK, -1)], axis=-1))
    mla_params = (w1, _row(mla_cq_norm), _row(mla_ckv_norm), wuq, wukv, _gain3(mla_q_norm), _gain3(mla_k_norm))
    w_gla = bf(jnp.concatenate([w_in[:, :, o_gla:o_glr], _pad_last(w_in[:, :, o_glr:o_rb], LANE),
                                w_in[:, :, o_rb:o_c]], axis=-1))
    w2 = bf(jnp.pad(gla_w_gate2, ((0, 0), (0, LANE - GLA_GATE_RANK), (0, 0))))
    gla_params = (w_gla, w2, _row(gla_b_gate2))
    go = _row(gla_o_norm)
    dswa_params = (bf(w_in[:, :, o_c:o_ga]), _row(dswa_q_norm), _row(dswa_k_norm))
    merge_params = (bf(w_in[:, :, o_ga:o_gb]), bf(w_in[:, :, o_gb:o_gc]), bf(w_in[:, :, o_gc:o_gc + d]),
                    bf(w_branch_a), bf(w_branch_b), bf(w_branch_c), bf(w_out))

    for l in range(depth):
        xf = _ffn(xf, *ffn1, l, tm=tm_ffn, tf=tf)

        qa, ka, va = _mla_prep(xf, gmix, cos, sin, *mla_params, l, tm=tm)
        y_a = _flash(qa, ka, va, batch=batch, seq=seq, tq=1024)

        qb, kb, vb, la, sr = _gla_proj(xf, gmix, *gla_params, l, tm=tm)
        y_b = _gla(qb, kb, vb, la, sr, go, l, batch=batch, seq=seq, tb=(1024 if l == 1 else 512))

        qkv_c = _dswa_proj(xf, gmix, *dswa_params, l, batch=batch, seq=seq, tm=tm)
        outs_lses = []
        for gi in range(len(DSWA_GROUPS)):
            outs_lses += _dswa_group(slopes, *qkv_c[3 * gi:3 * gi + 3], group=gi)
        y_c = _dswa_combine(outs_lses, batch=batch, seq=seq, tm=tm)

        xf = _merge(xf, gmix, y_a, y_b, y_c, *merge_params, l, tm=tm, tn=512)
        xf = _ffn(xf, *ffn2, l, tm=tm_ffn, tf=tf)
    return xf.reshape(batch, seq, d)
```

```python
import functools

import numpy as np
import jax
import jax.numpy as jnp
from jax import lax
from jax.experimental import pallas as pl
from jax.experimental.pallas import tpu as pltpu

F32 = jnp.float32
BF16 = jnp.bfloat16

MLA_HEADS = 6
MLA_Q_RANK = 512
MLA_KV_RANK = 256
MLA_NOPE = 128
MLA_ROPE = 64
MLA_V = 128
MLA_QK = MLA_NOPE + MLA_ROPE
MLA_QK_PAD = 256
GLA_HEADS = 4
GLA_DK = 64
GLA_DV = 128
GLA_GATE_RANK = 16
GLA_TAU = 16.0
GLA_CHUNK = 64
GLA_SUB = 8
GLA_LEVELS = (32, 16, 8)
DSWA_GROUPS = ((128, 1), (512, 4), (2048, 16))
DSWA_HEADS_PER_GROUP = 2
DSWA_HEADS = 6
DSWA_HEAD_DIM = 128
DSWA_W = 128
ROPE_THETA = 10000.0
ALIBI_MAX_EXP = 8.0
NORM_EPS = 1e-6
NEG_INF = -1e30
LOG2E = 1.4426950408889634

LANE = 128
BF16_ROWS = 16
MIB = 1024 * 1024

TOKEN_BLOCK = 512
FFN_TOKEN_BLOCK = 1024
FFN_CHUNK = 512
MERGE_CHUNK = 512
FLASH_TILE = 1024
GLA_TOKENS = 512
DSWA_MAX_Q = 1024
VMEM_FFN = 57
VMEM_RESIDENT = 52
VMEM_SCAN = 40
VMEM_SMALL = 32


def _cparams(sem, vmem_mib):
    return pltpu.CompilerParams(dimension_semantics=sem, vmem_limit_bytes=int(vmem_mib * MIB))


def _inv_rms(xf, n):
    return lax.rsqrt(jnp.sum(xf * xf, axis=-1, keepdims=True) / n + NORM_EPS)


def _const_spec(shape):
    nd = len(shape)
    return pl.BlockSpec(shape, lambda *_: (0,) * nd)


def _layer_spec(arr, layer):
    nd = arr.ndim - 1
    return pl.BlockSpec((None,) + arr.shape[1:], lambda *_: (layer,) + (0,) * nd)


def _ffn_kernel(x_ref, g_ref, wg_ref, wu_ref, wd_ref, o_ref, h_scr):
    @pl.when(pl.program_id(1) == 0)
    def _():
        xf = x_ref[...]
        h_scr[...] = (xf * _inv_rms(xf, xf.shape[-1]) * g_ref[...]).astype(BF16)
        o_ref[...] = xf

    h = h_scr[...]
    gate = jnp.dot(h, wg_ref[...], preferred_element_type=F32)
    up = jnp.dot(h, wu_ref[...], preferred_element_type=F32)
    inter = (gate * jax.nn.sigmoid(gate) * up).astype(BF16)
    o_ref[...] += 0.5 * jnp.dot(inter, wd_ref[...], preferred_element_type=F32)


def _ffn(x, g, wg, wu, wd, layer, *, tm, tf):
    t, d = x.shape
    fp = wg.shape[-1]
    return pl.pallas_call(
        _ffn_kernel,
        grid=(t // tm, fp // tf),
        in_specs=[
            pl.BlockSpec((tm, d), lambda i, j: (i, 0)),
            _layer_spec(g, layer),
            pl.BlockSpec((None, d, tf), lambda i, j: (layer, 0, j)),
            pl.BlockSpec((None, d, tf), lambda i, j: (layer, 0, j)),
            pl.BlockSpec((None, tf, d), lambda i, j: (layer, j, 0)),
        ],
        out_specs=pl.BlockSpec((tm, d), lambda i, j: (i, 0)),
        out_shape=jax.ShapeDtypeStruct((t, d), F32),
        scratch_shapes=[pltpu.VMEM((tm, d), BF16)],
        compiler_params=_cparams(("parallel", "arbitrary"), VMEM_FFN),
        name="ffn",
    )(x, g, wg, wu, wd)


def _rope_kernel(pos_ref, inv_ref, cos_ref, sin_ref):
    ang = pos_ref[...].astype(F32) * inv_ref[...]
    cos_ref[...] = jnp.cos(ang)
    sin_ref[...] = jnp.sin(ang)


def _rope_tables(pos_col, inv_pad, *, tm):
    t = pos_col.shape[0]
    return pl.pallas_call(
        _rope_kernel,
        grid=(t // tm,),
        in_specs=[pl.BlockSpec((tm, 1), lambda i: (i, 0)), _const_spec((1, LANE))],
        out_specs=[pl.BlockSpec((tm, LANE), lambda i: (i, 0))] * 2,
        out_shape=[jax.ShapeDtypeStruct((t, LANE), F32)] * 2,
        compiler_params=_cparams(("parallel",), VMEM_SMALL),
        name="rope_tables",
    )(pos_col, inv_pad)


def _mla_prep_kernel(x_ref, g_ref, cos_ref, sin_ref, w1_ref, gcq_ref, gckv_ref, wuq_ref, wukv_ref,
                     gq_ref, gk_ref, q_ref, k_ref, v_ref):
    xf = x_ref[...]
    h = (xf * _inv_rms(xf, xf.shape[-1]) * g_ref[...]).astype(BF16)
    u = jnp.dot(h, w1_ref[...], preferred_element_type=F32)
    c_q = u[:, :MLA_Q_RANK]
    c_kv = u[:, MLA_Q_RANK:MLA_Q_RANK + MLA_KV_RANK]
    kr = u[:, MLA_Q_RANK + MLA_KV_RANK:MLA_Q_RANK + MLA_KV_RANK + LANE]
    krot = u[:, MLA_Q_RANK + MLA_KV_RANK + LANE:]
    c_q = (c_q * _inv_rms(c_q, MLA_Q_RANK) * gcq_ref[...]).astype(BF16)
    c_kv = (c_kv * _inv_rms(c_kv, MLA_KV_RANK) * gckv_ref[...]).astype(BF16)
    qall = jnp.dot(c_q, wuq_ref[...], preferred_element_type=F32)
    kv = jnp.dot(c_kv, wukv_ref[...], preferred_element_type=F32)
    cos = cos_ref[...]
    sin = sin_ref[...]
    scale = MLA_QK ** -0.5 * LOG2E
    gq_n, gq_r, gq_t = gq_ref[0:1, :], gq_ref[1:2, :], gq_ref[2:3, :]
    gk_n, gk_r, gk_t = gk_ref[0:1, :], gk_ref[1:2, :], gk_ref[2:3, :]
    k_roped = kr * gk_r * cos + krot * gk_t * sin
    kr_ss = jnp.sum(kr * kr, axis=-1, keepdims=True)
    for hh in range(MLA_HEADS):
        base = hh * 3 * LANE
        nope = qall[:, base:base + LANE]
        rope = qall[:, base + LANE:base + 2 * LANE]
        rot = qall[:, base + 2 * LANE:base + 3 * LANE]
        ss = jnp.sum(nope * nope, axis=-1, keepdims=True) + jnp.sum(rope * rope, axis=-1, keepdims=True)
        r = lax.rsqrt(ss / MLA_QK + NORM_EPS) * scale
        q_ref[hh, :, 0:LANE] = (nope * r * gq_n).astype(BF16)
        q_ref[hh, :, LANE:2 * LANE] = ((rope * gq_r * cos + rot * gq_t * sin) * r).astype(BF16)
        kn = kv[:, hh * LANE:(hh + 1) * LANE]
        rk = lax.rsqrt((jnp.sum(kn * kn, axis=-1, keepdims=True) + kr_ss) / MLA_QK + NORM_EPS)
        k_ref[hh, :, 0:LANE] = (kn * rk * gk_n).astype(BF16)
        k_ref[hh, :, LANE:2 * LANE] = (k_roped * rk).astype(BF16)
        v_ref[hh] = kv[:, (MLA_HEADS + hh) * LANE:(MLA_HEADS + hh + 1) * LANE].astype(BF16)


def _mla_prep(x, g, cos, sin, w1, gcq, gckv, wuq, wukv, gq3, gk3, layer, *, tm):
    t, d = x.shape
    row = lambda i: (i, 0)
    hrow = lambda i: (0, i, 0)
    return pl.pallas_call(
        _mla_prep_kernel,
        grid=(t // tm,),
        in_specs=[pl.BlockSpec((tm, d), row), _layer_spec(g, layer),
                  pl.BlockSpec((tm, LANE), row), pl.BlockSpec((tm, LANE), row)]
                 + [_layer_spec(p, layer) for p in (w1, gcq, gckv, wuq, wukv, gq3, gk3)],
        out_specs=[
            pl.BlockSpec((MLA_HEADS, tm, MLA_QK_PAD), hrow),
            pl.BlockSpec((MLA_HEADS, tm, MLA_QK_PAD), hrow),
            pl.BlockSpec((MLA_HEADS, tm, MLA_V), hrow),
        ],
        out_shape=[
            jax.ShapeDtypeStruct((MLA_HEADS, t, MLA_QK_PAD), BF16),
            jax.ShapeDtypeStruct((MLA_HEADS, t, MLA_QK_PAD), BF16),
            jax.ShapeDtypeStruct((MLA_HEADS, t, MLA_V), BF16),
        ],
        compiler_params=_cparams(("parallel",), VMEM_RESIDENT),
        name="mla_prep",
    )(x, g, cos, sin, w1, gcq, gckv, wuq, wukv, gq3, gk3)


def _flash_kernel(q_ref, k_ref, v_ref, o_ref, sa_scr, sb_scr, m_scr, acc_scr, *, tq):
    qi = pl.program_id(2)
    m_scr[...] = jnp.full_like(m_scr, NEG_INF)
    acc_scr[...] = jnp.zeros_like(acc_scr)
    q = q_ref[...]
    ones = jnp.ones((tq, MLA_V), BF16)

    def scores(tile, s_scr):
        r0 = pl.multiple_of(tile * tq, tq)
        s_scr[...] = lax.dot_general(q, k_ref[pl.ds(r0, tq), :], (((1,), (1,)), ((), ())),
                                     preferred_element_type=F32)

    def accumulate(tile, s_scr, diagonal):
        r0 = pl.multiple_of(tile * tq, tq)
        sc = s_scr[...]
        if diagonal:
            row = lax.broadcasted_iota(jnp.int32, sc.shape, 0)
            col = lax.broadcasted_iota(jnp.int32, sc.shape, 1)
            sc = jnp.where(col <= row, sc, NEG_INF)
        m_prev = m_scr[...]
        m_new = jnp.maximum(m_prev, jnp.max(sc, axis=-1, keepdims=True))
        alpha = jnp.exp2(m_prev - m_new)
        p = jnp.exp2(sc - jnp.tile(m_new, (1, tq // LANE)))
        v1 = jnp.concatenate([v_ref[pl.ds(r0, tq), :], ones], axis=1)
        pv = jnp.dot(p.astype(BF16), v1, preferred_element_type=F32)
        acc_scr[...] = jnp.tile(alpha, (1, 2)) * acc_scr[...] + pv
        m_scr[...] = m_new

    scores(0, sa_scr)

    def body(j, carry):
        scores(2 * j + 1, sb_scr)
        accumulate(2 * j, sa_scr, False)
        scores(2 * j + 2, sa_scr)
        accumulate(2 * j + 1, sb_scr, False)
        return carry

    lax.fori_loop(0, qi // 2, body, 0)

    @pl.when(qi % 2 == 0)
    def _():
        accumulate(qi, sa_scr, True)

    @pl.when(qi % 2 == 1)
    def _():
        scores(qi, sb_scr)
        accumulate(qi - 1, sa_scr, False)
        accumulate(qi, sb_scr, True)

    acc = acc_scr[...]
    o_ref[...] = (acc[:, 0:MLA_V] / acc[:, MLA_V:]).astype(o_ref.dtype)


def _flash(q, k, v, *, batch, seq, tq):
    assert seq % tq == 0
    nq = seq // tq
    t = batch * seq
    return pl.pallas_call(
        functools.partial(_flash_kernel, tq=tq),
        grid=(batch, MLA_HEADS, nq),
        in_specs=[
            pl.BlockSpec((None, tq, MLA_QK_PAD), lambda b, h, i: (h, b * nq + i, 0)),
            pl.BlockSpec((None, seq, MLA_QK_PAD), lambda b, h, i: (h, b, 0)),
            pl.BlockSpec((None, seq, MLA_V), lambda b, h, i: (h, b, 0)),
        ],
        out_specs=pl.BlockSpec((tq, MLA_V), lambda b, h, i: (b * nq + i, h)),
        out_shape=jax.ShapeDtypeStruct((t, MLA_HEADS * MLA_V), BF16),
        scratch_shapes=[pltpu.VMEM((tq, tq), F32), pltpu.VMEM((tq, tq), F32),
                        pltpu.VMEM((tq, MLA_V), F32), pltpu.VMEM((tq, 2 * MLA_V), F32)],
        compiler_params=_cparams(("parallel", "parallel", "arbitrary"), VMEM_RESIDENT),
        name="mla_flash",
    )(q, k, v)


def _gla_proj_kernel(x_ref, g_ref, w_ref, w2_ref, b2_ref, q_ref, k_ref, v_ref, la_ref, sr_ref):
    xf = x_ref[...]
    h = (xf * _inv_rms(xf, xf.shape[-1]) * g_ref[...]).astype(BF16)
    u = jnp.dot(h, w_ref[...], preferred_element_type=F32)
    nk = GLA_HEADS * GLA_DK
    nv = GLA_HEADS * GLA_DV
    q_ref[...] = u[:, 0:nk] * (GLA_DK ** -0.5)
    k_ref[...] = u[:, nk:2 * nk]
    v_ref[...] = u[:, 2 * nk:2 * nk + nv]
    gate_lr = u[:, 2 * nk + nv:2 * nk + nv + LANE].astype(BF16)
    z = jnp.dot(gate_lr, w2_ref[...], preferred_element_type=F32) + b2_ref[...]
    la_ref[...] = -(jnp.maximum(-z, 0.0) + jnp.log1p(jnp.exp(-jnp.abs(z)))) / GLA_TAU
    r = u[:, 2 * nk + nv + LANE:]
    sr_ref[...] = r * jax.nn.sigmoid(r)


def _gla_proj(x, g, w, w2, b2, layer, *, tm):
    t, d = x.shape
    nk = GLA_HEADS * GLA_DK
    nv = GLA_HEADS * GLA_DV
    row = lambda i: (i, 0)
    return pl.pallas_call(
        _gla_proj_kernel,
        grid=(t // tm,),
        in_specs=[pl.BlockSpec((tm, d), row)] + [_layer_spec(p, layer) for p in (g, w, w2, b2)],
        out_specs=[pl.BlockSpec((tm, nk), row), pl.BlockSpec((tm, nk), row), pl.BlockSpec((tm, nv), row),
                   pl.BlockSpec((tm, nk), row), pl.BlockSpec((tm, nv), row)],
        out_shape=[jax.ShapeDtypeStruct((t, nk), F32), jax.ShapeDtypeStruct((t, nk), F32),
                   jax.ShapeDtypeStruct((t, nv), F32), jax.ShapeDtypeStruct((t, nk), F32),
                   jax.ShapeDtypeStruct((t, nv), F32)],
        compiler_params=_cparams(("parallel",), VMEM_RESIDENT),
        name="gla_proj",
    )(x, g, w, w2, b2)


def _gla_constants():
    c = GLA_CHUNK
    idx = np.arange(c)
    mats = [(idx[None, :] <= idx[:, None])]
    for lvl in GLA_LEVELS:
        p = ((idx // lvl) | 1) * lvl
        qrow = (idx >= p)[:, None] & (idx[None, :] >= p[:, None]) & (idx[None, :] <= idx[:, None])
        krow = (idx < p)[:, None] & (idx[None, :] > idx[:, None]) & (idx[None, :] < p[:, None])
        mats.append(qrow | krow)
    mats.append(idx[None, :] > idx[:, None])
    mstack = np.concatenate(mats, axis=0).astype(np.float32)
    lmask = []
    for lvl in GLA_LEVELS:
        blk = idx // lvl
        m = ((blk[:, None] % 2) == 1) & (blk[None, :] == blk[:, None] - 1)
        lmask.append(np.tile(m.astype(np.float32), (GLA_HEADS, 1)))
    lmask = np.stack(lmask, axis=0)
    hk = np.arange(GLA_HEADS * GLA_DK) // GLA_DK
    hv = np.arange(GLA_HEADS * GLA_DV) // GLA_DV
    hrow = np.arange(GLA_HEADS * c) // c
    headmask = (hrow[:, None] == hk[None, :]).astype(np.float32)
    e_ind = (hk[:, None] == hv[None, :]).astype(np.float32)
    return mstack, lmask, headmask, e_ind


def _gla_kernel(q_ref, k_ref, v_ref, la_ref, sr_ref, go_ref, mst_ref, lmask_ref, hmask_ref, eind_ref, eindt_ref,
                y_ref, st_scr, *, chunks):
    c = GLA_CHUNK
    nk = GLA_HEADS * GLA_DK
    nv = GLA_HEADS * GLA_DV
    nsub = c // GLA_SUB

    @pl.when(pl.program_id(1) == 0)
    def _():
        st_scr[...] = jnp.zeros_like(st_scr)

    def chunk(ci, carry):
        r0 = pl.multiple_of(ci * c, c)
        q = q_ref[pl.ds(r0, c), :]
        k = k_ref[pl.ds(r0, c), :]
        v = v_ref[pl.ds(r0, c), :]
        a = la_ref[pl.ds(r0, c), :]
        a1 = a.astype(BF16)
        r1 = a - a1.astype(F32)
        a2 = r1.astype(BF16)
        a3 = (r1 - a2.astype(F32)).astype(BF16)
        mst = mst_ref[...]
        ex = (jnp.dot(mst, a1, preferred_element_type=F32) + jnp.dot(mst, a2, preferred_element_type=F32)
              + jnp.dot(mst, a3, preferred_element_type=F32))
        b = ex[0:c]
        f = jnp.exp(ex)
        v_bf = v.astype(BF16)

        amat = jnp.zeros((GLA_HEADS * c, c), F32)
        hmask = hmask_ref[...]
        for li in range(len(GLA_LEVELS)):
            fl = f[(li + 1) * c:(li + 2) * c]
            qt = q * fl
            kt = (k * fl).astype(BF16)
            qs = (jnp.concatenate([qt] * GLA_HEADS, axis=0) * hmask).astype(BF16)
            sc = lax.dot_general(qs, kt, (((1,), (1,)), ((), ())), preferred_element_type=F32)
            amat = amat + sc * lmask_ref[li]
        a_bf = amat.astype(BF16)
        o = jnp.concatenate(
            [jnp.dot(a_bf[hh * c:(hh + 1) * c], v_bf[:, hh * GLA_DV:(hh + 1) * GLA_DV], preferred_element_type=F32)
             for hh in range(GLA_HEADS)], axis=1)

        qb = (q * f[0:c]).astype(BF16)
        st = st_scr[...]
        o = o + lax.dot_general(qb, st.astype(BF16), (((1,), (1,)), ((), ())), preferred_element_type=F32)

        q3 = q.reshape(nsub, GLA_SUB, nk)
        k3 = k.reshape(nsub, GLA_SUB, nk)
        b3 = b.reshape(nsub, GLA_SUB, nk)
        v3 = v.reshape(nsub, GLA_SUB, nv)
        tt = lax.broadcasted_iota(jnp.int32, (nsub, GLA_SUB, nk), 1)
        od = jnp.zeros((nsub, GLA_SUB, nv), F32)
        eind = eind_ref[...]
        for s in range(GLA_SUB):
            w = jnp.exp(jnp.minimum(b3 - b3[:, s:s + 1, :], 0.0))
            x = jnp.where(tt >= s, q3 * k3[:, s:s + 1, :] * w, 0.0)
            rr = jnp.dot(x.reshape(c, nk).astype(BF16), eind, preferred_element_type=F32)
            od = od + rr.reshape(nsub, GLA_SUB, nv) * v3[:, s:s + 1, :]
        o = o + od.reshape(c, nv)

        kend = (k * f[(len(GLA_LEVELS) + 1) * c:(len(GLA_LEVELS) + 2) * c]).astype(BF16)
        upd = lax.dot_general(v_bf, kend, (((0,), (0,)), ((), ())), preferred_element_type=F32)
        st_scr[...] = st * f[c - 1:c] + upd * eindt_ref[...]

        go = go_ref[...]
        sr = sr_ref[pl.ds(r0, c), :]
        ys = []
        for hh in range(GLA_HEADS):
            oh = o[:, hh * GLA_DV:(hh + 1) * GLA_DV]
            ys.append(oh * _inv_rms(oh, GLA_DV) * go)
        y_ref[pl.ds(r0, c), :] = (jnp.concatenate(ys, axis=1) * sr).astype(y_ref.dtype)
        return carry

    lax.fori_loop(0, chunks, chunk, 0, unroll=2)


def _gla(q, k, v, la, sr, go, layer, *, batch, seq, tb):
    nk = GLA_HEADS * GLA_DK
    nv = GLA_HEADS * GLA_DV
    nb = seq // tb
    mstack, lmask, headmask, e_ind = _gla_constants()
    consts = [jnp.asarray(mstack, BF16), jnp.asarray(lmask, F32), jnp.asarray(headmask, F32),
              jnp.asarray(e_ind, BF16), jnp.asarray(e_ind.T, F32)]
    row = lambda b, i: (b * nb + i, 0)
    return pl.pallas_call(
        functools.partial(_gla_kernel, chunks=tb // GLA_CHUNK),
        grid=(batch, nb),
        in_specs=[pl.BlockSpec((tb, nk), row), pl.BlockSpec((tb, nk), row), pl.BlockSpec((tb, nv), row),
                  pl.BlockSpec((tb, nk), row), pl.BlockSpec((tb, nv), row), _layer_spec(go, layer)]
                 + [_const_spec(cst.shape) for cst in consts],
        out_specs=pl.BlockSpec((tb, nv), row),
        out_shape=jax.ShapeDtypeStruct((batch * seq, nv), BF16),
        scratch_shapes=[pltpu.VMEM((nv, nk), F32)],
        compiler_params=_cparams(("parallel", "arbitrary"), VMEM_SCAN),
        name="gla_scan",
    )(q, k, v, la, sr, go, *consts)


def _dswa_proj_kernel(x_ref, g_ref, w_ref, gq_ref, gk_ref, *refs, tm):
    ng = len(DSWA_GROUPS)
    out_refs = refs[:3 * ng]
    q_scr, k_scr, v_scr = refs[3 * ng:]
    xf = x_ref[...]
    h = (xf * _inv_rms(xf, xf.shape[-1]) * g_ref[...]).astype(BF16)
    u = jnp.dot(h, w_ref[...], preferred_element_type=F32)
    n = DSWA_HEADS * DSWA_HEAD_DIM
    gq = gq_ref[...]
    gk = gk_ref[...]
    scale = DSWA_HEAD_DIM ** -0.5
    for hh in range(DSWA_HEADS):
        sl = slice(hh * DSWA_HEAD_DIM, (hh + 1) * DSWA_HEAD_DIM)
        qh = u[:, sl]
        q_scr[hh] = qh * _inv_rms(qh, DSWA_HEAD_DIM) * gq * scale
        kh = u[:, n + hh * DSWA_HEAD_DIM:n + (hh + 1) * DSWA_HEAD_DIM]
        k_scr[hh] = kh * _inv_rms(kh, DSWA_HEAD_DIM) * gk
        v_scr[hh] = u[:, 2 * n + hh * DSWA_HEAD_DIM:2 * n + (hh + 1) * DSWA_HEAD_DIM]
    for gi, (_, dil) in enumerate(DSWA_GROUPS):
        for src, dst in zip((q_scr, k_scr, v_scr), out_refs[3 * gi:3 * gi + 3]):
            for hh in range(DSWA_HEADS_PER_GROUP):
                head = gi * DSWA_HEADS_PER_GROUP + hh
                cols = slice(hh * DSWA_HEAD_DIM, (hh + 1) * DSWA_HEAD_DIM)
                for r in range(dil):
                    rows = pl.ds(r, tm // dil, stride=dil) if dil > 1 else slice(None)
                    dst[r, :, cols] = src[head, rows, :].astype(BF16)


def _dswa_proj(x, g, w, gq, gk, layer, *, batch, seq, tm):
    t, d = x.shape
    n = DSWA_HEADS * DSWA_HEAD_DIM
    gw = DSWA_HEADS_PER_GROUP * DSWA_HEAD_DIM
    nblk = seq // tm
    out_specs, out_shape = [], []
    for _, dil in DSWA_GROUPS:
        assert tm % (BF16_ROWS * dil) == 0
        out_specs += [pl.BlockSpec((None, dil, tm // dil, gw), lambda b, i: (b, 0, i, 0))] * 3
        out_shape += [jax.ShapeDtypeStruct((batch, dil, seq // dil, gw), BF16)] * 3
    return pl.pallas_call(
        functools.partial(_dswa_proj_kernel, tm=tm),
        grid=(batch, nblk),
        in_specs=[pl.BlockSpec((tm, d), lambda b, i: (b * nblk + i, 0))]
                 + [_layer_spec(p, layer) for p in (g, w, gq, gk)],
        out_specs=out_specs,
        out_shape=out_shape,
        scratch_shapes=[pltpu.VMEM((DSWA_HEADS, tm, DSWA_HEAD_DIM), F32)] * 3,
        compiler_params=_cparams(("parallel", "parallel"), VMEM_RESIDENT),
        name="dswa_proj",
    )(x, g, w, gq, gk)


def _dswa_kernel(slope_ref, q_ref, kc_ref, kp_ref, vc_ref, vp_ref, o_ref, l_ref, *, dilation, group, nb):
    w = DSWA_W
    ib = pl.program_id(2)
    i = lax.broadcasted_iota(jnp.int32, (w, 2 * w), 0)
    j = lax.broadcasted_iota(jnp.int32, (w, 2 * w), 1)
    steps = w + i - j
    dist = (steps * dilation).astype(F32)
    in_window = (steps >= 0) & (steps <= w)
    for hh in range(DSWA_HEADS_PER_GROUP):
        cols = slice(hh * DSWA_HEAD_DIM, (hh + 1) * DSWA_HEAD_DIM)
        bias_all = jnp.where(in_window, -slope_ref[group * DSWA_HEADS_PER_GROUP + hh] * dist, NEG_INF)
        bias_first = jnp.where(j >= w, bias_all, NEG_INF)
        for jb in range(nb):
            rows = slice(jb * w, (jb + 1) * w)
            if jb == 0:
                kprev, vprev = kp_ref[:, cols], vp_ref[:, cols]
                bias_j = jnp.where(ib > 0, bias_all, bias_first)
            else:
                prows = slice((jb - 1) * w, jb * w)
                kprev, vprev = kc_ref[prows, cols], vc_ref[prows, cols]
                bias_j = bias_all
            keys = jnp.concatenate([kprev, kc_ref[rows, cols]], axis=0)
            vals = jnp.concatenate([vprev, vc_ref[rows, cols]], axis=0)
            sc = lax.dot_general(q_ref[rows, cols], keys, (((1,), (1,)), ((), ())), preferred_element_type=F32)
            sc = jnp.where(bias_j > 0.5 * NEG_INF, sc + bias_j, NEG_INF)
            m = jnp.max(sc, axis=-1, keepdims=True)
            p = jnp.exp(sc - m)
            lsum = jnp.sum(p, axis=-1, keepdims=True)
            o_ref[rows, cols] = jnp.dot(p.astype(BF16), vals, preferred_element_type=F32) / lsum
            l_ref[rows, cols] = jnp.broadcast_to(m + jnp.log(lsum), (w, DSWA_HEAD_DIM))


def _dswa_group(slopes, qg, kg, vg, *, group):
    window, dilation = DSWA_GROUPS[group]
    assert window // dilation == DSWA_W
    batch, _, sd, gw = qg.shape
    tq = min(DSWA_MAX_Q, sd)
    nb = tq // DSWA_W
    cur = pl.BlockSpec((None, None, tq, gw), lambda b, r, i: (b, r, i, 0))
    prev = pl.BlockSpec((None, None, DSWA_W, gw), lambda b, r, i: (b, r, jnp.maximum(i * nb - 1, 0), 0))
    return pl.pallas_call(
        functools.partial(_dswa_kernel, dilation=dilation, group=group, nb=nb),
        grid=(batch, dilation, sd // tq),
        in_specs=[pl.BlockSpec(memory_space=pltpu.SMEM), cur, cur, prev, cur, prev],
        out_specs=[cur] * 2,
        out_shape=[jax.ShapeDtypeStruct(qg.shape, F32)] * 2,
        compiler_params=_cparams(("parallel", "parallel", "arbitrary"), VMEM_SMALL),
        name="dswa_g%d" % group,
    )(slopes, qg, kg, kg, vg, vg)


def _dswa_combine_kernel(*refs, tm):
    ng = len(DSWA_GROUPS)
    in_refs, y_ref, scr = refs[:2 * ng], refs[2 * ng], refs[2 * ng + 1]
    vals = []
    for gi, (_, dil) in enumerate(DSWA_GROUPS):
        for which in range(2):
            src = in_refs[2 * gi + which]
            if dil == 1:
                vals.append(src[0])
            else:
                halves = []
                for hh in range(DSWA_HEADS_PER_GROUP):
                    slot = (2 * gi + which) * DSWA_HEADS_PER_GROUP + hh
                    for r in range(dil):
                        scr[slot, pl.ds(r, tm // dil, stride=dil), :] = (
                            src[r, :, hh * DSWA_HEAD_DIM:(hh + 1) * DSWA_HEAD_DIM])
                    halves.append(scr[slot])
                vals.append(jnp.concatenate(halves, axis=1))
    lses = vals[1::2]
    mx = functools.reduce(jnp.maximum, lses)
    es = [jnp.exp(l - mx) for l in lses]
    den = functools.reduce(lambda a, b: a + b, es)
    y = functools.reduce(lambda a, b: a + b, [(e / den) * o for e, o in zip(es, vals[0::2])])
    y_ref[...] = y.astype(y_ref.dtype)


def _dswa_combine(outs_lses, *, batch, seq, tm):
    gw = DSWA_HEADS_PER_GROUP * DSWA_HEAD_DIM
    nblk = seq // tm
    in_specs = []
    for _, dil in DSWA_GROUPS:
        in_specs += [pl.BlockSpec((None, dil, tm // dil, gw), lambda b, i: (b, 0, i, 0))] * 2
    return pl.pallas_call(
        functools.partial(_dswa_combine_kernel, tm=tm),
        grid=(batch, nblk),
        in_specs=in_specs,
        out_specs=pl.BlockSpec((tm, gw), lambda b, i: (b * nblk + i, 0)),
        out_shape=jax.ShapeDtypeStruct((batch * seq, gw), BF16),
        scratch_shapes=[pltpu.VMEM((2 * len(DSWA_GROUPS) * DSWA_HEADS_PER_GROUP, tm, DSWA_HEAD_DIM), F32)],
        compiler_params=_cparams(("parallel", "parallel"), VMEM_SMALL),
        name="dswa_combine",
    )(*outs_lses)


def _merge_kernel(x_ref, g_ref, ya_ref, yb_ref, yc_ref, wga_ref, wgb_ref, wgc_ref, wa_ref, wb_ref, wc_ref, wo_ref,
                  out_ref, h_scr):
    @pl.when(pl.program_id(1) == 0)
    def _():
        xf = x_ref[...]
        h_scr[...] = (xf * _inv_rms(xf, xf.shape[-1]) * g_ref[...]).astype(BF16)
        out_ref[...] = xf

    h = h_scr[...]
    dot = lambda a, b: jnp.dot(a, b, preferred_element_type=F32)
    inter = (jax.nn.sigmoid(dot(h, wga_ref[...])) * dot(ya_ref[...], wa_ref[...])
             + jax.nn.sigmoid(dot(h, wgb_ref[...])) * dot(yb_ref[...], wb_ref[...])
             + jax.nn.sigmoid(dot(h, wgc_ref[...])) * dot(yc_ref[...], wc_ref[...]))
    out_ref[...] += dot(inter.astype(BF16), wo_ref[...])


def _merge(x, g, ya, yb, yc, wga, wgb, wgc, wa, wb, wc, wo, layer, *, tm, tn):
    t, d = x.shape
    row = lambda i, j: (i, 0)
    colw = lambda i, j: (layer, 0, j)
    return pl.pallas_call(
        _merge_kernel,
        grid=(t // tm, d // tn),
        in_specs=[pl.BlockSpec((tm, d), row), _layer_spec(g, layer),
                  pl.BlockSpec((tm, ya.shape[1]), row), pl.BlockSpec((tm, yb.shape[1]), row),
                  pl.BlockSpec((tm, yc.shape[1]), row)]
                 + [pl.BlockSpec((None, w.shape[1], tn), colw) for w in (wga, wgb, wgc, wa, wb, wc)]
                 + [pl.BlockSpec((None, tn, d), lambda i, j: (layer, j, 0))],
        out_specs=pl.BlockSpec((tm, d), row),
        out_shape=jax.ShapeDtypeStruct((t, d), F32),
        scratch_shapes=[pltpu.VMEM((tm, d), BF16)],
        compiler_params=_cparams(("parallel", "arbitrary"), VMEM_RESIDENT),
        name="merge",
    )(x, g, ya, yb, yc, wga, wgb, wgc, wa, wb, wc, wo)


def _pad_last(w, n):
    return jnp.pad(w, [(0, 0)] * (w.ndim - 1) + [(0, n - w.shape[-1])])


def _rot_half_cols(w):
    half = w.shape[-1] // 2
    return jnp.concatenate([-w[..., half:], w[..., :half]], axis=-1)


def _swap_half(g):
    half = g.shape[-1] // 2
    return jnp.concatenate([g[..., half:], g[..., :half]], axis=-1)


def _gain3(g):
    rope = g[:, MLA_NOPE:]
    return jnp.stack([g[:, :MLA_NOPE], _pad_last(rope, LANE), _pad_last(_swap_half(rope), LANE)], axis=1)


def _row(g):
    return g[:, None, :]


def kernel(x, positions, ffn1_norm, ffn1_w_gate, ffn1_w_up, ffn1_w_down, mix_norm, w_in, mla_cq_norm, mla_ckv_norm, mla_w_uq, mla_w_ukv, mla_q_norm, mla_k_norm, gla_w_gate2, gla_b_gate2, gla_o_norm, dswa_q_norm, dswa_k_norm, w_branch_a, w_branch_b, w_branch_c, w_out, ffn2_norm, ffn2_w_gate, ffn2_w_up, ffn2_w_down):
    batch, seq, d = x.shape
    depth = w_in.shape[0]
    d_ff = ffn1_w_gate.shape[-1]
    t = batch * seq
    tm, tm_ffn, tf = TOKEN_BLOCK, FFN_TOKEN_BLOCK, FFN_CHUNK
    ff_pad = -(-d_ff // tf) * tf
    assert seq % max(wd for wd, _ in DSWA_GROUPS) == 0 and seq % tm == 0 and t % tm_ffn == 0

    xf = x.reshape(t, d)
    half = MLA_ROPE // 2
    inv_freq = ROPE_THETA ** (-jnp.arange(half, dtype=F32) / half)
    inv_pad = jnp.pad(jnp.concatenate([inv_freq, inv_freq]), (0, LANE - MLA_ROPE)).reshape(1, LANE)
    cos, sin = _rope_tables(positions.reshape(t, 1), inv_pad, tm=tm)
    slopes = 2.0 ** (-ALIBI_MAX_EXP * jnp.arange(1, DSWA_HEADS + 1, dtype=F32) / DSWA_HEADS)

    o_cq, o_ckv, o_kr = 0, MLA_Q_RANK, MLA_Q_RANK + MLA_KV_RANK
    o_gla = o_kr + MLA_ROPE
    n_qkv_b = GLA_HEADS * (2 * GLA_DK + GLA_DV)
    o_glr = o_gla + n_qkv_b
    o_rb = o_glr + GLA_GATE_RANK
    o_c = o_rb + GLA_HEADS * GLA_DV
    o_ga = o_c + 3 * DSWA_HEADS * DSWA_HEAD_DIM
    o_gb, o_gc = o_ga + d, o_ga + 2 * d

    bf = lambda w: w.astype(BF16)

    def ffn_weights(wg, wu, wdn):
        zc = jnp.zeros((depth, d, ff_pad - d_ff), BF16)
        zr = jnp.zeros((depth, ff_pad - d_ff, d), BF16)
        return (jnp.concatenate([bf(wg), zc], axis=2), jnp.concatenate([bf(wu), zc], axis=2),
                jnp.concatenate([bf(wdn), zr], axis=1))

    ffn1 = (_row(ffn1_norm),) + ffn_weights(ffn1_w_gate, ffn1_w_up, ffn1_w_down)
    ffn2 = (_row(ffn2_norm),) + ffn_weights(ffn2_w_gate, ffn2_w_up, ffn2_w_down)
    gmix = _row(mix_norm)
    w_in = bf(w_in)
    w_kr = w_in[:, :, o_kr:o_kr + MLA_ROPE]
    w1 = bf(jnp.concatenate([w_in[:, :, o_cq:o_kr], _pad_last(w_kr, LANE), _pad_last(_rot_half_cols(w_kr), LANE)],
                            axis=-1))
    wq4 = mla_w_uq.reshape(depth, MLA_Q_RANK, MLA_HEADS, MLA_QK)
    wq_rope = wq4[..., MLA_NOPE:]
    wuq = jnp.concatenate([wq4[..., :MLA_NOPE], _pad_last(wq_rope, LANE), _pad_last(_rot_half_cols(wq_rope), LANE)],
                          axis=-1)
    wuq = bf(wuq.reshape(depth, MLA_Q_RANK, MLA_HEADS * 3 * LANE))
    wkv4 = mla_w_ukv.reshape(depth, MLA_KV_RANK, MLA_HEADS, MLA_NOPE + MLA_V)
    wukv = bf(jnp.concatenate([wkv4[..., :MLA_NOPE].reshape(depth, MLA_KV_RANK, -1),
                               wkv4[..., MLA_NOPE:].reshape(depth, MLA_KV_RANK, -1)], axis=-1))
    mla_params = (w1, _row(mla_cq_norm), _row(mla_ckv_norm), wuq, wukv, _gain3(mla_q_norm), _gain3(mla_k_norm))
    w_gla = bf(jnp.concatenate([w_in[:, :, o_gla:o_glr], _pad_last(w_in[:, :, o_glr:o_rb], LANE),
                                w_in[:, :, o_rb:o_c]], axis=-1))
    w2 = bf(jnp.pad(gla_w_gate2, ((0, 0), (0, LANE - GLA_GATE_RANK), (0, 0))))
    gla_params = (w_gla, w2, _row(gla_b_gate2))
    go = _row(gla_o_norm)
    dswa_params = (bf(w_in[:, :, o_c:o_ga]), _row(dswa_q_norm), _row(dswa_k_norm))
    merge_params = (bf(w_in[:, :, o_ga:o_gb]), bf(w_in[:, :, o_gb:o_gc]), bf(w_in[:, :, o_gc:o_gc + d]),
                    bf(w_branch_a), bf(w_branch_b), bf(w_branch_c), bf(w_out))

    for l in range(depth):
        xf = _ffn(xf, *ffn1, l, tm=tm_ffn, tf=tf)

        qa, ka, va = _mla_prep(xf, gmix, cos, sin, *mla_params, l, tm=tm)
        y_a = _flash(qa, ka, va, batch=batch, seq=seq, tq=FLASH_TILE)

        qb, kb, vb, la, sr = _gla_proj(xf, gmix, *gla_params, l, tm=tm)
        y_b = _gla(qb, kb, vb, la, sr, go, l, batch=batch, seq=seq, tb=GLA_TOKENS)

        qkv_c = _dswa_proj(xf, gmix, *dswa_params, l, batch=batch, seq=seq, tm=tm)
        outs_lses = []
        for gi in range(len(DSWA_GROUPS)):
            outs_lses += _dswa_group(slopes, *qkv_c[3 * gi:3 * gi + 3], group=gi)
        y_c = _dswa_combine(outs_lses, batch=batch, seq=seq, tm=tm)

        xf = _merge(xf, gmix, y_a, y_b, y_c, *merge_params, l, tm=tm, tn=MERGE_CHUNK)
        xf = _ffn(xf, *ffn2, l, tm=tm_ffn, tf=tf)
    return xf.reshape(batch, seq, d)
```

```python
import functools

import numpy as np
import jax
import jax.numpy as jnp
from jax import lax
from jax.experimental import pallas as pl
from jax.experimental.pallas import tpu as pltpu

F32 = jnp.float32
BF16 = jnp.bfloat16

MLA_HEADS = 6
MLA_Q_RANK = 512
MLA_KV_RANK = 256
MLA_NOPE = 128
MLA_ROPE = 64
MLA_V = 128
MLA_QK = MLA_NOPE + MLA_ROPE
MLA_QK_PAD = 256
GLA_HEADS = 4
GLA_DK = 64
GLA_DV = 128
GLA_GATE_RANK = 16
GLA_TAU = 16.0
GLA_CHUNK = 64
GLA_SUB = 8
GLA_LEVELS = (32, 16, 8)
DSWA_GROUPS = ((128, 1), (512, 4), (2048, 16))
DSWA_HEADS_PER_GROUP = 2
DSWA_HEADS = 6
DSWA_HEAD_DIM = 128
DSWA_W = 128
ROPE_THETA = 10000.0
ALIBI_MAX_EXP = 8.0
NORM_EPS = 1e-6
NEG_INF = -1e30
LOG2E = 1.4426950408889634

LANE = 128
BF16_ROWS = 16
MIB = 1024 * 1024

TOKEN_BLOCK = 512
FFN_TOKEN_BLOCK = 1024
FFN_CHUNK = 512
MERGE_CHUNK = 512
FLASH_TILE = 1024
GLA_TOKENS = 512
DSWA_MAX_Q = 1024
VMEM_FFN = 57
VMEM_RESIDENT = 52
VMEM_SCAN = 40
VMEM_SMALL = 32


def _cparams(sem, vmem_mib):
    return pltpu.CompilerParams(dimension_semantics=sem, vmem_limit_bytes=int(vmem_mib * MIB))


def _inv_rms(xf, n):
    return lax.rsqrt(jnp.sum(xf * xf, axis=-1, keepdims=True) / n + NORM_EPS)


def _const_spec(shape):
    nd = len(shape)
    return pl.BlockSpec(shape, lambda *_: (0,) * nd)


def _layer_spec(arr, layer):
    nd = arr.ndim - 1
    return pl.BlockSpec((None,) + arr.shape[1:], lambda *_: (layer,) + (0,) * nd)


def _ffn_kernel(x_ref, g_ref, wg_ref, wu_ref, wd_ref, o_ref, h_scr):
    def half_chunk(h):
        gate = jnp.dot(h, wg_ref[...], preferred_element_type=F32)
        up = jnp.dot(h, wu_ref[...], preferred_element_type=F32)
        inter = (gate * jax.nn.sigmoid(gate) * up).astype(BF16)
        return 0.5 * jnp.dot(inter, wd_ref[...], preferred_element_type=F32)

    @pl.when(pl.program_id(1) == 0)
    def _():
        xf = x_ref[...]
        h = (xf * _inv_rms(xf, xf.shape[-1]) * g_ref[...]).astype(BF16)
        h_scr[...] = h
        o_ref[...] = xf + half_chunk(h)

    @pl.when(pl.program_id(1) > 0)
    def _():
        o_ref[...] += half_chunk(h_scr[...])


def _ffn(x, g, wg, wu, wd, layer, *, tm, tf):
    t, d = x.shape
    fp = wg.shape[-1]
    return pl.pallas_call(
        _ffn_kernel,
        grid=(t // tm, fp // tf),
        in_specs=[
            pl.BlockSpec((tm, d), lambda i, j: (i, 0)),
            _layer_spec(g, layer),
            pl.BlockSpec((None, d, tf), lambda i, j: (layer, 0, j)),
            pl.BlockSpec((None, d, tf), lambda i, j: (layer, 0, j)),
            pl.BlockSpec((None, tf, d), lambda i, j: (layer, j, 0)),
        ],
        out_specs=pl.BlockSpec((tm, d), lambda i, j: (i, 0)),
        out_shape=jax.ShapeDtypeStruct((t, d), F32),
        scratch_shapes=[pltpu.VMEM((tm, d), BF16)],
        compiler_params=_cparams(("parallel", "arbitrary"), VMEM_FFN),
        name="ffn",
    )(x, g, wg, wu, wd)


def _rope_kernel(pos_ref, inv_ref, cos_ref, sin_ref):
    ang = pos_ref[...].astype(F32) * inv_ref[...]
    cos_ref[...] = jnp.cos(ang)
    sin_ref[...] = jnp.sin(ang)


def _rope_tables(pos_col, inv_pad, *, tm):
    t = pos_col.shape[0]
    return pl.pallas_call(
        _rope_kernel,
        grid=(t // tm,),
        in_specs=[pl.BlockSpec((tm, 1), lambda i: (i, 0)), _const_spec((1, LANE))],
        out_specs=[pl.BlockSpec((tm, LANE), lambda i: (i, 0))] * 2,
        out_shape=[jax.ShapeDtypeStruct((t, LANE), F32)] * 2,
        compiler_params=_cparams(("parallel",), VMEM_SMALL),
        name="rope_tables",
    )(pos_col, inv_pad)


def _mla_prep_kernel(x_ref, g_ref, cos_ref, sin_ref, w1_ref, gcq_ref, gckv_ref, wuq_ref, wukv_ref,
                     gq_ref, gk_ref, q_ref, k_ref, v_ref):
    xf = x_ref[...]
    h = (xf * _inv_rms(xf, xf.shape[-1]) * g_ref[...]).astype(BF16)
    u = jnp.dot(h, w1_ref[...], preferred_element_type=F32)
    c_q = u[:, :MLA_Q_RANK]
    c_kv = u[:, MLA_Q_RANK:MLA_Q_RANK + MLA_KV_RANK]
    kr = u[:, MLA_Q_RANK + MLA_KV_RANK:MLA_Q_RANK + MLA_KV_RANK + LANE]
    krot = u[:, MLA_Q_RANK + MLA_KV_RANK + LANE:]
    c_q = (c_q * _inv_rms(c_q, MLA_Q_RANK) * gcq_ref[...]).astype(BF16)
    c_kv = (c_kv * _inv_rms(c_kv, MLA_KV_RANK) * gckv_ref[...]).astype(BF16)
    qall = jnp.dot(c_q, wuq_ref[...], preferred_element_type=F32)
    kv = jnp.dot(c_kv, wukv_ref[...], preferred_element_type=F32)
    cos = cos_ref[...]
    sin = sin_ref[...]
    scale = MLA_QK ** -0.5 * LOG2E
    gq_n, gq_r, gq_t = gq_ref[0:1, :], gq_ref[1:2, :], gq_ref[2:3, :]
    gk_n, gk_r, gk_t = gk_ref[0:1, :], gk_ref[1:2, :], gk_ref[2:3, :]
    k_roped = kr * gk_r * cos + krot * gk_t * sin
    kr_ss = jnp.sum(kr * kr, axis=-1, keepdims=True)
    for hh in range(MLA_HEADS):
        base = hh * 3 * LANE
        nope = qall[:, base:base + LANE]
        rope = qall[:, base + LANE:base + 2 * LANE]
        rot = qall[:, base + 2 * LANE:base + 3 * LANE]
        ss = jnp.sum(nope * nope, axis=-1, keepdims=True) + jnp.sum(rope * rope, axis=-1, keepdims=True)
        r = lax.rsqrt(ss / MLA_QK + NORM_EPS) * scale
        q_ref[hh, :, 0:LANE] = (nope * r * gq_n).astype(BF16)
        q_ref[hh, :, LANE:2 * LANE] = ((rope * gq_r * cos + rot * gq_t * sin) * r).astype(BF16)
        kn = kv[:, hh * LANE:(hh + 1) * LANE]
        rk = lax.rsqrt((jnp.sum(kn * kn, axis=-1, keepdims=True) + kr_ss) / MLA_QK + NORM_EPS)
        k_ref[hh, :, 0:LANE] = (kn * rk * gk_n).astype(BF16)
        k_ref[hh, :, LANE:2 * LANE] = (k_roped * rk).astype(BF16)
        v_ref[hh] = kv[:, (MLA_HEADS + hh) * LANE:(MLA_HEADS + hh + 1) * LANE].astype(BF16)


def _mla_prep(x, g, cos, sin, w1, gcq, gckv, wuq, wukv, gq3, gk3, layer, *, tm):
    t, d = x.shape
    row = lambda i: (i, 0)
    hrow = lambda i: (0, i, 0)
    return pl.pallas_call(
        _mla_prep_kernel,
        grid=(t // tm,),
        in_specs=[pl.BlockSpec((tm, d), row), _layer_spec(g, layer),
                  pl.BlockSpec((tm, LANE), row), pl.BlockSpec((tm, LANE), row)]
                 + [_layer_spec(p, layer) for p in (w1, gcq, gckv, wuq, wukv, gq3, gk3)],
        out_specs=[
            pl.BlockSpec((MLA_HEADS, tm, MLA_QK_PAD), hrow),
            pl.BlockSpec((MLA_HEADS, tm, MLA_QK_PAD), hrow),
            pl.BlockSpec((MLA_HEADS, tm, MLA_V), hrow),
        ],
        out_shape=[
            jax.ShapeDtypeStruct((MLA_HEADS, t, MLA_QK_PAD), BF16),
            jax.ShapeDtypeStruct((MLA_HEADS, t, MLA_QK_PAD), BF16),
            jax.ShapeDtypeStruct((MLA_HEADS, t, MLA_V), BF16),
        ],
        compiler_params=_cparams(("parallel",), VMEM_RESIDENT),
        name="mla_prep",
    )(x, g, cos, sin, w1, gcq, gckv, wuq, wukv, gq3, gk3)


def _flash_kernel(q_ref, k_ref, v_ref, o_ref, sa_scr, sb_scr, m_scr, acc_scr, *, tq):
    qi = pl.program_id(2)
    m_scr[...] = jnp.full_like(m_scr, NEG_INF)
    acc_scr[...] = jnp.zeros_like(acc_scr)
    q = q_ref[...]
    ones = jnp.ones((tq, MLA_V), BF16)

    def scores(tile, s_scr):
        r0 = pl.multiple_of(tile * tq, tq)
        s_scr[...] = lax.dot_general(q, k_ref[pl.ds(r0, tq), :], (((1,), (1,)), ((), ())),
                                     preferred_element_type=F32)

    def accumulate(tile, s_scr, diagonal):
        r0 = pl.multiple_of(tile * tq, tq)
        sc = s_scr[...]
        if diagonal:
            row = lax.broadcasted_iota(jnp.int32, sc.shape, 0)
            col = lax.broadcasted_iota(jnp.int32, sc.shape, 1)
            sc = jnp.where(col <= row, sc, NEG_INF)
        m_prev = m_scr[...]
        m_new = jnp.maximum(m_prev, jnp.max(sc, axis=-1, keepdims=True))
        alpha = jnp.exp2(m_prev - m_new)
        p = jnp.exp2(sc - jnp.tile(m_new, (1, tq // LANE)))
        v1 = jnp.concatenate([v_ref[pl.ds(r0, tq), :], ones], axis=1)
        pv = jnp.dot(p.astype(BF16), v1, preferred_element_type=F32)
        acc_scr[...] = jnp.tile(alpha, (1, 2)) * acc_scr[...] + pv
        m_scr[...] = m_new

    scores(0, sa_scr)

    def body(j, carry):
        scores(2 * j + 1, sb_scr)
        accumulate(2 * j, sa_scr, False)
        scores(2 * j + 2, sa_scr)
        accumulate(2 * j + 1, sb_scr, False)
        return carry

    lax.fori_loop(0, qi // 2, body, 0)

    @pl.when(qi % 2 == 0)
    def _():
        accumulate(qi, sa_scr, True)

    @pl.when(qi % 2 == 1)
    def _():
        scores(qi, sb_scr)
        accumulate(qi - 1, sa_scr, False)
        accumulate(qi, sb_scr, True)

    acc = acc_scr[...]
    o_ref[...] = (acc[:, 0:MLA_V] / acc[:, MLA_V:]).astype(o_ref.dtype)


def _flash(q, k, v, *, batch, seq, tq):
    assert seq % tq == 0
    nq = seq // tq
    t = batch * seq
    return pl.pallas_call(
        functools.partial(_flash_kernel, tq=tq),
        grid=(batch, MLA_HEADS, nq),
        in_specs=[
            pl.BlockSpec((None, tq, MLA_QK_PAD), lambda b, h, i: (h, b * nq + i, 0)),
            pl.BlockSpec((None, seq, MLA_QK_PAD), lambda b, h, i: (h, b, 0)),
            pl.BlockSpec((None, seq, MLA_V), lambda b, h, i: (h, b, 0)),
        ],
        out_specs=pl.BlockSpec((tq, MLA_V), lambda b, h, i: (b * nq + i, h)),
        out_shape=jax.ShapeDtypeStruct((t, MLA_HEADS * MLA_V), BF16),
        scratch_shapes=[pltpu.VMEM((tq, tq), F32), pltpu.VMEM((tq, tq), F32),
                        pltpu.VMEM((tq, MLA_V), F32), pltpu.VMEM((tq, 2 * MLA_V), F32)],
        compiler_params=_cparams(("parallel", "parallel", "arbitrary"), VMEM_RESIDENT),
        name="mla_flash",
    )(q, k, v)


def _gla_proj_kernel(x_ref, g_ref, w_ref, w2_ref, b2_ref, q_ref, k_ref, v_ref, la_ref, sr_ref):
    xf = x_ref[...]
    h = (xf * _inv_rms(xf, xf.shape[-1]) * g_ref[...]).astype(BF16)
    u = jnp.dot(h, w_ref[...], preferred_element_type=F32)
    nk = GLA_HEADS * GLA_DK
    nv = GLA_HEADS * GLA_DV
    q_ref[...] = u[:, 0:nk] * (GLA_DK ** -0.5)
    k_ref[...] = u[:, nk:2 * nk]
    v_ref[...] = u[:, 2 * nk:2 * nk + nv]
    gate_lr = u[:, 2 * nk + nv:2 * nk + nv + LANE].astype(BF16)
    z = jnp.dot(gate_lr, w2_ref[...], preferred_element_type=F32) + b2_ref[...]
    la_ref[...] = -(jnp.maximum(-z, 0.0) + jnp.log1p(jnp.exp(-jnp.abs(z)))) / GLA_TAU
    r = u[:, 2 * nk + nv + LANE:]
    sr_ref[...] = r * jax.nn.sigmoid(r)


def _gla_proj(x, g, w, w2, b2, layer, *, tm):
    t, d = x.shape
    nk = GLA_HEADS * GLA_DK
    nv = GLA_HEADS * GLA_DV
    row = lambda i: (i, 0)
    return pl.pallas_call(
        _gla_proj_kernel,
        grid=(t // tm,),
        in_specs=[pl.BlockSpec((tm, d), row)] + [_layer_spec(p, layer) for p in (g, w, w2, b2)],
        out_specs=[pl.BlockSpec((tm, nk), row), pl.BlockSpec((tm, nk), row), pl.BlockSpec((tm, nv), row),
                   pl.BlockSpec((tm, nk), row), pl.BlockSpec((tm, nv), row)],
        out_shape=[jax.ShapeDtypeStruct((t, nk), F32), jax.ShapeDtypeStruct((t, nk), F32),
                   jax.ShapeDtypeStruct((t, nv), F32), jax.ShapeDtypeStruct((t, nk), F32),
                   jax.ShapeDtypeStruct((t, nv), F32)],
        compiler_params=_cparams(("parallel",), VMEM_RESIDENT),
        name="gla_proj",
    )(x, g, w, w2, b2)


def _gla_constants():
    c = GLA_CHUNK
    idx = np.arange(c)
    mats = [(idx[None, :] <= idx[:, None])]
    for lvl in GLA_LEVELS:
        p = ((idx // lvl) | 1) * lvl
        qrow = (idx >= p)[:, None] & (idx[None, :] >= p[:, None]) & (idx[None, :] <= idx[:, None])
        krow = (idx < p)[:, None] & (idx[None, :] > idx[:, None]) & (idx[None, :] < p[:, None])
        mats.append(qrow | krow)
    mats.append(idx[None, :] > idx[:, None])
    mstack = np.concatenate(mats, axis=0).astype(np.float32)
    lmask = []
    for lvl in GLA_LEVELS:
        blk = idx // lvl
        m = ((blk[:, None] % 2) == 1) & (blk[None, :] == blk[:, None] - 1)
        lmask.append(np.tile(m.astype(np.float32), (GLA_HEADS, 1)))
    lmask = np.stack(lmask, axis=0)
    hk = np.arange(GLA_HEADS * GLA_DK) // GLA_DK
    hv = np.arange(GLA_HEADS * GLA_DV) // GLA_DV
    hrow = np.arange(GLA_HEADS * c) // c
    headmask = (hrow[:, None] == hk[None, :]).astype(np.float32)
    e_ind = (hk[:, None] == hv[None, :]).astype(np.float32)
    return mstack, lmask, headmask, e_ind


def _gla_kernel(q_ref, k_ref, v_ref, la_ref, sr_ref, go_ref, mst_ref, lmask_ref, hmask_ref, eind_ref, eindt_ref,
                y_ref, st_scr, *, chunks):
    c = GLA_CHUNK
    nk = GLA_HEADS * GLA_DK
    nv = GLA_HEADS * GLA_DV
    nsub = c // GLA_SUB

    @pl.when(pl.program_id(1) == 0)
    def _():
        st_scr[...] = jnp.zeros_like(st_scr)

    def chunk(ci, carry):
        r0 = pl.multiple_of(ci * c, c)
        q = q_ref[pl.ds(r0, c), :]
        k = k_ref[pl.ds(r0, c), :]
        v = v_ref[pl.ds(r0, c), :]
        a = la_ref[pl.ds(r0, c), :]
        a1 = a.astype(BF16)
        r1 = a - a1.astype(F32)
        a2 = r1.astype(BF16)
        a3 = (r1 - a2.astype(F32)).astype(BF16)
        mst = mst_ref[...]
        ex = (jnp.dot(mst, a1, preferred_element_type=F32) + jnp.dot(mst, a2, preferred_element_type=F32)
              + jnp.dot(mst, a3, preferred_element_type=F32))
        b = ex[0:c]
        f = jnp.exp(ex)
        v_bf = v.astype(BF16)

        amat = jnp.zeros((GLA_HEADS * c, c), F32)
        hmask = hmask_ref[...]
        for li in range(len(GLA_LEVELS)):
            fl = f[(li + 1) * c:(li + 2) * c]
            qt = q * fl
            kt = (k * fl).astype(BF16)
            qs = (jnp.concatenate([qt] * GLA_HEADS, axis=0) * hmask).astype(BF16)
            sc = lax.dot_general(qs, kt, (((1,), (1,)), ((), ())), preferred_element_type=F32)
            amat = amat + sc * lmask_ref[li]
        a_bf = amat.astype(BF16)
        o = jnp.concatenate(
            [jnp.dot(a_bf[hh * c:(hh + 1) * c], v_bf[:, hh * GLA_DV:(hh + 1) * GLA_DV], preferred_element_type=F32)
             for hh in range(GLA_HEADS)], axis=1)

        qb = (q * f[0:c]).astype(BF16)
        st = st_scr[...]
        o = o + lax.dot_general(qb, st.astype(BF16), (((1,), (1,)), ((), ())), preferred_element_type=F32)

        q3 = q.reshape(nsub, GLA_SUB, nk)
        k3 = k.reshape(nsub, GLA_SUB, nk)
        b3 = b.reshape(nsub, GLA_SUB, nk)
        v3 = v.reshape(nsub, GLA_SUB, nv)
        tt = lax.broadcasted_iota(jnp.int32, (nsub, GLA_SUB, nk), 1)
        od = jnp.zeros((nsub, GLA_SUB, nv), F32)
        eind = eind_ref[...]
        for s in range(GLA_SUB):
            w = jnp.exp(jnp.minimum(b3 - b3[:, s:s + 1, :], 0.0))
            x = jnp.where(tt >= s, q3 * k3[:, s:s + 1, :] * w, 0.0)
            rr = jnp.dot(x.reshape(c, nk).astype(BF16), eind, preferred_element_type=F32)
            od = od + rr.reshape(nsub, GLA_SUB, nv) * v3[:, s:s + 1, :]
        o = o + od.reshape(c, nv)

        kend = (k * f[(len(GLA_LEVELS) + 1) * c:(len(GLA_LEVELS) + 2) * c]).astype(BF16)
        upd = lax.dot_general(v_bf, kend, (((0,), (0,)), ((), ())), preferred_element_type=F32)
        st_scr[...] = st * f[c - 1:c] + upd * eindt_ref[...]

        go = go_ref[...]
        sr = sr_ref[pl.ds(r0, c), :]
        ys = []
        for hh in range(GLA_HEADS):
            oh = o[:, hh * GLA_DV:(hh + 1) * GLA_DV]
            ys.append(oh * _inv_rms(oh, GLA_DV) * go)
        y_ref[pl.ds(r0, c), :] = (jnp.concatenate(ys, axis=1) * sr).astype(y_ref.dtype)
        return carry

    lax.fori_loop(0, chunks, chunk, 0, unroll=2)


def _gla(q, k, v, la, sr, go, layer, *, batch, seq, tb):
    nk = GLA_HEADS * GLA_DK
    nv = GLA_HEADS * GLA_DV
    nb = seq // tb
    mstack, lmask, headmask, e_ind = _gla_constants()
    consts = [jnp.asarray(mstack, BF16), jnp.asarray(lmask, F32), jnp.asarray(headmask, F32),
              jnp.asarray(e_ind, BF16), jnp.asarray(e_ind.T, F32)]
    row = lambda b, i: (b * nb + i, 0)
    return pl.pallas_call(
        functools.partial(_gla_kernel, chunks=tb // GLA_CHUNK),
        grid=(batch, nb),
        in_specs=[pl.BlockSpec((tb, nk), row), pl.BlockSpec((tb, nk), row), pl.BlockSpec((tb, nv), row),
                  pl.BlockSpec((tb, nk), row), pl.BlockSpec((tb, nv), row), _layer_spec(go, layer)]
                 + [_const_spec(cst.shape) for cst in consts],
        out_specs=pl.BlockSpec((tb, nv), row),
        out_shape=jax.ShapeDtypeStruct((batch * seq, nv), BF16),
        scratch_shapes=[pltpu.VMEM((nv, nk), F32)],
        compiler_params=_cparams(("parallel", "arbitrary"), VMEM_SCAN),
        name="gla_scan",
    )(q, k, v, la, sr, go, *consts)


def _dswa_proj_kernel(x_ref, g_ref, w_ref, gq_ref, gk_ref, *refs, tm):
    ng = len(DSWA_GROUPS)
    out_refs = refs[:3 * ng]
    q_scr, k_scr, v_scr = refs[3 * ng:]
    xf = x_ref[...]
    h = (xf * _inv_rms(xf, xf.shape[-1]) * g_ref[...]).astype(BF16)
    u = jnp.dot(h, w_ref[...], preferred_element_type=F32)
    n = DSWA_HEADS * DSWA_HEAD_DIM
    gq = gq_ref[...]
    gk = gk_ref[...]
    scale = DSWA_HEAD_DIM ** -0.5
    for hh in range(DSWA_HEADS):
        sl = slice(hh * DSWA_HEAD_DIM, (hh + 1) * DSWA_HEAD_DIM)
        qh = u[:, sl]
        q_scr[hh] = qh * _inv_rms(qh, DSWA_HEAD_DIM) * gq * scale
        kh = u[:, n + hh * DSWA_HEAD_DIM:n + (hh + 1) * DSWA_HEAD_DIM]
        k_scr[hh] = kh * _inv_rms(kh, DSWA_HEAD_DIM) * gk
        v_scr[hh] = u[:, 2 * n + hh * DSWA_HEAD_DIM:2 * n + (hh + 1) * DSWA_HEAD_DIM]
    for gi, (_, dil) in enumerate(DSWA_GROUPS):
        for src, dst in zip((q_scr, k_scr, v_scr), out_refs[3 * gi:3 * gi + 3]):
            for hh in range(DSWA_HEADS_PER_GROUP):
                head = gi * DSWA_HEADS_PER_GROUP + hh
                cols = slice(hh * DSWA_HEAD_DIM, (hh + 1) * DSWA_HEAD_DIM)
                for r in range(dil):
                    rows = pl.ds(r, tm // dil, stride=dil) if dil > 1 else slice(None)
                    dst[r, :, cols] = src[head, rows, :].astype(BF16)


def _dswa_proj(x, g, w, gq, gk, layer, *, batch, seq, tm):
    t, d = x.shape
    n = DSWA_HEADS * DSWA_HEAD_DIM
    gw = DSWA_HEADS_PER_GROUP * DSWA_HEAD_DIM
    nblk = seq // tm
    out_specs, out_shape = [], []
    for _, dil in DSWA_GROUPS:
        assert tm % (BF16_ROWS * dil) == 0
        out_specs += [pl.BlockSpec((None, dil, tm // dil, gw), lambda b, i: (b, 0, i, 0))] * 3
        out_shape += [jax.ShapeDtypeStruct((batch, dil, seq // dil, gw), BF16)] * 3
    return pl.pallas_call(
        functools.partial(_dswa_proj_kernel, tm=tm),
        grid=(batch, nblk),
        in_specs=[pl.BlockSpec((tm, d), lambda b, i: (b * nblk + i, 0))]
                 + [_layer_spec(p, layer) for p in (g, w, gq, gk)],
        out_specs=out_specs,
        out_shape=out_shape,
        scratch_shapes=[pltpu.VMEM((DSWA_HEADS, tm, DSWA_HEAD_DIM), F32)] * 3,
        compiler_params=_cparams(("parallel", "parallel"), VMEM_RESIDENT),
        name="dswa_proj",
    )(x, g, w, gq, gk)


def _dswa_kernel(slope_ref, q_ref, kc_ref, kp_ref, vc_ref, vp_ref, o_ref, l_ref, *, dilation, group, nb):
    w = DSWA_W
    ib = pl.program_id(2)
    i = lax.broadcasted_iota(jnp.int32, (w, 2 * w), 0)
    j = lax.broadcasted_iota(jnp.int32, (w, 2 * w), 1)
    steps = w + i - j
    dist = (steps * dilation).astype(F32)
    in_window = (steps >= 0) & (steps <= w)
    for hh in range(DSWA_HEADS_PER_GROUP):
        cols = slice(hh * DSWA_HEAD_DIM, (hh + 1) * DSWA_HEAD_DIM)
        bias_all = jnp.where(in_window, -slope_ref[group * DSWA_HEADS_PER_GROUP + hh] * dist, NEG_INF)
        bias_first = jnp.where(j >= w, bias_all, NEG_INF)
        for jb in range(nb):
            rows = slice(jb * w, (jb + 1) * w)
            if jb == 0:
                kprev, vprev = kp_ref[:, cols], vp_ref[:, cols]
                bias_j = jnp.where(ib > 0, bias_all, bias_first)
            else:
                prows = slice((jb - 1) * w, jb * w)
                kprev, vprev = kc_ref[prows, cols], vc_ref[prows, cols]
                bias_j = bias_all
            keys = jnp.concatenate([kprev, kc_ref[rows, cols]], axis=0)
            vals = jnp.concatenate([vprev, vc_ref[rows, cols]], axis=0)
            sc = lax.dot_general(q_ref[rows, cols], keys, (((1,), (1,)), ((), ())), preferred_element_type=F32)
            sc = jnp.where(bias_j > 0.5 * NEG_INF, sc + bias_j, NEG_INF)
            m = jnp.max(sc, axis=-1, keepdims=True)
            p = jnp.exp(sc - m)
            lsum = jnp.sum(p, axis=-1, keepdims=True)
            o_ref[rows, cols] = jnp.dot(p.astype(BF16), vals, preferred_element_type=F32) / lsum
            l_ref[rows, cols] = jnp.broadcast_to(m + jnp.log(lsum), (w, DSWA_HEAD_DIM))


def _dswa_group(slopes, qg, kg, vg, *, group):
    window, dilation = DSWA_GROUPS[group]
    assert window // dilation == DSWA_W
    batch, _, sd, gw = qg.shape
    tq = min(DSWA_MAX_Q, sd)
    nb = tq // DSWA_W
    cur = pl.BlockSpec((None, None, tq, gw), lambda b, r, i: (b, r, i, 0))
    prev = pl.BlockSpec((None, None, DSWA_W, gw), lambda b, r, i: (b, r, jnp.maximum(i * nb - 1, 0), 0))
    return pl.pallas_call(
        functools.partial(_dswa_kernel, dilation=dilation, group=group, nb=nb),
        grid=(batch, dilation, sd // tq),
        in_specs=[pl.BlockSpec(memory_space=pltpu.SMEM), cur, cur, prev, cur, prev],
        out_specs=[cur] * 2,
        out_shape=[jax.ShapeDtypeStruct(qg.shape, F32)] * 2,
        compiler_params=_cparams(("parallel", "parallel", "arbitrary"), VMEM_SMALL),
        name="dswa_g%d" % group,
    )(slopes, qg, kg, kg, vg, vg)


def _dswa_combine_kernel(*refs, tm):
    ng = len(DSWA_GROUPS)
    in_refs, y_ref, scr = refs[:2 * ng], refs[2 * ng], refs[2 * ng + 1]
    vals = []
    for gi, (_, dil) in enumerate(DSWA_GROUPS):
        for which in range(2):
            src = in_refs[2 * gi + which]
            if dil == 1:
                vals.append(src[0])
            else:
                halves = []
                for hh in range(DSWA_HEADS_PER_GROUP):
                    slot = (2 * gi + which) * DSWA_HEADS_PER_GROUP + hh
                    for r in range(dil):
                        scr[slot, pl.ds(r, tm // dil, stride=dil), :] = (
                            src[r, :, hh * DSWA_HEAD_DIM:(hh + 1) * DSWA_HEAD_DIM])
                    halves.append(scr[slot])
                vals.append(jnp.concatenate(halves, axis=1))
    lses = vals[1::2]
    mx = functools.reduce(jnp.maximum, lses)
    es = [jnp.exp(l - mx) for l in lses]
    den = functools.reduce(lambda a, b: a + b, es)
    y = functools.reduce(lambda a, b: a + b, [(e / den) * o for e, o in zip(es, vals[0::2])])
    y_ref[...] = y.astype(y_ref.dtype)


def _dswa_combine(outs_lses, *, batch, seq, tm):
    gw = DSWA_HEADS_PER_GROUP * DSWA_HEAD_DIM
    nblk = seq // tm
    in_specs = []
    for _, dil in DSWA_GROUPS:
        in_specs += [pl.BlockSpec((None, dil, tm // dil, gw), lambda b, i: (b, 0, i, 0))] * 2
    return pl.pallas_call(
        functools.partial(_dswa_combine_kernel, tm=tm),
        grid=(batch, nblk),
        in_specs=in_specs,
        out_specs=pl.BlockSpec((tm, gw), lambda b, i: (b * nblk + i, 0)),
        out_shape=jax.ShapeDtypeStruct((batch * seq, gw), BF16),
        scratch_shapes=[pltpu.VMEM((2 * len(DSWA_GROUPS) * DSWA_HEADS_PER_GROUP, tm, DSWA_HEAD_DIM), F32)],
        compiler_params=_cparams(("parallel", "parallel"), VMEM_SMALL),
        name="dswa_combine",
    )(*outs_lses)


def _merge_kernel(x_ref, g_ref, ya_ref, yb_ref, yc_ref, wga_ref, wgb_ref, wgc_ref, wa_ref, wb_ref, wc_ref, wo_ref,
                  out_ref, h_scr):
    dot = lambda a, b: jnp.dot(a, b, preferred_element_type=F32)

    def chunk(h):
        inter = (jax.nn.sigmoid(dot(h, wga_ref[...])) * dot(ya_ref[...], wa_ref[...])
                 + jax.nn.sigmoid(dot(h, wgb_ref[...])) * dot(yb_ref[...], wb_ref[...])
                 + jax.nn.sigmoid(dot(h, wgc_ref[...])) * dot(yc_ref[...], wc_ref[...]))
        return dot(inter.astype(BF16), wo_ref[...])

    @pl.when(pl.program_id(1) == 0)
    def _():
        xf = x_ref[...]
        h = (xf * _inv_rms(xf, xf.shape[-1]) * g_ref[...]).astype(BF16)
        h_scr[...] = h
        out_ref[...] = xf + chunk(h)

    @pl.when(pl.program_id(1) > 0)
    def _():
        out_ref[...] += chunk(h_scr[...])


def _merge(x, g, ya, yb, yc, wga, wgb, wgc, wa, wb, wc, wo, layer, *, tm, tn):
    t, d = x.shape
    row = lambda i, j: (i, 0)
    colw = lambda i, j: (layer, 0, j)
    return pl.pallas_call(
        _merge_kernel,
        grid=(t // tm, d // tn),
        in_specs=[pl.BlockSpec((tm, d), row), _layer_spec(g, layer),
                  pl.BlockSpec((tm, ya.shape[1]), row), pl.BlockSpec((tm, yb.shape[1]), row),
                  pl.BlockSpec((tm, yc.shape[1]), row)]
                 + [pl.BlockSpec((None, w.shape[1], tn), colw) for w in (wga, wgb, wgc, wa, wb, wc)]
                 + [pl.BlockSpec((None, tn, d), lambda i, j: (layer, j, 0))],
        out_specs=pl.BlockSpec((tm, d), row),
        out_shape=jax.ShapeDtypeStruct((t, d), F32),
        scratch_shapes=[pltpu.VMEM((tm, d), BF16)],
        compiler_params=_cparams(("parallel", "arbitrary"), VMEM_RESIDENT),
        name="merge",
    )(x, g, ya, yb, yc, wga, wgb, wgc, wa, wb, wc, wo)


def _pad_last(w, n):
    return jnp.pad(w, [(0, 0)] * (w.ndim - 1) + [(0, n - w.shape[-1])])


def _rot_half_cols(w):
    half = w.shape[-1] // 2
    return jnp.concatenate([-w[..., half:], w[..., :half]], axis=-1)


def _swap_half(g):
    half = g.shape[-1] // 2
    return jnp.concatenate([g[..., half:], g[..., :half]], axis=-1)


def _gain3(g):
    rope = g[:, MLA_NOPE:]
    return jnp.stack([g[:, :MLA_NOPE], _pad_last(rope, LANE), _pad_last(_swap_half(rope), LANE)], axis=1)


def _row(g):
    return g[:, None, :]


def kernel(x, positions, ffn1_norm, ffn1_w_gate, ffn1_w_up, ffn1_w_down, mix_norm, w_in, mla_cq_norm, mla_ckv_norm, mla_w_uq, mla_w_ukv, mla_q_norm, mla_k_norm, gla_w_gate2, gla_b_gate2, gla_o_norm, dswa_q_norm, dswa_k_norm, w_branch_a, w_branch_b, w_branch_c, w_out, ffn2_norm, ffn2_w_gate, ffn2_w_up, ffn2_w_down):
    batch, seq, d = x.shape
    depth = w_in.shape[0]
    d_ff = ffn1_w_gate.shape[-1]
    t = batch * seq
    tm, tm_ffn, tf = TOKEN_BLOCK, FFN_TOKEN_BLOCK, FFN_CHUNK
    ff_pad = -(-d_ff // tf) * tf
    assert seq % max(wd for wd, _ in DSWA_GROUPS) == 0 and seq % tm == 0 and t % tm_ffn == 0

    xf = x.reshape(t, d)
    half = MLA_ROPE // 2
    inv_freq = ROPE_THETA ** (-jnp.arange(half, dtype=F32) / half)
    inv_pad = jnp.pad(jnp.concatenate([inv_freq, inv_freq]), (0, LANE - MLA_ROPE)).reshape(1, LANE)
    cos, sin = _rope_tables(positions.reshape(t, 1), inv_pad, tm=tm)
    slopes = 2.0 ** (-ALIBI_MAX_EXP * jnp.arange(1, DSWA_HEADS + 1, dtype=F32) / DSWA_HEADS)

    o_cq, o_ckv, o_kr = 0, MLA_Q_RANK, MLA_Q_RANK + MLA_KV_RANK
    o_gla = o_kr + MLA_ROPE
    n_qkv_b = GLA_HEADS * (2 * GLA_DK + GLA_DV)
    o_glr = o_gla + n_qkv_b
    o_rb = o_glr + GLA_GATE_RANK
    o_c = o_rb + GLA_HEADS * GLA_DV
    o_ga = o_c + 3 * DSWA_HEADS * DSWA_HEAD_DIM
    o_gb, o_gc = o_ga + d, o_ga + 2 * d

    bf = lambda w: w.astype(BF16)

    def ffn_weights(wg, wu, wdn):
        zc = jnp.zeros((depth, d, ff_pad - d_ff), BF16)
        zr = jnp.zeros((depth, ff_pad - d_ff, d), BF16)
        return (jnp.concatenate([bf(wg), zc], axis=2), jnp.concatenate([bf(wu), zc], axis=2),
                jnp.concatenate([bf(wdn), zr], axis=1))

    ffn1 = (_row(ffn1_norm),) + ffn_weights(ffn1_w_gate, ffn1_w_up, ffn1_w_down)
    ffn2 = (_row(ffn2_norm),) + ffn_weights(ffn2_w_gate, ffn2_w_up, ffn2_w_down)
    gmix = _row(mix_norm)
    w_in = bf(w_in)
    w_kr = w_in[:, :, o_kr:o_kr + MLA_ROPE]
    w1 = bf(jnp.concatenate([w_in[:, :, o_cq:o_kr], _pad_last(w_kr, LANE), _pad_last(_rot_half_cols(w_kr), LANE)],
                            axis=-1))
    wq4 = mla_w_uq.reshape(depth, MLA_Q_RANK, MLA_HEADS, MLA_QK)
    wq_rope = wq4[..., MLA_NOPE:]
    wuq = jnp.concatenate([wq4[..., :MLA_NOPE], _pad_last(wq_rope, LANE), _pad_last(_rot_half_cols(wq_rope), LANE)],
                          axis=-1)
    wuq = bf(wuq.reshape(depth, MLA_Q_RANK, MLA_HEADS * 3 * LANE))
    wkv4 = mla_w_ukv.reshape(depth, MLA_KV_RANK, MLA_HEADS, MLA_NOPE + MLA_V)
    wukv = bf(jnp.concatenate([wkv4[..., :MLA_NOPE].reshape(depth, MLA_KV_RANK, -1),
                               wkv4[..., MLA_NOPE:].reshape(depth, MLA_KV_RANK, -1)], axis=-1))
    mla_params = (w1, _row(mla_cq_norm), _row(mla_ckv_norm), wuq, wukv, _gain3(mla_q_norm), _gain3(mla_k_norm))
    w_gla = bf(jnp.concatenate([w_in[:, :, o_gla:o_glr], _pad_last(w_in[:, :, o_glr:o_rb], LANE),
                                w_in[:, :, o_rb:o_c]], axis=-1))
    w2 = bf(jnp.pad(gla_w_gate2, ((0, 0), (0, LANE - GLA_GATE_RANK), (0, 0))))
    gla_params = (w_gla, w2, _row(gla_b_gate2))
    go = _row(gla_o_norm)
    dswa_params = (bf(w_in[:, :, o_c:o_ga]), _row(dswa_q_norm), _row(dswa_k_norm))
    merge_params = (bf(w_in[:, :, o_ga:o_gb]), bf(w_in[:, :, o_gb:o_gc]), bf(w_in[:, :, o_gc:o_gc + d]),
                    bf(w_branch_a), bf(w_branch_b), bf(w_branch_c), bf(w_out))

    for l in range(depth):
        xf = _ffn(xf, *ffn1, l, tm=tm_ffn, tf=tf)

        qa, ka, va = _mla_prep(xf, gmix, cos, sin, *mla_params, l, tm=tm)
        y_a = _flash(qa, ka, va, batch=batch, seq=seq, tq=FLASH_TILE)

        qb, kb, vb, la, sr = _gla_proj(xf, gmix, *gla_params, l, tm=tm)
        y_b = _gla(qb, kb, vb, la, sr, go, l, batch=batch, seq=seq, tb=GLA_TOKENS)

        qkv_c = _dswa_proj(xf, gmix, *dswa_params, l, batch=batch, seq=seq, tm=tm)
        outs_lses = []
        for gi in range(len(DSWA_GROUPS)):
            outs_lses += _dswa_group(slopes, *qkv_c[3 * gi:3 * gi + 3], group=gi)
        y_c = _dswa_combine(outs_lses, batch=batch, seq=seq, tm=tm)

        xf = _merge(xf, gmix, y_a, y_b, y_c, *merge_params, l, tm=tm, tn=MERGE_CHUNK)
        xf = _ffn(xf, *ffn2, l, tm=tm_ffn, tf=tf)
    return xf.reshape(batch, seq, d)
```

```python
import functools

import numpy as np
import jax
import jax.numpy as jnp
from jax import lax
from jax.experimental import pallas as pl
from jax.experimental.pallas import tpu as pltpu

F32 = jnp.float32
BF16 = jnp.bfloat16

MLA_HEADS = 6
MLA_Q_RANK = 512
MLA_KV_RANK = 256
MLA_NOPE = 128
MLA_ROPE = 64
MLA_V = 128
MLA_QK = MLA_NOPE + MLA_ROPE
MLA_QK_PAD = 256
GLA_HEADS = 4
GLA_DK = 64
GLA_DV = 128
GLA_GATE_RANK = 16
GLA_TAU = 16.0
GLA_CHUNK = 64
GLA_SUB = 8
GLA_LEVELS = (32, 16, 8)
DSWA_GROUPS = ((128, 1), (512, 4), (2048, 16))
DSWA_HEADS_PER_GROUP = 2
DSWA_HEADS = 6
DSWA_HEAD_DIM = 128
DSWA_W = 128
ROPE_THETA = 10000.0
ALIBI_MAX_EXP = 8.0
NORM_EPS = 1e-6
NEG_INF = -1e30
LOG2E = 1.4426950408889634

LANE = 128
BF16_ROWS = 16
MIB = 1024 * 1024

TOKEN_BLOCK = 512
FFN_TOKEN_BLOCK = 1024
FFN_CHUNK = 512
MERGE_CHUNK = 512
FLASH_TILE = 1024
GLA_TOKENS = 512
DSWA_MAX_Q = 1024
VMEM_FFN = 57
VMEM_RESIDENT = 52
VMEM_SCAN = 40
VMEM_SMALL = 32


def _cparams(sem, vmem_mib):
    return pltpu.CompilerParams(dimension_semantics=sem, vmem_limit_bytes=int(vmem_mib * MIB))


def _inv_rms(xf, n):
    return lax.rsqrt(jnp.sum(xf * xf, axis=-1, keepdims=True) / n + NORM_EPS)


def _const_spec(shape):
    nd = len(shape)
    return pl.BlockSpec(shape, lambda *_: (0,) * nd)


def _layer_spec(arr, layer):
    nd = arr.ndim - 1
    return pl.BlockSpec((None,) + arr.shape[1:], lambda *_: (layer,) + (0,) * nd)


def _ffn_kernel(x_ref, g_ref, wg_ref, wu_ref, wd_ref, o_ref, h_scr):
    def half_chunk(h):
        gate = jnp.dot(h, wg_ref[...], preferred_element_type=F32)
        up = jnp.dot(h, wu_ref[...], preferred_element_type=F32)
        inter = (gate * jax.nn.sigmoid(gate) * up).astype(BF16)
        return 0.5 * jnp.dot(inter, wd_ref[...], preferred_element_type=F32)

    @pl.when(pl.program_id(1) == 0)
    def _():
        xf = x_ref[...]
        h = (xf * _inv_rms(xf, xf.shape[-1]) * g_ref[...]).astype(BF16)
        h_scr[...] = h
        o_ref[...] = xf + half_chunk(h)

    @pl.when(pl.program_id(1) > 0)
    def _():
        o_ref[...] += half_chunk(h_scr[...])


def _ffn(x, g, wg, wu, wd, layer, *, tm, tf):
    t, d = x.shape
    fp = wg.shape[-1]
    return pl.pallas_call(
        _ffn_kernel,
        grid=(t // tm, fp // tf),
        in_specs=[
            pl.BlockSpec((tm, d), lambda i, j: (i, 0)),
            _layer_spec(g, layer),
            pl.BlockSpec((None, d, tf), lambda i, j: (layer, 0, j)),
            pl.BlockSpec((None, d, tf), lambda i, j: (layer, 0, j)),
            pl.BlockSpec((None, tf, d), lambda i, j: (layer, j, 0)),
        ],
        out_specs=pl.BlockSpec((tm, d), lambda i, j: (i, 0)),
        out_shape=jax.ShapeDtypeStruct((t, d), F32),
        scratch_shapes=[pltpu.VMEM((tm, d), BF16)],
        compiler_params=_cparams(("parallel", "arbitrary"), VMEM_FFN),
        name="ffn",
    )(x, g, wg, wu, wd)


def _rope_kernel(pos_ref, inv_ref, cos_ref, sin_ref):
    ang = pos_ref[...].astype(F32) * inv_ref[...]
    cos_ref[...] = jnp.cos(ang)
    sin_ref[...] = jnp.sin(ang)


def _rope_tables(pos_col, inv_pad, *, tm):
    t = pos_col.shape[0]
    return pl.pallas_call(
        _rope_kernel,
        grid=(t // tm,),
        in_specs=[pl.BlockSpec((tm, 1), lambda i: (i, 0)), _const_spec((1, LANE))],
        out_specs=[pl.BlockSpec((tm, LANE), lambda i: (i, 0))] * 2,
        out_shape=[jax.ShapeDtypeStruct((t, LANE), F32)] * 2,
        compiler_params=_cparams(("parallel",), VMEM_SMALL),
        name="rope_tables",
    )(pos_col, inv_pad)


def _mla_prep_kernel(x_ref, g_ref, cos_ref, sin_ref, w1_ref, gcq_ref, gckv_ref, wuq_ref, wukv_ref,
                     gq_ref, gk_ref, q_ref, k_ref, v_ref):
    xf = x_ref[...]
    h = (xf * _inv_rms(xf, xf.shape[-1]) * g_ref[...]).astype(BF16)
    u = jnp.dot(h, w1_ref[...], preferred_element_type=F32)
    c_q = u[:, :MLA_Q_RANK]
    c_kv = u[:, MLA_Q_RANK:MLA_Q_RANK + MLA_KV_RANK]
    kr = u[:, MLA_Q_RANK + MLA_KV_RANK:MLA_Q_RANK + MLA_KV_RANK + LANE]
    krot = u[:, MLA_Q_RANK + MLA_KV_RANK + LANE:]
    c_q = (c_q * _inv_rms(c_q, MLA_Q_RANK) * gcq_ref[...]).astype(BF16)
    c_kv = (c_kv * _inv_rms(c_kv, MLA_KV_RANK) * gckv_ref[...]).astype(BF16)
    qall = jnp.dot(c_q, wuq_ref[...], preferred_element_type=F32)
    kv = jnp.dot(c_kv, wukv_ref[...], preferred_element_type=F32)
    cos = cos_ref[...]
    sin = sin_ref[...]
    scale = MLA_QK ** -0.5 * LOG2E
    gq_n, gq_r, gq_t = gq_ref[0:1, :], gq_ref[1:2, :], gq_ref[2:3, :]
    gk_n, gk_r, gk_t = gk_ref[0:1, :], gk_ref[1:2, :], gk_ref[2:3, :]
    k_roped = kr * gk_r * cos + krot * gk_t * sin
    kr_ss = jnp.sum(kr * kr, axis=-1, keepdims=True)
    for hh in range(MLA_HEADS):
        base = hh * 3 * LANE
        nope = qall[:, base:base + LANE]
        rope = qall[:, base + LANE:base + 2 * LANE]
        rot = qall[:, base + 2 * LANE:base + 3 * LANE]
        ss = jnp.sum(nope * nope, axis=-1, keepdims=True) + jnp.sum(rope * rope, axis=-1, keepdims=True)
        r = lax.rsqrt(ss / MLA_QK + NORM_EPS) * scale
        q_ref[hh, :, 0:LANE] = (nope * r * gq_n).astype(BF16)
        q_ref[hh, :, LANE:2 * LANE] = ((rope * gq_r * cos + rot * gq_t * sin) * r).astype(BF16)
        kn = kv[:, hh * LANE:(hh + 1) * LANE]
        rk = lax.rsqrt((jnp.sum(kn * kn, axis=-1, keepdims=True) + kr_ss) / MLA_QK + NORM_EPS)
        k_ref[hh, :, 0:LANE] = (kn * rk * gk_n).astype(BF16)
        k_ref[hh, :, LANE:2 * LANE] = (k_roped * rk).astype(BF16)
        v_ref[hh] = kv[:, (MLA_HEADS + hh) * LANE:(MLA_HEADS + hh + 1) * LANE].astype(BF16)


def _mla_prep(x, g, cos, sin, w1, gcq, gckv, wuq, wukv, gq3, gk3, layer, *, tm):
    t, d = x.shape
    row = lambda i: (i, 0)
    hrow = lambda i: (0, i, 0)
    return pl.pallas_call(
        _mla_prep_kernel,
        grid=(t // tm,),
        in_specs=[pl.BlockSpec((tm, d), row), _layer_spec(g, layer),
                  pl.BlockSpec((tm, LANE), row), pl.BlockSpec((tm, LANE), row)]
                 + [_layer_spec(p, layer) for p in (w1, gcq, gckv, wuq, wukv, gq3, gk3)],
        out_specs=[
            pl.BlockSpec((MLA_HEADS, tm, MLA_QK_PAD), hrow),
            pl.BlockSpec((MLA_HEADS, tm, MLA_QK_PAD), hrow),
            pl.BlockSpec((MLA_HEADS, tm, MLA_V), hrow),
        ],
        out_shape=[
            jax.ShapeDtypeStruct((MLA_HEADS, t, MLA_QK_PAD), BF16),
            jax.ShapeDtypeStruct((MLA_HEADS, t, MLA_QK_PAD), BF16),
            jax.ShapeDtypeStruct((MLA_HEADS, t, MLA_V), BF16),
        ],
        compiler_params=_cparams(("parallel",), VMEM_RESIDENT),
        name="mla_prep",
    )(x, g, cos, sin, w1, gcq, gckv, wuq, wukv, gq3, gk3)


def _flash_kernel(q_ref, k_ref, v_ref, o_ref, sa_scr, sb_scr, m_scr, acc_scr, *, tq):
    qi = pl.program_id(2)
    m_scr[...] = jnp.full_like(m_scr, NEG_INF)
    acc_scr[...] = jnp.zeros_like(acc_scr)
    q = q_ref[...]
    ones = jnp.ones((tq, MLA_V), BF16)

    def scores(tile, s_scr):
        r0 = pl.multiple_of(tile * tq, tq)
        s_scr[...] = lax.dot_general(q, k_ref[pl.ds(r0, tq), :], (((1,), (1,)), ((), ())),
                                     preferred_element_type=F32)

    def accumulate(tile, s_scr, diagonal):
        r0 = pl.multiple_of(tile * tq, tq)
        sc = s_scr[...]
        if diagonal:
            row = lax.broadcasted_iota(jnp.int32, sc.shape, 0)
            col = lax.broadcasted_iota(jnp.int32, sc.shape, 1)
            sc = jnp.where(col <= row, sc, NEG_INF)
        m_prev = m_scr[...]
        m_new = jnp.maximum(m_prev, jnp.max(sc, axis=-1, keepdims=True))
        alpha = jnp.exp2(m_prev - m_new)
        p = jnp.exp2(sc - jnp.tile(m_new, (1, tq // LANE)))
        v1 = jnp.concatenate([v_ref[pl.ds(r0, tq), :], ones], axis=1)
        pv = jnp.dot(p.astype(BF16), v1, preferred_element_type=F32)
        acc_scr[...] = jnp.tile(alpha, (1, 2)) * acc_scr[...] + pv
        m_scr[...] = m_new

    scores(0, sa_scr)

    def body(j, carry):
        scores(2 * j + 1, sb_scr)
        accumulate(2 * j, sa_scr, False)
        scores(2 * j + 2, sa_scr)
        accumulate(2 * j + 1, sb_scr, False)
        return carry

    def body_x2(jj, carry):
        body(2 * jj, carry)
        return body(2 * jj + 1, carry)

    npair = qi // 2
    lax.fori_loop(0, npair // 2, body_x2, 0)
    lax.fori_loop(2 * (npair // 2), npair, body, 0)

    @pl.when(qi % 2 == 0)
    def _():
        accumulate(qi, sa_scr, True)

    @pl.when(qi % 2 == 1)
    def _():
        scores(qi, sb_scr)
        accumulate(qi - 1, sa_scr, False)
        accumulate(qi, sb_scr, True)

    acc = acc_scr[...]
    o_ref[...] = (acc[:, 0:MLA_V] / acc[:, MLA_V:]).astype(o_ref.dtype)


def _flash(q, k, v, *, batch, seq, tq):
    assert seq % tq == 0
    nq = seq // tq
    t = batch * seq
    return pl.pallas_call(
        functools.partial(_flash_kernel, tq=tq),
        grid=(batch, MLA_HEADS, nq),
        in_specs=[
            pl.BlockSpec((None, tq, MLA_QK_PAD), lambda b, h, i: (h, b * nq + i, 0)),
            pl.BlockSpec((None, seq, MLA_QK_PAD), lambda b, h, i: (h, b, 0)),
            pl.BlockSpec((None, seq, MLA_V), lambda b, h, i: (h, b, 0)),
        ],
        out_specs=pl.BlockSpec((tq, MLA_V), lambda b, h, i: (b * nq + i, h)),
        out_shape=jax.ShapeDtypeStruct((t, MLA_HEADS * MLA_V), BF16),
        scratch_shapes=[pltpu.VMEM((tq, tq), F32), pltpu.VMEM((tq, tq), F32),
                        pltpu.VMEM((tq, MLA_V), F32), pltpu.VMEM((tq, 2 * MLA_V), F32)],
        compiler_params=_cparams(("parallel", "parallel", "arbitrary"), VMEM_RESIDENT),
        name="mla_flash",
    )(q, k, v)


def _gla_proj_kernel(x_ref, g_ref, w_ref, w2_ref, b2_ref, q_ref, k_ref, v_ref, la_ref, sr_ref):
    xf = x_ref[...]
    h = (xf * _inv_rms(xf, xf.shape[-1]) * g_ref[...]).astype(BF16)
    u = jnp.dot(h, w_ref[...], preferred_element_type=F32)
    nk = GLA_HEADS * GLA_DK
    nv = GLA_HEADS * GLA_DV
    q_ref[...] = u[:, 0:nk] * (GLA_DK ** -0.5)
    k_ref[...] = u[:, nk:2 * nk]
    v_ref[...] = u[:, 2 * nk:2 * nk + nv]
    gate_lr = u[:, 2 * nk + nv:2 * nk + nv + LANE].astype(BF16)
    z = jnp.dot(gate_lr, w2_ref[...], preferred_element_type=F32) + b2_ref[...]
    la_ref[...] = -(jnp.maximum(-z, 0.0) + jnp.log1p(jnp.exp(-jnp.abs(z)))) / GLA_TAU
    r = u[:, 2 * nk + nv + LANE:]
    sr_ref[...] = r * jax.nn.sigmoid(r)


def _gla_proj(x, g, w, w2, b2, layer, *, tm):
    t, d = x.shape
    nk = GLA_HEADS * GLA_DK
    nv = GLA_HEADS * GLA_DV
    row = lambda i: (i, 0)
    return pl.pallas_call(
        _gla_proj_kernel,
        grid=(t // tm,),
        in_specs=[pl.BlockSpec((tm, d), row)] + [_layer_spec(p, layer) for p in (g, w, w2, b2)],
        out_specs=[pl.BlockSpec((tm, nk), row), pl.BlockSpec((tm, nk), row), pl.BlockSpec((tm, nv), row),
                   pl.BlockSpec((tm, nk), row), pl.BlockSpec((tm, nv), row)],
        out_shape=[jax.ShapeDtypeStruct((t, nk), F32), jax.ShapeDtypeStruct((t, nk), F32),
                   jax.ShapeDtypeStruct((t, nv), F32), jax.ShapeDtypeStruct((t, nk), F32),
                   jax.ShapeDtypeStruct((t, nv), F32)],
        compiler_params=_cparams(("parallel",), VMEM_RESIDENT),
        name="gla_proj",
    )(x, g, w, w2, b2)


def _gla_constants():
    c = GLA_CHUNK
    idx = np.arange(c)
    mats = [(idx[None, :] <= idx[:, None])]
    for lvl in GLA_LEVELS:
        p = ((idx // lvl) | 1) * lvl
        qrow = (idx >= p)[:, None] & (idx[None, :] >= p[:, None]) & (idx[None, :] <= idx[:, None])
        krow = (idx < p)[:, None] & (idx[None, :] > idx[:, None]) & (idx[None, :] < p[:, None])
        mats.append(qrow | krow)
    mats.append(idx[None, :] > idx[:, None])
    mstack = np.concatenate(mats, axis=0).astype(np.float32)
    lmask = []
    for lvl in GLA_LEVELS:
        blk = idx // lvl
        m = ((blk[:, None] % 2) == 1) & (blk[None, :] == blk[:, None] - 1)
        lmask.append(np.tile(m.astype(np.float32), (GLA_HEADS, 1)))
    lmask = np.stack(lmask, axis=0)
    hk = np.arange(GLA_HEADS * GLA_DK) // GLA_DK
    hv = np.arange(GLA_HEADS * GLA_DV) // GLA_DV
    hrow = np.arange(GLA_HEADS * c) // c
    headmask = (hrow[:, None] == hk[None, :]).astype(np.float32)
    e_ind = (hk[:, None] == hv[None, :]).astype(np.float32)
    return mstack, lmask, headmask, e_ind


def _gla_kernel(q_ref, k_ref, v_ref, la_ref, sr_ref, go_ref, mst_ref, lmask_ref, hmask_ref, eind_ref, eindt_ref,
                y_ref, st_scr, *, chunks):
    c = GLA_CHUNK
    nk = GLA_HEADS * GLA_DK
    nv = GLA_HEADS * GLA_DV
    nsub = c // GLA_SUB

    @pl.when(pl.program_id(1) == 0)
    def _():
        st_scr[...] = jnp.zeros_like(st_scr)

    def chunk(ci, carry):
        r0 = pl.multiple_of(ci * c, c)
        q = q_ref[pl.ds(r0, c), :]
        k = k_ref[pl.ds(r0, c), :]
        v = v_ref[pl.ds(r0, c), :]
        a = la_ref[pl.ds(r0, c), :]
        a1 = a.astype(BF16)
        r1 = a - a1.astype(F32)
        a2 = r1.astype(BF16)
        a3 = (r1 - a2.astype(F32)).astype(BF16)
        mst = mst_ref[...]
        ex = (jnp.dot(mst, a1, preferred_element_type=F32) + jnp.dot(mst, a2, preferred_element_type=F32)
              + jnp.dot(mst, a3, preferred_element_type=F32))
        b = ex[0:c]
        f = jnp.exp(ex)
        v_bf = v.astype(BF16)

        amat = jnp.zeros((GLA_HEADS * c, c), F32)
        hmask = hmask_ref[...]
        for li in range(len(GLA_LEVELS)):
            fl = f[(li + 1) * c:(li + 2) * c]
            qt = q * fl
            kt = (k * fl).astype(BF16)
            qs = (jnp.concatenate([qt] * GLA_HEADS, axis=0) * hmask).astype(BF16)
            sc = lax.dot_general(qs, kt, (((1,), (1,)), ((), ())), preferred_element_type=F32)
            amat = amat + sc * lmask_ref[li]
        a_bf = amat.astype(BF16)
        o = jnp.concatenate(
            [jnp.dot(a_bf[hh * c:(hh + 1) * c], v_bf[:, hh * GLA_DV:(hh + 1) * GLA_DV], preferred_element_type=F32)
             for hh in range(GLA_HEADS)], axis=1)

        qb = (q * f[0:c]).astype(BF16)
        st = st_scr[...]
        o = o + lax.dot_general(qb, st.astype(BF16), (((1,), (1,)), ((), ())), preferred_element_type=F32)

        q3 = q.reshape(nsub, GLA_SUB, nk)
        k3 = k.reshape(nsub, GLA_SUB, nk)
        b3 = b.reshape(nsub, GLA_SUB, nk)
        v3 = v.reshape(nsub, GLA_SUB, nv)
        tt = lax.broadcasted_iota(jnp.int32, (nsub, GLA_SUB, nk), 1)
        od = jnp.zeros((nsub, GLA_SUB, nv), F32)
        eind = eind_ref[...]
        for s in range(GLA_SUB):
            w = jnp.exp(jnp.minimum(b3 - b3[:, s:s + 1, :], 0.0))
            x = jnp.where(tt >= s, q3 * k3[:, s:s + 1, :] * w, 0.0)
            rr = jnp.dot(x.reshape(c, nk).astype(BF16), eind, preferred_element_type=F32)
            od = od + rr.reshape(nsub, GLA_SUB, nv) * v3[:, s:s + 1, :]
        o = o + od.reshape(c, nv)

        kend = (k * f[(len(GLA_LEVELS) + 1) * c:(len(GLA_LEVELS) + 2) * c]).astype(BF16)
        upd = lax.dot_general(v_bf, kend, (((0,), (0,)), ((), ())), preferred_element_type=F32)
        st_scr[...] = st * f[c - 1:c] + upd * eindt_ref[...]

        go = go_ref[...]
        sr = sr_ref[pl.ds(r0, c), :]
        ys = []
        for hh in range(GLA_HEADS):
            oh = o[:, hh * GLA_DV:(hh + 1) * GLA_DV]
            ys.append(oh * _inv_rms(oh, GLA_DV) * go)
        y_ref[pl.ds(r0, c), :] = (jnp.concatenate(ys, axis=1) * sr).astype(y_ref.dtype)
        return carry

    lax.fori_loop(0, chunks, chunk, 0, unroll=2)


def _gla(q, k, v, la, sr, go, layer, *, batch, seq, tb):
    nk = GLA_HEADS * GLA_DK
    nv = GLA_HEADS * GLA_DV
    nb = seq // tb
    mstack, lmask, headmask, e_ind = _gla_constants()
    consts = [jnp.asarray(mstack, BF16), jnp.asarray(lmask, F32), jnp.asarray(headmask, F32),
              jnp.asarray(e_ind, BF16), jnp.asarray(e_ind.T, F32)]
    row = lambda b, i: (b * nb + i, 0)
    return pl.pallas_call(
        functools.partial(_gla_kernel, chunks=tb // GLA_CHUNK),
        grid=(batch, nb),
        in_specs=[pl.BlockSpec((tb, nk), row), pl.BlockSpec((tb, nk), row), pl.BlockSpec((tb, nv), row),
                  pl.BlockSpec((tb, nk), row), pl.BlockSpec((tb, nv), row), _layer_spec(go, layer)]
                 + [_const_spec(cst.shape) for cst in consts],
        out_specs=pl.BlockSpec((tb, nv), row),
        out_shape=jax.ShapeDtypeStruct((batch * seq, nv), BF16),
        scratch_shapes=[pltpu.VMEM((nv, nk), F32)],
        compiler_params=_cparams(("parallel", "arbitrary"), VMEM_SCAN),
        name="gla_scan",
    )(q, k, v, la, sr, go, *consts)


def _dswa_proj_kernel(x_ref, g_ref, w_ref, gq_ref, gk_ref, *refs, tm):
    ng = len(DSWA_GROUPS)
    out_refs = refs[:3 * ng]
    q_scr, k_scr, v_scr = refs[3 * ng:]
    xf = x_ref[...]
    h = (xf * _inv_rms(xf, xf.shape[-1]) * g_ref[...]).astype(BF16)
    u = jnp.dot(h, w_ref[...], preferred_element_type=F32)
    n = DSWA_HEADS * DSWA_HEAD_DIM
    gq = gq_ref[...]
    gk = gk_ref[...]
    scale = DSWA_HEAD_DIM ** -0.5
    for hh in range(DSWA_HEADS):
        sl = slice(hh * DSWA_HEAD_DIM, (hh + 1) * DSWA_HEAD_DIM)
        qh = u[:, sl]
        q_scr[hh] = qh * _inv_rms(qh, DSWA_HEAD_DIM) * gq * scale
        kh = u[:, n + hh * DSWA_HEAD_DIM:n + (hh + 1) * DSWA_HEAD_DIM]
        k_scr[hh] = kh * _inv_rms(kh, DSWA_HEAD_DIM) * gk
        v_scr[hh] = u[:, 2 * n + hh * DSWA_HEAD_DIM:2 * n + (hh + 1) * DSWA_HEAD_DIM]
    for gi, (_, dil) in enumerate(DSWA_GROUPS):
        for src, dst in zip((q_scr, k_scr, v_scr), out_refs[3 * gi:3 * gi + 3]):
            for hh in range(DSWA_HEADS_PER_GROUP):
                head = gi * DSWA_HEADS_PER_GROUP + hh
                cols = slice(hh * DSWA_HEAD_DIM, (hh + 1) * DSWA_HEAD_DIM)
                for r in range(dil):
                    rows = pl.ds(r, tm // dil, stride=dil) if dil > 1 else slice(None)
                    dst[r, :, cols] = src[head, rows, :].astype(BF16)


def _dswa_proj(x, g, w, gq, gk, layer, *, batch, seq, tm):
    t, d = x.shape
    n = DSWA_HEADS * DSWA_HEAD_DIM
    gw = DSWA_HEADS_PER_GROUP * DSWA_HEAD_DIM
    nblk = seq // tm
    out_specs, out_shape = [], []
    for _, dil in DSWA_GROUPS:
        assert tm % (BF16_ROWS * dil) == 0
        out_specs += [pl.BlockSpec((None, dil, tm // dil, gw), lambda b, i: (b, 0, i, 0))] * 3
        out_shape += [jax.ShapeDtypeStruct((batch, dil, seq // dil, gw), BF16)] * 3
    return pl.pallas_call(
        functools.partial(_dswa_proj_kernel, tm=tm),
        grid=(batch, nblk),
        in_specs=[pl.BlockSpec((tm, d), lambda b, i: (b * nblk + i, 0))]
                 + [_layer_spec(p, layer) for p in (g, w, gq, gk)],
        out_specs=out_specs,
        out_shape=out_shape,
        scratch_shapes=[pltpu.VMEM((DSWA_HEADS, tm, DSWA_HEAD_DIM), F32)] * 3,
        compiler_params=_cparams(("parallel", "parallel"), VMEM_RESIDENT),
        name="dswa_proj",
    )(x, g, w, gq, gk)


def _dswa_kernel(slope_ref, q_ref, kc_ref, kp_ref, vc_ref, vp_ref, o_ref, l_ref, *, dilation, group, nb):
    w = DSWA_W
    ib = pl.program_id(2)
    i = lax.broadcasted_iota(jnp.int32, (w, 2 * w), 0)
    j = lax.broadcasted_iota(jnp.int32, (w, 2 * w), 1)
    steps = w + i - j
    dist = (steps * dilation).astype(F32)
    in_window = (steps >= 0) & (steps <= w)
    for hh in range(DSWA_HEADS_PER_GROUP):
        cols = slice(hh * DSWA_HEAD_DIM, (hh + 1) * DSWA_HEAD_DIM)
        bias_all = jnp.where(in_window, -slope_ref[group * DSWA_HEADS_PER_GROUP + hh] * dist, NEG_INF)
        bias_first = jnp.where(j >= w, bias_all, NEG_INF)
        for jb in range(nb):
            rows = slice(jb * w, (jb + 1) * w)
            if jb == 0:
                kprev, vprev = kp_ref[:, cols], vp_ref[:, cols]
                bias_j = jnp.where(ib > 0, bias_all, bias_first)
            else:
                prows = slice((jb - 1) * w, jb * w)
                kprev, vprev = kc_ref[prows, cols], vc_ref[prows, cols]
                bias_j = bias_all
            keys = jnp.concatenate([kprev, kc_ref[rows, cols]], axis=0)
            vals = jnp.concatenate([vprev, vc_ref[rows, cols]], axis=0)
            sc = lax.dot_general(q_ref[rows, cols], keys, (((1,), (1,)), ((), ())), preferred_element_type=F32)
            sc = jnp.where(bias_j > 0.5 * NEG_INF, sc + bias_j, NEG_INF)
            m = jnp.max(sc, axis=-1, keepdims=True)
            p = jnp.exp(sc - m)
            lsum = jnp.sum(p, axis=-1, keepdims=True)
            o_ref[rows, cols] = jnp.dot(p.astype(BF16), vals, preferred_element_type=F32) / lsum
            l_ref[rows, cols] = jnp.broadcast_to(m + jnp.log(lsum), (w, DSWA_HEAD_DIM))


def _dswa_group(slopes, qg, kg, vg, *, group):
    window, dilation = DSWA_GROUPS[group]
    assert window // dilation == DSWA_W
    batch, _, sd, gw = qg.shape
    tq = min(DSWA_MAX_Q, sd)
    nb = tq // DSWA_W
    cur = pl.BlockSpec((None, None, tq, gw), lambda b, r, i: (b, r, i, 0))
    prev = pl.BlockSpec((None, None, DSWA_W, gw), lambda b, r, i: (b, r, jnp.maximum(i * nb - 1, 0), 0))
    return pl.pallas_call(
        functools.partial(_dswa_kernel, dilation=dilation, group=group, nb=nb),
        grid=(batch, dilation, sd // tq),
        in_specs=[pl.BlockSpec(memory_space=pltpu.SMEM), cur, cur, prev, cur, prev],
        out_specs=[cur] * 2,
        out_shape=[jax.ShapeDtypeStruct(qg.shape, F32)] * 2,
        compiler_params=_cparams(("parallel", "parallel", "arbitrary"), VMEM_SMALL),
        name="dswa_g%d" % group,
    )(slopes, qg, kg, kg, vg, vg)


def _dswa_combine_kernel(*refs, tm):
    ng = len(DSWA_GROUPS)
    in_refs, y_ref, scr = refs[:2 * ng], refs[2 * ng], refs[2 * ng + 1]
    vals = []
    for gi, (_, dil) in enumerate(DSWA_GROUPS):
        for which in range(2):
            src = in_refs[2 * gi + which]
            if dil == 1:
                vals.append(src[0])
            else:
                halves = []
                for hh in range(DSWA_HEADS_PER_GROUP):
                    slot = (2 * gi + which) * DSWA_HEADS_PER_GROUP + hh
                    for r in range(dil):
                        scr[slot, pl.ds(r, tm // dil, stride=dil), :] = (
                            src[r, :, hh * DSWA_HEAD_DIM:(hh + 1) * DSWA_HEAD_DIM])
                    halves.append(scr[slot])
                vals.append(jnp.concatenate(halves, axis=1))
    lses = vals[1::2]
    mx = functools.reduce(jnp.maximum, lses)
    es = [jnp.exp(l - mx) for l in lses]
    den = functools.reduce(lambda a, b: a + b, es)
    y = functools.reduce(lambda a, b: a + b, [(e / den) * o for e, o in zip(es, vals[0::2])])
    y_ref[...] = y.astype(y_ref.dtype)


def _dswa_combine(outs_lses, *, batch, seq, tm):
    gw = DSWA_HEADS_PER_GROUP * DSWA_HEAD_DIM
    nblk = seq // tm
    in_specs = []
    for _, dil in DSWA_GROUPS:
        in_specs += [pl.BlockSpec((None, dil, tm // dil, gw), lambda b, i: (b, 0, i, 0))] * 2
    return pl.pallas_call(
        functools.partial(_dswa_combine_kernel, tm=tm),
        grid=(batch, nblk),
        in_specs=in_specs,
        out_specs=pl.BlockSpec((tm, gw), lambda b, i: (b * nblk + i, 0)),
        out_shape=jax.ShapeDtypeStruct((batch * seq, gw), BF16),
        scratch_shapes=[pltpu.VMEM((2 * len(DSWA_GROUPS) * DSWA_HEADS_PER_GROUP, tm, DSWA_HEAD_DIM), F32)],
        compiler_params=_cparams(("parallel", "parallel"), VMEM_SMALL),
        name="dswa_combine",
    )(*outs_lses)


def _merge_kernel(x_ref, g_ref, ya_ref, yb_ref, yc_ref, wga_ref, wgb_ref, wgc_ref, wa_ref, wb_ref, wc_ref, wo_ref,
                  out_ref, h_scr):
    dot = lambda a, b: jnp.dot(a, b, preferred_element_type=F32)

    def chunk(h):
        inter = (jax.nn.sigmoid(dot(h, wga_ref[...])) * dot(ya_ref[...], wa_ref[...])
                 + jax.nn.sigmoid(dot(h, wgb_ref[...])) * dot(yb_ref[...], wb_ref[...])
                 + jax.nn.sigmoid(dot(h, wgc_ref[...])) * dot(yc_ref[...], wc_ref[...]))
        return dot(inter.astype(BF16), wo_ref[...])

    @pl.when(pl.program_id(1) == 0)
    def _():
        xf = x_ref[...]
        h = (xf * _inv_rms(xf, xf.shape[-1]) * g_ref[...]).astype(BF16)
        h_scr[...] = h
        out_ref[...] = xf + chunk(h)

    @pl.when(pl.program_id(1) > 0)
    def _():
        out_ref[...] += chunk(h_scr[...])


def _merge(x, g, ya, yb, yc, wga, wgb, wgc, wa, wb, wc, wo, layer, *, tm, tn):
    t, d = x.shape
    row = lambda i, j: (i, 0)
    colw = lambda i, j: (layer, 0, j)
    return pl.pallas_call(
        _merge_kernel,
        grid=(t // tm, d // tn),
        in_specs=[pl.BlockSpec((tm, d), row), _layer_spec(g, layer),
                  pl.BlockSpec((tm, ya.shape[1]), row), pl.BlockSpec((tm, yb.shape[1]), row),
                  pl.BlockSpec((tm, yc.shape[1]), row)]
                 + [pl.BlockSpec((None, w.shape[1], tn), colw) for w in (wga, wgb, wgc, wa, wb, wc)]
                 + [pl.BlockSpec((None, tn, d), lambda i, j: (layer, j, 0))],
        out_specs=pl.BlockSpec((tm, d), row),
        out_shape=jax.ShapeDtypeStruct((t, d), F32),
        scratch_shapes=[pltpu.VMEM((tm, d), BF16)],
        compiler_params=_cparams(("parallel", "arbitrary"), VMEM_RESIDENT),
        name="merge",
    )(x, g, ya, yb, yc, wga, wgb, wgc, wa, wb, wc, wo)


def _pad_last(w, n):
    return jnp.pad(w, [(0, 0)] * (w.ndim - 1) + [(0, n - w.shape[-1])])


def _rot_half_cols(w):
    half = w.shape[-1] // 2
    return jnp.concatenate([-w[..., half:], w[..., :half]], axis=-1)


def _swap_half(g):
    half = g.shape[-1] // 2
    return jnp.concatenate([g[..., half:], g[..., :half]], axis=-1)


def _gain3(g):
    rope = g[:, MLA_NOPE:]
    return jnp.stack([g[:, :MLA_NOPE], _pad_last(rope, LANE), _pad_last(_swap_half(rope), LANE)], axis=1)


def _row(g):
    return g[:, None, :]


def kernel(x, positions, ffn1_norm, ffn1_w_gate, ffn1_w_up, ffn1_w_down, mix_norm, w_in, mla_cq_norm, mla_ckv_norm, mla_w_uq, mla_w_ukv, mla_q_norm, mla_k_norm, gla_w_gate2, gla_b_gate2, gla_o_norm, dswa_q_norm, dswa_k_norm, w_branch_a, w_branch_b, w_branch_c, w_out, ffn2_norm, ffn2_w_gate, ffn2_w_up, ffn2_w_down):
    batch, seq, d = x.shape
    depth = w_in.shape[0]
    d_ff = ffn1_w_gate.shape[-1]
    t = batch * seq
    tm, tm_ffn, tf = TOKEN_BLOCK, FFN_TOKEN_BLOCK, FFN_CHUNK
    ff_pad = -(-d_ff // tf) * tf
    assert seq % max(wd for wd, _ in DSWA_GROUPS) == 0 and seq % tm == 0 and t % tm_ffn == 0

    xf = x.reshape(t, d)
    half = MLA_ROPE // 2
    inv_freq = ROPE_THETA ** (-jnp.arange(half, dtype=F32) / half)
    inv_pad = jnp.pad(jnp.concatenate([inv_freq, inv_freq]), (0, LANE - MLA_ROPE)).reshape(1, LANE)
    cos, sin = _rope_tables(positions.reshape(t, 1), inv_pad, tm=tm)
    slopes = 2.0 ** (-ALIBI_MAX_EXP * jnp.arange(1, DSWA_HEADS + 1, dtype=F32) / DSWA_HEADS)

    o_cq, o_ckv, o_kr = 0, MLA_Q_RANK, MLA_Q_RANK + MLA_KV_RANK
    o_gla = o_kr + MLA_ROPE
    n_qkv_b = GLA_HEADS * (2 * GLA_DK + GLA_DV)
    o_glr = o_gla + n_qkv_b
    o_rb = o_glr + GLA_GATE_RANK
    o_c = o_rb + GLA_HEADS * GLA_DV
    o_ga = o_c + 3 * DSWA_HEADS * DSWA_HEAD_DIM
    o_gb, o_gc = o_ga + d, o_ga + 2 * d

    bf = lambda w: w.astype(BF16)

    def ffn_weights(wg, wu, wdn):
        zc = jnp.zeros((depth, d, ff_pad - d_ff), BF16)
        zr = jnp.zeros((depth, ff_pad - d_ff, d), BF16)
        return (jnp.concatenate([bf(wg), zc], axis=2), jnp.concatenate([bf(wu), zc], axis=2),
                jnp.concatenate([bf(wdn), zr], axis=1))

    ffn1 = (_row(ffn1_norm),) + ffn_weights(ffn1_w_gate, ffn1_w_up, ffn1_w_down)
    ffn2 = (_row(ffn2_norm),) + ffn_weights(ffn2_w_gate, ffn2_w_up, ffn2_w_down)
    gmix = _row(mix_norm)
    w_in = bf(w_in)
    w_kr = w_in[:, :, o_kr:o_kr + MLA_ROPE]
    w1 = bf(jnp.concatenate([w_in[:, :, o_cq:o_kr], _pad_last(w_kr, LANE), _pad_last(_rot_half_cols(w_kr), LANE)],
                            axis=-1))
    wq4 = mla_w_uq.reshape(depth, MLA_Q_RANK, MLA_HEADS, MLA_QK)
    wq_rope = wq4[..., MLA_NOPE:]
    wuq = jnp.concatenate([wq4[..., :MLA_NOPE], _pad_last(wq_rope, LANE), _pad_last(_rot_half_cols(wq_rope), LANE)],
                          axis=-1)
    wuq = bf(wuq.reshape(depth, MLA_Q_RANK, MLA_HEADS * 3 * LANE))
    wkv4 = mla_w_ukv.reshape(depth, MLA_KV_RANK, MLA_HEADS, MLA_NOPE + MLA_V)
    wukv = bf(jnp.concatenate([wkv4[..., :MLA_NOPE].reshape(depth, MLA_KV_RANK, -1),
                               wkv4[..., MLA_NOPE:].reshape(depth, MLA_KV_RANK, -1)], axis=-1))
    mla_params = (w1, _row(mla_cq_norm), _row(mla_ckv_norm), wuq, wukv, _gain3(mla_q_norm), _gain3(mla_k_norm))
    w_gla = bf(jnp.concatenate([w_in[:, :, o_gla:o_glr], _pad_last(w_in[:, :, o_glr:o_rb], LANE),
                                w_in[:, :, o_rb:o_c]], axis=-1))
    w2 = bf(jnp.pad(gla_w_gate2, ((0, 0), (0, LANE - GLA_GATE_RANK), (0, 0))))
    gla_params = (w_gla, w2, _row(gla_b_gate2))
    go = _row(gla_o_norm)
    dswa_params = (bf(w_in[:, :, o_c:o_ga]), _row(dswa_q_norm), _row(dswa_k_norm))
    merge_params = (bf(w_in[:, :, o_ga:o_gb]), bf(w_in[:, :, o_gb:o_gc]), bf(w_in[:, :, o_gc:o_gc + d]),
                    bf(w_branch_a), bf(w_branch_b), bf(w_branch_c), bf(w_out))

    for l in range(depth):
        xf = _ffn(xf, *ffn1, l, tm=tm_ffn, tf=tf)

        qa, ka, va = _mla_prep(xf, gmix, cos, sin, *mla_params, l, tm=tm)
        y_a = _flash(qa, ka, va, batch=batch, seq=seq, tq=FLASH_TILE)

        qb, kb, vb, la, sr = _gla_proj(xf, gmix, *gla_params, l, tm=tm)
        y_b = _gla(qb, kb, vb, la, sr, go, l, batch=batch, seq=seq, tb=GLA_TOKENS)

        qkv_c = _dswa_proj(xf, gmix, *dswa_params, l, batch=batch, seq=seq, tm=tm)
        outs_lses = []
        for gi in range(len(DSWA_GROUPS)):
            outs_lses += _dswa_group(slopes, *qkv_c[3 * gi:3 * gi + 3], group=gi)
        y_c = _dswa_combine(outs_lses, batch=batch, seq=seq, tm=tm)

        xf = _merge(xf, gmix, y_a, y_b, y_c, *merge_params, l, tm=tm, tn=MERGE_CHUNK)
        xf = _ffn(xf, *ffn2, l, tm=tm_ffn, tf=tf)
    return xf.reshape(batch, seq, d)
```

```python
import functools

import numpy as np
import jax
import jax.numpy as jnp
from jax import lax
from jax.experimental import pallas as pl
from jax.experimental.pallas import tpu as pltpu

F32 = jnp.float32
BF16 = jnp.bfloat16

MLA_HEADS = 6
MLA_Q_RANK = 512
MLA_KV_RANK = 256
MLA_NOPE = 128
MLA_ROPE = 64
MLA_V = 128
MLA_QK = MLA_NOPE + MLA_ROPE
MLA_QK_PAD = 256
GLA_HEADS = 4
GLA_DK = 64
GLA_DV = 128
GLA_GATE_RANK = 16
GLA_TAU = 16.0
GLA_CHUNK = 64
GLA_SUB = 8
GLA_LEVELS = (32, 16, 8)
DSWA_GROUPS = ((128, 1), (512, 4), (2048, 16))
DSWA_HEADS_PER_GROUP = 2
DSWA_HEADS = 6
DSWA_HEAD_DIM = 128
DSWA_W = 128
ROPE_THETA = 10000.0
ALIBI_MAX_EXP = 8.0
NORM_EPS = 1e-6
NEG_INF = -1e30
LOG2E = 1.4426950408889634

LANE = 128
BF16_ROWS = 16
MIB = 1024 * 1024

TOKEN_BLOCK = 512
FFN_TOKEN_BLOCK = 1024
FFN_CHUNK = 512
MERGE_CHUNK = 512
FLASH_TILE = 1024
GLA_TOKENS = 512
DSWA_MAX_Q = 1024
VMEM_FFN = 57
VMEM_RESIDENT = 52
VMEM_SCAN = 40
VMEM_SMALL = 32


def _cparams(sem, vmem_mib):
    return pltpu.CompilerParams(dimension_semantics=sem, vmem_limit_bytes=int(vmem_mib * MIB))


def _inv_rms(xf, n):
    return lax.rsqrt(jnp.sum(xf * xf, axis=-1, keepdims=True) / n + NORM_EPS)


def _const_spec(shape):
    nd = len(shape)
    return pl.BlockSpec(shape, lambda *_: (0,) * nd)


def _layer_spec(arr, layer):
    nd = arr.ndim - 1
    return pl.BlockSpec((None,) + arr.shape[1:], lambda *_: (layer,) + (0,) * nd)


def _ffn_kernel(x_ref, g_ref, wg_ref, wu_ref, wd_ref, o_ref, h_scr):
    def half_chunk(h):
        gate = jnp.dot(h, wg_ref[...], preferred_element_type=F32)
        up = jnp.dot(h, wu_ref[...], preferred_element_type=F32)
        inter = (gate * jax.nn.sigmoid(gate) * up).astype(BF16)
        return 0.5 * jnp.dot(inter, wd_ref[...], preferred_element_type=F32)

    @pl.when(pl.program_id(1) == 0)
    def _():
        xf = x_ref[...]
        h = (xf * _inv_rms(xf, xf.shape[-1]) * g_ref[...]).astype(BF16)
        h_scr[...] = h
        o_ref[...] = xf + half_chunk(h)

    @pl.when(pl.program_id(1) > 0)
    def _():
        o_ref[...] += half_chunk(h_scr[...])


def _ffn(x, g, wg, wu, wd, layer, *, tm, tf):
    t, d = x.shape
    fp = wg.shape[-1]
    return pl.pallas_call(
        _ffn_kernel,
        grid=(t // tm, fp // tf),
        in_specs=[
            pl.BlockSpec((tm, d), lambda i, j: (i, 0)),
            _layer_spec(g, layer),
            pl.BlockSpec((None, d, tf), lambda i, j: (layer, 0, j)),
            pl.BlockSpec((None, d, tf), lambda i, j: (layer, 0, j)),
            pl.BlockSpec((None, tf, d), lambda i, j: (layer, j, 0)),
        ],
        out_specs=pl.BlockSpec((tm, d), lambda i, j: (i, 0)),
        out_shape=jax.ShapeDtypeStruct((t, d), F32),
        scratch_shapes=[pltpu.VMEM((tm, d), BF16)],
        compiler_params=_cparams(("parallel", "arbitrary"), VMEM_FFN),
        name="ffn",
    )(x, g, wg, wu, wd)


def _rope_kernel(pos_ref, inv_ref, cos_ref, sin_ref):
    ang = pos_ref[...].astype(F32) * inv_ref[...]
    cos_ref[...] = jnp.cos(ang)
    sin_ref[...] = jnp.sin(ang)


def _rope_tables(pos_col, inv_pad, *, tm):
    t = pos_col.shape[0]
    return pl.pallas_call(
        _rope_kernel,
        grid=(t // tm,),
        in_specs=[pl.BlockSpec((tm, 1), lambda i: (i, 0)), _const_spec((1, LANE))],
        out_specs=[pl.BlockSpec((tm, LANE), lambda i: (i, 0))] * 2,
        out_shape=[jax.ShapeDtypeStruct((t, LANE), F32)] * 2,
        compiler_params=_cparams(("parallel",), VMEM_SMALL),
        name="rope_tables",
    )(pos_col, inv_pad)


def _mla_prep_kernel(x_ref, g_ref, cos_ref, sin_ref, w1_ref, gcq_ref, gckv_ref, wuq_ref, wukv_ref,
                     gq_ref, gk_ref, q_ref, k_ref, v_ref):
    xf = x_ref[...]
    h = (xf * _inv_rms(xf, xf.shape[-1]) * g_ref[...]).astype(BF16)
    u = jnp.dot(h, w1_ref[...], preferred_element_type=F32)
    c_q = u[:, :MLA_Q_RANK]
    c_kv = u[:, MLA_Q_RANK:MLA_Q_RANK + MLA_KV_RANK]
    kr = u[:, MLA_Q_RANK + MLA_KV_RANK:MLA_Q_RANK + MLA_KV_RANK + LANE]
    krot = u[:, MLA_Q_RANK + MLA_KV_RANK + LANE:]
    c_q = (c_q * _inv_rms(c_q, MLA_Q_RANK) * gcq_ref[...]).astype(BF16)
    c_kv = (c_kv * _inv_rms(c_kv, MLA_KV_RANK) * gckv_ref[...]).astype(BF16)
    qall = jnp.dot(c_q, wuq_ref[...], preferred_element_type=F32)
    kv = jnp.dot(c_kv, wukv_ref[...], preferred_element_type=F32)
    cos = cos_ref[...]
    sin = sin_ref[...]
    scale = MLA_QK ** -0.5 * LOG2E
    gq_n, gq_r, gq_t = gq_ref[0:1, :], gq_ref[1:2, :], gq_ref[2:3, :]
    gk_n, gk_r, gk_t = gk_ref[0:1, :], gk_ref[1:2, :], gk_ref[2:3, :]
    k_roped = kr * gk_r * cos + krot * gk_t * sin
    kr_ss = jnp.sum(kr * kr, axis=-1, keepdims=True)
    for hh in range(MLA_HEADS):
        base = hh * 3 * LANE
        nope = qall[:, base:base + LANE]
        rope = qall[:, base + LANE:base + 2 * LANE]
        rot = qall[:, base + 2 * LANE:base + 3 * LANE]
        ss = jnp.sum(nope * nope, axis=-1, keepdims=True) + jnp.sum(rope * rope, axis=-1, keepdims=True)
        r = lax.rsqrt(ss / MLA_QK + NORM_EPS) * scale
        q_ref[hh, :, 0:LANE] = (nope * r * gq_n).astype(BF16)
        q_ref[hh, :, LANE:2 * LANE] = ((rope * gq_r * cos + rot * gq_t * sin) * r).astype(BF16)
        kn = kv[:, hh * LANE:(hh + 1) * LANE]
        rk = lax.rsqrt((jnp.sum(kn * kn, axis=-1, keepdims=True) + kr_ss) / MLA_QK + NORM_EPS)
        k_ref[hh, :, 0:LANE] = (kn * rk * gk_n).astype(BF16)
        k_ref[hh, :, LANE:2 * LANE] = (k_roped * rk).astype(BF16)
        v_ref[hh] = kv[:, (MLA_HEADS + hh) * LANE:(MLA_HEADS + hh + 1) * LANE].astype(BF16)


def _mla_prep(x, g, cos, sin, w1, gcq, gckv, wuq, wukv, gq3, gk3, layer, *, tm):
    t, d = x.shape
    row = lambda i: (i, 0)
    hrow = lambda i: (0, i, 0)
    return pl.pallas_call(
        _mla_prep_kernel,
        grid=(t // tm,),
        in_specs=[pl.BlockSpec((tm, d), row), _layer_spec(g, layer),
                  pl.BlockSpec((tm, LANE), row), pl.BlockSpec((tm, LANE), row)]
                 + [_layer_spec(p, layer) for p in (w1, gcq, gckv, wuq, wukv, gq3, gk3)],
        out_specs=[
            pl.BlockSpec((MLA_HEADS, tm, MLA_QK_PAD), hrow),
            pl.BlockSpec((MLA_HEADS, tm, MLA_QK_PAD), hrow),
            pl.BlockSpec((MLA_HEADS, tm, MLA_V), hrow),
        ],
        out_shape=[
            jax.ShapeDtypeStruct((MLA_HEADS, t, MLA_QK_PAD), BF16),
            jax.ShapeDtypeStruct((MLA_HEADS, t, MLA_QK_PAD), BF16),
            jax.ShapeDtypeStruct((MLA_HEADS, t, MLA_V), BF16),
        ],
        compiler_params=_cparams(("parallel",), VMEM_RESIDENT),
        name="mla_prep",
    )(x, g, cos, sin, w1, gcq, gckv, wuq, wukv, gq3, gk3)


def _flash_kernel(q_ref, k_ref, v_ref, o_ref, sa_scr, sb_scr, m_scr, acc_scr, *, tq):
    qi = pl.program_id(2)
    m_scr[...] = jnp.full_like(m_scr, NEG_INF)
    acc_scr[...] = jnp.zeros_like(acc_scr)
    q = q_ref[...]
    ones = jnp.ones((tq, MLA_V), BF16)

    def scores(tile, s_scr):
        r0 = pl.multiple_of(tile * tq, tq)
        s_scr[...] = lax.dot_general(q, k_ref[pl.ds(r0, tq), :], (((1,), (1,)), ((), ())),
                                     preferred_element_type=F32)

    def accumulate(tile, s_scr, diagonal):
        r0 = pl.multiple_of(tile * tq, tq)
        sc = s_scr[...]
        if diagonal:
            row = lax.broadcasted_iota(jnp.int32, sc.shape, 0)
            col = lax.broadcasted_iota(jnp.int32, sc.shape, 1)
            sc = jnp.where(col <= row, sc, NEG_INF)
        m_prev = m_scr[...]
        m_new = jnp.maximum(m_prev, jnp.max(sc, axis=-1, keepdims=True))
        alpha = jnp.exp2(m_prev - m_new)
        p = jnp.exp2(sc - jnp.tile(m_new, (1, tq // LANE)))
        v1 = jnp.concatenate([v_ref[pl.ds(r0, tq), :], ones], axis=1)
        pv = jnp.dot(p.astype(BF16), v1, preferred_element_type=F32)
        acc_scr[...] = jnp.tile(alpha, (1, 2)) * acc_scr[...] + pv
        m_scr[...] = m_new

    scores(0, sa_scr)

    def body(j, carry):
        scores(2 * j + 1, sb_scr)
        accumulate(2 * j, sa_scr, False)
        scores(2 * j + 2, sa_scr)
        accumulate(2 * j + 1, sb_scr, False)
        return carry

    def body_x2(jj, carry):
        body(2 * jj, carry)
        return body(2 * jj + 1, carry)

    npair = qi // 2
    lax.fori_loop(0, npair // 2, body_x2, 0)
    lax.fori_loop(2 * (npair // 2), npair, body, 0)

    @pl.when(qi % 2 == 0)
    def _():
        accumulate(qi, sa_scr, True)

    @pl.when(qi % 2 == 1)
    def _():
        scores(qi, sb_scr)
        accumulate(qi - 1, sa_scr, False)
        accumulate(qi, sb_scr, True)

    acc = acc_scr[...]
    o_ref[...] = (acc[:, 0:MLA_V] / acc[:, MLA_V:]).astype(o_ref.dtype)


def _flash(q, k, v, *, batch, seq, tq):
    assert seq % tq == 0
    nq = seq // tq
    t = batch * seq
    return pl.pallas_call(
        functools.partial(_flash_kernel, tq=tq),
        grid=(batch, MLA_HEADS, nq),
        in_specs=[
            pl.BlockSpec((None, tq, MLA_QK_PAD), lambda b, h, i: (h, b * nq + i, 0)),
            pl.BlockSpec((None, seq, MLA_QK_PAD), lambda b, h, i: (h, b, 0)),
            pl.BlockSpec((None, seq, MLA_V), lambda b, h, i: (h, b, 0)),
        ],
        out_specs=pl.BlockSpec((tq, MLA_V), lambda b, h, i: (b * nq + i, h)),
        out_shape=jax.ShapeDtypeStruct((t, MLA_HEADS * MLA_V), BF16),
        scratch_shapes=[pltpu.VMEM((tq, tq), F32), pltpu.VMEM((tq, tq), F32),
                        pltpu.VMEM((tq, MLA_V), F32), pltpu.VMEM((tq, 2 * MLA_V), F32)],
        compiler_params=_cparams(("parallel", "parallel", "arbitrary"), VMEM_RESIDENT),
        name="mla_flash",
    )(q, k, v)


def _gla_proj_kernel(x_ref, g_ref, w_ref, w2_ref, b2_ref, q_ref, k_ref, v_ref, la_ref, sr_ref):
    xf = x_ref[...]
    h = (xf * _inv_rms(xf, xf.shape[-1]) * g_ref[...]).astype(BF16)
    u = jnp.dot(h, w_ref[...], preferred_element_type=F32)
    nk = GLA_HEADS * GLA_DK
    nv = GLA_HEADS * GLA_DV
    q_ref[...] = u[:, 0:nk] * (GLA_DK ** -0.5)
    k_ref[...] = u[:, nk:2 * nk]
    v_ref[...] = u[:, 2 * nk:2 * nk + nv]
    gate_lr = u[:, 2 * nk + nv:2 * nk + nv + LANE].astype(BF16)
    z = jnp.dot(gate_lr, w2_ref[...], preferred_element_type=F32) + b2_ref[...]
    la_ref[...] = -(jnp.maximum(-z, 0.0) + jnp.log1p(jnp.exp(-jnp.abs(z)))) / GLA_TAU
    r = u[:, 2 * nk + nv + LANE:]
    sr_ref[...] = r * jax.nn.sigmoid(r)


def _gla_proj(x, g, w, w2, b2, layer, *, tm):
    t, d = x.shape
    nk = GLA_HEADS * GLA_DK
    nv = GLA_HEADS * GLA_DV
    row = lambda i: (i, 0)
    return pl.pallas_call(
        _gla_proj_kernel,
        grid=(t // tm,),
        in_specs=[pl.BlockSpec((tm, d), row)] + [_layer_spec(p, layer) for p in (g, w, w2, b2)],
        out_specs=[pl.BlockSpec((tm, nk), row), pl.BlockSpec((tm, nk), row), pl.BlockSpec((tm, nv), row),
                   pl.BlockSpec((tm, nk), row), pl.BlockSpec((tm, nv), row)],
        out_shape=[jax.ShapeDtypeStruct((t, nk), F32), jax.ShapeDtypeStruct((t, nk), F32),
                   jax.ShapeDtypeStruct((t, nv), F32), jax.ShapeDtypeStruct((t, nk), F32),
                   jax.ShapeDtypeStruct((t, nv), F32)],
        compiler_params=_cparams(("parallel",), VMEM_RESIDENT),
        name="gla_proj",
    )(x, g, w, w2, b2)


def _gla_constants():
    c = GLA_CHUNK
    idx = np.arange(c)
    mats = [(idx[None, :] <= idx[:, None])]
    for lvl in GLA_LEVELS:
        p = ((idx // lvl) | 1) * lvl
        qrow = (idx >= p)[:, None] & (idx[None, :] >= p[:, None]) & (idx[None, :] <= idx[:, None])
        krow = (idx < p)[:, None] & (idx[None, :] > idx[:, None]) & (idx[None, :] < p[:, None])
        mats.append(qrow | krow)
    mats.append(idx[None, :] > idx[:, None])
    mstack = np.concatenate(mats, axis=0).astype(np.float32)
    lmask = []
    for lvl in GLA_LEVELS:
        blk = idx // lvl
        m = ((blk[:, None] % 2) == 1) & (blk[None, :] == blk[:, None] - 1)
        lmask.append(np.tile(m.astype(np.float32), (GLA_HEADS, 1)))
    lmask = np.stack(lmask, axis=0)
    hk = np.arange(GLA_HEADS * GLA_DK) // GLA_DK
    hv = np.arange(GLA_HEADS * GLA_DV) // GLA_DV
    hrow = np.arange(GLA_HEADS * c) // c
    headmask = (hrow[:, None] == hk[None, :]).astype(np.float32)
    e_ind = (hk[:, None] == hv[None, :]).astype(np.float32)
    return mstack, lmask, headmask, e_ind


def _gla_kernel(q_ref, k_ref, v_ref, la_ref, sr_ref, go_ref, mst_ref, lmask_ref, hmask_ref, eind_ref, eindt_ref,
                y_ref, st_scr, *, chunks):
    c = GLA_CHUNK
    nk = GLA_HEADS * GLA_DK
    nv = GLA_HEADS * GLA_DV
    nsub = c // GLA_SUB

    @pl.when(pl.program_id(1) == 0)
    def _():
        st_scr[...] = jnp.zeros_like(st_scr)

    def chunk(ci, carry):
        r0 = pl.multiple_of(ci * c, c)
        q = q_ref[pl.ds(r0, c), :]
        k = k_ref[pl.ds(r0, c), :]
        v = v_ref[pl.ds(r0, c), :]
        a = la_ref[pl.ds(r0, c), :]
        a1 = a.astype(BF16)
        r1 = a - a1.astype(F32)
        a2 = r1.astype(BF16)
        a3 = (r1 - a2.astype(F32)).astype(BF16)
        mst = mst_ref[...]
        ex = (jnp.dot(mst, a1, preferred_element_type=F32) + jnp.dot(mst, a2, preferred_element_type=F32)
              + jnp.dot(mst, a3, preferred_element_type=F32))
        b = ex[0:c]
        f = jnp.exp(ex)
        v_bf = v.astype(BF16)

        amat = jnp.zeros((GLA_HEADS * c, c), F32)
        hmask = hmask_ref[...]
        for li in range(len(GLA_LEVELS)):
            fl = f[(li + 1) * c:(li + 2) * c]
            qt = q * fl
            kt = (k * fl).astype(BF16)
            qs = (jnp.concatenate([qt] * GLA_HEADS, axis=0) * hmask).astype(BF16)
            sc = lax.dot_general(qs, kt, (((1,), (1,)), ((), ())), preferred_element_type=F32)
            amat = amat + sc * lmask_ref[li]
        a_bf = amat.astype(BF16)
        o = jnp.concatenate(
            [jnp.dot(a_bf[hh * c:(hh + 1) * c], v_bf[:, hh * GLA_DV:(hh + 1) * GLA_DV], preferred_element_type=F32)
             for hh in range(GLA_HEADS)], axis=1)

        qb = (q * f[0:c]).astype(BF16)
        st = st_scr[...]
        o = o + lax.dot_general(qb, st.astype(BF16), (((1,), (1,)), ((), ())), preferred_element_type=F32)

        q3 = q.reshape(nsub, GLA_SUB, nk)
        k3 = k.reshape(nsub, GLA_SUB, nk)
        b3 = b.reshape(nsub, GLA_SUB, nk)
        v3 = v.reshape(nsub, GLA_SUB, nv)
        tt = lax.broadcasted_iota(jnp.int32, (nsub, GLA_SUB, nk), 1)
        od = jnp.zeros((nsub, GLA_SUB, nv), F32)
        eind = eind_ref[...]
        for s in range(GLA_SUB):
            w = jnp.exp(jnp.minimum(b3 - b3[:, s:s + 1, :], 0.0))
            x = jnp.where(tt >= s, q3 * k3[:, s:s + 1, :] * w, 0.0)
            rr = jnp.dot(x.reshape(c, nk).astype(BF16), eind, preferred_element_type=F32)
            od = od + rr.reshape(nsub, GLA_SUB, nv) * v3[:, s:s + 1, :]
        o = o + od.reshape(c, nv)

        kend = (k * f[(len(GLA_LEVELS) + 1) * c:(len(GLA_LEVELS) + 2) * c]).astype(BF16)
        upd = lax.dot_general(v_bf, kend, (((0,), (0,)), ((), ())), preferred_element_type=F32)
        st_scr[...] = st * f[c - 1:c] + upd * eindt_ref[...]

        go = go_ref[...]
        sr = sr_ref[pl.ds(r0, c), :]
        ys = []
        for hh in range(GLA_HEADS):
            oh = o[:, hh * GLA_DV:(hh + 1) * GLA_DV]
            ys.append(oh * _inv_rms(oh, GLA_DV) * go)
        y_ref[pl.ds(r0, c), :] = (jnp.concatenate(ys, axis=1) * sr).astype(y_ref.dtype)
        return carry

    lax.fori_loop(0, chunks, chunk, 0, unroll=2)


def _gla(q, k, v, la, sr, go, layer, *, batch, seq, tb):
    nk = GLA_HEADS * GLA_DK
    nv = GLA_HEADS * GLA_DV
    nb = seq // tb
    mstack, lmask, headmask, e_ind = _gla_constants()
    consts = [jnp.asarray(mstack, BF16), jnp.asarray(lmask, F32), jnp.asarray(headmask, F32),
              jnp.asarray(e_ind, BF16), jnp.asarray(e_ind.T, F32)]
    row = lambda b, i: (b * nb + i, 0)
    return pl.pallas_call(
        functools.partial(_gla_kernel, chunks=tb // GLA_CHUNK),
        grid=(batch, nb),
        in_specs=[pl.BlockSpec((tb, nk), row), pl.BlockSpec((tb, nk), row), pl.BlockSpec((tb, nv), row),
                  pl.BlockSpec((tb, nk), row), pl.BlockSpec((tb, nv), row), _layer_spec(go, layer)]
                 + [_const_spec(cst.shape) for cst in consts],
        out_specs=pl.BlockSpec((tb, nv), row),
        out_shape=jax.ShapeDtypeStruct((batch * seq, nv), BF16),
        scratch_shapes=[pltpu.VMEM((nv, nk), F32)],
        compiler_params=_cparams(("parallel", "arbitrary"), VMEM_SCAN),
        name="gla_scan",
    )(q, k, v, la, sr, go, *consts)


def _dswa_proj_kernel(x_ref, g_ref, w_ref, gq_ref, gk_ref, *refs, tm):
    ng = len(DSWA_GROUPS)
    out_refs = refs[:3 * ng]
    q_scr, k_scr, v_scr = refs[3 * ng:]
    xf = x_ref[...]
    h = (xf * _inv_rms(xf, xf.shape[-1]) * g_ref[...]).astype(BF16)
    u = jnp.dot(h, w_ref[...], preferred_element_type=F32)
    n = DSWA_HEADS * DSWA_HEAD_DIM
    gq = gq_ref[...]
    gk = gk_ref[...]
    scale = DSWA_HEAD_DIM ** -0.5
    for hh in range(DSWA_HEADS):
        sl = slice(hh * DSWA_HEAD_DIM, (hh + 1) * DSWA_HEAD_DIM)
        qh = u[:, sl]
        q_scr[hh] = qh * _inv_rms(qh, DSWA_HEAD_DIM) * gq * scale
        kh = u[:, n + hh * DSWA_HEAD_DIM:n + (hh + 1) * DSWA_HEAD_DIM]
        k_scr[hh] = kh * _inv_rms(kh, DSWA_HEAD_DIM) * gk
        v_scr[hh] = u[:, 2 * n + hh * DSWA_HEAD_DIM:2 * n + (hh + 1) * DSWA_HEAD_DIM]
    for gi, (_, dil) in enumerate(DSWA_GROUPS):
        for src, dst in zip((q_scr, k_scr, v_scr), out_refs[3 * gi:3 * gi + 3]):
            for hh in range(DSWA_HEADS_PER_GROUP):
                head = gi * DSWA_HEADS_PER_GROUP + hh
                cols = slice(hh * DSWA_HEAD_DIM, (hh + 1) * DSWA_HEAD_DIM)
                for r in range(dil):
                    rows = pl.ds(r, tm // dil, stride=dil) if dil > 1 else slice(None)
                    dst[r, :, cols] = src[head, rows, :].astype(BF16)


def _dswa_proj(x, g, w, gq, gk, layer, *, batch, seq, tm):
    t, d = x.shape
    n = DSWA_HEADS * DSWA_HEAD_DIM
    gw = DSWA_HEADS_PER_GROUP * DSWA_HEAD_DIM
    nblk = seq // tm
    out_specs, out_shape = [], []
    for _, dil in DSWA_GROUPS:
        assert tm % (BF16_ROWS * dil) == 0
        out_specs += [pl.BlockSpec((None, dil, tm // dil, gw), lambda b, i: (b, 0, i, 0))] * 3
        out_shape += [jax.ShapeDtypeStruct((batch, dil, seq // dil, gw), BF16)] * 3
    return pl.pallas_call(
        functools.partial(_dswa_proj_kernel, tm=tm),
        grid=(batch, nblk),
        in_specs=[pl.BlockSpec((tm, d), lambda b, i: (b * nblk + i, 0))]
                 + [_layer_spec(p, layer) for p in (g, w, gq, gk)],
        out_specs=out_specs,
        out_shape=out_shape,
        scratch_shapes=[pltpu.VMEM((DSWA_HEADS, tm, DSWA_HEAD_DIM), F32)] * 3,
        compiler_params=_cparams(("parallel", "parallel"), VMEM_RESIDENT),
        name="dswa_proj",
    )(x, g, w, gq, gk)


def _dswa_kernel(slope_ref, q_ref, kc_ref, kp_ref, vc_ref, vp_ref, o_ref, l_ref, *, dilation, group, nb):
    w = DSWA_W
    ib = pl.program_id(2)
    i = lax.broadcasted_iota(jnp.int32, (w, 2 * w), 0)
    j = lax.broadcasted_iota(jnp.int32, (w, 2 * w), 1)
    steps = w + i - j
    dist = (steps * dilation).astype(F32)
    in_window = (steps >= 0) & (steps <= w)
    for hh in range(DSWA_HEADS_PER_GROUP):
        cols = slice(hh * DSWA_HEAD_DIM, (hh + 1) * DSWA_HEAD_DIM)
        bias_all = jnp.where(in_window, -slope_ref[group * DSWA_HEADS_PER_GROUP + hh] * dist, NEG_INF)
        bias_first = jnp.where(j >= w, bias_all, NEG_INF)
        for jb in range(nb):
            rows = slice(jb * w, (jb + 1) * w)
            if jb == 0:
                kprev, vprev = kp_ref[:, cols], vp_ref[:, cols]
                bias_j = jnp.where(ib > 0, bias_all, bias_first)
            else:
                prows = slice((jb - 1) * w, jb * w)
                kprev, vprev = kc_ref[prows, cols], vc_ref[prows, cols]
                bias_j = bias_all
            keys = jnp.concatenate([kprev, kc_ref[rows, cols]], axis=0)
            vals = jnp.concatenate([vprev, vc_ref[rows, cols]], axis=0)
            sc = lax.dot_general(q_ref[rows, cols], keys, (((1,), (1,)), ((), ())), preferred_element_type=F32)
            sc = jnp.where(bias_j > 0.5 * NEG_INF, sc + bias_j, NEG_INF)
            m = jnp.max(sc, axis=-1, keepdims=True)
            p = jnp.exp(sc - m)
            lsum = jnp.sum(p, axis=-1, keepdims=True)
            o_ref[rows, cols] = jnp.dot(p.astype(BF16), vals, preferred_element_type=F32) / lsum
            l_ref[rows, cols] = jnp.broadcast_to(m + jnp.log(lsum), (w, DSWA_HEAD_DIM))


def _dswa_group(slopes, qg, kg, vg, *, group):
    window, dilation = DSWA_GROUPS[group]
    assert window // dilation == DSWA_W
    batch, _, sd, gw = qg.shape
    tq = min(DSWA_MAX_Q, sd)
    nb = tq // DSWA_W
    cur = pl.BlockSpec((None, None, tq, gw), lambda b, r, i: (b, r, i, 0))
    prev = pl.BlockSpec((None, None, DSWA_W, gw), lambda b, r, i: (b, r, jnp.maximum(i * nb - 1, 0), 0))
    return pl.pallas_call(
        functools.partial(_dswa_kernel, dilation=dilation, group=group, nb=nb),
        grid=(batch, dilation, sd // tq),
        in_specs=[pl.BlockSpec(memory_space=pltpu.SMEM), cur, cur, prev, cur, prev],
        out_specs=[cur] * 2,
        out_shape=[jax.ShapeDtypeStruct(qg.shape, F32)] * 2,
        compiler_params=_cparams(("parallel", "parallel", "arbitrary"), VMEM_SMALL),
        name="dswa_g%d" % group,
    )(slopes, qg, kg, kg, vg, vg)


def _dswa_combine_kernel(*refs, tm):
    ng = len(DSWA_GROUPS)
    in_refs, y_ref, scr = refs[:2 * ng], refs[2 * ng], refs[2 * ng + 1]
    vals = []
    for gi, (_, dil) in enumerate(DSWA_GROUPS):
        for which in range(2):
            src = in_refs[2 * gi + which]
            if dil == 1:
                vals.append(src[0])
            else:
                halves = []
                for hh in range(DSWA_HEADS_PER_GROUP):
                    slot = (2 * gi + which) * DSWA_HEADS_PER_GROUP + hh
                    for r in range(dil):
                        scr[slot, pl.ds(r, tm // dil, stride=dil), :] = (
                            src[r, :, hh * DSWA_HEAD_DIM:(hh + 1) * DSWA_HEAD_DIM])
                    halves.append(scr[slot])
                vals.append(jnp.concatenate(halves, axis=1))
    lses = vals[1::2]
    mx = functools.reduce(jnp.maximum, lses)
    es = [jnp.exp(l - mx) for l in lses]
    den = functools.reduce(lambda a, b: a + b, es)
    y = functools.reduce(lambda a, b: a + b, [(e / den) * o for e, o in zip(es, vals[0::2])])
    y_ref[...] = y.astype(y_ref.dtype)


def _dswa_combine(outs_lses, *, batch, seq, tm):
    gw = DSWA_HEADS_PER_GROUP * DSWA_HEAD_DIM
    nblk = seq // tm
    in_specs = []
    for _, dil in DSWA_GROUPS:
        in_specs += [pl.BlockSpec((None, dil, tm // dil, gw), lambda b, i: (b, 0, i, 0))] * 2
    return pl.pallas_call(
        functools.partial(_dswa_combine_kernel, tm=tm),
        grid=(batch, nblk),
        in_specs=in_specs,
        out_specs=pl.BlockSpec((tm, gw), lambda b, i: (b * nblk + i, 0)),
        out_shape=jax.ShapeDtypeStruct((batch * seq, gw), BF16),
        scratch_shapes=[pltpu.VMEM((2 * len(DSWA_GROUPS) * DSWA_HEADS_PER_GROUP, tm, DSWA_HEAD_DIM), F32)],
        compiler_params=_cparams(("parallel", "parallel"), VMEM_SMALL),
        name="dswa_combine",
    )(*outs_lses)


def _merge_kernel(x_ref, g_ref, ya_ref, yb_ref, yc_ref, wga_ref, wgb_ref, wgc_ref, wa_ref, wb_ref, wc_ref, wo_ref,
                  out_ref, h_scr):
    dot = lambda a, b: jnp.dot(a, b, preferred_element_type=F32)

    def chunk(h):
        inter = (jax.nn.sigmoid(dot(h, wga_ref[...])) * dot(ya_ref[...], wa_ref[...])
                 + jax.nn.sigmoid(dot(h, wgb_ref[...])) * dot(yb_ref[...], wb_ref[...])
                 + jax.nn.sigmoid(dot(h, wgc_ref[...])) * dot(yc_ref[...], wc_ref[...]))
        return dot(inter.astype(BF16), wo_ref[...])

    @pl.when(pl.program_id(1) == 0)
    def _():
        xf = x_ref[...]
        h = (xf * _inv_rms(xf, xf.shape[-1]) * g_ref[...]).astype(BF16)
        h_scr[...] = h
        out_ref[...] = xf + chunk(h)

    @pl.when(pl.program_id(1) > 0)
    def _():
        out_ref[...] += chunk(h_scr[...])


def _merge(x, g, ya, yb, yc, wga, wgb, wgc, wa, wb, wc, wo, layer, *, tm, tn):
    t, d = x.shape
    row = lambda i, j: (i, 0)
    colw = lambda i, j: (layer, 0, j)
    return pl.pallas_call(
        _merge_kernel,
        grid=(t // tm, d // tn),
        in_specs=[pl.BlockSpec((tm, d), row), _layer_spec(g, layer),
                  pl.BlockSpec((tm, ya.shape[1]), row), pl.BlockSpec((tm, yb.shape[1]), row),
                  pl.BlockSpec((tm, yc.shape[1]), row)]
                 + [pl.BlockSpec((None, w.shape[1], tn), colw) for w in (wga, wgb, wgc, wa, wb, wc)]
                 + [pl.BlockSpec((None, tn, d), lambda i, j: (layer, j, 0))],
        out_specs=pl.BlockSpec((tm, d), row),
        out_shape=jax.ShapeDtypeStruct((t, d), F32),
        scratch_shapes=[pltpu.VMEM((tm, d), BF16)],
        compiler_params=_cparams(("parallel", "arbitrary"), VMEM_RESIDENT),
        name="merge",
    )(x, g, ya, yb, yc, wga, wgb, wgc, wa, wb, wc, wo)


def _pad_last(w, n):
    return jnp.pad(w, [(0, 0)] * (w.ndim - 1) + [(0, n - w.shape[-1])])


def _rot_half_cols(w):
    half = w.shape[-1] // 2
    return jnp.concatenate([-w[..., half:], w[..., :half]], axis=-1)


def _swap_half(g):
    half = g.shape[-1] // 2
    return jnp.concatenate([g[..., half:], g[..., :half]], axis=-1)


def _gain3(g):
    rope = g[:, MLA_NOPE:]
    return jnp.stack([g[:, :MLA_NOPE], _pad_last(rope, LANE), _pad_last(_swap_half(rope), LANE)], axis=1)


def _row(g):
    return g[:, None, :]


def kernel(x, positions, ffn1_norm, ffn1_w_gate, ffn1_w_up, ffn1_w_down, mix_norm, w_in, mla_cq_norm, mla_ckv_norm, mla_w_uq, mla_w_ukv, mla_q_norm, mla_k_norm, gla_w_gate2, gla_b_gate2, gla_o_norm, dswa_q_norm, dswa_k_norm, w_branch_a, w_branch_b, w_branch_c, w_out, ffn2_norm, ffn2_w_gate, ffn2_w_up, ffn2_w_down):
    batch, seq, d = x.shape
    depth = w_in.shape[0]
    d_ff = ffn1_w_gate.shape[-1]
    t = batch * seq
    tm, tm_ffn, tf = TOKEN_BLOCK, FFN_TOKEN_BLOCK, FFN_CHUNK
    ff_pad = -(-d_ff // tf) * tf
    assert seq % max(wd for wd, _ in DSWA_GROUPS) == 0 and seq % tm == 0 and t % tm_ffn == 0

    xf = x.reshape(t, d)
    half = MLA_ROPE // 2
    inv_freq = ROPE_THETA ** (-jnp.arange(half, dtype=F32) / half)
    inv_pad = jnp.pad(jnp.concatenate([inv_freq, inv_freq]), (0, LANE - MLA_ROPE)).reshape(1, LANE)
    cos, sin = _rope_tables(positions.reshape(t, 1), inv_pad, tm=tm)
    slopes = 2.0 ** (-ALIBI_MAX_EXP * jnp.arange(1, DSWA_HEADS + 1, dtype=F32) / DSWA_HEADS)

    o_cq, o_ckv, o_kr = 0, MLA_Q_RANK, MLA_Q_RANK + MLA_KV_RANK
    o_gla = o_kr + MLA_ROPE
    n_qkv_b = GLA_HEADS * (2 * GLA_DK + GLA_DV)
    o_glr = o_gla + n_qkv_b
    o_rb = o_glr + GLA_GATE_RANK
    o_c = o_rb + GLA_HEADS * GLA_DV
    o_ga = o_c + 3 * DSWA_HEADS * DSWA_HEAD_DIM
    o_gb, o_gc = o_ga + d, o_ga + 2 * d

    bf = lambda w: w.astype(BF16)

    def ffn_weights(wg, wu, wdn):
        pad_cols = lambda w: jnp.zeros((depth, d, ff_pad), BF16).at[:, :, :d_ff].set(bf(w))
        return pad_cols(wg), pad_cols(wu), jnp.zeros((depth, ff_pad, d), BF16).at[:, :d_ff, :].set(bf(wdn))

    ffn1 = (_row(ffn1_norm),) + ffn_weights(ffn1_w_gate, ffn1_w_up, ffn1_w_down)
    ffn2 = (_row(ffn2_norm),) + ffn_weights(ffn2_w_gate, ffn2_w_up, ffn2_w_down)
    gmix = _row(mix_norm)
    w_in = bf(w_in)
    w_kr = w_in[:, :, o_kr:o_kr + MLA_ROPE]
    w1 = bf(jnp.concatenate([w_in[:, :, o_cq:o_kr], _pad_last(w_kr, LANE), _pad_last(_rot_half_cols(w_kr), LANE)],
                            axis=-1))
    wq4 = mla_w_uq.reshape(depth, MLA_Q_RANK, MLA_HEADS, MLA_QK)
    wq_rope = wq4[..., MLA_NOPE:]
    wuq = jnp.concatenate([wq4[..., :MLA_NOPE], _pad_last(wq_rope, LANE), _pad_last(_rot_half_cols(wq_rope), LANE)],
                          axis=-1)
    wuq = bf(wuq.reshape(depth, MLA_Q_RANK, MLA_HEADS * 3 * LANE))
    wkv4 = mla_w_ukv.reshape(depth, MLA_KV_RANK, MLA_HEADS, MLA_NOPE + MLA_V)
    wukv = bf(jnp.concatenate([wkv4[..., :MLA_NOPE].reshape(depth, MLA_KV_RANK, -1),
                               wkv4[..., MLA_NOPE:].reshape(depth, MLA_KV_RANK, -1)], axis=-1))
    mla_params = (w1, _row(mla_cq_norm), _row(mla_ckv_norm), wuq, wukv, _gain3(mla_q_norm), _gain3(mla_k_norm))
    w_gla = bf(jnp.concatenate([w_in[:, :, o_gla:o_glr], _pad_last(w_in[:, :, o_glr:o_rb], LANE),
                                w_in[:, :, o_rb:o_c]], axis=-1))
    w2 = bf(jnp.pad(gla_w_gate2, ((0, 0), (0, LANE - GLA_GATE_RANK), (0, 0))))
    gla_params = (w_gla, w2, _row(gla_b_gate2))
    go = _row(gla_o_norm)
    dswa_params = (bf(w_in[:, :, o_c:o_ga]), _row(dswa_q_norm), _row(dswa_k_norm))
    merge_params = (bf(w_in[:, :, o_ga:o_gb]), bf(w_in[:, :, o_gb:o_gc]), bf(w_in[:, :, o_gc:o_gc + d]),
                    bf(w_branch_a), bf(w_branch_b), bf(w_branch_c), bf(w_out))

    for l in range(depth):
        xf = _ffn(xf, *ffn1, l, tm=tm_ffn, tf=tf)

        qa, ka, va = _mla_prep(xf, gmix, cos, sin, *mla_params, l, tm=tm)
        y_a = _flash(qa, ka, va, batch=batch, seq=seq, tq=FLASH_TILE)

        qb, kb, vb, la, sr = _gla_proj(xf, gmix, *gla_params, l, tm=tm)
        y_b = _gla(qb, kb, vb, la, sr, go, l, batch=batch, seq=seq, tb=GLA_TOKENS)

        qkv_c = _dswa_proj(xf, gmix, *dswa_params, l, batch=batch, seq=seq, tm=tm)
        outs_lses = []
        for gi in range(len(DSWA_GROUPS)):
            outs_lses += _dswa_group(slopes, *qkv_c[3 * gi:3 * gi + 3], group=gi)
        y_c = _dswa_combine(outs_lses, batch=batch, seq=seq, tm=tm)

        xf = _merge(xf, gmix, y_a, y_b, y_c, *merge_params, l, tm=tm, tn=MERGE_CHUNK)
        xf = _ffn(xf, *ffn2, l, tm=tm_ffn, tf=tf)
    return xf.reshape(batch, seq, d)
```

```python
import functools

import numpy as np
import jax
import jax.numpy as jnp
from jax import lax
from jax.experimental import pallas as pl
from jax.experimental.pallas import tpu as pltpu

F32 = jnp.float32
BF16 = jnp.bfloat16

MLA_HEADS = 6
MLA_Q_RANK = 512
MLA_KV_RANK = 256
MLA_NOPE = 128
MLA_ROPE = 64
MLA_V = 128
MLA_QK = MLA_NOPE + MLA_ROPE
MLA_QK_PAD = 256
GLA_HEADS = 4
GLA_DK = 64
GLA_DV = 128
GLA_GATE_RANK = 16
GLA_TAU = 16.0
GLA_CHUNK = 64
GLA_SUB = 8
GLA_LEVELS = (32, 16, 8)
DSWA_GROUPS = ((128, 1), (512, 4), (2048, 16))
DSWA_HEADS_PER_GROUP = 2
DSWA_HEADS = 6
DSWA_HEAD_DIM = 128
DSWA_W = 128
ROPE_THETA = 10000.0
ALIBI_MAX_EXP = 8.0
NORM_EPS = 1e-6
NEG_INF = -1e30
LOG2E = 1.4426950408889634

LANE = 128
LSE_LANES = LANE // DSWA_HEADS_PER_GROUP
BF16_ROWS = 16
MIB = 1024 * 1024

TOKEN_BLOCK = 512
FFN_TOKEN_BLOCK = 1024
FFN_CHUNK = 512
MERGE_CHUNK = 512
FLASH_TILE = 1024
GLA_TOKENS = 512
DSWA_MAX_Q = 1024
VMEM_FFN = 57
VMEM_RESIDENT = 52
VMEM_SCAN = 40
VMEM_SMALL = 32


def _cparams(sem, vmem_mib):
    return pltpu.CompilerParams(dimension_semantics=sem, vmem_limit_bytes=int(vmem_mib * MIB))


def _inv_rms(xf, n):
    return lax.rsqrt(jnp.sum(xf * xf, axis=-1, keepdims=True) / n + NORM_EPS)


def _const_spec(shape):
    nd = len(shape)
    return pl.BlockSpec(shape, lambda *_: (0,) * nd)


def _layer_spec(arr, layer):
    nd = arr.ndim - 1
    return pl.BlockSpec((None,) + arr.shape[1:], lambda *_: (layer,) + (0,) * nd)


def _ffn_kernel(x_ref, g_ref, wg_ref, wu_ref, wd_ref, o_ref, h_scr):
    def half_chunk(h):
        gate = jnp.dot(h, wg_ref[...], preferred_element_type=F32)
        up = jnp.dot(h, wu_ref[...], preferred_element_type=F32)
        inter = (gate * jax.nn.sigmoid(gate) * up).astype(BF16)
        return 0.5 * jnp.dot(inter, wd_ref[...], preferred_element_type=F32)

    @pl.when(pl.program_id(1) == 0)
    def _():
        xf = x_ref[...]
        h = (xf * _inv_rms(xf, xf.shape[-1]) * g_ref[...]).astype(BF16)
        h_scr[...] = h
        o_ref[...] = xf + half_chunk(h)

    @pl.when(pl.program_id(1) > 0)
    def _():
        o_ref[...] += half_chunk(h_scr[...])


def _ffn(x, g, wg, wu, wd, layer, *, tm, tf):
    t, d = x.shape
    fp = wg.shape[-1]
    return pl.pallas_call(
        _ffn_kernel,
        grid=(t // tm, fp // tf),
        in_specs=[
            pl.BlockSpec((tm, d), lambda i, j: (i, 0)),
            _layer_spec(g, layer),
            pl.BlockSpec((None, d, tf), lambda i, j: (layer, 0, j)),
            pl.BlockSpec((None, d, tf), lambda i, j: (layer, 0, j)),
            pl.BlockSpec((None, tf, d), lambda i, j: (layer, j, 0)),
        ],
        out_specs=pl.BlockSpec((tm, d), lambda i, j: (i, 0)),
        out_shape=jax.ShapeDtypeStruct((t, d), F32),
        scratch_shapes=[pltpu.VMEM((tm, d), BF16)],
        compiler_params=_cparams(("parallel", "arbitrary"), VMEM_FFN),
        name="ffn",
    )(x, g, wg, wu, wd)


def _rope_kernel(pos_ref, inv_ref, cos_ref, sin_ref):
    ang = pos_ref[...].astype(F32) * inv_ref[...]
    cos_ref[...] = jnp.cos(ang)
    sin_ref[...] = jnp.sin(ang)


def _rope_tables(pos_col, inv_pad, *, tm):
    t = pos_col.shape[0]
    return pl.pallas_call(
        _rope_kernel,
        grid=(t // tm,),
        in_specs=[pl.BlockSpec((tm, 1), lambda i: (i, 0)), _const_spec((1, LANE))],
        out_specs=[pl.BlockSpec((tm, LANE), lambda i: (i, 0))] * 2,
        out_shape=[jax.ShapeDtypeStruct((t, LANE), F32)] * 2,
        compiler_params=_cparams(("parallel",), VMEM_SMALL),
        name="rope_tables",
    )(pos_col, inv_pad)


def _mla_prep_kernel(x_ref, g_ref, cos_ref, sin_ref, w1_ref, gcq_ref, gckv_ref, wuq_ref, wukv_ref,
                     gq_ref, gk_ref, q_ref, k_ref, v_ref):
    xf = x_ref[...]
    h = (xf * _inv_rms(xf, xf.shape[-1]) * g_ref[...]).astype(BF16)
    u = jnp.dot(h, w1_ref[...], preferred_element_type=F32)
    c_q = u[:, :MLA_Q_RANK]
    c_kv = u[:, MLA_Q_RANK:MLA_Q_RANK + MLA_KV_RANK]
    kr = u[:, MLA_Q_RANK + MLA_KV_RANK:MLA_Q_RANK + MLA_KV_RANK + LANE]
    krot = u[:, MLA_Q_RANK + MLA_KV_RANK + LANE:]
    c_q = (c_q * _inv_rms(c_q, MLA_Q_RANK) * gcq_ref[...]).astype(BF16)
    c_kv = (c_kv * _inv_rms(c_kv, MLA_KV_RANK) * gckv_ref[...]).astype(BF16)
    qall = jnp.dot(c_q, wuq_ref[...], preferred_element_type=F32)
    kv = jnp.dot(c_kv, wukv_ref[...], preferred_element_type=F32)
    cos = cos_ref[...]
    sin = sin_ref[...]
    scale = MLA_QK ** -0.5 * LOG2E
    gq_n, gq_r, gq_t = gq_ref[0:1, :], gq_ref[1:2, :], gq_ref[2:3, :]
    gk_n, gk_r, gk_t = gk_ref[0:1, :], gk_ref[1:2, :], gk_ref[2:3, :]
    k_roped = kr * gk_r * cos + krot * gk_t * sin
    kr_ss = jnp.sum(kr * kr, axis=-1, keepdims=True)
    for hh in range(MLA_HEADS):
        base = hh * 3 * LANE
        nope = qall[:, base:base + LANE]
        rope = qall[:, base + LANE:base + 2 * LANE]
        rot = qall[:, base + 2 * LANE:base + 3 * LANE]
        ss = jnp.sum(nope * nope, axis=-1, keepdims=True) + jnp.sum(rope * rope, axis=-1, keepdims=True)
        r = lax.rsqrt(ss / MLA_QK + NORM_EPS) * scale
        q_ref[hh, :, 0:LANE] = (nope * r * gq_n).astype(BF16)
        q_ref[hh, :, LANE:2 * LANE] = ((rope * gq_r * cos + rot * gq_t * sin) * r).astype(BF16)
        kn = kv[:, hh * LANE:(hh + 1) * LANE]
        rk = lax.rsqrt((jnp.sum(kn * kn, axis=-1, keepdims=True) + kr_ss) / MLA_QK + NORM_EPS)
        k_ref[hh, :, 0:LANE] = (kn * rk * gk_n).astype(BF16)
        k_ref[hh, :, LANE:2 * LANE] = (k_roped * rk).astype(BF16)
        v_ref[hh] = kv[:, (MLA_HEADS + hh) * LANE:(MLA_HEADS + hh + 1) * LANE].astype(BF16)


def _mla_prep(x, g, cos, sin, w1, gcq, gckv, wuq, wukv, gq3, gk3, layer, *, tm):
    t, d = x.shape
    row = lambda i: (i, 0)
    hrow = lambda i: (0, i, 0)
    return pl.pallas_call(
        _mla_prep_kernel,
        grid=(t // tm,),
        in_specs=[pl.BlockSpec((tm, d), row), _layer_spec(g, layer),
                  pl.BlockSpec((tm, LANE), row), pl.BlockSpec((tm, LANE), row)]
                 + [_layer_spec(p, layer) for p in (w1, gcq, gckv, wuq, wukv, gq3, gk3)],
        out_specs=[
            pl.BlockSpec((MLA_HEADS, tm, MLA_QK_PAD), hrow),
            pl.BlockSpec((MLA_HEADS, tm, MLA_QK_PAD), hrow),
            pl.BlockSpec((MLA_HEADS, tm, MLA_V), hrow),
        ],
        out_shape=[
            jax.ShapeDtypeStruct((MLA_HEADS, t, MLA_QK_PAD), BF16),
            jax.ShapeDtypeStruct((MLA_HEADS, t, MLA_QK_PAD), BF16),
            jax.ShapeDtypeStruct((MLA_HEADS, t, MLA_V), BF16),
        ],
        compiler_params=_cparams(("parallel",), VMEM_RESIDENT),
        name="mla_prep",
    )(x, g, cos, sin, w1, gcq, gckv, wuq, wukv, gq3, gk3)


def _flash_kernel(q_ref, k_ref, v_ref, o_ref, sa_scr, sb_scr, m_scr, acc_scr, *, tq):
    qi = pl.program_id(2)
    m_scr[...] = jnp.full_like(m_scr, NEG_INF)
    acc_scr[...] = jnp.zeros_like(acc_scr)
    q = q_ref[...]
    ones = jnp.ones((tq, MLA_V), BF16)

    def scores(tile, s_scr):
        r0 = pl.multiple_of(tile * tq, tq)
        s_scr[...] = lax.dot_general(q, k_ref[pl.ds(r0, tq), :], (((1,), (1,)), ((), ())),
                                     preferred_element_type=F32)

    def accumulate(tile, s_scr, diagonal):
        r0 = pl.multiple_of(tile * tq, tq)
        sc = s_scr[...]
        if diagonal:
            row = lax.broadcasted_iota(jnp.int32, sc.shape, 0)
            col = lax.broadcasted_iota(jnp.int32, sc.shape, 1)
            sc = jnp.where(col <= row, sc, NEG_INF)
        m_prev = m_scr[...]
        m_new = jnp.maximum(m_prev, jnp.max(sc, axis=-1, keepdims=True))
        alpha = jnp.exp2(m_prev - m_new)
        p = jnp.exp2(sc - jnp.tile(m_new, (1, tq // LANE)))
        v1 = jnp.concatenate([v_ref[pl.ds(r0, tq), :], ones], axis=1)
        pv = jnp.dot(p.astype(BF16), v1, preferred_element_type=F32)
        acc_scr[...] = jnp.tile(alpha, (1, 2)) * acc_scr[...] + pv
        m_scr[...] = m_new

    scores(0, sa_scr)

    def body(j, carry):
        scores(2 * j + 1, sb_scr)
        accumulate(2 * j, sa_scr, False)
        scores(2 * j + 2, sa_scr)
        accumulate(2 * j + 1, sb_scr, False)
        return carry

    def body_x2(jj, carry):
        body(2 * jj, carry)
        return body(2 * jj + 1, carry)

    npair = qi // 2
    lax.fori_loop(0, npair // 2, body_x2, 0)
    lax.fori_loop(2 * (npair // 2), npair, body, 0)

    @pl.when(qi % 2 == 0)
    def _():
        accumulate(qi, sa_scr, True)

    @pl.when(qi % 2 == 1)
    def _():
        scores(qi, sb_scr)
        accumulate(qi - 1, sa_scr, False)
        accumulate(qi, sb_scr, True)

    acc = acc_scr[...]
    o_ref[...] = (acc[:, 0:MLA_V] / acc[:, MLA_V:]).astype(o_ref.dtype)


def _flash(q, k, v, *, batch, seq, tq):
    assert seq % tq == 0
    nq = seq // tq
    t = batch * seq
    return pl.pallas_call(
        functools.partial(_flash_kernel, tq=tq),
        grid=(batch, MLA_HEADS, nq),
        in_specs=[
            pl.BlockSpec((None, tq, MLA_QK_PAD), lambda b, h, i: (h, b * nq + i, 0)),
            pl.BlockSpec((None, seq, MLA_QK_PAD), lambda b, h, i: (h, b, 0)),
            pl.BlockSpec((None, seq, MLA_V), lambda b, h, i: (h, b, 0)),
        ],
        out_specs=pl.BlockSpec((tq, MLA_V), lambda b, h, i: (b * nq + i, h)),
        out_shape=jax.ShapeDtypeStruct((t, MLA_HEADS * MLA_V), BF16),
        scratch_shapes=[pltpu.VMEM((tq, tq), F32), pltpu.VMEM((tq, tq), F32),
                        pltpu.VMEM((tq, MLA_V), F32), pltpu.VMEM((tq, 2 * MLA_V), F32)],
        compiler_params=_cparams(("parallel", "parallel", "arbitrary"), VMEM_RESIDENT),
        name="mla_flash",
    )(q, k, v)


def _gla_proj_kernel(x_ref, g_ref, w_ref, w2_ref, b2_ref, q_ref, k_ref, v_ref, la_ref, sr_ref):
    xf = x_ref[...]
    h = (xf * _inv_rms(xf, xf.shape[-1]) * g_ref[...]).astype(BF16)
    u = jnp.dot(h, w_ref[...], preferred_element_type=F32)
    nk = GLA_HEADS * GLA_DK
    nv = GLA_HEADS * GLA_DV
    q_ref[...] = u[:, 0:nk] * (GLA_DK ** -0.5)
    k_ref[...] = u[:, nk:2 * nk]
    v_ref[...] = u[:, 2 * nk:2 * nk + nv]
    gate_lr = u[:, 2 * nk + nv:2 * nk + nv + LANE].astype(BF16)
    z = jnp.dot(gate_lr, w2_ref[...], preferred_element_type=F32) + b2_ref[...]
    la_ref[...] = -(jnp.maximum(-z, 0.0) + jnp.log1p(jnp.exp(-jnp.abs(z)))) / GLA_TAU
    r = u[:, 2 * nk + nv + LANE:]
    sr_ref[...] = r * jax.nn.sigmoid(r)


def _gla_proj(x, g, w, w2, b2, layer, *, tm):
    t, d = x.shape
    nk = GLA_HEADS * GLA_DK
    nv = GLA_HEADS * GLA_DV
    row = lambda i: (i, 0)
    return pl.pallas_call(
        _gla_proj_kernel,
        grid=(t // tm,),
        in_specs=[pl.BlockSpec((tm, d), row)] + [_layer_spec(p, layer) for p in (g, w, w2, b2)],
        out_specs=[pl.BlockSpec((tm, nk), row), pl.BlockSpec((tm, nk), row), pl.BlockSpec((tm, nv), row),
                   pl.BlockSpec((tm, nk), row), pl.BlockSpec((tm, nv), row)],
        out_shape=[jax.ShapeDtypeStruct((t, nk), F32), jax.ShapeDtypeStruct((t, nk), F32),
                   jax.ShapeDtypeStruct((t, nv), F32), jax.ShapeDtypeStruct((t, nk), F32),
                   jax.ShapeDtypeStruct((t, nv), F32)],
        compiler_params=_cparams(("parallel",), VMEM_RESIDENT),
        name="gla_proj",
    )(x, g, w, w2, b2)


def _gla_constants():
    c = GLA_CHUNK
    idx = np.arange(c)
    mats = [(idx[None, :] <= idx[:, None])]
    for lvl in GLA_LEVELS:
        p = ((idx // lvl) | 1) * lvl
        qrow = (idx >= p)[:, None] & (idx[None, :] >= p[:, None]) & (idx[None, :] <= idx[:, None])
        krow = (idx < p)[:, None] & (idx[None, :] > idx[:, None]) & (idx[None, :] < p[:, None])
        mats.append(qrow | krow)
    mats.append(idx[None, :] > idx[:, None])
    mstack = np.concatenate(mats, axis=0).astype(np.float32)
    lmask = []
    for lvl in GLA_LEVELS:
        blk = idx // lvl
        m = ((blk[:, None] % 2) == 1) & (blk[None, :] == blk[:, None] - 1)
        lmask.append(np.tile(m.astype(np.float32), (GLA_HEADS, 1)))
    lmask = np.stack(lmask, axis=0)
    hk = np.arange(GLA_HEADS * GLA_DK) // GLA_DK
    hv = np.arange(GLA_HEADS * GLA_DV) // GLA_DV
    hrow = np.arange(GLA_HEADS * c) // c
    headmask = (hrow[:, None] == hk[None, :]).astype(np.float32)
    e_ind = (hk[:, None] == hv[None, :]).astype(np.float32)
    return mstack, lmask, headmask, e_ind


def _gla_kernel(q_ref, k_ref, v_ref, la_ref, sr_ref, go_ref, mst_ref, lmask_ref, hmask_ref, eind_ref, eindt_ref,
                y_ref, st_scr, *, chunks):
    c = GLA_CHUNK
    nk = GLA_HEADS * GLA_DK
    nv = GLA_HEADS * GLA_DV
    nsub = c // GLA_SUB

    @pl.when(pl.program_id(1) == 0)
    def _():
        st_scr[...] = jnp.zeros_like(st_scr)

    def chunk(ci, carry):
        r0 = pl.multiple_of(ci * c, c)
        q = q_ref[pl.ds(r0, c), :]
        k = k_ref[pl.ds(r0, c), :]
        v = v_ref[pl.ds(r0, c), :]
        a = la_ref[pl.ds(r0, c), :]
        a1 = a.astype(BF16)
        r1 = a - a1.astype(F32)
        a2 = r1.astype(BF16)
        a3 = (r1 - a2.astype(F32)).astype(BF16)
        mst = mst_ref[...]
        ex = (jnp.dot(mst, a1, preferred_element_type=F32) + jnp.dot(mst, a2, preferred_element_type=F32)
              + jnp.dot(mst, a3, preferred_element_type=F32))
        b = ex[0:c]
        f = jnp.exp(ex)
        v_bf = v.astype(BF16)

        amat = jnp.zeros((GLA_HEADS * c, c), F32)
        hmask = hmask_ref[...]
        for li in range(len(GLA_LEVELS)):
            fl = f[(li + 1) * c:(li + 2) * c]
            qt = q * fl
            kt = (k * fl).astype(BF16)
            qs = (jnp.concatenate([qt] * GLA_HEADS, axis=0) * hmask).astype(BF16)
            sc = lax.dot_general(qs, kt, (((1,), (1,)), ((), ())), preferred_element_type=F32)
            amat = amat + sc * lmask_ref[li]
        a_bf = amat.astype(BF16)
        o = jnp.concatenate(
            [jnp.dot(a_bf[hh * c:(hh + 1) * c], v_bf[:, hh * GLA_DV:(hh + 1) * GLA_DV], preferred_element_type=F32)
             for hh in range(GLA_HEADS)], axis=1)

        qb = (q * f[0:c]).astype(BF16)
        st = st_scr[...]
        o = o + lax.dot_general(qb, st.astype(BF16), (((1,), (1,)), ((), ())), preferred_element_type=F32)

        q3 = q.reshape(nsub, GLA_SUB, nk)
        k3 = k.reshape(nsub, GLA_SUB, nk)
        b3 = b.reshape(nsub, GLA_SUB, nk)
        v3 = v.reshape(nsub, GLA_SUB, nv)
        tt = lax.broadcasted_iota(jnp.int32, (nsub, GLA_SUB, nk), 1)
        od = jnp.zeros((nsub, GLA_SUB, nv), F32)
        eind = eind_ref[...]
        for s in range(GLA_SUB):
            w = jnp.exp(jnp.minimum(b3 - b3[:, s:s + 1, :], 0.0))
            x = jnp.where(tt >= s, q3 * k3[:, s:s + 1, :] * w, 0.0)
            rr = jnp.dot(x.reshape(c, nk).astype(BF16), eind, preferred_element_type=F32)
            od = od + rr.reshape(nsub, GLA_SUB, nv) * v3[:, s:s + 1, :]
        o = o + od.reshape(c, nv)

        kend = (k * f[(len(GLA_LEVELS) + 1) * c:(len(GLA_LEVELS) + 2) * c]).astype(BF16)
        upd = lax.dot_general(v_bf, kend, (((0,), (0,)), ((), ())), preferred_element_type=F32)
        st_scr[...] = st * f[c - 1:c] + upd * eindt_ref[...]

        go = go_ref[...]
        sr = sr_ref[pl.ds(r0, c), :]
        ys = []
        for hh in range(GLA_HEADS):
            oh = o[:, hh * GLA_DV:(hh + 1) * GLA_DV]
            ys.append(oh * _inv_rms(oh, GLA_DV) * go)
        y_ref[pl.ds(r0, c), :] = (jnp.concatenate(ys, axis=1) * sr).astype(y_ref.dtype)
        return carry

    lax.fori_loop(0, chunks, chunk, 0, unroll=2)


def _gla(q, k, v, la, sr, go, layer, *, batch, seq, tb):
    nk = GLA_HEADS * GLA_DK
    nv = GLA_HEADS * GLA_DV
    nb = seq // tb
    mstack, lmask, headmask, e_ind = _gla_constants()
    consts = [jnp.asarray(mstack, BF16), jnp.asarray(lmask, F32), jnp.asarray(headmask, F32),
              jnp.asarray(e_ind, BF16), jnp.asarray(e_ind.T, F32)]
    row = lambda b, i: (b * nb + i, 0)
    return pl.pallas_call(
        functools.partial(_gla_kernel, chunks=tb // GLA_CHUNK),
        grid=(batch, nb),
        in_specs=[pl.BlockSpec((tb, nk), row), pl.BlockSpec((tb, nk), row), pl.BlockSpec((tb, nv), row),
                  pl.BlockSpec((tb, nk), row), pl.BlockSpec((tb, nv), row), _layer_spec(go, layer)]
                 + [_const_spec(cst.shape) for cst in consts],
        out_specs=pl.BlockSpec((tb, nv), row),
        out_shape=jax.ShapeDtypeStruct((batch * seq, nv), BF16),
        scratch_shapes=[pltpu.VMEM((nv, nk), F32)],
        compiler_params=_cparams(("parallel", "arbitrary"), VMEM_SCAN),
        name="gla_scan",
    )(q, k, v, la, sr, go, *consts)


def _dswa_proj_kernel(x_ref, g_ref, w_ref, gq_ref, gk_ref, *refs, tm):
    ng = len(DSWA_GROUPS)
    out_refs = refs[:3 * ng]
    q_scr, k_scr, v_scr = refs[3 * ng:]
    xf = x_ref[...]
    h = (xf * _inv_rms(xf, xf.shape[-1]) * g_ref[...]).astype(BF16)
    u = jnp.dot(h, w_ref[...], preferred_element_type=F32)
    n = DSWA_HEADS * DSWA_HEAD_DIM
    gq = gq_ref[...]
    gk = gk_ref[...]
    scale = DSWA_HEAD_DIM ** -0.5
    for hh in range(DSWA_HEADS):
        sl = slice(hh * DSWA_HEAD_DIM, (hh + 1) * DSWA_HEAD_DIM)
        qh = u[:, sl]
        q_scr[hh] = qh * _inv_rms(qh, DSWA_HEAD_DIM) * gq * scale
        kh = u[:, n + hh * DSWA_HEAD_DIM:n + (hh + 1) * DSWA_HEAD_DIM]
        k_scr[hh] = kh * _inv_rms(kh, DSWA_HEAD_DIM) * gk
        v_scr[hh] = u[:, 2 * n + hh * DSWA_HEAD_DIM:2 * n + (hh + 1) * DSWA_HEAD_DIM]
    for gi, (_, dil) in enumerate(DSWA_GROUPS):
        for src, dst in zip((q_scr, k_scr, v_scr), out_refs[3 * gi:3 * gi + 3]):
            for hh in range(DSWA_HEADS_PER_GROUP):
                head = gi * DSWA_HEADS_PER_GROUP + hh
                cols = slice(hh * DSWA_HEAD_DIM, (hh + 1) * DSWA_HEAD_DIM)
                for r in range(dil):
                    rows = pl.ds(r, tm // dil, stride=dil) if dil > 1 else slice(None)
                    dst[r, :, cols] = src[head, rows, :].astype(BF16)


def _dswa_proj(x, g, w, gq, gk, layer, *, batch, seq, tm):
    t, d = x.shape
    n = DSWA_HEADS * DSWA_HEAD_DIM
    gw = DSWA_HEADS_PER_GROUP * DSWA_HEAD_DIM
    nblk = seq // tm
    out_specs, out_shape = [], []
    for _, dil in DSWA_GROUPS:
        assert tm % (BF16_ROWS * dil) == 0
        out_specs += [pl.BlockSpec((None, dil, tm // dil, gw), lambda b, i: (b, 0, i, 0))] * 3
        out_shape += [jax.ShapeDtypeStruct((batch, dil, seq // dil, gw), BF16)] * 3
    return pl.pallas_call(
        functools.partial(_dswa_proj_kernel, tm=tm),
        grid=(batch, nblk),
        in_specs=[pl.BlockSpec((tm, d), lambda b, i: (b * nblk + i, 0))]
                 + [_layer_spec(p, layer) for p in (g, w, gq, gk)],
        out_specs=out_specs,
        out_shape=out_shape,
        scratch_shapes=[pltpu.VMEM((DSWA_HEADS, tm, DSWA_HEAD_DIM), F32)] * 3,
        compiler_params=_cparams(("parallel", "parallel"), VMEM_RESIDENT),
        name="dswa_proj",
    )(x, g, w, gq, gk)


def _dswa_kernel(slope_ref, q_ref, kc_ref, kp_ref, vc_ref, vp_ref, o_ref, l_ref, *, dilation, group, nb):
    w = DSWA_W
    ib = pl.program_id(2)
    i = lax.broadcasted_iota(jnp.int32, (w, 2 * w), 0)
    j = lax.broadcasted_iota(jnp.int32, (w, 2 * w), 1)
    steps = w + i - j
    dist = (steps * dilation).astype(F32)
    in_window = (steps >= 0) & (steps <= w)
    for hh in range(DSWA_HEADS_PER_GROUP):
        cols = slice(hh * DSWA_HEAD_DIM, (hh + 1) * DSWA_HEAD_DIM)
        bias_all = jnp.where(in_window, -slope_ref[group * DSWA_HEADS_PER_GROUP + hh] * dist, NEG_INF)
        bias_first = jnp.where(j >= w, bias_all, NEG_INF)
        for jb in range(nb):
            rows = slice(jb * w, (jb + 1) * w)
            if jb == 0:
                kprev, vprev = kp_ref[:, cols], vp_ref[:, cols]
                bias_j = jnp.where(ib > 0, bias_all, bias_first)
            else:
                prows = slice((jb - 1) * w, jb * w)
                kprev, vprev = kc_ref[prows, cols], vc_ref[prows, cols]
                bias_j = bias_all
            keys = jnp.concatenate([kprev, kc_ref[rows, cols]], axis=0)
            vals = jnp.concatenate([vprev, vc_ref[rows, cols]], axis=0)
            sc = lax.dot_general(q_ref[rows, cols], keys, (((1,), (1,)), ((), ())), preferred_element_type=F32)
            sc = jnp.where(bias_j > 0.5 * NEG_INF, sc + bias_j, NEG_INF)
            m = jnp.max(sc, axis=-1, keepdims=True)
            p = jnp.exp(sc - m)
            lsum = jnp.sum(p, axis=-1, keepdims=True)
            out = jnp.dot(p.astype(BF16), vals, preferred_element_type=F32) / lsum
            o_ref[rows, cols] = out.astype(o_ref.dtype)
            l_ref[rows, hh * LSE_LANES:(hh + 1) * LSE_LANES] = jnp.broadcast_to(m + jnp.log(lsum), (w, LSE_LANES))


def _dswa_group(slopes, qg, kg, vg, *, group):
    window, dilation = DSWA_GROUPS[group]
    assert window // dilation == DSWA_W
    batch, _, sd, gw = qg.shape
    tq = min(DSWA_MAX_Q, sd)
    nb = tq // DSWA_W
    cur = pl.BlockSpec((None, None, tq, gw), lambda b, r, i: (b, r, i, 0))
    prev = pl.BlockSpec((None, None, DSWA_W, gw), lambda b, r, i: (b, r, jnp.maximum(i * nb - 1, 0), 0))
    lse = pl.BlockSpec((None, None, tq, LANE), lambda b, r, i: (b, r, i, 0))
    return pl.pallas_call(
        functools.partial(_dswa_kernel, dilation=dilation, group=group, nb=nb),
        grid=(batch, dilation, sd // tq),
        in_specs=[pl.BlockSpec(memory_space=pltpu.SMEM), cur, cur, prev, cur, prev],
        out_specs=[cur, lse],
        out_shape=[jax.ShapeDtypeStruct(qg.shape, BF16), jax.ShapeDtypeStruct(qg.shape[:3] + (LANE,), F32)],
        compiler_params=_cparams(("parallel", "parallel", "arbitrary"), VMEM_SMALL),
        name="dswa_g%d" % group,
    )(slopes, qg, kg, kg, vg, vg)


def _dswa_combine_kernel(*refs, tm):
    ng = len(DSWA_GROUPS)
    in_refs, y_ref, scr = refs[:2 * ng], refs[2 * ng], refs[2 * ng + 1]
    nslot = DSWA_HEADS_PER_GROUP + 1

    def token_order(slot, dil, piece):
        if dil == 1:
            return piece(0)
        for r in range(dil):
            scr[slot, pl.ds(r, tm // dil, stride=dil), :] = piece(r)
        return scr[slot]

    outs, lses = [], []
    for gi, (_, dil) in enumerate(DSWA_GROUPS):
        o_src, l_src = in_refs[2 * gi], in_refs[2 * gi + 1]
        heads = []
        for hh in range(DSWA_HEADS_PER_GROUP):
            cols = slice(hh * DSWA_HEAD_DIM, (hh + 1) * DSWA_HEAD_DIM)
            heads.append(token_order(gi * nslot + hh, dil, lambda r, cols=cols: o_src[r, :, cols].astype(F32)))
        outs.append(jnp.concatenate(heads, axis=1))
        lse = token_order(gi * nslot + DSWA_HEADS_PER_GROUP, dil, lambda r: l_src[r])
        bands = [lse[:, hh * LSE_LANES:(hh + 1) * LSE_LANES] for hh in range(DSWA_HEADS_PER_GROUP)]
        lses.append(jnp.concatenate([b for b in bands for _ in range(DSWA_HEAD_DIM // LSE_LANES)], axis=1))
    mx = functools.reduce(jnp.maximum, lses)
    es = [jnp.exp(l - mx) for l in lses]
    den = functools.reduce(lambda a, b: a + b, es)
    y = functools.reduce(lambda a, b: a + b, [(e / den) * o for e, o in zip(es, outs)])
    y_ref[...] = y.astype(y_ref.dtype)


def _dswa_combine(outs_lses, *, batch, seq, tm):
    gw = DSWA_HEADS_PER_GROUP * DSWA_HEAD_DIM
    nblk = seq // tm
    in_specs = []
    for _, dil in DSWA_GROUPS:
        in_specs += [pl.BlockSpec((None, dil, tm // dil, gw), lambda b, i: (b, 0, i, 0)),
                     pl.BlockSpec((None, dil, tm // dil, LANE), lambda b, i: (b, 0, i, 0))]
    return pl.pallas_call(
        functools.partial(_dswa_combine_kernel, tm=tm),
        grid=(batch, nblk),
        in_specs=in_specs,
        out_specs=pl.BlockSpec((tm, gw), lambda b, i: (b * nblk + i, 0)),
        out_shape=jax.ShapeDtypeStruct((batch * seq, gw), BF16),
        scratch_shapes=[pltpu.VMEM((len(DSWA_GROUPS) * (DSWA_HEADS_PER_GROUP + 1), tm, LANE), F32)],
        compiler_params=_cparams(("parallel", "parallel"), VMEM_SMALL),
        name="dswa_combine",
    )(*outs_lses)


def _merge_kernel(x_ref, g_ref, ya_ref, yb_ref, yc_ref, wga_ref, wgb_ref, wgc_ref, wa_ref, wb_ref, wc_ref, wo_ref,
                  out_ref, h_scr):
    dot = lambda a, b: jnp.dot(a, b, preferred_element_type=F32)

    def chunk(h):
        inter = (jax.nn.sigmoid(dot(h, wga_ref[...])) * dot(ya_ref[...], wa_ref[...])
                 + jax.nn.sigmoid(dot(h, wgb_ref[...])) * dot(yb_ref[...], wb_ref[...])
                 + jax.nn.sigmoid(dot(h, wgc_ref[...])) * dot(yc_ref[...], wc_ref[...]))
        return dot(inter.astype(BF16), wo_ref[...])

    @pl.when(pl.program_id(1) == 0)
    def _():
        xf = x_ref[...]
        h = (xf * _inv_rms(xf, xf.shape[-1]) * g_ref[...]).astype(BF16)
        h_scr[...] = h
        out_ref[...] = xf + chunk(h)

    @pl.when(pl.program_id(1) > 0)
    def _():
        out_ref[...] += chunk(h_scr[...])


def _merge(x, g, ya, yb, yc, wga, wgb, wgc, wa, wb, wc, wo, layer, *, tm, tn):
    t, d = x.shape
    row = lambda i, j: (i, 0)
    colw = lambda i, j: (layer, 0, j)
    return pl.pallas_call(
        _merge_kernel,
        grid=(t // tm, d // tn),
        in_specs=[pl.BlockSpec((tm, d), row), _layer_spec(g, layer),
                  pl.BlockSpec((tm, ya.shape[1]), row), pl.BlockSpec((tm, yb.shape[1]), row),
                  pl.BlockSpec((tm, yc.shape[1]), row)]
                 + [pl.BlockSpec((None, w.shape[1], tn), colw) for w in (wga, wgb, wgc, wa, wb, wc)]
                 + [pl.BlockSpec((None, tn, d), lambda i, j: (layer, j, 0))],
        out_specs=pl.BlockSpec((tm, d), row),
        out_shape=jax.ShapeDtypeStruct((t, d), F32),
        scratch_shapes=[pltpu.VMEM((tm, d), BF16)],
        compiler_params=_cparams(("parallel", "arbitrary"), VMEM_RESIDENT),
        name="merge",
    )(x, g, ya, yb, yc, wga, wgb, wgc, wa, wb, wc, wo)


def _pad_last(w, n):
    return jnp.pad(w, [(0, 0)] * (w.ndim - 1) + [(0, n - w.shape[-1])])


def _rot_half_cols(w):
    half = w.shape[-1] // 2
    return jnp.concatenate([-w[..., half:], w[..., :half]], axis=-1)


def _swap_half(g):
    half = g.shape[-1] // 2
    return jnp.concatenate([g[..., half:], g[..., :half]], axis=-1)


def _gain3(g):
    rope = g[:, MLA_NOPE:]
    return jnp.stack([g[:, :MLA_NOPE], _pad_last(rope, LANE), _pad_last(_swap_half(rope), LANE)], axis=1)


def _row(g):
    return g[:, None, :]


def kernel(x, positions, ffn1_norm, ffn1_w_gate, ffn1_w_up, ffn1_w_down, mix_norm, w_in, mla_cq_norm, mla_ckv_norm, mla_w_uq, mla_w_ukv, mla_q_norm, mla_k_norm, gla_w_gate2, gla_b_gate2, gla_o_norm, dswa_q_norm, dswa_k_norm, w_branch_a, w_branch_b, w_branch_c, w_out, ffn2_norm, ffn2_w_gate, ffn2_w_up, ffn2_w_down):
    batch, seq, d = x.shape
    depth = w_in.shape[0]
    d_ff = ffn1_w_gate.shape[-1]
    t = batch * seq
    tm, tm_ffn, tf = TOKEN_BLOCK, FFN_TOKEN_BLOCK, FFN_CHUNK
    ff_pad = -(-d_ff // tf) * tf
    assert seq % max(wd for wd, _ in DSWA_GROUPS) == 0 and seq % tm == 0 and t % tm_ffn == 0

    xf = x.reshape(t, d)
    half = MLA_ROPE // 2
    inv_freq = ROPE_THETA ** (-jnp.arange(half, dtype=F32) / half)
    inv_pad = jnp.pad(jnp.concatenate([inv_freq, inv_freq]), (0, LANE - MLA_ROPE)).reshape(1, LANE)
    cos, sin = _rope_tables(positions.reshape(t, 1), inv_pad, tm=tm)
    slopes = 2.0 ** (-ALIBI_MAX_EXP * jnp.arange(1, DSWA_HEADS + 1, dtype=F32) / DSWA_HEADS)

    o_cq, o_ckv, o_kr = 0, MLA_Q_RANK, MLA_Q_RANK + MLA_KV_RANK
    o_gla = o_kr + MLA_ROPE
    n_qkv_b = GLA_HEADS * (2 * GLA_DK + GLA_DV)
    o_glr = o_gla + n_qkv_b
    o_rb = o_glr + GLA_GATE_RANK
    o_c = o_rb + GLA_HEADS * GLA_DV
    o_ga = o_c + 3 * DSWA_HEADS * DSWA_HEAD_DIM
    o_gb, o_gc = o_ga + d, o_ga + 2 * d

    bf = lambda w: w.astype(BF16)

    def ffn_weights(wg, wu, wdn):
        zc = jnp.zeros((depth, d, ff_pad - d_ff), BF16)
        zr = jnp.zeros((depth, ff_pad - d_ff, d), BF16)
        return (jnp.concatenate([bf(wg), zc], axis=2), jnp.concatenate([bf(wu), zc], axis=2),
                jnp.concatenate([bf(wdn), zr], axis=1))

    ffn1 = (_row(ffn1_norm),) + ffn_weights(ffn1_w_gate, ffn1_w_up, ffn1_w_down)
    ffn2 = (_row(ffn2_norm),) + ffn_weights(ffn2_w_gate, ffn2_w_up, ffn2_w_down)
    gmix = _row(mix_norm)
    w_in = bf(w_in)
    w_kr = w_in[:, :, o_kr:o_kr + MLA_ROPE]
    w1 = bf(jnp.concatenate([w_in[:, :, o_cq:o_kr], _pad_last(w_kr, LANE), _pad_last(_rot_half_cols(w_kr), LANE)],
                            axis=-1))
    wq4 = mla_w_uq.reshape(depth, MLA_Q_RANK, MLA_HEADS, MLA_QK)
    wq_rope = wq4[..., MLA_NOPE:]
    wuq = jnp.concatenate([wq4[..., :MLA_NOPE], _pad_last(wq_rope, LANE), _pad_last(_rot_half_cols(wq_rope), LANE)],
                          axis=-1)
    wuq = bf(wuq.reshape(depth, MLA_Q_RANK, MLA_HEADS * 3 * LANE))
    wkv4 = mla_w_ukv.reshape(depth, MLA_KV_RANK, MLA_HEADS, MLA_NOPE + MLA_V)
    wukv = bf(jnp.concatenate([wkv4[..., :MLA_NOPE].reshape(depth, MLA_KV_RANK, -1),
                               wkv4[..., MLA_NOPE:].reshape(depth, MLA_KV_RANK, -1)], axis=-1))
    mla_params = (w1, _row(mla_cq_norm), _row(mla_ckv_norm), wuq, wukv, _gain3(mla_q_norm), _gain3(mla_k_norm))
    w_gla = bf(jnp.concatenate([w_in[:, :, o_gla:o_glr], _pad_last(w_in[:, :, o_glr:o_rb], LANE),
                                w_in[:, :, o_rb:o_c]], axis=-1))
    w2 = bf(jnp.pad(gla_w_gate2, ((0, 0), (0, LANE - GLA_GATE_RANK), (0, 0))))
    gla_params = (w_gla, w2, _row(gla_b_gate2))
    go = _row(gla_o_norm)
    dswa_params = (bf(w_in[:, :, o_c:o_ga]), _row(dswa_q_norm), _row(dswa_k_norm))
    merge_params = (bf(w_in[:, :, o_ga:o_gb]), bf(w_in[:, :, o_gb:o_gc]), bf(w_in[:, :, o_gc:o_gc + d]),
                    bf(w_branch_a), bf(w_branch_b), bf(w_branch_c), bf(w_out))

    for l in range(depth):
        xf = _ffn(xf, *ffn1, l, tm=tm_ffn, tf=tf)

        qa, ka, va = _mla_prep(xf, gmix, cos, sin, *mla_params, l, tm=tm)
        y_a = _flash(qa, ka, va, batch=batch, seq=seq, tq=FLASH_TILE)

        qb, kb, vb, la, sr = _gla_proj(xf, gmix, *gla_params, l, tm=tm)
        y_b = _gla(qb, kb, vb, la, sr, go, l, batch=batch, seq=seq, tb=GLA_TOKENS)

        qkv_c = _dswa_proj(xf, gmix, *dswa_params, l, batch=batch, seq=seq, tm=tm)
        outs_lses = []
        for gi in range(len(DSWA_GROUPS)):
            outs_lses += _dswa_group(slopes, *qkv_c[3 * gi:3 * gi + 3], group=gi)
        y_c = _dswa_combine(outs_lses, batch=batch, seq=seq, tm=tm)

        xf = _merge(xf, gmix, y_a, y_b, y_c, *merge_params, l, tm=tm, tn=MERGE_CHUNK)
        xf = _ffn(xf, *ffn2, l, tm=tm_ffn, tf=tf)
    return xf.reshape(batch, seq, d)
```

```python
import functools

import numpy as np
import jax
import jax.numpy as jnp
from jax import lax
from jax.experimental import pallas as pl
from jax.experimental.pallas import tpu as pltpu

F32 = jnp.float32
BF16 = jnp.bfloat16

MLA_HEADS = 6
MLA_Q_RANK = 512
MLA_KV_RANK = 256
MLA_NOPE = 128
MLA_ROPE = 64
MLA_V = 128
MLA_QK = MLA_NOPE + MLA_ROPE
MLA_QK_PAD = 256
GLA_HEADS = 4
GLA_DK = 64
GLA_DV = 128
GLA_GATE_RANK = 16
GLA_TAU = 16.0
GLA_CHUNK = 64
GLA_SUB = 8
GLA_LEVELS = (32, 16, 8)
DSWA_GROUPS = ((128, 1), (512, 4), (2048, 16))
DSWA_HEADS_PER_GROUP = 2
DSWA_HEADS = 6
DSWA_HEAD_DIM = 128
DSWA_W = 128
ROPE_THETA = 10000.0
ALIBI_MAX_EXP = 8.0
NORM_EPS = 1e-6
NEG_INF = -1e30
LOG2E = 1.4426950408889634

LANE = 128
LSE_LANES = LANE // DSWA_HEADS_PER_GROUP
BF16_ROWS = 16
MIB = 1024 * 1024

TOKEN_BLOCK = 512
FFN_TOKEN_BLOCK = 1024
FFN_CHUNK = 512
MERGE_CHUNK = 512
FLASH_TILE = 1024
GLA_TOKENS = 512
DSWA_MAX_Q = 1024
VMEM_FFN = 57
VMEM_RESIDENT = 52
VMEM_SCAN = 40
VMEM_SMALL = 32


def _cparams(sem, vmem_mib):
    return pltpu.CompilerParams(dimension_semantics=sem, vmem_limit_bytes=int(vmem_mib * MIB))


def _inv_rms(xf, n):
    return lax.rsqrt(jnp.sum(xf * xf, axis=-1, keepdims=True) / n + NORM_EPS)


def _const_spec(shape):
    nd = len(shape)
    return pl.BlockSpec(shape, lambda *_: (0,) * nd)


def _layer_spec(arr, layer):
    nd = arr.ndim - 1
    return pl.BlockSpec((None,) + arr.shape[1:], lambda *_: (layer,) + (0,) * nd)


def _ffn_kernel(x_ref, g_ref, wg_ref, wu_ref, wd_ref, o_ref, h_scr):
    def half_chunk(h):
        gate = jnp.dot(h, wg_ref[...], preferred_element_type=F32)
        up = jnp.dot(h, wu_ref[...], preferred_element_type=F32)
        inter = (gate * jax.nn.sigmoid(gate) * up).astype(BF16)
        return 0.5 * jnp.dot(inter, wd_ref[...], preferred_element_type=F32)

    @pl.when(pl.program_id(1) == 0)
    def _():
        xf = x_ref[...]
        h = (xf * _inv_rms(xf, xf.shape[-1]) * g_ref[...]).astype(BF16)
        h_scr[...] = h
        o_ref[...] = xf + half_chunk(h)

    @pl.when(pl.program_id(1) > 0)
    def _():
        o_ref[...] += half_chunk(h_scr[...])


def _ffn(x, g, wg, wu, wd, layer, *, tm, tf):
    t, d = x.shape
    fp = wg.shape[-1]
    return pl.pallas_call(
        _ffn_kernel,
        grid=(t // tm, fp // tf),
        in_specs=[
            pl.BlockSpec((tm, d), lambda i, j: (i, 0)),
            _layer_spec(g, layer),
            pl.BlockSpec((None, d, tf), lambda i, j: (layer, 0, j)),
            pl.BlockSpec((None, d, tf), lambda i, j: (layer, 0, j)),
            pl.BlockSpec((None, tf, d), lambda i, j: (layer, j, 0)),
        ],
        out_specs=pl.BlockSpec((tm, d), lambda i, j: (i, 0)),
        out_shape=jax.ShapeDtypeStruct((t, d), F32),
        scratch_shapes=[pltpu.VMEM((tm, d), BF16)],
        compiler_params=_cparams(("parallel", "arbitrary"), VMEM_FFN),
        name="ffn",
    )(x, g, wg, wu, wd)


def _rope_kernel(pos_ref, inv_ref, cos_ref, sin_ref):
    ang = pos_ref[...].astype(F32) * inv_ref[...]
    cos_ref[...] = jnp.cos(ang)
    sin_ref[...] = jnp.sin(ang)


def _rope_tables(pos_col, inv_pad, *, tm):
    t = pos_col.shape[0]
    return pl.pallas_call(
        _rope_kernel,
        grid=(t // tm,),
        in_specs=[pl.BlockSpec((tm, 1), lambda i: (i, 0)), _const_spec((1, LANE))],
        out_specs=[pl.BlockSpec((tm, LANE), lambda i: (i, 0))] * 2,
        out_shape=[jax.ShapeDtypeStruct((t, LANE), F32)] * 2,
        compiler_params=_cparams(("parallel",), VMEM_SMALL),
        name="rope_tables",
    )(pos_col, inv_pad)


def _mla_prep_kernel(x_ref, g_ref, cos_ref, sin_ref, w1_ref, gcq_ref, gckv_ref, wuq_ref, wukv_ref,
                     gq_ref, gk_ref, q_ref, k_ref, v_ref):
    xf = x_ref[...]
    h = (xf * _inv_rms(xf, xf.shape[-1]) * g_ref[...]).astype(BF16)
    u = jnp.dot(h, w1_ref[...], preferred_element_type=F32)
    c_q = u[:, :MLA_Q_RANK]
    c_kv = u[:, MLA_Q_RANK:MLA_Q_RANK + MLA_KV_RANK]
    kr = u[:, MLA_Q_RANK + MLA_KV_RANK:MLA_Q_RANK + MLA_KV_RANK + LANE]
    krot = u[:, MLA_Q_RANK + MLA_KV_RANK + LANE:]
    c_q = (c_q * _inv_rms(c_q, MLA_Q_RANK) * gcq_ref[...]).astype(BF16)
    c_kv = (c_kv * _inv_rms(c_kv, MLA_KV_RANK) * gckv_ref[...]).astype(BF16)
    qall = jnp.dot(c_q, wuq_ref[...], preferred_element_type=F32)
    kv = jnp.dot(c_kv, wukv_ref[...], preferred_element_type=F32)
    cos = cos_ref[...]
    sin = sin_ref[...]
    scale = MLA_QK ** -0.5 * LOG2E
    gq_n, gq_r, gq_t = gq_ref[0:1, :], gq_ref[1:2, :], gq_ref[2:3, :]
    gk_n, gk_r, gk_t = gk_ref[0:1, :], gk_ref[1:2, :], gk_ref[2:3, :]
    k_roped = kr * gk_r * cos + krot * gk_t * sin
    kr_ss = jnp.sum(kr * kr, axis=-1, keepdims=True)
    for hh in range(MLA_HEADS):
        base = hh * 3 * LANE
        nope = qall[:, base:base + LANE]
        rope = qall[:, base + LANE:base + 2 * LANE]
        rot = qall[:, base + 2 * LANE:base + 3 * LANE]
        ss = jnp.sum(nope * nope, axis=-1, keepdims=True) + jnp.sum(rope * rope, axis=-1, keepdims=True)
        r = lax.rsqrt(ss / MLA_QK + NORM_EPS) * scale
        q_ref[hh, :, 0:LANE] = (nope * r * gq_n).astype(BF16)
        q_ref[hh, :, LANE:2 * LANE] = ((rope * gq_r * cos + rot * gq_t * sin) * r).astype(BF16)
        kn = kv[:, hh * LANE:(hh + 1) * LANE]
        rk = lax.rsqrt((jnp.sum(kn * kn, axis=-1, keepdims=True) + kr_ss) / MLA_QK + NORM_EPS)
        k_ref[hh, :, 0:LANE] = (kn * rk * gk_n).astype(BF16)
        k_ref[hh, :, LANE:2 * LANE] = (k_roped * rk).astype(BF16)
        v_ref[hh] = kv[:, (MLA_HEADS + hh) * LANE:(MLA_HEADS + hh + 1) * LANE].astype(BF16)


def _mla_prep(x, g, cos, sin, w1, gcq, gckv, wuq, wukv, gq3, gk3, layer, *, tm):
    t, d = x.shape
    row = lambda i: (i, 0)
    hrow = lambda i: (0, i, 0)
    return pl.pallas_call(
        _mla_prep_kernel,
        grid=(t // tm,),
        in_specs=[pl.BlockSpec((tm, d), row), _layer_spec(g, layer),
                  pl.BlockSpec((tm, LANE), row), pl.BlockSpec((tm, LANE), row)]
                 + [_layer_spec(p, layer) for p in (w1, gcq, gckv, wuq, wukv, gq3, gk3)],
        out_specs=[
            pl.BlockSpec((MLA_HEADS, tm, MLA_QK_PAD), hrow),
            pl.BlockSpec((MLA_HEADS, tm, MLA_QK_PAD), hrow),
            pl.BlockSpec((MLA_HEADS, tm, MLA_V), hrow),
        ],
        out_shape=[
            jax.ShapeDtypeStruct((MLA_HEADS, t, MLA_QK_PAD), BF16),
            jax.ShapeDtypeStruct((MLA_HEADS, t, MLA_QK_PAD), BF16),
            jax.ShapeDtypeStruct((MLA_HEADS, t, MLA_V), BF16),
        ],
        compiler_params=_cparams(("parallel",), VMEM_RESIDENT),
        name="mla_prep",
    )(x, g, cos, sin, w1, gcq, gckv, wuq, wukv, gq3, gk3)


def _flash_kernel(q_ref, k_ref, v_ref, o_ref, sa_scr, sb_scr, m_scr, acc_scr, *, tq):
    qi = pl.program_id(2)
    m_scr[...] = jnp.full_like(m_scr, NEG_INF)
    acc_scr[...] = jnp.zeros_like(acc_scr)
    q = q_ref[...]
    ones = jnp.ones((tq, MLA_V), BF16)

    def scores(tile, s_scr):
        r0 = pl.multiple_of(tile * tq, tq)
        s_scr[...] = lax.dot_general(q, k_ref[pl.ds(r0, tq), :], (((1,), (1,)), ((), ())),
                                     preferred_element_type=F32)

    def accumulate(tile, s_scr, diagonal):
        r0 = pl.multiple_of(tile * tq, tq)
        sc = s_scr[...]
        if diagonal:
            row = lax.broadcasted_iota(jnp.int32, sc.shape, 0)
            col = lax.broadcasted_iota(jnp.int32, sc.shape, 1)
            sc = jnp.where(col <= row, sc, NEG_INF)
        m_prev = m_scr[...]
        m_new = jnp.maximum(m_prev, jnp.max(sc, axis=-1, keepdims=True))
        alpha = jnp.exp2(m_prev - m_new)
        p = jnp.exp2(sc - jnp.tile(m_new, (1, tq // LANE)))
        v1 = jnp.concatenate([v_ref[pl.ds(r0, tq), :], ones], axis=1)
        pv = jnp.dot(p.astype(BF16), v1, preferred_element_type=F32)
        acc_scr[...] = jnp.tile(alpha, (1, 2)) * acc_scr[...] + pv
        m_scr[...] = m_new

    scores(0, sa_scr)

    def body(j, carry):
        scores(2 * j + 1, sb_scr)
        accumulate(2 * j, sa_scr, False)
        scores(2 * j + 2, sa_scr)
        accumulate(2 * j + 1, sb_scr, False)
        return carry

    def body_x2(jj, carry):
        body(2 * jj, carry)
        return body(2 * jj + 1, carry)

    npair = qi // 2
    lax.fori_loop(0, npair // 2, body_x2, 0)
    lax.fori_loop(2 * (npair // 2), npair, body, 0)

    @pl.when(qi % 2 == 0)
    def _():
        accumulate(qi, sa_scr, True)

    @pl.when(qi % 2 == 1)
    def _():
        scores(qi, sb_scr)
        accumulate(qi - 1, sa_scr, False)
        accumulate(qi, sb_scr, True)

    acc = acc_scr[...]
    o_ref[...] = (acc[:, 0:MLA_V] / acc[:, MLA_V:]).astype(o_ref.dtype)


def _flash(q, k, v, *, batch, seq, tq):
    assert seq % tq == 0
    nq = seq // tq
    t = batch * seq
    return pl.pallas_call(
        functools.partial(_flash_kernel, tq=tq),
        grid=(batch, MLA_HEADS, nq),
        in_specs=[
            pl.BlockSpec((None, tq, MLA_QK_PAD), lambda b, h, i: (h, b * nq + i, 0)),
            pl.BlockSpec((None, seq, MLA_QK_PAD), lambda b, h, i: (h, b, 0)),
            pl.BlockSpec((None, seq, MLA_V), lambda b, h, i: (h, b, 0)),
        ],
        out_specs=pl.BlockSpec((tq, MLA_V), lambda b, h, i: (b * nq + i, h)),
        out_shape=jax.ShapeDtypeStruct((t, MLA_HEADS * MLA_V), BF16),
        scratch_shapes=[pltpu.VMEM((tq, tq), F32), pltpu.VMEM((tq, tq), F32),
                        pltpu.VMEM((tq, MLA_V), F32), pltpu.VMEM((tq, 2 * MLA_V), F32)],
        compiler_params=_cparams(("parallel", "parallel", "arbitrary"), VMEM_RESIDENT),
        name="mla_flash",
    )(q, k, v)


def _gla_proj_kernel(x_ref, g_ref, w_ref, w2_ref, b2_ref, q_ref, k_ref, v_ref, la_ref, sr_ref):
    xf = x_ref[...]
    h = (xf * _inv_rms(xf, xf.shape[-1]) * g_ref[...]).astype(BF16)
    u = jnp.dot(h, w_ref[...], preferred_element_type=F32)
    nk = GLA_HEADS * GLA_DK
    nv = GLA_HEADS * GLA_DV
    q_ref[...] = u[:, 0:nk] * (GLA_DK ** -0.5)
    k_ref[...] = u[:, nk:2 * nk]
    v_ref[...] = u[:, 2 * nk:2 * nk + nv]
    gate_lr = u[:, 2 * nk + nv:2 * nk + nv + LANE].astype(BF16)
    z = jnp.dot(gate_lr, w2_ref[...], preferred_element_type=F32) + b2_ref[...]
    la_ref[...] = -(jnp.maximum(-z, 0.0) + jnp.log1p(jnp.exp(-jnp.abs(z)))) / GLA_TAU
    r = u[:, 2 * nk + nv + LANE:]
    sr_ref[...] = r * jax.nn.sigmoid(r)


def _gla_proj(x, g, w, w2, b2, layer, *, tm):
    t, d = x.shape
    nk = GLA_HEADS * GLA_DK
    nv = GLA_HEADS * GLA_DV
    row = lambda i: (i, 0)
    return pl.pallas_call(
        _gla_proj_kernel,
        grid=(t // tm,),
        in_specs=[pl.BlockSpec((tm, d), row)] + [_layer_spec(p, layer) for p in (g, w, w2, b2)],
        out_specs=[pl.BlockSpec((tm, nk), row), pl.BlockSpec((tm, nk), row), pl.BlockSpec((tm, nv), row),
                   pl.BlockSpec((tm, nk), row), pl.BlockSpec((tm, nv), row)],
        out_shape=[jax.ShapeDtypeStruct((t, nk), F32), jax.ShapeDtypeStruct((t, nk), F32),
                   jax.ShapeDtypeStruct((t, nv), F32), jax.ShapeDtypeStruct((t, nk), F32),
                   jax.ShapeDtypeStruct((t, nv), F32)],
        compiler_params=_cparams(("parallel",), VMEM_RESIDENT),
        name="gla_proj",
    )(x, g, w, w2, b2)


def _gla_constants():
    c = GLA_CHUNK
    idx = np.arange(c)
    lmask = []
    for lvl in GLA_LEVELS:
        blk = idx // lvl
        m = ((blk[:, None] % 2) == 1) & (blk[None, :] == blk[:, None] - 1)
        lmask.append(np.tile(m.astype(np.float32), (GLA_HEADS, 1)))
    lmask = np.stack(lmask, axis=0)
    hk = np.arange(GLA_HEADS * GLA_DK) // GLA_DK
    hv = np.arange(GLA_HEADS * GLA_DV) // GLA_DV
    hrow = np.arange(GLA_HEADS * c) // c
    headmask = (hrow[:, None] == hk[None, :]).astype(np.float32)
    e_ind = (hk[:, None] == hv[None, :]).astype(np.float32)
    return lmask, headmask, e_ind


def _gla_kernel(q_ref, k_ref, v_ref, la_ref, sr_ref, go_ref, lmask_ref, hmask_ref, eind_ref, eindt_ref,
                y_ref, st_scr, *, chunks):
    c = GLA_CHUNK
    nk = GLA_HEADS * GLA_DK
    nv = GLA_HEADS * GLA_DV
    nsub = c // GLA_SUB

    @pl.when(pl.program_id(1) == 0)
    def _():
        st_scr[...] = jnp.zeros_like(st_scr)

    def chunk(ci, carry):
        r0 = pl.multiple_of(ci * c, c)
        q = q_ref[pl.ds(r0, c), :]
        k = k_ref[pl.ds(r0, c), :]
        v = v_ref[pl.ds(r0, c), :]
        a = la_ref[pl.ds(r0, c), :]
        row = lax.broadcasted_iota(jnp.int32, (c, nk), 0)
        b = a
        shift = 1
        while shift < c:
            b = b + jnp.where(row >= shift, pltpu.roll(b, shift, axis=0), 0.0)
            shift *= 2
        parts = [b]
        for lvl in GLA_LEVELS:
            ref = b[lvl - 1:lvl]
            for start in range(2 * lvl, c, 2 * lvl):
                ref = jnp.where(row >= start, b[start + lvl - 1:start + lvl], ref)
            parts.append(jnp.where((row & lvl) != 0, b - ref, ref - b))
        parts.append(b[c - 1:c] - b)
        f = jnp.exp(jnp.concatenate(parts, axis=0))
        v_bf = v.astype(BF16)

        amat = jnp.zeros((GLA_HEADS * c, c), F32)
        hmask = hmask_ref[...]
        for li in range(len(GLA_LEVELS)):
            fl = f[(li + 1) * c:(li + 2) * c]
            qt = q * fl
            kt = (k * fl).astype(BF16)
            qs = (jnp.concatenate([qt] * GLA_HEADS, axis=0) * hmask).astype(BF16)
            sc = lax.dot_general(qs, kt, (((1,), (1,)), ((), ())), preferred_element_type=F32)
            amat = amat + sc * lmask_ref[li]
        a_bf = amat.astype(BF16)
        o = jnp.concatenate(
            [jnp.dot(a_bf[hh * c:(hh + 1) * c], v_bf[:, hh * GLA_DV:(hh + 1) * GLA_DV], preferred_element_type=F32)
             for hh in range(GLA_HEADS)], axis=1)

        qb = (q * f[0:c]).astype(BF16)
        st = st_scr[...]
        o = o + lax.dot_general(qb, st.astype(BF16), (((1,), (1,)), ((), ())), preferred_element_type=F32)

        q3 = q.reshape(nsub, GLA_SUB, nk)
        k3 = k.reshape(nsub, GLA_SUB, nk)
        b3 = b.reshape(nsub, GLA_SUB, nk)
        v3 = v.reshape(nsub, GLA_SUB, nv)
        tt = lax.broadcasted_iota(jnp.int32, (nsub, GLA_SUB, nk), 1)
        od = jnp.zeros((nsub, GLA_SUB, nv), F32)
        eind = eind_ref[...]
        for s in range(GLA_SUB):
            w = jnp.exp(jnp.minimum(b3 - b3[:, s:s + 1, :], 0.0))
            x = jnp.where(tt >= s, q3 * k3[:, s:s + 1, :] * w, 0.0)
            rr = jnp.dot(x.reshape(c, nk).astype(BF16), eind, preferred_element_type=F32)
            od = od + rr.reshape(nsub, GLA_SUB, nv) * v3[:, s:s + 1, :]
        o = o + od.reshape(c, nv)

        kend = (k * f[(len(GLA_LEVELS) + 1) * c:(len(GLA_LEVELS) + 2) * c]).astype(BF16)
        upd = lax.dot_general(v_bf, kend, (((0,), (0,)), ((), ())), preferred_element_type=F32)
        st_scr[...] = st * f[c - 1:c] + upd * eindt_ref[...]

        go = go_ref[...]
        sr = sr_ref[pl.ds(r0, c), :]
        ys = []
        for hh in range(GLA_HEADS):
            oh = o[:, hh * GLA_DV:(hh + 1) * GLA_DV]
            ys.append(oh * _inv_rms(oh, GLA_DV) * go)
        y_ref[pl.ds(r0, c), :] = (jnp.concatenate(ys, axis=1) * sr).astype(y_ref.dtype)
        return carry

    lax.fori_loop(0, chunks, chunk, 0, unroll=2)


def _gla(q, k, v, la, sr, go, layer, *, batch, seq, tb):
    nk = GLA_HEADS * GLA_DK
    nv = GLA_HEADS * GLA_DV
    nb = seq // tb
    lmask, headmask, e_ind = _gla_constants()
    consts = [jnp.asarray(lmask, F32), jnp.asarray(headmask, F32),
              jnp.asarray(e_ind, BF16), jnp.asarray(e_ind.T, F32)]
    row = lambda b, i: (b * nb + i, 0)
    return pl.pallas_call(
        functools.partial(_gla_kernel, chunks=tb // GLA_CHUNK),
        grid=(batch, nb),
        in_specs=[pl.BlockSpec((tb, nk), row), pl.BlockSpec((tb, nk), row), pl.BlockSpec((tb, nv), row),
                  pl.BlockSpec((tb, nk), row), pl.BlockSpec((tb, nv), row), _layer_spec(go, layer)]
                 + [_const_spec(cst.shape) for cst in consts],
        out_specs=pl.BlockSpec((tb, nv), row),
        out_shape=jax.ShapeDtypeStruct((batch * seq, nv), BF16),
        scratch_shapes=[pltpu.VMEM((nv, nk), F32)],
        compiler_params=_cparams(("parallel", "arbitrary"), VMEM_SCAN),
        name="gla_scan",
    )(q, k, v, la, sr, go, *consts)


def _dswa_proj_kernel(x_ref, g_ref, w_ref, gq_ref, gk_ref, *refs, tm):
    ng = len(DSWA_GROUPS)
    out_refs = refs[:3 * ng]
    q_scr, k_scr, v_scr = refs[3 * ng:]
    xf = x_ref[...]
    h = (xf * _inv_rms(xf, xf.shape[-1]) * g_ref[...]).astype(BF16)
    u = jnp.dot(h, w_ref[...], preferred_element_type=F32)
    n = DSWA_HEADS * DSWA_HEAD_DIM
    gq = gq_ref[...]
    gk = gk_ref[...]
    scale = DSWA_HEAD_DIM ** -0.5
    for hh in range(DSWA_HEADS):
        sl = slice(hh * DSWA_HEAD_DIM, (hh + 1) * DSWA_HEAD_DIM)
        qh = u[:, sl]
        q_scr[hh] = qh * _inv_rms(qh, DSWA_HEAD_DIM) * gq * scale
        kh = u[:, n + hh * DSWA_HEAD_DIM:n + (hh + 1) * DSWA_HEAD_DIM]
        k_scr[hh] = kh * _inv_rms(kh, DSWA_HEAD_DIM) * gk
        v_scr[hh] = u[:, 2 * n + hh * DSWA_HEAD_DIM:2 * n + (hh + 1) * DSWA_HEAD_DIM]
    for gi, (_, dil) in enumerate(DSWA_GROUPS):
        for src, dst in zip((q_scr, k_scr, v_scr), out_refs[3 * gi:3 * gi + 3]):
            for hh in range(DSWA_HEADS_PER_GROUP):
                head = gi * DSWA_HEADS_PER_GROUP + hh
                cols = slice(hh * DSWA_HEAD_DIM, (hh + 1) * DSWA_HEAD_DIM)
                for r in range(dil):
                    rows = pl.ds(r, tm // dil, stride=dil) if dil > 1 else slice(None)
                    dst[r, :, cols] = src[head, rows, :].astype(BF16)


def _dswa_proj(x, g, w, gq, gk, layer, *, batch, seq, tm):
    t, d = x.shape
    n = DSWA_HEADS * DSWA_HEAD_DIM
    gw = DSWA_HEADS_PER_GROUP * DSWA_HEAD_DIM
    nblk = seq // tm
    out_specs, out_shape = [], []
    for _, dil in DSWA_GROUPS:
        assert tm % (BF16_ROWS * dil) == 0
        out_specs += [pl.BlockSpec((None, dil, tm // dil, gw), lambda b, i: (b, 0, i, 0))] * 3
        out_shape += [jax.ShapeDtypeStruct((batch, dil, seq // dil, gw), BF16)] * 3
    return pl.pallas_call(
        functools.partial(_dswa_proj_kernel, tm=tm),
        grid=(batch, nblk),
        in_specs=[pl.BlockSpec((tm, d), lambda b, i: (b * nblk + i, 0))]
                 + [_layer_spec(p, layer) for p in (g, w, gq, gk)],
        out_specs=out_specs,
        out_shape=out_shape,
        scratch_shapes=[pltpu.VMEM((DSWA_HEADS, tm, DSWA_HEAD_DIM), F32)] * 3,
        compiler_params=_cparams(("parallel", "parallel"), VMEM_RESIDENT),
        name="dswa_proj",
    )(x, g, w, gq, gk)


def _dswa_kernel(slope_ref, q_ref, kc_ref, kp_ref, vc_ref, vp_ref, o_ref, l_ref, *, dilation, group, nb):
    w = DSWA_W
    ib = pl.program_id(2)
    i = lax.broadcasted_iota(jnp.int32, (w, 2 * w), 0)
    j = lax.broadcasted_iota(jnp.int32, (w, 2 * w), 1)
    steps = w + i - j
    dist = (steps * dilation).astype(F32)
    in_window = (steps >= 0) & (steps <= w)
    for hh in range(DSWA_HEADS_PER_GROUP):
        cols = slice(hh * DSWA_HEAD_DIM, (hh + 1) * DSWA_HEAD_DIM)
        bias_all = jnp.where(in_window, -slope_ref[group * DSWA_HEADS_PER_GROUP + hh] * dist, NEG_INF)
        bias_first = jnp.where(j >= w, bias_all, NEG_INF)
        for jb in range(nb):
            rows = slice(jb * w, (jb + 1) * w)
            if jb == 0:
                kprev, vprev = kp_ref[:, cols], vp_ref[:, cols]
                bias_j = jnp.where(ib > 0, bias_all, bias_first)
            else:
                prows = slice((jb - 1) * w, jb * w)
                kprev, vprev = kc_ref[prows, cols], vc_ref[prows, cols]
                bias_j = bias_all
            keys = jnp.concatenate([kprev, kc_ref[rows, cols]], axis=0)
            vals = jnp.concatenate([vprev, vc_ref[rows, cols]], axis=0)
            sc = lax.dot_general(q_ref[rows, cols], keys, (((1,), (1,)), ((), ())), preferred_element_type=F32)
            sc = jnp.where(bias_j > 0.5 * NEG_INF, sc + bias_j, NEG_INF)
            m = jnp.max(sc, axis=-1, keepdims=True)
            p = jnp.exp(sc - m)
            lsum = jnp.sum(p, axis=-1, keepdims=True)
            out = jnp.dot(p.astype(BF16), vals, preferred_element_type=F32) / lsum
            o_ref[rows, cols] = out.astype(o_ref.dtype)
            l_ref[rows, hh * LSE_LANES:(hh + 1) * LSE_LANES] = jnp.broadcast_to(m + jnp.log(lsum), (w, LSE_LANES))


def _dswa_group(slopes, qg, kg, vg, *, group):
    window, dilation = DSWA_GROUPS[group]
    assert window // dilation == DSWA_W
    batch, _, sd, gw = qg.shape
    tq = min(DSWA_MAX_Q, sd)
    nb = tq // DSWA_W
    cur = pl.BlockSpec((None, None, tq, gw), lambda b, r, i: (b, r, i, 0))
    prev = pl.BlockSpec((None, None, DSWA_W, gw), lambda b, r, i: (b, r, jnp.maximum(i * nb - 1, 0), 0))
    lse = pl.BlockSpec((None, None, tq, LANE), lambda b, r, i: (b, r, i, 0))
    return pl.pallas_call(
        functools.partial(_dswa_kernel, dilation=dilation, group=group, nb=nb),
        grid=(batch, dilation, sd // tq),
        in_specs=[pl.BlockSpec(memory_space=pltpu.SMEM), cur, cur, prev, cur, prev],
        out_specs=[cur, lse],
        out_shape=[jax.ShapeDtypeStruct(qg.shape, BF16), jax.ShapeDtypeStruct(qg.shape[:3] + (LANE,), F32)],
        compiler_params=_cparams(("parallel", "parallel", "arbitrary"), VMEM_SMALL),
        name="dswa_g%d" % group,
    )(slopes, qg, kg, kg, vg, vg)


def _dswa_combine_kernel(*refs, tm):
    ng = len(DSWA_GROUPS)
    in_refs, y_ref, scr = refs[:2 * ng], refs[2 * ng], refs[2 * ng + 1]
    nslot = DSWA_HEADS_PER_GROUP + 1

    def token_order(slot, dil, piece):
        if dil == 1:
            return piece(0)
        for r in range(dil):
            scr[slot, pl.ds(r, tm // dil, stride=dil), :] = piece(r)
        return scr[slot]

    outs, lses = [], []
    for gi, (_, dil) in enumerate(DSWA_GROUPS):
        o_src, l_src = in_refs[2 * gi], in_refs[2 * gi + 1]
        heads = []
        for hh in range(DSWA_HEADS_PER_GROUP):
            cols = slice(hh * DSWA_HEAD_DIM, (hh + 1) * DSWA_HEAD_DIM)
            heads.append(token_order(gi * nslot + hh, dil, lambda r, cols=cols: o_src[r, :, cols].astype(F32)))
        outs.append(jnp.concatenate(heads, axis=1))
        lse = token_order(gi * nslot + DSWA_HEADS_PER_GROUP, dil, lambda r: l_src[r])
        bands = [lse[:, hh * LSE_LANES:(hh + 1) * LSE_LANES] for hh in range(DSWA_HEADS_PER_GROUP)]
        lses.append(jnp.concatenate([b for b in bands for _ in range(DSWA_HEAD_DIM // LSE_LANES)], axis=1))
    mx = functools.reduce(jnp.maximum, lses)
    es = [jnp.exp(l - mx) for l in lses]
    den = functools.reduce(lambda a, b: a + b, es)
    y = functools.reduce(lambda a, b: a + b, [(e / den) * o for e, o in zip(es, outs)])
    y_ref[...] = y.astype(y_ref.dtype)


def _dswa_combine(outs_lses, *, batch, seq, tm):
    gw = DSWA_HEADS_PER_GROUP * DSWA_HEAD_DIM
    nblk = seq // tm
    in_specs = []
    for _, dil in DSWA_GROUPS:
        in_specs += [pl.BlockSpec((None, dil, tm // dil, gw), lambda b, i: (b, 0, i, 0)),
                     pl.BlockSpec((None, dil, tm // dil, LANE), lambda b, i: (b, 0, i, 0))]
    return pl.pallas_call(
        functools.partial(_dswa_combine_kernel, tm=tm),
        grid=(batch, nblk),
        in_specs=in_specs,
        out_specs=pl.BlockSpec((tm, gw), lambda b, i: (b * nblk + i, 0)),
        out_shape=jax.ShapeDtypeStruct((batch * seq, gw), BF16),
        scratch_shapes=[pltpu.VMEM((len(DSWA_GROUPS) * (DSWA_HEADS_PER_GROUP + 1), tm, LANE), F32)],
        compiler_params=_cparams(("parallel", "parallel"), VMEM_SMALL),
        name="dswa_combine",
    )(*outs_lses)


def _merge_kernel(x_ref, g_ref, ya_ref, yb_ref, yc_ref, wga_ref, wgb_ref, wgc_ref, wa_ref, wb_ref, wc_ref, wo_ref,
                  out_ref, h_scr):
    dot = lambda a, b: jnp.dot(a, b, preferred_element_type=F32)

    def chunk(h):
        inter = (jax.nn.sigmoid(dot(h, wga_ref[...])) * dot(ya_ref[...], wa_ref[...])
                 + jax.nn.sigmoid(dot(h, wgb_ref[...])) * dot(yb_ref[...], wb_ref[...])
                 + jax.nn.sigmoid(dot(h, wgc_ref[...])) * dot(yc_ref[...], wc_ref[...]))
        return dot(inter.astype(BF16), wo_ref[...])

    @pl.when(pl.program_id(1) == 0)
    def _():
        xf = x_ref[...]
        h = (xf * _inv_rms(xf, xf.shape[-1]) * g_ref[...]).astype(BF16)
        h_scr[...] = h
        out_ref[...] = xf + chunk(h)

    @pl.when(pl.program_id(1) > 0)
    def _():
        out_ref[...] += chunk(h_scr[...])


def _merge(x, g, ya, yb, yc, wga, wgb, wgc, wa, wb, wc, wo, layer, *, tm, tn):
    t, d = x.shape
    row = lambda i, j: (i, 0)
    colw = lambda i, j: (layer, 0, j)
    return pl.pallas_call(
        _merge_kernel,
        grid=(t // tm, d // tn),
        in_specs=[pl.BlockSpec((tm, d), row), _layer_spec(g, layer),
                  pl.BlockSpec((tm, ya.shape[1]), row), pl.BlockSpec((tm, yb.shape[1]), row),
                  pl.BlockSpec((tm, yc.shape[1]), row)]
                 + [pl.BlockSpec((None, w.shape[1], tn), colw) for w in (wga, wgb, wgc, wa, wb, wc)]
                 + [pl.BlockSpec((None, tn, d), lambda i, j: (layer, j, 0))],
        out_specs=pl.BlockSpec((tm, d), row),
        out_shape=jax.ShapeDtypeStruct((t, d), F32),
        scratch_shapes=[pltpu.VMEM((tm, d), BF16)],
        compiler_params=_cparams(("parallel", "arbitrary"), VMEM_RESIDENT),
        name="merge",
    )(x, g, ya, yb, yc, wga, wgb, wgc, wa, wb, wc, wo)


def _pad_last(w, n):
    return jnp.pad(w, [(0, 0)] * (w.ndim - 1) + [(0, n - w.shape[-1])])


def _rot_half_cols(w):
    half = w.shape[-1] // 2
    return jnp.concatenate([-w[..., half:], w[..., :half]], axis=-1)


def _swap_half(g):
    half = g.shape[-1] // 2
    return jnp.concatenate([g[..., half:], g[..., :half]], axis=-1)


def _gain3(g):
    rope = g[:, MLA_NOPE:]
    return jnp.stack([g[:, :MLA_NOPE], _pad_last(rope, LANE), _pad_last(_swap_half(rope), LANE)], axis=1)


def _row(g):
    return g[:, None, :]


def kernel(x, positions, ffn1_norm, ffn1_w_gate, ffn1_w_up, ffn1_w_down, mix_norm, w_in, mla_cq_norm, mla_ckv_norm, mla_w_uq, mla_w_ukv, mla_q_norm, mla_k_norm, gla_w_gate2, gla_b_gate2, gla_o_norm, dswa_q_norm, dswa_k_norm, w_branch_a, w_branch_b, w_branch_c, w_out, ffn2_norm, ffn2_w_gate, ffn2_w_up, ffn2_w_down):
    batch, seq, d = x.shape
    depth = w_in.shape[0]
    d_ff = ffn1_w_gate.shape[-1]
    t = batch * seq
    tm, tm_ffn, tf = TOKEN_BLOCK, FFN_TOKEN_BLOCK, FFN_CHUNK
    ff_pad = -(-d_ff // tf) * tf
    assert seq % max(wd for wd, _ in DSWA_GROUPS) == 0 and seq % tm == 0 and t % tm_ffn == 0

    xf = x.reshape(t, d)
    half = MLA_ROPE // 2
    inv_freq = ROPE_THETA ** (-jnp.arange(half, dtype=F32) / half)
    inv_pad = jnp.pad(jnp.concatenate([inv_freq, inv_freq]), (0, LANE - MLA_ROPE)).reshape(1, LANE)
    cos, sin = _rope_tables(positions.reshape(t, 1), inv_pad, tm=tm)
    slopes = 2.0 ** (-ALIBI_MAX_EXP * jnp.arange(1, DSWA_HEADS + 1, dtype=F32) / DSWA_HEADS)

    o_cq, o_ckv, o_kr = 0, MLA_Q_RANK, MLA_Q_RANK + MLA_KV_RANK
    o_gla = o_kr + MLA_ROPE
    n_qkv_b = GLA_HEADS * (2 * GLA_DK + GLA_DV)
    o_glr = o_gla + n_qkv_b
    o_rb = o_glr + GLA_GATE_RANK
    o_c = o_rb + GLA_HEADS * GLA_DV
    o_ga = o_c + 3 * DSWA_HEADS * DSWA_HEAD_DIM
    o_gb, o_gc = o_ga + d, o_ga + 2 * d

    bf = lambda w: w.astype(BF16)

    def ffn_weights(wg, wu, wdn):
        zc = jnp.zeros((depth, d, ff_pad - d_ff), BF16)
        zr = jnp.zeros((depth, ff_pad - d_ff, d), BF16)
        return (jnp.concatenate([bf(wg), zc], axis=2), jnp.concatenate([bf(wu), zc], axis=2),
                jnp.concatenate([bf(wdn), zr], axis=1))

    ffn1 = (_row(ffn1_norm),) + ffn_weights(ffn1_w_gate, ffn1_w_up, ffn1_w_down)
    ffn2 = (_row(ffn2_norm),) + ffn_weights(ffn2_w_gate, ffn2_w_up, ffn2_w_down)
    gmix = _row(mix_norm)
    w_in = bf(w_in)
    w_kr = w_in[:, :, o_kr:o_kr + MLA_ROPE]
    w1 = bf(jnp.concatenate([w_in[:, :, o_cq:o_kr], _pad_last(w_kr, LANE), _pad_last(_rot_half_cols(w_kr), LANE)],
                            axis=-1))
    wq4 = mla_w_uq.reshape(depth, MLA_Q_RANK, MLA_HEADS, MLA_QK)
    wq_rope = wq4[..., MLA_NOPE:]
    wuq = jnp.concatenate([wq4[..., :MLA_NOPE], _pad_last(wq_rope, LANE), _pad_last(_rot_half_cols(wq_rope), LANE)],
                          axis=-1)
    wuq = bf(wuq.reshape(depth, MLA_Q_RANK, MLA_HEADS * 3 * LANE))
    wkv4 = mla_w_ukv.reshape(depth, MLA_KV_RANK, MLA_HEADS, MLA_NOPE + MLA_V)
    wukv = bf(jnp.concatenate([wkv4[..., :MLA_NOPE].reshape(depth, MLA_KV_RANK, -1),
                               wkv4[..., MLA_NOPE:].reshape(depth, MLA_KV_RANK, -1)], axis=-1))
    mla_params = (w1, _row(mla_cq_norm), _row(mla_ckv_norm), wuq, wukv, _gain3(mla_q_norm), _gain3(mla_k_norm))
    w_gla = bf(jnp.concatenate([w_in[:, :, o_gla:o_glr], _pad_last(w_in[:, :, o_glr:o_rb], LANE),
                                w_in[:, :, o_rb:o_c]], axis=-1))
    w2 = bf(jnp.pad(gla_w_gate2, ((0, 0), (0, LANE - GLA_GATE_RANK), (0, 0))))
    gla_params = (w_gla, w2, _row(gla_b_gate2))
    go = _row(gla_o_norm)
    dswa_params = (bf(w_in[:, :, o_c:o_ga]), _row(dswa_q_norm), _row(dswa_k_norm))
    merge_params = (bf(w_in[:, :, o_ga:o_gb]), bf(w_in[:, :, o_gb:o_gc]), bf(w_in[:, :, o_gc:o_gc + d]),
                    bf(w_branch_a), bf(w_branch_b), bf(w_branch_c), bf(w_out))

    for l in range(depth):
        xf = _ffn(xf, *ffn1, l, tm=tm_ffn, tf=tf)

        qa, ka, va = _mla_prep(xf, gmix, cos, sin, *mla_params, l, tm=tm)
        y_a = _flash(qa, ka, va, batch=batch, seq=seq, tq=FLASH_TILE)

        qb, kb, vb, la, sr = _gla_proj(xf, gmix, *gla_params, l, tm=tm)
        y_b = _gla(qb, kb, vb, la, sr, go, l, batch=batch, seq=seq, tb=GLA_TOKENS)

        qkv_c = _dswa_proj(xf, gmix, *dswa_params, l, batch=batch, seq=seq, tm=tm)
        outs_lses = []
        for gi in range(len(DSWA_GROUPS)):
            outs_lses += _dswa_group(slopes, *qkv_c[3 * gi:3 * gi + 3], group=gi)
        y_c = _dswa_combine(outs_lses, batch=batch, seq=seq, tm=tm)

        xf = _merge(xf, gmix, y_a, y_b, y_c, *merge_params, l, tm=tm, tn=MERGE_CHUNK)
        xf = _ffn(xf, *ffn2, l, tm=tm_ffn, tf=tf)
    return xf.reshape(batch, seq, d)
```

```python
import functools

import numpy as np
import jax
import jax.numpy as jnp
from jax import lax
from jax.experimental import pallas as pl
from jax.experimental.pallas import tpu as pltpu

F32 = jnp.float32
BF16 = jnp.bfloat16

MLA_HEADS = 6
MLA_Q_RANK = 512
MLA_KV_RANK = 256
MLA_NOPE = 128
MLA_ROPE = 64
MLA_V = 128
MLA_QK = MLA_NOPE + MLA_ROPE
MLA_QK_PAD = 256
GLA_HEADS = 4
GLA_DK = 64
GLA_DV = 128
GLA_GATE_RANK = 16
GLA_TAU = 16.0
GLA_CHUNK = 64
GLA_SUB = 8
GLA_LEVELS = (32, 16, 8)
DSWA_GROUPS = ((128, 1), (512, 4), (2048, 16))
DSWA_HEADS_PER_GROUP = 2
DSWA_HEADS = 6
DSWA_HEAD_DIM = 128
DSWA_W = 128
ROPE_THETA = 10000.0
ALIBI_MAX_EXP = 8.0
NORM_EPS = 1e-6
NEG_INF = -1e30
LOG2E = 1.4426950408889634

LANE = 128
LSE_LANES = LANE // DSWA_HEADS_PER_GROUP
BF16_ROWS = 16
MIB = 1024 * 1024

TOKEN_BLOCK = 512
ROW_PARTS = 2
FFN_TOKEN_BLOCK = 1024
FFN_CHUNK = 512
MERGE_CHUNK = 512
FLASH_TILE = 1024
GLA_TOKENS = 512
DSWA_MAX_Q = 1024
VMEM_FFN = 57
VMEM_RESIDENT = 52
VMEM_SCAN = 40
VMEM_SMALL = 32


def _cparams(sem, vmem_mib):
    return pltpu.CompilerParams(dimension_semantics=sem, vmem_limit_bytes=int(vmem_mib * MIB))


def _inv_rms(xf, n):
    return lax.rsqrt(jnp.sum(xf * xf, axis=-1, keepdims=True) / n + NORM_EPS)


def _const_spec(shape):
    nd = len(shape)
    return pl.BlockSpec(shape, lambda *_: (0,) * nd)


def _layer_spec(arr, layer):
    nd = arr.ndim - 1
    return pl.BlockSpec((None,) + arr.shape[1:], lambda *_: (layer,) + (0,) * nd)


def _ffn_kernel(x_ref, g_ref, wg_ref, wu_ref, wd_ref, o_ref, h_scr):
    def half_chunk(h):
        gate = jnp.dot(h, wg_ref[...], preferred_element_type=F32)
        up = jnp.dot(h, wu_ref[...], preferred_element_type=F32)
        inter = (gate * jax.nn.sigmoid(gate) * up).astype(BF16)
        return 0.5 * jnp.dot(inter, wd_ref[...], preferred_element_type=F32)

    @pl.when(pl.program_id(1) == 0)
    def _():
        xf = x_ref[...]
        h = (xf * _inv_rms(xf, xf.shape[-1]) * g_ref[...]).astype(BF16)
        h_scr[...] = h
        o_ref[...] = xf + half_chunk(h)

    @pl.when(pl.program_id(1) > 0)
    def _():
        o_ref[...] += half_chunk(h_scr[...])


def _ffn(x, g, wg, wu, wd, layer, *, tm, tf):
    t, d = x.shape
    fp = wg.shape[-1]
    return pl.pallas_call(
        _ffn_kernel,
        grid=(t // tm, fp // tf),
        in_specs=[
            pl.BlockSpec((tm, d), lambda i, j: (i, 0)),
            _layer_spec(g, layer),
            pl.BlockSpec((None, d, tf), lambda i, j: (layer, 0, j)),
            pl.BlockSpec((None, d, tf), lambda i, j: (layer, 0, j)),
            pl.BlockSpec((None, tf, d), lambda i, j: (layer, j, 0)),
        ],
        out_specs=pl.BlockSpec((tm, d), lambda i, j: (i, 0)),
        out_shape=jax.ShapeDtypeStruct((t, d), F32),
        scratch_shapes=[pltpu.VMEM((tm, d), BF16)],
        compiler_params=_cparams(("parallel", "arbitrary"), VMEM_FFN),
        name="ffn",
    )(x, g, wg, wu, wd)


def _rope_kernel(pos_ref, inv_ref, cos_ref, sin_ref):
    ang = pos_ref[...].astype(F32) * inv_ref[...]
    cos_ref[...] = jnp.cos(ang)
    sin_ref[...] = jnp.sin(ang)


def _rope_tables(pos_col, inv_pad, *, tm):
    t = pos_col.shape[0]
    return pl.pallas_call(
        _rope_kernel,
        grid=(t // tm,),
        in_specs=[pl.BlockSpec((tm, 1), lambda i: (i, 0)), _const_spec((1, LANE))],
        out_specs=[pl.BlockSpec((tm, LANE), lambda i: (i, 0))] * 2,
        out_shape=[jax.ShapeDtypeStruct((t, LANE), F32)] * 2,
        compiler_params=_cparams(("parallel",), VMEM_SMALL),
        name="rope_tables",
    )(pos_col, inv_pad)


def _mla_prep_kernel(x_ref, g_ref, cos_ref, sin_ref, w1_ref, gcq_ref, gckv_ref, wuq_ref, wukv_ref,
                     gq_ref, gk_ref, q_ref, k_ref, v_ref):
    scale = MLA_QK ** -0.5 * LOG2E
    gq_n, gq_r, gq_t = gq_ref[0:1, :], gq_ref[1:2, :], gq_ref[2:3, :]
    gk_n, gk_r, gk_t = gk_ref[0:1, :], gk_ref[1:2, :], gk_ref[2:3, :]
    tm = x_ref.shape[0]
    for part in range(ROW_PARTS):
        rows = slice(part * tm // ROW_PARTS, (part + 1) * tm // ROW_PARTS)
        xf = x_ref[rows, :]
        h = (xf * _inv_rms(xf, xf.shape[-1]) * g_ref[...]).astype(BF16)
        u = jnp.dot(h, w1_ref[...], preferred_element_type=F32)
        c_q = u[:, :MLA_Q_RANK]
        c_kv = u[:, MLA_Q_RANK:MLA_Q_RANK + MLA_KV_RANK]
        kr = u[:, MLA_Q_RANK + MLA_KV_RANK:MLA_Q_RANK + MLA_KV_RANK + LANE]
        krot = u[:, MLA_Q_RANK + MLA_KV_RANK + LANE:]
        c_q = (c_q * _inv_rms(c_q, MLA_Q_RANK) * gcq_ref[...]).astype(BF16)
        c_kv = (c_kv * _inv_rms(c_kv, MLA_KV_RANK) * gckv_ref[...]).astype(BF16)
        qall = jnp.dot(c_q, wuq_ref[...], preferred_element_type=F32)
        kv = jnp.dot(c_kv, wukv_ref[...], preferred_element_type=F32)
        cos = cos_ref[rows, :]
        sin = sin_ref[rows, :]
        k_roped = kr * gk_r * cos + krot * gk_t * sin
        kr_ss = jnp.sum(kr * kr, axis=-1, keepdims=True)
        for hh in range(MLA_HEADS):
            base = hh * 3 * LANE
            nope = qall[:, base:base + LANE]
            rope = qall[:, base + LANE:base + 2 * LANE]
            rot = qall[:, base + 2 * LANE:base + 3 * LANE]
            ss = jnp.sum(nope * nope, axis=-1, keepdims=True) + jnp.sum(rope * rope, axis=-1, keepdims=True)
            r = lax.rsqrt(ss / MLA_QK + NORM_EPS) * scale
            q_ref[hh, rows, 0:LANE] = (nope * r * gq_n).astype(BF16)
            q_ref[hh, rows, LANE:2 * LANE] = ((rope * gq_r * cos + rot * gq_t * sin) * r).astype(BF16)
            kn = kv[:, hh * LANE:(hh + 1) * LANE]
            rk = lax.rsqrt((jnp.sum(kn * kn, axis=-1, keepdims=True) + kr_ss) / MLA_QK + NORM_EPS)
            k_ref[hh, rows, 0:LANE] = (kn * rk * gk_n).astype(BF16)
            k_ref[hh, rows, LANE:2 * LANE] = (k_roped * rk).astype(BF16)
            v_ref[hh, rows, :] = kv[:, (MLA_HEADS + hh) * LANE:(MLA_HEADS + hh + 1) * LANE].astype(BF16)


def _mla_prep(x, g, cos, sin, w1, gcq, gckv, wuq, wukv, gq3, gk3, layer, *, tm):
    t, d = x.shape
    row = lambda i: (i, 0)
    hrow = lambda i: (0, i, 0)
    return pl.pallas_call(
        _mla_prep_kernel,
        grid=(t // tm,),
        in_specs=[pl.BlockSpec((tm, d), row), _layer_spec(g, layer),
                  pl.BlockSpec((tm, LANE), row), pl.BlockSpec((tm, LANE), row)]
                 + [_layer_spec(p, layer) for p in (w1, gcq, gckv, wuq, wukv, gq3, gk3)],
        out_specs=[
            pl.BlockSpec((MLA_HEADS, tm, MLA_QK_PAD), hrow),
            pl.BlockSpec((MLA_HEADS, tm, MLA_QK_PAD), hrow),
            pl.BlockSpec((MLA_HEADS, tm, MLA_V), hrow),
        ],
        out_shape=[
            jax.ShapeDtypeStruct((MLA_HEADS, t, MLA_QK_PAD), BF16),
            jax.ShapeDtypeStruct((MLA_HEADS, t, MLA_QK_PAD), BF16),
            jax.ShapeDtypeStruct((MLA_HEADS, t, MLA_V), BF16),
        ],
        compiler_params=_cparams(("parallel",), VMEM_RESIDENT),
        name="mla_prep",
    )(x, g, cos, sin, w1, gcq, gckv, wuq, wukv, gq3, gk3)


def _flash_kernel(q_ref, k_ref, v_ref, o_ref, sa_scr, sb_scr, m_scr, acc_scr, *, tq):
    qi = pl.program_id(2)
    m_scr[...] = jnp.full_like(m_scr, NEG_INF)
    acc_scr[...] = jnp.zeros_like(acc_scr)
    q = q_ref[...]
    ones = jnp.ones((tq, MLA_V), BF16)

    def scores(tile, s_scr):
        r0 = pl.multiple_of(tile * tq, tq)
        s_scr[...] = lax.dot_general(q, k_ref[pl.ds(r0, tq), :], (((1,), (1,)), ((), ())),
                                     preferred_element_type=F32)

    def accumulate(tile, s_scr, diagonal):
        r0 = pl.multiple_of(tile * tq, tq)
        sc = s_scr[...]
        if diagonal:
            row = lax.broadcasted_iota(jnp.int32, sc.shape, 0)
            col = lax.broadcasted_iota(jnp.int32, sc.shape, 1)
            sc = jnp.where(col <= row, sc, NEG_INF)
        m_prev = m_scr[...]
        m_new = jnp.maximum(m_prev, jnp.max(sc, axis=-1, keepdims=True))
        alpha = jnp.exp2(m_prev - m_new)
        p = jnp.exp2(sc - jnp.tile(m_new, (1, tq // LANE)))
        v1 = jnp.concatenate([v_ref[pl.ds(r0, tq), :], ones], axis=1)
        pv = jnp.dot(p.astype(BF16), v1, preferred_element_type=F32)
        acc_scr[...] = jnp.tile(alpha, (1, 2)) * acc_scr[...] + pv
        m_scr[...] = m_new

    scores(0, sa_scr)

    def body(j, carry):
        scores(2 * j + 1, sb_scr)
        accumulate(2 * j, sa_scr, False)
        scores(2 * j + 2, sa_scr)
        accumulate(2 * j + 1, sb_scr, False)
        return carry

    def body_x2(jj, carry):
        body(2 * jj, carry)
        return body(2 * jj + 1, carry)

    npair = qi // 2
    lax.fori_loop(0, npair // 2, body_x2, 0)
    lax.fori_loop(2 * (npair // 2), npair, body, 0)

    @pl.when(qi % 2 == 0)
    def _():
        accumulate(qi, sa_scr, True)

    @pl.when(qi % 2 == 1)
    def _():
        scores(qi, sb_scr)
        accumulate(qi - 1, sa_scr, False)
        accumulate(qi, sb_scr, True)

    acc = acc_scr[...]
    o_ref[...] = (acc[:, 0:MLA_V] / acc[:, MLA_V:]).astype(o_ref.dtype)


def _flash(q, k, v, *, batch, seq, tq):
    assert seq % tq == 0
    nq = seq // tq
    t = batch * seq
    return pl.pallas_call(
        functools.partial(_flash_kernel, tq=tq),
        grid=(batch, MLA_HEADS, nq),
        in_specs=[
            pl.BlockSpec((None, tq, MLA_QK_PAD), lambda b, h, i: (h, b * nq + i, 0)),
            pl.BlockSpec((None, seq, MLA_QK_PAD), lambda b, h, i: (h, b, 0)),
            pl.BlockSpec((None, seq, MLA_V), lambda b, h, i: (h, b, 0)),
        ],
        out_specs=pl.BlockSpec((tq, MLA_V), lambda b, h, i: (b * nq + i, h)),
        out_shape=jax.ShapeDtypeStruct((t, MLA_HEADS * MLA_V), BF16),
        scratch_shapes=[pltpu.VMEM((tq, tq), F32), pltpu.VMEM((tq, tq), F32),
                        pltpu.VMEM((tq, MLA_V), F32), pltpu.VMEM((tq, 2 * MLA_V), F32)],
        compiler_params=_cparams(("parallel", "parallel", "arbitrary"), VMEM_RESIDENT),
        name="mla_flash",
    )(q, k, v)


def _gla_proj_kernel(x_ref, g_ref, w_ref, w2_ref, b2_ref, q_ref, k_ref, v_ref, la_ref, sr_ref):
    nk = GLA_HEADS * GLA_DK
    nv = GLA_HEADS * GLA_DV
    tm = x_ref.shape[0]
    for part in range(ROW_PARTS):
        rows = slice(part * tm // ROW_PARTS, (part + 1) * tm // ROW_PARTS)
        xf = x_ref[rows, :]
        h = (xf * _inv_rms(xf, xf.shape[-1]) * g_ref[...]).astype(BF16)
        u = jnp.dot(h, w_ref[...], preferred_element_type=F32)
        q_ref[rows, :] = u[:, 0:nk] * (GLA_DK ** -0.5)
        k_ref[rows, :] = u[:, nk:2 * nk]
        v_ref[rows, :] = u[:, 2 * nk:2 * nk + nv]
        gate_lr = u[:, 2 * nk + nv:2 * nk + nv + LANE].astype(BF16)
        z = jnp.dot(gate_lr, w2_ref[...], preferred_element_type=F32) + b2_ref[...]
        la_ref[rows, :] = -(jnp.maximum(-z, 0.0) + jnp.log1p(jnp.exp(-jnp.abs(z)))) / GLA_TAU
        r = u[:, 2 * nk + nv + LANE:]
        sr_ref[rows, :] = r * jax.nn.sigmoid(r)


def _gla_proj(x, g, w, w2, b2, layer, *, tm):
    t, d = x.shape
    nk = GLA_HEADS * GLA_DK
    nv = GLA_HEADS * GLA_DV
    row = lambda i: (i, 0)
    return pl.pallas_call(
        _gla_proj_kernel,
        grid=(t // tm,),
        in_specs=[pl.BlockSpec((tm, d), row)] + [_layer_spec(p, layer) for p in (g, w, w2, b2)],
        out_specs=[pl.BlockSpec((tm, nk), row), pl.BlockSpec((tm, nk), row), pl.BlockSpec((tm, nv), row),
                   pl.BlockSpec((tm, nk), row), pl.BlockSpec((tm, nv), row)],
        out_shape=[jax.ShapeDtypeStruct((t, nk), F32), jax.ShapeDtypeStruct((t, nk), F32),
                   jax.ShapeDtypeStruct((t, nv), F32), jax.ShapeDtypeStruct((t, nk), F32),
                   jax.ShapeDtypeStruct((t, nv), F32)],
        compiler_params=_cparams(("parallel",), VMEM_RESIDENT),
        name="gla_proj",
    )(x, g, w, w2, b2)


def _gla_constants():
    c = GLA_CHUNK
    idx = np.arange(c)
    lmask = []
    for lvl in GLA_LEVELS:
        blk = idx // lvl
        m = ((blk[:, None] % 2) == 1) & (blk[None, :] == blk[:, None] - 1)
        lmask.append(np.tile(m.astype(np.float32), (GLA_HEADS, 1)))
    lmask = np.stack(lmask, axis=0)
    hk = np.arange(GLA_HEADS * GLA_DK) // GLA_DK
    hv = np.arange(GLA_HEADS * GLA_DV) // GLA_DV
    hrow = np.arange(GLA_HEADS * c) // c
    headmask = (hrow[:, None] == hk[None, :]).astype(np.float32)
    e_ind = (hk[:, None] == hv[None, :]).astype(np.float32)
    return lmask, headmask, e_ind


def _gla_kernel(q_ref, k_ref, v_ref, la_ref, sr_ref, go_ref, lmask_ref, hmask_ref, eind_ref, eindt_ref,
                y_ref, st_scr, *, chunks):
    c = GLA_CHUNK
    nk = GLA_HEADS * GLA_DK
    nv = GLA_HEADS * GLA_DV
    nsub = c // GLA_SUB

    @pl.when(pl.program_id(1) == 0)
    def _():
        st_scr[...] = jnp.zeros_like(st_scr)

    def chunk(ci, carry):
        r0 = pl.multiple_of(ci * c, c)
        q = q_ref[pl.ds(r0, c), :]
        k = k_ref[pl.ds(r0, c), :]
        v = v_ref[pl.ds(r0, c), :]
        a = la_ref[pl.ds(r0, c), :]
        row = lax.broadcasted_iota(jnp.int32, (c, nk), 0)
        b = a
        shift = 1
        while shift < c:
            b = b + jnp.where(row >= shift, pltpu.roll(b, shift, axis=0), 0.0)
            shift *= 2
        parts = [b]
        for lvl in GLA_LEVELS:
            ref = b[lvl - 1:lvl]
            for start in range(2 * lvl, c, 2 * lvl):
                ref = jnp.where(row >= start, b[start + lvl - 1:start + lvl], ref)
            parts.append(jnp.where((row & lvl) != 0, b - ref, ref - b))
        parts.append(b[c - 1:c] - b)
        f = jnp.exp(jnp.concatenate(parts, axis=0))
        v_bf = v.astype(BF16)

        amat = jnp.zeros((GLA_HEADS * c, c), F32)
        hmask = hmask_ref[...]
        for li in range(len(GLA_LEVELS)):
            fl = f[(li + 1) * c:(li + 2) * c]
            qt = q * fl
            kt = (k * fl).astype(BF16)
            qs = (jnp.concatenate([qt] * GLA_HEADS, axis=0) * hmask).astype(BF16)
            sc = lax.dot_general(qs, kt, (((1,), (1,)), ((), ())), preferred_element_type=F32)
            amat = amat + sc * lmask_ref[li]
        a_bf = amat.astype(BF16)
        o = jnp.concatenate(
            [jnp.dot(a_bf[hh * c:(hh + 1) * c], v_bf[:, hh * GLA_DV:(hh + 1) * GLA_DV], preferred_element_type=F32)
             for hh in range(GLA_HEADS)], axis=1)

        qb = (q * f[0:c]).astype(BF16)
        st = st_scr[...]
        o = o + lax.dot_general(qb, st.astype(BF16), (((1,), (1,)), ((), ())), preferred_element_type=F32)

        q3 = q.reshape(nsub, GLA_SUB, nk)
        k3 = k.reshape(nsub, GLA_SUB, nk)
        b3 = b.reshape(nsub, GLA_SUB, nk)
        v3 = v.reshape(nsub, GLA_SUB, nv)
        tt = lax.broadcasted_iota(jnp.int32, (nsub, GLA_SUB, nk), 1)
        od = jnp.zeros((nsub, GLA_SUB, nv), F32)
        eind = eind_ref[...]
        for s in range(GLA_SUB):
            w = jnp.exp(jnp.minimum(b3 - b3[:, s:s + 1, :], 0.0))
            x = jnp.where(tt >= s, q3 * k3[:, s:s + 1, :] * w, 0.0)
            rr = jnp.dot(x.reshape(c, nk).astype(BF16), eind, preferred_element_type=F32)
            od = od + rr.reshape(nsub, GLA_SUB, nv) * v3[:, s:s + 1, :]
        o = o + od.reshape(c, nv)

        kend = (k * f[(len(GLA_LEVELS) + 1) * c:(len(GLA_LEVELS) + 2) * c]).astype(BF16)
        upd = lax.dot_general(v_bf, kend, (((0,), (0,)), ((), ())), preferred_element_type=F32)
        st_scr[...] = st * f[c - 1:c] + upd * eindt_ref[...]

        go = go_ref[...]
        sr = sr_ref[pl.ds(r0, c), :]
        ys = []
        for hh in range(GLA_HEADS):
            oh = o[:, hh * GLA_DV:(hh + 1) * GLA_DV]
            ys.append(oh * _inv_rms(oh, GLA_DV) * go)
        y_ref[pl.ds(r0, c), :] = (jnp.concatenate(ys, axis=1) * sr).astype(y_ref.dtype)
        return carry

    lax.fori_loop(0, chunks, chunk, 0, unroll=4)


def _gla(q, k, v, la, sr, go, layer, *, batch, seq, tb):
    nk = GLA_HEADS * GLA_DK
    nv = GLA_HEADS * GLA_DV
    nb = seq // tb
    lmask, headmask, e_ind = _gla_constants()
    consts = [jnp.asarray(lmask, F32), jnp.asarray(headmask, F32),
              jnp.asarray(e_ind, BF16), jnp.asarray(e_ind.T, F32)]
    row = lambda b, i: (b * nb + i, 0)
    return pl.pallas_call(
        functools.partial(_gla_kernel, chunks=tb // GLA_CHUNK),
        grid=(batch, nb),
        in_specs=[pl.BlockSpec((tb, nk), row), pl.BlockSpec((tb, nk), row), pl.BlockSpec((tb, nv), row),
                  pl.BlockSpec((tb, nk), row), pl.BlockSpec((tb, nv), row), _layer_spec(go, layer)]
                 + [_const_spec(cst.shape) for cst in consts],
        out_specs=pl.BlockSpec((tb, nv), row),
        out_shape=jax.ShapeDtypeStruct((batch * seq, nv), BF16),
        scratch_shapes=[pltpu.VMEM((nv, nk), F32)],
        compiler_params=_cparams(("parallel", "arbitrary"), VMEM_SCAN),
        name="gla_scan",
    )(q, k, v, la, sr, go, *consts)


def _dswa_proj_kernel(x_ref, g_ref, w_ref, gq_ref, gk_ref, *refs, tm):
    ng = len(DSWA_GROUPS)
    out_refs = refs[:3 * ng]
    q_scr, k_scr, v_scr = refs[3 * ng:]
    n = DSWA_HEADS * DSWA_HEAD_DIM
    gq = gq_ref[...]
    gk = gk_ref[...]
    scale = DSWA_HEAD_DIM ** -0.5
    for part in range(ROW_PARTS):
        rows = slice(part * tm // ROW_PARTS, (part + 1) * tm // ROW_PARTS)
        xf = x_ref[rows, :]
        h = (xf * _inv_rms(xf, xf.shape[-1]) * g_ref[...]).astype(BF16)
        u = jnp.dot(h, w_ref[...], preferred_element_type=F32)
        for hh in range(DSWA_HEADS):
            sl = slice(hh * DSWA_HEAD_DIM, (hh + 1) * DSWA_HEAD_DIM)
            qh = u[:, sl]
            q_scr[hh, rows, :] = qh * _inv_rms(qh, DSWA_HEAD_DIM) * gq * scale
            kh = u[:, n + hh * DSWA_HEAD_DIM:n + (hh + 1) * DSWA_HEAD_DIM]
            k_scr[hh, rows, :] = kh * _inv_rms(kh, DSWA_HEAD_DIM) * gk
            v_scr[hh, rows, :] = u[:, 2 * n + hh * DSWA_HEAD_DIM:2 * n + (hh + 1) * DSWA_HEAD_DIM]
    for gi, (_, dil) in enumerate(DSWA_GROUPS):
        for src, dst in zip((q_scr, k_scr, v_scr), out_refs[3 * gi:3 * gi + 3]):
            for hh in range(DSWA_HEADS_PER_GROUP):
                head = gi * DSWA_HEADS_PER_GROUP + hh
                cols = slice(hh * DSWA_HEAD_DIM, (hh + 1) * DSWA_HEAD_DIM)
                for r in range(dil):
                    rows = pl.ds(r, tm // dil, stride=dil) if dil > 1 else slice(None)
                    dst[r, :, cols] = src[head, rows, :].astype(BF16)


def _dswa_proj(x, g, w, gq, gk, layer, *, batch, seq, tm):
    t, d = x.shape
    n = DSWA_HEADS * DSWA_HEAD_DIM
    gw = DSWA_HEADS_PER_GROUP * DSWA_HEAD_DIM
    nblk = seq // tm
    out_specs, out_shape = [], []
    for _, dil in DSWA_GROUPS:
        assert tm % (BF16_ROWS * dil) == 0
        out_specs += [pl.BlockSpec((None, dil, tm // dil, gw), lambda b, i: (b, 0, i, 0))] * 3
        out_shape += [jax.ShapeDtypeStruct((batch, dil, seq // dil, gw), BF16)] * 3
    return pl.pallas_call(
        functools.partial(_dswa_proj_kernel, tm=tm),
        grid=(batch, nblk),
        in_specs=[pl.BlockSpec((tm, d), lambda b, i: (b * nblk + i, 0))]
                 + [_layer_spec(p, layer) for p in (g, w, gq, gk)],
        out_specs=out_specs,
        out_shape=out_shape,
        scratch_shapes=[pltpu.VMEM((DSWA_HEADS, tm, DSWA_HEAD_DIM), F32)] * 3,
        compiler_params=_cparams(("parallel", "parallel"), VMEM_RESIDENT),
        name="dswa_proj",
    )(x, g, w, gq, gk)


def _dswa_kernel(slope_ref, q_ref, kc_ref, kp_ref, vc_ref, vp_ref, o_ref, l_ref, *, dilation, group, nb):
    w = DSWA_W
    ib = pl.program_id(2)
    i = lax.broadcasted_iota(jnp.int32, (w, 2 * w), 0)
    j = lax.broadcasted_iota(jnp.int32, (w, 2 * w), 1)
    steps = w + i - j
    dist = (steps * dilation).astype(F32)
    in_window = (steps >= 0) & (steps <= w)
    for hh in range(DSWA_HEADS_PER_GROUP):
        cols = slice(hh * DSWA_HEAD_DIM, (hh + 1) * DSWA_HEAD_DIM)
        bias_all = jnp.where(in_window, -slope_ref[group * DSWA_HEADS_PER_GROUP + hh] * dist, NEG_INF)
        bias_first = jnp.where(j >= w, bias_all, NEG_INF)
        for jb in range(nb):
            rows = slice(jb * w, (jb + 1) * w)
            if jb == 0:
                kprev, vprev = kp_ref[:, cols], vp_ref[:, cols]
                bias_j = jnp.where(ib > 0, bias_all, bias_first)
            else:
                prows = slice((jb - 1) * w, jb * w)
                kprev, vprev = kc_ref[prows, cols], vc_ref[prows, cols]
                bias_j = bias_all
            keys = jnp.concatenate([kprev, kc_ref[rows, cols]], axis=0)
            vals = jnp.concatenate([vprev, vc_ref[rows, cols]], axis=0)
            sc = lax.dot_general(q_ref[rows, cols], keys, (((1,), (1,)), ((), ())), preferred_element_type=F32)
            sc = jnp.where(bias_j > 0.5 * NEG_INF, sc + bias_j, NEG_INF)
            m = jnp.max(sc, axis=-1, keepdims=True)
            p = jnp.exp(sc - m)
            lsum = jnp.sum(p, axis=-1, keepdims=True)
            out = jnp.dot(p.astype(BF16), vals, preferred_element_type=F32) / lsum
            o_ref[rows, cols] = out.astype(o_ref.dtype)
            l_ref[rows, hh * LSE_LANES:(hh + 1) * LSE_LANES] = jnp.broadcast_to(m + jnp.log(lsum), (w, LSE_LANES))


def _dswa_group(slopes, qg, kg, vg, *, group):
    window, dilation = DSWA_GROUPS[group]
    assert window // dilation == DSWA_W
    batch, _, sd, gw = qg.shape
    tq = min(DSWA_MAX_Q, sd)
    nb = tq // DSWA_W
    cur = pl.BlockSpec((None, None, tq, gw), lambda b, r, i: (b, r, i, 0))
    prev = pl.BlockSpec((None, None, DSWA_W, gw), lambda b, r, i: (b, r, jnp.maximum(i * nb - 1, 0), 0))
    lse = pl.BlockSpec((None, None, tq, LANE), lambda b, r, i: (b, r, i, 0))
    return pl.pallas_call(
        functools.partial(_dswa_kernel, dilation=dilation, group=group, nb=nb),
        grid=(batch, dilation, sd // tq),
        in_specs=[pl.BlockSpec(memory_space=pltpu.SMEM), cur, cur, prev, cur, prev],
        out_specs=[cur, lse],
        out_shape=[jax.ShapeDtypeStruct(qg.shape, BF16), jax.ShapeDtypeStruct(qg.shape[:3] + (LANE,), F32)],
        compiler_params=_cparams(("parallel", "parallel", "arbitrary"), VMEM_SMALL),
        name="dswa_g%d" % group,
    )(slopes, qg, kg, kg, vg, vg)


def _dswa_combine_kernel(*refs, tm):
    ng = len(DSWA_GROUPS)
    in_refs, y_ref, scr = refs[:2 * ng], refs[2 * ng], refs[2 * ng + 1]
    nslot = DSWA_HEADS_PER_GROUP + 1

    def token_order(slot, dil, piece):
        if dil == 1:
            return piece(0)
        for r in range(dil):
            scr[slot, pl.ds(r, tm // dil, stride=dil), :] = piece(r)
        return scr[slot]

    outs, lses = [], []
    for gi, (_, dil) in enumerate(DSWA_GROUPS):
        o_src, l_src = in_refs[2 * gi], in_refs[2 * gi + 1]
        heads = []
        for hh in range(DSWA_HEADS_PER_GROUP):
            cols = slice(hh * DSWA_HEAD_DIM, (hh + 1) * DSWA_HEAD_DIM)
            heads.append(token_order(gi * nslot + hh, dil, lambda r, cols=cols: o_src[r, :, cols].astype(F32)))
        outs.append(jnp.concatenate(heads, axis=1))
        lse = token_order(gi * nslot + DSWA_HEADS_PER_GROUP, dil, lambda r: l_src[r])
        bands = [lse[:, hh * LSE_LANES:(hh + 1) * LSE_LANES] for hh in range(DSWA_HEADS_PER_GROUP)]
        lses.append(jnp.concatenate([b for b in bands for _ in range(DSWA_HEAD_DIM // LSE_LANES)], axis=1))
    mx = functools.reduce(jnp.maximum, lses)
    es = [jnp.exp(l - mx) for l in lses]
    den = functools.reduce(lambda a, b: a + b, es)
    y = functools.reduce(lambda a, b: a + b, [(e / den) * o for e, o in zip(es, outs)])
    y_ref[...] = y.astype(y_ref.dtype)


def _dswa_combine(outs_lses, *, batch, seq, tm):
    gw = DSWA_HEADS_PER_GROUP * DSWA_HEAD_DIM
    nblk = seq // tm
    in_specs = []
    for _, dil in DSWA_GROUPS:
        in_specs += [pl.BlockSpec((None, dil, tm // dil, gw), lambda b, i: (b, 0, i, 0)),
                     pl.BlockSpec((None, dil, tm // dil, LANE), lambda b, i: (b, 0, i, 0))]
    return pl.pallas_call(
        functools.partial(_dswa_combine_kernel, tm=tm),
        grid=(batch, nblk),
        in_specs=in_specs,
        out_specs=pl.BlockSpec((tm, gw), lambda b, i: (b * nblk + i, 0)),
        out_shape=jax.ShapeDtypeStruct((batch * seq, gw), BF16),
        scratch_shapes=[pltpu.VMEM((len(DSWA_GROUPS) * (DSWA_HEADS_PER_GROUP + 1), tm, LANE), F32)],
        compiler_params=_cparams(("parallel", "parallel"), VMEM_SMALL),
        name="dswa_combine",
    )(*outs_lses)


def _merge_kernel(x_ref, g_ref, ya_ref, yb_ref, yc_ref, wga_ref, wgb_ref, wgc_ref, wa_ref, wb_ref, wc_ref, wo_ref,
                  out_ref, h_scr):
    dot = lambda a, b: jnp.dot(a, b, preferred_element_type=F32)

    def chunk(h):
        inter = (jax.nn.sigmoid(dot(h, wga_ref[...])) * dot(ya_ref[...], wa_ref[...])
                 + jax.nn.sigmoid(dot(h, wgb_ref[...])) * dot(yb_ref[...], wb_ref[...])
                 + jax.nn.sigmoid(dot(h, wgc_ref[...])) * dot(yc_ref[...], wc_ref[...]))
        return dot(inter.astype(BF16), wo_ref[...])

    @pl.when(pl.program_id(1) == 0)
    def _():
        xf = x_ref[...]
        h = (xf * _inv_rms(xf, xf.shape[-1]) * g_ref[...]).astype(BF16)
        h_scr[...] = h
        out_ref[...] = xf + chunk(h)

    @pl.when(pl.program_id(1) > 0)
    def _():
        out_ref[...] += chunk(h_scr[...])


def _merge(x, g, ya, yb, yc, wga, wgb, wgc, wa, wb, wc, wo, layer, *, tm, tn):
    t, d = x.shape
    row = lambda i, j: (i, 0)
    colw = lambda i, j: (layer, 0, j)
    return pl.pallas_call(
        _merge_kernel,
        grid=(t // tm, d // tn),
        in_specs=[pl.BlockSpec((tm, d), row), _layer_spec(g, layer),
                  pl.BlockSpec((tm, ya.shape[1]), row), pl.BlockSpec((tm, yb.shape[1]), row),
                  pl.BlockSpec((tm, yc.shape[1]), row)]
                 + [pl.BlockSpec((None, w.shape[1], tn), colw) for w in (wga, wgb, wgc, wa, wb, wc)]
                 + [pl.BlockSpec((None, tn, d), lambda i, j: (layer, j, 0))],
        out_specs=pl.BlockSpec((tm, d), row),
        out_shape=jax.ShapeDtypeStruct((t, d), F32),
        scratch_shapes=[pltpu.VMEM((tm, d), BF16)],
        compiler_params=_cparams(("parallel", "arbitrary"), VMEM_RESIDENT),
        name="merge",
    )(x, g, ya, yb, yc, wga, wgb, wgc, wa, wb, wc, wo)


def _pad_last(w, n):
    return jnp.pad(w, [(0, 0)] * (w.ndim - 1) + [(0, n - w.shape[-1])])


def _rot_half_cols(w):
    half = w.shape[-1] // 2
    return jnp.concatenate([-w[..., half:], w[..., :half]], axis=-1)


def _swap_half(g):
    half = g.shape[-1] // 2
    return jnp.concatenate([g[..., half:], g[..., :half]], axis=-1)


def _gain3(g):
    rope = g[:, MLA_NOPE:]
    return jnp.stack([g[:, :MLA_NOPE], _pad_last(rope, LANE), _pad_last(_swap_half(rope), LANE)], axis=1)


def _row(g):
    return g[:, None, :]


def kernel(x, positions, ffn1_norm, ffn1_w_gate, ffn1_w_up, ffn1_w_down, mix_norm, w_in, mla_cq_norm, mla_ckv_norm, mla_w_uq, mla_w_ukv, mla_q_norm, mla_k_norm, gla_w_gate2, gla_b_gate2, gla_o_norm, dswa_q_norm, dswa_k_norm, w_branch_a, w_branch_b, w_branch_c, w_out, ffn2_norm, ffn2_w_gate, ffn2_w_up, ffn2_w_down):
    batch, seq, d = x.shape
    depth = w_in.shape[0]
    d_ff = ffn1_w_gate.shape[-1]
    t = batch * seq
    tm, tm_ffn, tf = TOKEN_BLOCK, FFN_TOKEN_BLOCK, FFN_CHUNK
    ff_pad = -(-d_ff // tf) * tf
    assert seq % max(wd for wd, _ in DSWA_GROUPS) == 0 and seq % tm == 0 and t % tm_ffn == 0

    xf = x.reshape(t, d)
    half = MLA_ROPE // 2
    inv_freq = ROPE_THETA ** (-jnp.arange(half, dtype=F32) / half)
    inv_pad = jnp.pad(jnp.concatenate([inv_freq, inv_freq]), (0, LANE - MLA_ROPE)).reshape(1, LANE)
    cos, sin = _rope_tables(positions.reshape(t, 1), inv_pad, tm=tm)
    slopes = 2.0 ** (-ALIBI_MAX_EXP * jnp.arange(1, DSWA_HEADS + 1, dtype=F32) / DSWA_HEADS)

    o_cq, o_ckv, o_kr = 0, MLA_Q_RANK, MLA_Q_RANK + MLA_KV_RANK
    o_gla = o_kr + MLA_ROPE
    n_qkv_b = GLA_HEADS * (2 * GLA_DK + GLA_DV)
    o_glr = o_gla + n_qkv_b
    o_rb = o_glr + GLA_GATE_RANK
    o_c = o_rb + GLA_HEADS * GLA_DV
    o_ga = o_c + 3 * DSWA_HEADS * DSWA_HEAD_DIM
    o_gb, o_gc = o_ga + d, o_ga + 2 * d

    bf = lambda w: w.astype(BF16)

    def ffn_weights(wg, wu, wdn):
        zc = jnp.zeros((depth, d, ff_pad - d_ff), BF16)
        zr = jnp.zeros((depth, ff_pad - d_ff, d), BF16)
        return (jnp.concatenate([bf(wg), zc], axis=2), jnp.concatenate([bf(wu), zc], axis=2),
                jnp.concatenate([bf(wdn), zr], axis=1))

    ffn1 = (_row(ffn1_norm),) + ffn_weights(ffn1_w_gate, ffn1_w_up, ffn1_w_down)
    ffn2 = (_row(ffn2_norm),) + ffn_weights(ffn2_w_gate, ffn2_w_up, ffn2_w_down)
    gmix = _row(mix_norm)
    w_in = bf(w_in)
    w_kr = w_in[:, :, o_kr:o_kr + MLA_ROPE]
    w1 = bf(jnp.concatenate([w_in[:, :, o_cq:o_kr], _pad_last(w_kr, LANE), _pad_last(_rot_half_cols(w_kr), LANE)],
                            axis=-1))
    wq4 = mla_w_uq.reshape(depth, MLA_Q_RANK, MLA_HEADS, MLA_QK)
    wq_rope = wq4[..., MLA_NOPE:]
    wuq = jnp.concatenate([wq4[..., :MLA_NOPE], _pad_last(wq_rope, LANE), _pad_last(_rot_half_cols(wq_rope), LANE)],
                          axis=-1)
    wuq = bf(wuq.reshape(depth, MLA_Q_RANK, MLA_HEADS * 3 * LANE))
    wkv4 = mla_w_ukv.reshape(depth, MLA_KV_RANK, MLA_HEADS, MLA_NOPE + MLA_V)
    wukv = bf(jnp.concatenate([wkv4[..., :MLA_NOPE].reshape(depth, MLA_KV_RANK, -1),
                               wkv4[..., MLA_NOPE:].reshape(depth, MLA_KV_RANK, -1)], axis=-1))
    mla_params = (w1, _row(mla_cq_norm), _row(mla_ckv_norm), wuq, wukv, _gain3(mla_q_norm), _gain3(mla_k_norm))
    w_gla = bf(jnp.concatenate([w_in[:, :, o_gla:o_glr], _pad_last(w_in[:, :, o_glr:o_rb], LANE),
                                w_in[:, :, o_rb:o_c]], axis=-1))
    w2 = bf(jnp.pad(gla_w_gate2, ((0, 0), (0, LANE - GLA_GATE_RANK), (0, 0))))
    gla_params = (w_gla, w2, _row(gla_b_gate2))
    go = _row(gla_o_norm)
    dswa_params = (bf(w_in[:, :, o_c:o_ga]), _row(dswa_q_norm), _row(dswa_k_norm))
    merge_params = (bf(w_in[:, :, o_ga:o_gb]), bf(w_in[:, :, o_gb:o_gc]), bf(w_in[:, :, o_gc:o_gc + d]),
                    bf(w_branch_a), bf(w_branch_b), bf(w_branch_c), bf(w_out))

    for l in range(depth):
        xf = _ffn(xf, *ffn1, l, tm=tm_ffn, tf=tf)

        qa, ka, va = _mla_prep(xf, gmix, cos, sin, *mla_params, l, tm=tm)
        y_a = _flash(qa, ka, va, batch=batch, seq=seq, tq=FLASH_TILE)

        qb, kb, vb, la, sr = _gla_proj(xf, gmix, *gla_params, l, tm=tm)
        y_b = _gla(qb, kb, vb, la, sr, go, l, batch=batch, seq=seq, tb=GLA_TOKENS)

        qkv_c = _dswa_proj(xf, gmix, *dswa_params, l, batch=batch, seq=seq, tm=tm)
        outs_lses = []
        for gi in range(len(DSWA_GROUPS)):
            outs_lses += _dswa_group(slopes, *qkv_c[3 * gi:3 * gi + 3], group=gi)
        y_c = _dswa_combine(outs_lses, batch=batch, seq=seq, tm=tm)

        xf = _merge(xf, gmix, y_a, y_b, y_c, *merge_params, l, tm=tm, tn=MERGE_CHUNK)
        xf = _ffn(xf, *ffn2, l, tm=tm_ffn, tf=tf)
    return xf.reshape(batch, seq, d)
```

```python
import functools

import numpy as np
import jax
import jax.numpy as jnp
from jax import lax
from jax.experimental import pallas as pl
from jax.experimental.pallas import tpu as pltpu

F32 = jnp.float32
BF16 = jnp.bfloat16

MLA_HEADS = 6
MLA_Q_RANK = 512
MLA_KV_RANK = 256
MLA_NOPE = 128
MLA_ROPE = 64
MLA_V = 128
MLA_QK = MLA_NOPE + MLA_ROPE
MLA_QK_PAD = 256
GLA_HEADS = 4
GLA_DK = 64
GLA_DV = 128
GLA_GATE_RANK = 16
GLA_TAU = 16.0
GLA_CHUNK = 64
GLA_SUB = 8
GLA_LEVELS = (32, 16, 8)
DSWA_GROUPS = ((128, 1), (512, 4), (2048, 16))
DSWA_HEADS_PER_GROUP = 2
DSWA_HEADS = 6
DSWA_HEAD_DIM = 128
DSWA_W = 128
ROPE_THETA = 10000.0
ALIBI_MAX_EXP = 8.0
NORM_EPS = 1e-6
NEG_INF = -1e30
LOG2E = 1.4426950408889634

LANE = 128
LSE_LANES = LANE // DSWA_HEADS_PER_GROUP
BF16_ROWS = 16
MIB = 1024 * 1024

TOKEN_BLOCK = 512
ROW_PARTS = 2
FFN_TOKEN_BLOCK = 1024
FFN_CHUNK = 512
MERGE_CHUNK = 512
FLASH_TILE = 1024
GLA_TOKENS = 512
DSWA_MAX_Q = 1024
VMEM_FFN = 57
VMEM_RESIDENT = 52
VMEM_SCAN = 40
VMEM_SMALL = 32


def _cparams(sem, vmem_mib):
    return pltpu.CompilerParams(dimension_semantics=sem, vmem_limit_bytes=int(vmem_mib * MIB))


def _inv_rms(xf, n):
    return lax.rsqrt(jnp.sum(xf * xf, axis=-1, keepdims=True) / n + NORM_EPS)


def _const_spec(shape):
    nd = len(shape)
    return pl.BlockSpec(shape, lambda *_: (0,) * nd)


def _layer_spec(arr, layer):
    nd = arr.ndim - 1
    return pl.BlockSpec((None,) + arr.shape[1:], lambda *_: (layer,) + (0,) * nd)


def _ffn_kernel(x_ref, g_ref, wg_ref, wu_ref, wd_ref, o_ref, h_scr):
    def half_chunk(h):
        gate = jnp.dot(h, wg_ref[...], preferred_element_type=F32)
        up = jnp.dot(h, wu_ref[...], preferred_element_type=F32)
        inter = (gate * jax.nn.sigmoid(gate) * up).astype(BF16)
        return 0.5 * jnp.dot(inter, wd_ref[...], preferred_element_type=F32)

    @pl.when(pl.program_id(1) == 0)
    def _():
        xf = x_ref[...]
        h = (xf * _inv_rms(xf, xf.shape[-1]) * g_ref[...]).astype(BF16)
        h_scr[...] = h
        o_ref[...] = xf + half_chunk(h)

    @pl.when(pl.program_id(1) > 0)
    def _():
        o_ref[...] += half_chunk(h_scr[...])


def _ffn(x, g, wg, wu, wd, layer, *, tm, tf):
    t, d = x.shape
    fp = wg.shape[-1]
    return pl.pallas_call(
        _ffn_kernel,
        grid=(t // tm, fp // tf),
        in_specs=[
            pl.BlockSpec((tm, d), lambda i, j: (i, 0)),
            _layer_spec(g, layer),
            pl.BlockSpec((None, d, tf), lambda i, j: (layer, 0, j)),
            pl.BlockSpec((None, d, tf), lambda i, j: (layer, 0, j)),
            pl.BlockSpec((None, tf, d), lambda i, j: (layer, j, 0)),
        ],
        out_specs=pl.BlockSpec((tm, d), lambda i, j: (i, 0)),
        out_shape=jax.ShapeDtypeStruct((t, d), F32),
        scratch_shapes=[pltpu.VMEM((tm, d), BF16)],
        compiler_params=_cparams(("parallel", "arbitrary"), VMEM_FFN),
        name="ffn",
    )(x, g, wg, wu, wd)


def _rope_kernel(pos_ref, inv_ref, cos_ref, sin_ref):
    ang = pos_ref[...].astype(F32) * inv_ref[...]
    cos_ref[...] = jnp.cos(ang)
    sin_ref[...] = jnp.sin(ang)


def _rope_tables(pos_col, inv_pad, *, tm):
    t = pos_col.shape[0]
    return pl.pallas_call(
        _rope_kernel,
        grid=(t // tm,),
        in_specs=[pl.BlockSpec((tm, 1), lambda i: (i, 0)), _const_spec((1, LANE))],
        out_specs=[pl.BlockSpec((tm, LANE), lambda i: (i, 0))] * 2,
        out_shape=[jax.ShapeDtypeStruct((t, LANE), F32)] * 2,
        compiler_params=_cparams(("parallel",), VMEM_SMALL),
        name="rope_tables",
    )(pos_col, inv_pad)


def _mla_prep_kernel(x_ref, g_ref, cos_ref, sin_ref, w1_ref, gcq_ref, gckv_ref, wuq_ref, wukv_ref,
                     gq_ref, gk_ref, q_ref, k_ref, v_ref):
    scale = MLA_QK ** -0.5 * LOG2E
    gq_n, gq_r, gq_t = gq_ref[0:1, :], gq_ref[1:2, :], gq_ref[2:3, :]
    gk_n, gk_r, gk_t = gk_ref[0:1, :], gk_ref[1:2, :], gk_ref[2:3, :]
    tm = x_ref.shape[0]
    for part in range(ROW_PARTS):
        rows = slice(part * tm // ROW_PARTS, (part + 1) * tm // ROW_PARTS)
        xf = x_ref[rows, :]
        h = (xf * _inv_rms(xf, xf.shape[-1]) * g_ref[...]).astype(BF16)
        u = jnp.dot(h, w1_ref[...], preferred_element_type=F32)
        c_q = u[:, :MLA_Q_RANK]
        c_kv = u[:, MLA_Q_RANK:MLA_Q_RANK + MLA_KV_RANK]
        kr = u[:, MLA_Q_RANK + MLA_KV_RANK:MLA_Q_RANK + MLA_KV_RANK + LANE]
        krot = u[:, MLA_Q_RANK + MLA_KV_RANK + LANE:]
        c_q = (c_q * _inv_rms(c_q, MLA_Q_RANK) * gcq_ref[...]).astype(BF16)
        c_kv = (c_kv * _inv_rms(c_kv, MLA_KV_RANK) * gckv_ref[...]).astype(BF16)
        qall = jnp.dot(c_q, wuq_ref[...], preferred_element_type=F32)
        kv = jnp.dot(c_kv, wukv_ref[...], preferred_element_type=F32)
        cos = cos_ref[rows, :]
        sin = sin_ref[rows, :]
        k_roped = kr * gk_r * cos + krot * gk_t * sin
        kr_ss = jnp.sum(kr * kr, axis=-1, keepdims=True)
        for hh in range(MLA_HEADS):
            base = hh * 3 * LANE
            nope = qall[:, base:base + LANE]
            rope = qall[:, base + LANE:base + 2 * LANE]
            rot = qall[:, base + 2 * LANE:base + 3 * LANE]
            ss = jnp.sum(nope * nope, axis=-1, keepdims=True) + jnp.sum(rope * rope, axis=-1, keepdims=True)
            r = lax.rsqrt(ss / MLA_QK + NORM_EPS) * scale
            q_ref[hh, rows, 0:LANE] = (nope * r * gq_n).astype(BF16)
            q_ref[hh, rows, LANE:2 * LANE] = ((rope * gq_r * cos + rot * gq_t * sin) * r).astype(BF16)
            kn = kv[:, hh * LANE:(hh + 1) * LANE]
            rk = lax.rsqrt((jnp.sum(kn * kn, axis=-1, keepdims=True) + kr_ss) / MLA_QK + NORM_EPS)
            k_ref[hh, rows, 0:LANE] = (kn * rk * gk_n).astype(BF16)
            k_ref[hh, rows, LANE:2 * LANE] = (k_roped * rk).astype(BF16)
            v_ref[hh, rows, :] = kv[:, (MLA_HEADS + hh) * LANE:(MLA_HEADS + hh + 1) * LANE].astype(BF16)


def _mla_prep(x, g, cos, sin, w1, gcq, gckv, wuq, wukv, gq3, gk3, layer, *, tm):
    t, d = x.shape
    row = lambda i: (i, 0)
    hrow = lambda i: (0, i, 0)
    return pl.pallas_call(
        _mla_prep_kernel,
        grid=(t // tm,),
        in_specs=[pl.BlockSpec((tm, d), row), _layer_spec(g, layer),
                  pl.BlockSpec((tm, LANE), row), pl.BlockSpec((tm, LANE), row)]
                 + [_layer_spec(p, layer) for p in (w1, gcq, gckv, wuq, wukv, gq3, gk3)],
        out_specs=[
            pl.BlockSpec((MLA_HEADS, tm, MLA_QK_PAD), hrow),
            pl.BlockSpec((MLA_HEADS, tm, MLA_QK_PAD), hrow),
            pl.BlockSpec((MLA_HEADS, tm, MLA_V), hrow),
        ],
        out_shape=[
            jax.ShapeDtypeStruct((MLA_HEADS, t, MLA_QK_PAD), BF16),
            jax.ShapeDtypeStruct((MLA_HEADS, t, MLA_QK_PAD), BF16),
            jax.ShapeDtypeStruct((MLA_HEADS, t, MLA_V), BF16),
        ],
        compiler_params=_cparams(("parallel",), VMEM_RESIDENT),
        name="mla_prep",
    )(x, g, cos, sin, w1, gcq, gckv, wuq, wukv, gq3, gk3)


def _flash_kernel(q_ref, k_ref, v_ref, o_ref, sa_scr, sb_scr, m_scr, acc_scr, *, tq):
    qi = pl.program_id(2)
    m_scr[...] = jnp.full_like(m_scr, NEG_INF)
    acc_scr[...] = jnp.zeros_like(acc_scr)
    q = q_ref[...]
    ones = jnp.ones((tq, MLA_V), BF16)

    def scores(tile, s_scr):
        r0 = pl.multiple_of(tile * tq, tq)
        s_scr[...] = lax.dot_general(q, k_ref[pl.ds(r0, tq), :], (((1,), (1,)), ((), ())),
                                     preferred_element_type=F32)

    def accumulate(tile, s_scr, diagonal):
        r0 = pl.multiple_of(tile * tq, tq)
        sc = s_scr[...]
        if diagonal:
            row = lax.broadcasted_iota(jnp.int32, sc.shape, 0)
            col = lax.broadcasted_iota(jnp.int32, sc.shape, 1)
            sc = jnp.where(col <= row, sc, NEG_INF)
        m_prev = m_scr[...]
        m_new = jnp.maximum(m_prev, jnp.max(sc, axis=-1, keepdims=True))
        alpha = jnp.exp2(m_prev - m_new)
        p = jnp.exp2(sc - jnp.tile(m_new, (1, tq // LANE)))
        v1 = jnp.concatenate([v_ref[pl.ds(r0, tq), :], ones], axis=1)
        pv = jnp.dot(p.astype(BF16), v1, preferred_element_type=F32)
        acc_scr[...] = jnp.tile(alpha, (1, 2)) * acc_scr[...] + pv
        m_scr[...] = m_new

    scores(0, sa_scr)

    def body(j, carry):
        scores(2 * j + 1, sb_scr)
        accumulate(2 * j, sa_scr, False)
        scores(2 * j + 2, sa_scr)
        accumulate(2 * j + 1, sb_scr, False)
        return carry

    def body_x2(jj, carry):
        body(2 * jj, carry)
        return body(2 * jj + 1, carry)

    npair = qi // 2
    lax.fori_loop(0, npair // 2, body_x2, 0)
    lax.fori_loop(2 * (npair // 2), npair, body, 0)

    @pl.when(qi % 2 == 0)
    def _():
        accumulate(qi, sa_scr, True)

    @pl.when(qi % 2 == 1)
    def _():
        scores(qi, sb_scr)
        accumulate(qi - 1, sa_scr, False)
        accumulate(qi, sb_scr, True)

    acc = acc_scr[...]
    o_ref[...] = (acc[:, 0:MLA_V] / acc[:, MLA_V:]).astype(o_ref.dtype)


def _flash(q, k, v, *, batch, seq, tq):
    assert seq % tq == 0
    nq = seq // tq
    t = batch * seq
    return pl.pallas_call(
        functools.partial(_flash_kernel, tq=tq),
        grid=(batch, MLA_HEADS, nq),
        in_specs=[
            pl.BlockSpec((None, tq, MLA_QK_PAD), lambda b, h, i: (h, b * nq + i, 0)),
            pl.BlockSpec((None, seq, MLA_QK_PAD), lambda b, h, i: (h, b, 0)),
            pl.BlockSpec((None, seq, MLA_V), lambda b, h, i: (h, b, 0)),
        ],
        out_specs=pl.BlockSpec((tq, MLA_V), lambda b, h, i: (b * nq + i, h)),
        out_shape=jax.ShapeDtypeStruct((t, MLA_HEADS * MLA_V), BF16),
        scratch_shapes=[pltpu.VMEM((tq, tq), F32), pltpu.VMEM((tq, tq), F32),
                        pltpu.VMEM((tq, MLA_V), F32), pltpu.VMEM((tq, 2 * MLA_V), F32)],
        compiler_params=_cparams(("parallel", "parallel", "arbitrary"), VMEM_RESIDENT),
        name="mla_flash",
    )(q, k, v)


def _gla_proj_kernel(x_ref, g_ref, w_ref, w2_ref, b2_ref, q_ref, k_ref, v_ref, la_ref, sr_ref):
    nk = GLA_HEADS * GLA_DK
    nv = GLA_HEADS * GLA_DV
    tm = x_ref.shape[0]
    parts = 1
    for part in range(parts):
        rows = slice(part * tm // parts, (part + 1) * tm // parts)
        xf = x_ref[rows, :]
        h = (xf * _inv_rms(xf, xf.shape[-1]) * g_ref[...]).astype(BF16)
        u = jnp.dot(h, w_ref[...], preferred_element_type=F32)
        q_ref[rows, :] = u[:, 0:nk] * (GLA_DK ** -0.5)
        k_ref[rows, :] = u[:, nk:2 * nk]
        v_ref[rows, :] = u[:, 2 * nk:2 * nk + nv]
        gate_lr = u[:, 2 * nk + nv:2 * nk + nv + LANE].astype(BF16)
        z = jnp.dot(gate_lr, w2_ref[...], preferred_element_type=F32) + b2_ref[...]
        la_ref[rows, :] = -(jnp.maximum(-z, 0.0) + jnp.log1p(jnp.exp(-jnp.abs(z)))) / GLA_TAU
        r = u[:, 2 * nk + nv + LANE:]
        sr_ref[rows, :] = r * jax.nn.sigmoid(r)


def _gla_proj(x, g, w, w2, b2, layer, *, tm):
    t, d = x.shape
    nk = GLA_HEADS * GLA_DK
    nv = GLA_HEADS * GLA_DV
    row = lambda i: (i, 0)
    return pl.pallas_call(
        _gla_proj_kernel,
        grid=(t // tm,),
        in_specs=[pl.BlockSpec((tm, d), row)] + [_layer_spec(p, layer) for p in (g, w, w2, b2)],
        out_specs=[pl.BlockSpec((tm, nk), row), pl.BlockSpec((tm, nk), row), pl.BlockSpec((tm, nv), row),
                   pl.BlockSpec((tm, nk), row), pl.BlockSpec((tm, nv), row)],
        out_shape=[jax.ShapeDtypeStruct((t, nk), F32), jax.ShapeDtypeStruct((t, nk), F32),
                   jax.ShapeDtypeStruct((t, nv), F32), jax.ShapeDtypeStruct((t, nk), F32),
                   jax.ShapeDtypeStruct((t, nv), F32)],
        compiler_params=_cparams(("parallel",), VMEM_RESIDENT),
        name="gla_proj",
    )(x, g, w, w2, b2)


def _gla_constants():
    c = GLA_CHUNK
    idx = np.arange(c)
    lmask = []
    for lvl in GLA_LEVELS:
        blk = idx // lvl
        m = ((blk[:, None] % 2) == 1) & (blk[None, :] == blk[:, None] - 1)
        lmask.append(np.tile(m.astype(np.float32), (GLA_HEADS, 1)))
    lmask = np.stack(lmask, axis=0)
    hk = np.arange(GLA_HEADS * GLA_DK) // GLA_DK
    hv = np.arange(GLA_HEADS * GLA_DV) // GLA_DV
    hrow = np.arange(GLA_HEADS * c) // c
    headmask = (hrow[:, None] == hk[None, :]).astype(np.float32)
    e_ind = (hk[:, None] == hv[None, :]).astype(np.float32)
    return lmask, headmask, e_ind


def _gla_kernel(q_ref, k_ref, v_ref, la_ref, sr_ref, go_ref, lmask_ref, hmask_ref, eind_ref, eindt_ref,
                y_ref, st_scr, *, chunks):
    c = GLA_CHUNK
    nk = GLA_HEADS * GLA_DK
    nv = GLA_HEADS * GLA_DV
    nsub = c // GLA_SUB

    @pl.when(pl.program_id(1) == 0)
    def _():
        st_scr[...] = jnp.zeros_like(st_scr)

    def chunk(ci, carry):
        r0 = pl.multiple_of(ci * c, c)
        q = q_ref[pl.ds(r0, c), :]
        k = k_ref[pl.ds(r0, c), :]
        v = v_ref[pl.ds(r0, c), :]
        a = la_ref[pl.ds(r0, c), :]
        row = lax.broadcasted_iota(jnp.int32, (c, nk), 0)
        b = a
        shift = 1
        while shift < c:
            b = b + jnp.where(row >= shift, pltpu.roll(b, shift, axis=0), 0.0)
            shift *= 2
        parts = [b]
        for lvl in GLA_LEVELS:
            ref = b[lvl - 1:lvl]
            for start in range(2 * lvl, c, 2 * lvl):
                ref = jnp.where(row >= start, b[start + lvl - 1:start + lvl], ref)
            parts.append(jnp.where((row & lvl) != 0, b - ref, ref - b))
        parts.append(b[c - 1:c] - b)
        f = jnp.exp(jnp.concatenate(parts, axis=0))
        v_bf = v.astype(BF16)

        amat = jnp.zeros((GLA_HEADS * c, c), F32)
        hmask = hmask_ref[...]
        for li in range(len(GLA_LEVELS)):
            fl = f[(li + 1) * c:(li + 2) * c]
            qt = q * fl
            kt = (k * fl).astype(BF16)
            qs = (jnp.concatenate([qt] * GLA_HEADS, axis=0) * hmask).astype(BF16)
            sc = lax.dot_general(qs, kt, (((1,), (1,)), ((), ())), preferred_element_type=F32)
            amat = amat + sc * lmask_ref[li]
        a_bf = amat.astype(BF16)
        o = jnp.concatenate(
            [jnp.dot(a_bf[hh * c:(hh + 1) * c], v_bf[:, hh * GLA_DV:(hh + 1) * GLA_DV], preferred_element_type=F32)
             for hh in range(GLA_HEADS)], axis=1)

        qb = (q * f[0:c]).astype(BF16)
        st = st_scr[...]
        o = o + lax.dot_general(qb, st.astype(BF16), (((1,), (1,)), ((), ())), preferred_element_type=F32)

        q3 = q.reshape(nsub, GLA_SUB, nk)
        k3 = k.reshape(nsub, GLA_SUB, nk)
        b3 = b.reshape(nsub, GLA_SUB, nk)
        v3 = v.reshape(nsub, GLA_SUB, nv)
        tt = lax.broadcasted_iota(jnp.int32, (nsub, GLA_SUB, nk), 1)
        od = jnp.zeros((nsub, GLA_SUB, nv), F32)
        eind = eind_ref[...]
        for s in range(GLA_SUB):
            w = jnp.exp(jnp.minimum(b3 - b3[:, s:s + 1, :], 0.0))
            x = jnp.where(tt >= s, q3 * k3[:, s:s + 1, :] * w, 0.0)
            rr = jnp.dot(x.reshape(c, nk).astype(BF16), eind, preferred_element_type=F32)
            od = od + rr.reshape(nsub, GLA_SUB, nv) * v3[:, s:s + 1, :]
        o = o + od.reshape(c, nv)

        kend = (k * f[(len(GLA_LEVELS) + 1) * c:(len(GLA_LEVELS) + 2) * c]).astype(BF16)
        upd = lax.dot_general(v_bf, kend, (((0,), (0,)), ((), ())), preferred_element_type=F32)
        st_scr[...] = st * f[c - 1:c] + upd * eindt_ref[...]

        go = go_ref[...]
        sr = sr_ref[pl.ds(r0, c), :]
        ys = []
        for hh in range(GLA_HEADS):
            oh = o[:, hh * GLA_DV:(hh + 1) * GLA_DV]
            ys.append(oh * _inv_rms(oh, GLA_DV) * go)
        y_ref[pl.ds(r0, c), :] = (jnp.concatenate(ys, axis=1) * sr).astype(y_ref.dtype)
        return carry

    lax.fori_loop(0, chunks, chunk, 0, unroll=True)


def _gla(q, k, v, la, sr, go, layer, *, batch, seq, tb):
    nk = GLA_HEADS * GLA_DK
    nv = GLA_HEADS * GLA_DV
    nb = seq // tb
    lmask, headmask, e_ind = _gla_constants()
    consts = [jnp.asarray(lmask, F32), jnp.asarray(headmask, F32),
              jnp.asarray(e_ind, BF16), jnp.asarray(e_ind.T, F32)]
    row = lambda b, i: (b * nb + i, 0)
    return pl.pallas_call(
        functools.partial(_gla_kernel, chunks=tb // GLA_CHUNK),
        grid=(batch, nb),
        in_specs=[pl.BlockSpec((tb, nk), row), pl.BlockSpec((tb, nk), row), pl.BlockSpec((tb, nv), row),
                  pl.BlockSpec((tb, nk), row), pl.BlockSpec((tb, nv), row), _layer_spec(go, layer)]
                 + [_const_spec(cst.shape) for cst in consts],
        out_specs=pl.BlockSpec((tb, nv), row),
        out_shape=jax.ShapeDtypeStruct((batch * seq, nv), BF16),
        scratch_shapes=[pltpu.VMEM((nv, nk), F32)],
        compiler_params=_cparams(("parallel", "arbitrary"), VMEM_SCAN),
        name="gla_scan",
    )(q, k, v, la, sr, go, *consts)


def _dswa_proj_kernel(x_ref, g_ref, w_ref, gq_ref, gk_ref, *refs, tm):
    ng = len(DSWA_GROUPS)
    out_refs = refs[:3 * ng]
    q_scr, k_scr, v_scr = refs[3 * ng:]
    n = DSWA_HEADS * DSWA_HEAD_DIM
    gq = gq_ref[...]
    gk = gk_ref[...]
    scale = DSWA_HEAD_DIM ** -0.5
    for part in range(ROW_PARTS):
        rows = slice(part * tm // ROW_PARTS, (part + 1) * tm // ROW_PARTS)
        xf = x_ref[rows, :]
        h = (xf * _inv_rms(xf, xf.shape[-1]) * g_ref[...]).astype(BF16)
        u = jnp.dot(h, w_ref[...], preferred_element_type=F32)
        for hh in range(DSWA_HEADS):
            sl = slice(hh * DSWA_HEAD_DIM, (hh + 1) * DSWA_HEAD_DIM)
            qh = u[:, sl]
            q_scr[hh, rows, :] = qh * _inv_rms(qh, DSWA_HEAD_DIM) * gq * scale
            kh = u[:, n + hh * DSWA_HEAD_DIM:n + (hh + 1) * DSWA_HEAD_DIM]
            k_scr[hh, rows, :] = kh * _inv_rms(kh, DSWA_HEAD_DIM) * gk
            v_scr[hh, rows, :] = u[:, 2 * n + hh * DSWA_HEAD_DIM:2 * n + (hh + 1) * DSWA_HEAD_DIM]
    for gi, (_, dil) in enumerate(DSWA_GROUPS):
        for src, dst in zip((q_scr, k_scr, v_scr), out_refs[3 * gi:3 * gi + 3]):
            for hh in range(DSWA_HEADS_PER_GROUP):
                head = gi * DSWA_HEADS_PER_GROUP + hh
                cols = slice(hh * DSWA_HEAD_DIM, (hh + 1) * DSWA_HEAD_DIM)
                for r in range(dil):
                    rows = pl.ds(r, tm // dil, stride=dil) if dil > 1 else slice(None)
                    dst[r, :, cols] = src[head, rows, :].astype(BF16)


def _dswa_proj(x, g, w, gq, gk, layer, *, batch, seq, tm):
    t, d = x.shape
    n = DSWA_HEADS * DSWA_HEAD_DIM
    gw = DSWA_HEADS_PER_GROUP * DSWA_HEAD_DIM
    nblk = seq // tm
    out_specs, out_shape = [], []
    for _, dil in DSWA_GROUPS:
        assert tm % (BF16_ROWS * dil) == 0
        out_specs += [pl.BlockSpec((None, dil, tm // dil, gw), lambda b, i: (b, 0, i, 0))] * 3
        out_shape += [jax.ShapeDtypeStruct((batch, dil, seq // dil, gw), BF16)] * 3
    return pl.pallas_call(
        functools.partial(_dswa_proj_kernel, tm=tm),
        grid=(batch, nblk),
        in_specs=[pl.BlockSpec((tm, d), lambda b, i: (b * nblk + i, 0))]
                 + [_layer_spec(p, layer) for p in (g, w, gq, gk)],
        out_specs=out_specs,
        out_shape=out_shape,
        scratch_shapes=[pltpu.VMEM((DSWA_HEADS, tm, DSWA_HEAD_DIM), F32)] * 3,
        compiler_params=_cparams(("parallel", "parallel"), VMEM_RESIDENT),
        name="dswa_proj",
    )(x, g, w, gq, gk)


def _dswa_kernel(slope_ref, q_ref, kc_ref, kp_ref, vc_ref, vp_ref, o_ref, l_ref, *, dilation, group, nb):
    w = DSWA_W
    ib = pl.program_id(2)
    i = lax.broadcasted_iota(jnp.int32, (w, 2 * w), 0)
    j = lax.broadcasted_iota(jnp.int32, (w, 2 * w), 1)
    steps = w + i - j
    dist = (steps * dilation).astype(F32)
    in_window = (steps >= 0) & (steps <= w)
    for hh in range(DSWA_HEADS_PER_GROUP):
        cols = slice(hh * DSWA_HEAD_DIM, (hh + 1) * DSWA_HEAD_DIM)
        bias_all = jnp.where(in_window, -slope_ref[group * DSWA_HEADS_PER_GROUP + hh] * dist, NEG_INF)
        bias_first = jnp.where(j >= w, bias_all, NEG_INF)
        for jb in range(nb):
            rows = slice(jb * w, (jb + 1) * w)
            if jb == 0:
                kprev, vprev = kp_ref[:, cols], vp_ref[:, cols]
                bias_j = jnp.where(ib > 0, bias_all, bias_first)
            else:
                prows = slice((jb - 1) * w, jb * w)
                kprev, vprev = kc_ref[prows, cols], vc_ref[prows, cols]
                bias_j = bias_all
            keys = jnp.concatenate([kprev, kc_ref[rows, cols]], axis=0)
            vals = jnp.concatenate([vprev, vc_ref[rows, cols]], axis=0)
            sc = lax.dot_general(q_ref[rows, cols], keys, (((1,), (1,)), ((), ())), preferred_element_type=F32)
            sc = jnp.where(bias_j > 0.5 * NEG_INF, sc + bias_j, NEG_INF)
            m = jnp.max(sc, axis=-1, keepdims=True)
            p = jnp.exp(sc - m)
            lsum = jnp.sum(p, axis=-1, keepdims=True)
            out = jnp.dot(p.astype(BF16), vals, preferred_element_type=F32) / lsum
            o_ref[rows, cols] = out.astype(o_ref.dtype)
            l_ref[rows, hh * LSE_LANES:(hh + 1) * LSE_LANES] = jnp.broadcast_to(m + jnp.log(lsum), (w, LSE_LANES))


def _dswa_group(slopes, qg, kg, vg, *, group):
    window, dilation = DSWA_GROUPS[group]
    assert window // dilation == DSWA_W
    batch, _, sd, gw = qg.shape
    tq = min(DSWA_MAX_Q, sd)
    nb = tq // DSWA_W
    cur = pl.BlockSpec((None, None, tq, gw), lambda b, r, i: (b, r, i, 0))
    prev = pl.BlockSpec((None, None, DSWA_W, gw), lambda b, r, i: (b, r, jnp.maximum(i * nb - 1, 0), 0))
    lse = pl.BlockSpec((None, None, tq, LANE), lambda b, r, i: (b, r, i, 0))
    return pl.pallas_call(
        functools.partial(_dswa_kernel, dilation=dilation, group=group, nb=nb),
        grid=(batch, dilation, sd // tq),
        in_specs=[pl.BlockSpec(memory_space=pltpu.SMEM), cur, cur, prev, cur, prev],
        out_specs=[cur, lse],
        out_shape=[jax.ShapeDtypeStruct(qg.shape, BF16), jax.ShapeDtypeStruct(qg.shape[:3] + (LANE,), F32)],
        compiler_params=_cparams(("parallel", "parallel", "arbitrary"), VMEM_SMALL),
        name="dswa_g%d" % group,
    )(slopes, qg, kg, kg, vg, vg)


def _dswa_combine_kernel(*refs, tm):
    ng = len(DSWA_GROUPS)
    in_refs, y_ref, scr = refs[:2 * ng], refs[2 * ng], refs[2 * ng + 1]
    nslot = DSWA_HEADS_PER_GROUP + 1

    def token_order(slot, dil, piece):
        if dil == 1:
            return piece(0)
        for r in range(dil):
            scr[slot, pl.ds(r, tm // dil, stride=dil), :] = piece(r)
        return scr[slot]

    outs, lses = [], []
    for gi, (_, dil) in enumerate(DSWA_GROUPS):
        o_src, l_src = in_refs[2 * gi], in_refs[2 * gi + 1]
        heads = []
        for hh in range(DSWA_HEADS_PER_GROUP):
            cols = slice(hh * DSWA_HEAD_DIM, (hh + 1) * DSWA_HEAD_DIM)
            heads.append(token_order(gi * nslot + hh, dil, lambda r, cols=cols: o_src[r, :, cols].astype(F32)))
        outs.append(jnp.concatenate(heads, axis=1))
        lse = token_order(gi * nslot + DSWA_HEADS_PER_GROUP, dil, lambda r: l_src[r])
        bands = [lse[:, hh * LSE_LANES:(hh + 1) * LSE_LANES] for hh in range(DSWA_HEADS_PER_GROUP)]
        lses.append(jnp.concatenate([b for b in bands for _ in range(DSWA_HEAD_DIM // LSE_LANES)], axis=1))
    mx = functools.reduce(jnp.maximum, lses)
    es = [jnp.exp(l - mx) for l in lses]
    den = functools.reduce(lambda a, b: a + b, es)
    y = functools.reduce(lambda a, b: a + b, [(e / den) * o for e, o in zip(es, outs)])
    y_ref[...] = y.astype(y_ref.dtype)


def _dswa_combine(outs_lses, *, batch, seq, tm):
    gw = DSWA_HEADS_PER_GROUP * DSWA_HEAD_DIM
    nblk = seq // tm
    in_specs = []
    for _, dil in DSWA_GROUPS:
        in_specs += [pl.BlockSpec((None, dil, tm // dil, gw), lambda b, i: (b, 0, i, 0)),
                     pl.BlockSpec((None, dil, tm // dil, LANE), lambda b, i: (b, 0, i, 0))]
    return pl.pallas_call(
        functools.partial(_dswa_combine_kernel, tm=tm),
        grid=(batch, nblk),
        in_specs=in_specs,
        out_specs=pl.BlockSpec((tm, gw), lambda b, i: (b * nblk + i, 0)),
        out_shape=jax.ShapeDtypeStruct((batch * seq, gw), BF16),
        scratch_shapes=[pltpu.VMEM((len(DSWA_GROUPS) * (DSWA_HEADS_PER_GROUP + 1), tm, LANE), F32)],
        compiler_params=_cparams(("parallel", "parallel"), VMEM_SMALL),
        name="dswa_combine",
    )(*outs_lses)


def _merge_kernel(x_ref, g_ref, ya_ref, yb_ref, yc_ref, wga_ref, wgb_ref, wgc_ref, wa_ref, wb_ref, wc_ref, wo_ref,
                  out_ref, h_scr):
    dot = lambda a, b: jnp.dot(a, b, preferred_element_type=F32)

    def chunk(h):
        inter = (jax.nn.sigmoid(dot(h, wga_ref[...])) * dot(ya_ref[...], wa_ref[...])
                 + jax.nn.sigmoid(dot(h, wgb_ref[...])) * dot(yb_ref[...], wb_ref[...])
                 + jax.nn.sigmoid(dot(h, wgc_ref[...])) * dot(yc_ref[...], wc_ref[...]))
        return dot(inter.astype(BF16), wo_ref[...])

    @pl.when(pl.program_id(1) == 0)
    def _():
        xf = x_ref[...]
        h = (xf * _inv_rms(xf, xf.shape[-1]) * g_ref[...]).astype(BF16)
        h_scr[...] = h
        out_ref[...] = xf + chunk(h)

    @pl.when(pl.program_id(1) > 0)
    def _():
        out_ref[...] += chunk(h_scr[...])


def _merge(x, g, ya, yb, yc, wga, wgb, wgc, wa, wb, wc, wo, layer, *, tm, tn):
    t, d = x.shape
    row = lambda i, j: (i, 0)
    colw = lambda i, j: (layer, 0, j)
    return pl.pallas_call(
        _merge_kernel,
        grid=(t // tm, d // tn),
        in_specs=[pl.BlockSpec((tm, d), row), _layer_spec(g, layer),
                  pl.BlockSpec((tm, ya.shape[1]), row), pl.BlockSpec((tm, yb.shape[1]), row),
                  pl.BlockSpec((tm, yc.shape[1]), row)]
                 + [pl.BlockSpec((None, w.shape[1], tn), colw) for w in (wga, wgb, wgc, wa, wb, wc)]
                 + [pl.BlockSpec((None, tn, d), lambda i, j: (layer, j, 0))],
        out_specs=pl.BlockSpec((tm, d), row),
        out_shape=jax.ShapeDtypeStruct((t, d), F32),
        scratch_shapes=[pltpu.VMEM((tm, d), BF16)],
        compiler_params=_cparams(("parallel", "arbitrary"), VMEM_RESIDENT),
        name="merge",
    )(x, g, ya, yb, yc, wga, wgb, wgc, wa, wb, wc, wo)


def _pad_last(w, n):
    return jnp.pad(w, [(0, 0)] * (w.ndim - 1) + [(0, n - w.shape[-1])])


def _rot_half_cols(w):
    half = w.shape[-1] // 2
    return jnp.concatenate([-w[..., half:], w[..., :half]], axis=-1)


def _swap_half(g):
    half = g.shape[-1] // 2
    return jnp.concatenate([g[..., half:], g[..., :half]], axis=-1)


def _gain3(g):
    rope = g[:, MLA_NOPE:]
    return jnp.stack([g[:, :MLA_NOPE], _pad_last(rope, LANE), _pad_last(_swap_half(rope), LANE)], axis=1)


def _row(g):
    return g[:, None, :]


def kernel(x, positions, ffn1_norm, ffn1_w_gate, ffn1_w_up, ffn1_w_down, mix_norm, w_in, mla_cq_norm, mla_ckv_norm, mla_w_uq, mla_w_ukv, mla_q_norm, mla_k_norm, gla_w_gate2, gla_b_gate2, gla_o_norm, dswa_q_norm, dswa_k_norm, w_branch_a, w_branch_b, w_branch_c, w_out, ffn2_norm, ffn2_w_gate, ffn2_w_up, ffn2_w_down):
    batch, seq, d = x.shape
    depth = w_in.shape[0]
    d_ff = ffn1_w_gate.shape[-1]
    t = batch * seq
    tm, tm_ffn, tf = TOKEN_BLOCK, FFN_TOKEN_BLOCK, FFN_CHUNK
    ff_pad = -(-d_ff // tf) * tf
    assert seq % max(wd for wd, _ in DSWA_GROUPS) == 0 and seq % tm == 0 and t % tm_ffn == 0

    xf = x.reshape(t, d)
    half = MLA_ROPE // 2
    inv_freq = ROPE_THETA ** (-jnp.arange(half, dtype=F32) / half)
    inv_pad = jnp.pad(jnp.concatenate([inv_freq, inv_freq]), (0, LANE - MLA_ROPE)).reshape(1, LANE)
    cos, sin = _rope_tables(positions.reshape(t, 1), inv_pad, tm=tm)
    slopes = 2.0 ** (-ALIBI_MAX_EXP * jnp.arange(1, DSWA_HEADS + 1, dtype=F32) / DSWA_HEADS)

    o_cq, o_ckv, o_kr = 0, MLA_Q_RANK, MLA_Q_RANK + MLA_KV_RANK
    o_gla = o_kr + MLA_ROPE
    n_qkv_b = GLA_HEADS * (2 * GLA_DK + GLA_DV)
    o_glr = o_gla + n_qkv_b
    o_rb = o_glr + GLA_GATE_RANK
    o_c = o_rb + GLA_HEADS * GLA_DV
    o_ga = o_c + 3 * DSWA_HEADS * DSWA_HEAD_DIM
    o_gb, o_gc = o_ga + d, o_ga + 2 * d

    bf = lambda w: w.astype(BF16)

    def ffn_weights(wg, wu, wdn):
        zc = jnp.zeros((depth, d, ff_pad - d_ff), BF16)
        zr = jnp.zeros((depth, ff_pad - d_ff, d), BF16)
        return (jnp.concatenate([bf(wg), zc], axis=2), jnp.concatenate([bf(wu), zc], axis=2),
                jnp.concatenate([bf(wdn), zr], axis=1))

    ffn1 = (_row(ffn1_norm),) + ffn_weights(ffn1_w_gate, ffn1_w_up, ffn1_w_down)
    ffn2 = (_row(ffn2_norm),) + ffn_weights(ffn2_w_gate, ffn2_w_up, ffn2_w_down)
    gmix = _row(mix_norm)
    w_in = bf(w_in)
    w_kr = w_in[:, :, o_kr:o_kr + MLA_ROPE]
    w1 = bf(jnp.concatenate([w_in[:, :, o_cq:o_kr], _pad_last(w_kr, LANE), _pad_last(_rot_half_cols(w_kr), LANE)],
                            axis=-1))
    wq4 = mla_w_uq.reshape(depth, MLA_Q_RANK, MLA_HEADS, MLA_QK)
    wq_rope = wq4[..., MLA_NOPE:]
    wuq = jnp.concatenate([wq4[..., :MLA_NOPE], _pad_last(wq_rope, LANE), _pad_last(_rot_half_cols(wq_rope), LANE)],
                          axis=-1)
    wuq = bf(wuq.reshape(depth, MLA_Q_RANK, MLA_HEADS * 3 * LANE))
    wkv4 = mla_w_ukv.reshape(depth, MLA_KV_RANK, MLA_HEADS, MLA_NOPE + MLA_V)
    wukv = bf(jnp.concatenate([wkv4[..., :MLA_NOPE].reshape(depth, MLA_KV_RANK, -1),
                               wkv4[..., MLA_NOPE:].reshape(depth, MLA_KV_RANK, -1)], axis=-1))
    mla_params = (w1, _row(mla_cq_norm), _row(mla_ckv_norm), wuq, wukv, _gain3(mla_q_norm), _gain3(mla_k_norm))
    w_gla = bf(jnp.concatenate([w_in[:, :, o_gla:o_glr], _pad_last(w_in[:, :, o_glr:o_rb], LANE),
                                w_in[:, :, o_rb:o_c]], axis=-1))
    w2 = bf(jnp.pad(gla_w_gate2, ((0, 0), (0, LANE - GLA_GATE_RANK), (0, 0))))
    gla_params = (w_gla, w2, _row(gla_b_gate2))
    go = _row(gla_o_norm)
    dswa_params = (bf(w_in[:, :, o_c:o_ga]), _row(dswa_q_norm), _row(dswa_k_norm))
    merge_params = (bf(w_in[:, :, o_ga:o_gb]), bf(w_in[:, :, o_gb:o_gc]), bf(w_in[:, :, o_gc:o_gc + d]),
                    bf(w_branch_a), bf(w_branch_b), bf(w_branch_c), bf(w_out))

    for l in range(depth):
        xf = _ffn(xf, *ffn1, l, tm=tm_ffn, tf=tf)

        qa, ka, va = _mla_prep(xf, gmix, cos, sin, *mla_params, l, tm=tm)
        y_a = _flash(qa, ka, va, batch=batch, seq=seq, tq=FLASH_TILE)

        qb, kb, vb, la, sr = _gla_proj(xf, gmix, *gla_params, l, tm=tm)
        y_b = _gla(qb, kb, vb, la, sr, go, l, batch=batch, seq=seq, tb=GLA_TOKENS)

        qkv_c = _dswa_proj(xf, gmix, *dswa_params, l, batch=batch, seq=seq, tm=tm)
        outs_lses = []
        for gi in range(len(DSWA_GROUPS)):
            outs_lses += _dswa_group(slopes, *qkv_c[3 * gi:3 * gi + 3], group=gi)
        y_c = _dswa_combine(outs_lses, batch=batch, seq=seq, tm=tm)

        xf = _merge(xf, gmix, y_a, y_b, y_c, *merge_params, l, tm=tm, tn=MERGE_CHUNK)
        xf = _ffn(xf, *ffn2, l, tm=tm_ffn, tf=tf)
    return xf.reshape(batch, seq, d)
```

```python
import functools

import numpy as np
import jax
import jax.numpy as jnp
from jax import lax
from jax.experimental import pallas as pl
from jax.experimental.pallas import tpu as pltpu

F32 = jnp.float32
BF16 = jnp.bfloat16

MLA_HEADS = 6
MLA_Q_RANK = 512
MLA_KV_RANK = 256
MLA_NOPE = 128
MLA_ROPE = 64
MLA_V = 128
MLA_QK = MLA_NOPE + MLA_ROPE
MLA_QK_PAD = 256
GLA_HEADS = 4
GLA_DK = 64
GLA_DV = 128
GLA_GATE_RANK = 16
GLA_TAU = 16.0
GLA_CHUNK = 64
GLA_SUB = 8
GLA_LEVELS = (32, 16, 8)
DSWA_GROUPS = ((128, 1), (512, 4), (2048, 16))
DSWA_HEADS_PER_GROUP = 2
DSWA_HEADS = 6
DSWA_HEAD_DIM = 128
DSWA_W = 128
ROPE_THETA = 10000.0
ALIBI_MAX_EXP = 8.0
NORM_EPS = 1e-6
NEG_INF = -1e30
LOG2E = 1.4426950408889634

LANE = 128
LSE_LANES = LANE // DSWA_HEADS_PER_GROUP
BF16_ROWS = 16
MIB = 1024 * 1024

TOKEN_BLOCK = 512
ROW_PARTS = 2
FFN_TOKEN_BLOCK = 1024
FFN_CHUNK = 512
MERGE_CHUNK = 512
FLASH_TILE = 1024
GLA_TOKENS = 512
DSWA_MAX_Q = 1024
VMEM_FFN = 57
VMEM_RESIDENT = 52
VMEM_SCAN = 40
VMEM_SMALL = 32


def _cparams(sem, vmem_mib):
    return pltpu.CompilerParams(dimension_semantics=sem, vmem_limit_bytes=int(vmem_mib * MIB))


def _inv_rms(xf, n):
    return lax.rsqrt(jnp.sum(xf * xf, axis=-1, keepdims=True) / n + NORM_EPS)


def _const_spec(shape):
    nd = len(shape)
    return pl.BlockSpec(shape, lambda *_: (0,) * nd)


def _layer_spec(arr, layer):
    nd = arr.ndim - 1
    return pl.BlockSpec((None,) + arr.shape[1:], lambda *_: (layer,) + (0,) * nd)


def _ffn_kernel(x_ref, g_ref, wg_ref, wu_ref, wd_ref, o_ref, h_scr):
    def half_chunk(h):
        tf = wg_ref.shape[-1]
        halves = []
        for c0 in range(0, tf, tf // 2):
            cols = slice(c0, c0 + tf // 2)
            gate = jnp.dot(h, wg_ref[:, cols], preferred_element_type=F32)
            up = jnp.dot(h, wu_ref[:, cols], preferred_element_type=F32)
            halves.append((gate * jax.nn.sigmoid(gate) * up).astype(BF16))
        inter = jnp.concatenate(halves, axis=1)
        return 0.5 * jnp.dot(inter, wd_ref[...], preferred_element_type=F32)

    @pl.when(pl.program_id(1) == 0)
    def _():
        xf = x_ref[...]
        h = (xf * _inv_rms(xf, xf.shape[-1]) * g_ref[...]).astype(BF16)
        h_scr[...] = h
        o_ref[...] = xf + half_chunk(h)

    @pl.when(pl.program_id(1) > 0)
    def _():
        o_ref[...] += half_chunk(h_scr[...])


def _ffn(x, g, wg, wu, wd, layer, *, tm, tf):
    t, d = x.shape
    fp = wg.shape[-1]
    return pl.pallas_call(
        _ffn_kernel,
        grid=(t // tm, fp // tf),
        in_specs=[
            pl.BlockSpec((tm, d), lambda i, j: (i, 0)),
            _layer_spec(g, layer),
            pl.BlockSpec((None, d, tf), lambda i, j: (layer, 0, j)),
            pl.BlockSpec((None, d, tf), lambda i, j: (layer, 0, j)),
            pl.BlockSpec((None, tf, d), lambda i, j: (layer, j, 0)),
        ],
        out_specs=pl.BlockSpec((tm, d), lambda i, j: (i, 0)),
        out_shape=jax.ShapeDtypeStruct((t, d), F32),
        scratch_shapes=[pltpu.VMEM((tm, d), BF16)],
        compiler_params=_cparams(("parallel", "arbitrary"), VMEM_FFN),
        name="ffn",
    )(x, g, wg, wu, wd)


def _rope_kernel(pos_ref, inv_ref, cos_ref, sin_ref):
    ang = pos_ref[...].astype(F32) * inv_ref[...]
    cos_ref[...] = jnp.cos(ang)
    sin_ref[...] = jnp.sin(ang)


def _rope_tables(pos_col, inv_pad, *, tm):
    t = pos_col.shape[0]
    return pl.pallas_call(
        _rope_kernel,
        grid=(t // tm,),
        in_specs=[pl.BlockSpec((tm, 1), lambda i: (i, 0)), _const_spec((1, LANE))],
        out_specs=[pl.BlockSpec((tm, LANE), lambda i: (i, 0))] * 2,
        out_shape=[jax.ShapeDtypeStruct((t, LANE), F32)] * 2,
        compiler_params=_cparams(("parallel",), VMEM_SMALL),
        name="rope_tables",
    )(pos_col, inv_pad)


def _mla_prep_kernel(x_ref, g_ref, cos_ref, sin_ref, w1_ref, gcq_ref, gckv_ref, wuq_ref, wukv_ref,
                     gq_ref, gk_ref, q_ref, k_ref, v_ref):
    scale = MLA_QK ** -0.5 * LOG2E
    gq_n, gq_r, gq_t = gq_ref[0:1, :], gq_ref[1:2, :], gq_ref[2:3, :]
    gk_n, gk_r, gk_t = gk_ref[0:1, :], gk_ref[1:2, :], gk_ref[2:3, :]
    tm = x_ref.shape[0]
    for part in range(ROW_PARTS):
        rows = slice(part * tm // ROW_PARTS, (part + 1) * tm // ROW_PARTS)
        xf = x_ref[rows, :]
        h = (xf * _inv_rms(xf, xf.shape[-1]) * g_ref[...]).astype(BF16)
        u = jnp.dot(h, w1_ref[...], preferred_element_type=F32)
        c_q = u[:, :MLA_Q_RANK]
        c_kv = u[:, MLA_Q_RANK:MLA_Q_RANK + MLA_KV_RANK]
        kr = u[:, MLA_Q_RANK + MLA_KV_RANK:MLA_Q_RANK + MLA_KV_RANK + LANE]
        krot = u[:, MLA_Q_RANK + MLA_KV_RANK + LANE:]
        c_q = (c_q * _inv_rms(c_q, MLA_Q_RANK) * gcq_ref[...]).astype(BF16)
        c_kv = (c_kv * _inv_rms(c_kv, MLA_KV_RANK) * gckv_ref[...]).astype(BF16)
        qall = jnp.dot(c_q, wuq_ref[...], preferred_element_type=F32)
        kv = jnp.dot(c_kv, wukv_ref[...], preferred_element_type=F32)
        cos = cos_ref[rows, :]
        sin = sin_ref[rows, :]
        k_roped = kr * gk_r * cos + krot * gk_t * sin
        kr_ss = jnp.sum(kr * kr, axis=-1, keepdims=True)
        for hh in range(MLA_HEADS):
            base = hh * 3 * LANE
            nope = qall[:, base:base + LANE]
            rope = qall[:, base + LANE:base + 2 * LANE]
            rot = qall[:, base + 2 * LANE:base + 3 * LANE]
            ss = jnp.sum(nope * nope, axis=-1, keepdims=True) + jnp.sum(rope * rope, axis=-1, keepdims=True)
            r = lax.rsqrt(ss / MLA_QK + NORM_EPS) * scale
            q_ref[hh, rows, 0:LANE] = (nope * r * gq_n).astype(BF16)
            q_ref[hh, rows, LANE:2 * LANE] = ((rope * gq_r * cos + rot * gq_t * sin) * r).astype(BF16)
            kn = kv[:, hh * LANE:(hh + 1) * LANE]
            rk = lax.rsqrt((jnp.sum(kn * kn, axis=-1, keepdims=True) + kr_ss) / MLA_QK + NORM_EPS)
            k_ref[hh, rows, 0:LANE] = (kn * rk * gk_n).astype(BF16)
            k_ref[hh, rows, LANE:2 * LANE] = (k_roped * rk).astype(BF16)
            v_ref[hh, rows, :] = kv[:, (MLA_HEADS + hh) * LANE:(MLA_HEADS + hh + 1) * LANE].astype(BF16)


def _mla_prep(x, g, cos, sin, w1, gcq, gckv, wuq, wukv, gq3, gk3, layer, *, tm):
    t, d = x.shape
    row = lambda i: (i, 0)
    hrow = lambda i: (0, i, 0)
    return pl.pallas_call(
        _mla_prep_kernel,
        grid=(t // tm,),
        in_specs=[pl.BlockSpec((tm, d), row), _layer_spec(g, layer),
                  pl.BlockSpec((tm, LANE), row), pl.BlockSpec((tm, LANE), row)]
                 + [_layer_spec(p, layer) for p in (w1, gcq, gckv, wuq, wukv, gq3, gk3)],
        out_specs=[
            pl.BlockSpec((MLA_HEADS, tm, MLA_QK_PAD), hrow),
            pl.BlockSpec((MLA_HEADS, tm, MLA_QK_PAD), hrow),
            pl.BlockSpec((MLA_HEADS, tm, MLA_V), hrow),
        ],
        out_shape=[
            jax.ShapeDtypeStruct((MLA_HEADS, t, MLA_QK_PAD), BF16),
            jax.ShapeDtypeStruct((MLA_HEADS, t, MLA_QK_PAD), BF16),
            jax.ShapeDtypeStruct((MLA_HEADS, t, MLA_V), BF16),
        ],
        compiler_params=_cparams(("parallel",), VMEM_RESIDENT),
        name="mla_prep",
    )(x, g, cos, sin, w1, gcq, gckv, wuq, wukv, gq3, gk3)


def _flash_kernel(q_ref, k_ref, v_ref, o_ref, sa_scr, sb_scr, m_scr, acc_scr, *, tq):
    qi = pl.program_id(2)
    m_scr[...] = jnp.full_like(m_scr, NEG_INF)
    acc_scr[...] = jnp.zeros_like(acc_scr)
    q = q_ref[...]
    ones = jnp.ones((tq, MLA_V), BF16)

    def scores(tile, s_scr):
        r0 = pl.multiple_of(tile * tq, tq)
        s_scr[...] = lax.dot_general(q, k_ref[pl.ds(r0, tq), :], (((1,), (1,)), ((), ())),
                                     preferred_element_type=F32)

    def accumulate(tile, s_scr, diagonal):
        r0 = pl.multiple_of(tile * tq, tq)
        sc = s_scr[...]
        if diagonal:
            row = lax.broadcasted_iota(jnp.int32, sc.shape, 0)
            col = lax.broadcasted_iota(jnp.int32, sc.shape, 1)
            sc = jnp.where(col <= row, sc, NEG_INF)
        m_prev = m_scr[...]
        m_new = jnp.maximum(m_prev, jnp.max(sc, axis=-1, keepdims=True))
        alpha = jnp.exp2(m_prev - m_new)
        p = jnp.exp2(sc - jnp.tile(m_new, (1, tq // LANE)))
        v1 = jnp.concatenate([v_ref[pl.ds(r0, tq), :], ones], axis=1)
        pv = jnp.dot(p.astype(BF16), v1, preferred_element_type=F32)
        acc_scr[...] = jnp.tile(alpha, (1, 2)) * acc_scr[...] + pv
        m_scr[...] = m_new

    scores(0, sa_scr)

    def body(j, carry):
        scores(2 * j + 1, sb_scr)
        accumulate(2 * j, sa_scr, False)
        scores(2 * j + 2, sa_scr)
        accumulate(2 * j + 1, sb_scr, False)
        return carry

    def body_x2(jj, carry):
        body(2 * jj, carry)
        return body(2 * jj + 1, carry)

    npair = qi // 2
    lax.fori_loop(0, npair // 2, body_x2, 0)
    lax.fori_loop(2 * (npair // 2), npair, body, 0)

    @pl.when(qi % 2 == 0)
    def _():
        accumulate(qi, sa_scr, True)

    @pl.when(qi % 2 == 1)
    def _():
        scores(qi, sb_scr)
        accumulate(qi - 1, sa_scr, False)
        accumulate(qi, sb_scr, True)

    acc = acc_scr[...]
    o_ref[...] = (acc[:, 0:MLA_V] / acc[:, MLA_V:]).astype(o_ref.dtype)


def _flash(q, k, v, *, batch, seq, tq):
    assert seq % tq == 0
    nq = seq // tq
    t = batch * seq
    return pl.pallas_call(
        functools.partial(_flash_kernel, tq=tq),
        grid=(batch, MLA_HEADS, nq),
        in_specs=[
            pl.BlockSpec((None, tq, MLA_QK_PAD), lambda b, h, i: (h, b * nq + i, 0)),
            pl.BlockSpec((None, seq, MLA_QK_PAD), lambda b, h, i: (h, b, 0)),
            pl.BlockSpec((None, seq, MLA_V), lambda b, h, i: (h, b, 0)),
        ],
        out_specs=pl.BlockSpec((tq, MLA_V), lambda b, h, i: (b * nq + i, h)),
        out_shape=jax.ShapeDtypeStruct((t, MLA_HEADS * MLA_V), BF16),
        scratch_shapes=[pltpu.VMEM((tq, tq), F32), pltpu.VMEM((tq, tq), F32),
                        pltpu.VMEM((tq, MLA_V), F32), pltpu.VMEM((tq, 2 * MLA_V), F32)],
        compiler_params=_cparams(("parallel", "parallel", "arbitrary"), VMEM_RESIDENT),
        name="mla_flash",
    )(q, k, v)


def _gla_proj_kernel(x_ref, g_ref, w_ref, w2_ref, b2_ref, q_ref, k_ref, v_ref, la_ref, sr_ref):
    nk = GLA_HEADS * GLA_DK
    nv = GLA_HEADS * GLA_DV
    tm = x_ref.shape[0]
    parts = 1
    for part in range(parts):
        rows = slice(part * tm // parts, (part + 1) * tm // parts)
        xf = x_ref[rows, :]
        h = (xf * _inv_rms(xf, xf.shape[-1]) * g_ref[...]).astype(BF16)
        u = jnp.dot(h, w_ref[...], preferred_element_type=F32)
        q_ref[rows, :] = u[:, 0:nk] * (GLA_DK ** -0.5)
        k_ref[rows, :] = u[:, nk:2 * nk]
        v_ref[rows, :] = u[:, 2 * nk:2 * nk + nv]
        gate_lr = u[:, 2 * nk + nv:2 * nk + nv + LANE].astype(BF16)
        z = jnp.dot(gate_lr, w2_ref[...], preferred_element_type=F32) + b2_ref[...]
        la_ref[rows, :] = -(jnp.maximum(-z, 0.0) + jnp.log1p(jnp.exp(-jnp.abs(z)))) / GLA_TAU
        r = u[:, 2 * nk + nv + LANE:]
        sr_ref[rows, :] = r * jax.nn.sigmoid(r)


def _gla_proj(x, g, w, w2, b2, layer, *, tm):
    t, d = x.shape
    nk = GLA_HEADS * GLA_DK
    nv = GLA_HEADS * GLA_DV
    row = lambda i: (i, 0)
    return pl.pallas_call(
        _gla_proj_kernel,
        grid=(t // tm,),
        in_specs=[pl.BlockSpec((tm, d), row)] + [_layer_spec(p, layer) for p in (g, w, w2, b2)],
        out_specs=[pl.BlockSpec((tm, nk), row), pl.BlockSpec((tm, nk), row), pl.BlockSpec((tm, nv), row),
                   pl.BlockSpec((tm, nk), row), pl.BlockSpec((tm, nv), row)],
        out_shape=[jax.ShapeDtypeStruct((t, nk), F32), jax.ShapeDtypeStruct((t, nk), F32),
                   jax.ShapeDtypeStruct((t, nv), F32), jax.ShapeDtypeStruct((t, nk), F32),
                   jax.ShapeDtypeStruct((t, nv), F32)],
        compiler_params=_cparams(("parallel",), VMEM_RESIDENT),
        name="gla_proj",
    )(x, g, w, w2, b2)


def _gla_constants():
    c = GLA_CHUNK
    idx = np.arange(c)
    lmask = []
    for lvl in GLA_LEVELS:
        blk = idx // lvl
        m = ((blk[:, None] % 2) == 1) & (blk[None, :] == blk[:, None] - 1)
        lmask.append(np.tile(m.astype(np.float32), (GLA_HEADS, 1)))
    lmask = np.stack(lmask, axis=0)
    hk = np.arange(GLA_HEADS * GLA_DK) // GLA_DK
    hv = np.arange(GLA_HEADS * GLA_DV) // GLA_DV
    hrow = np.arange(GLA_HEADS * c) // c
    headmask = (hrow[:, None] == hk[None, :]).astype(np.float32)
    e_ind = (hk[:, None] == hv[None, :]).astype(np.float32)
    return lmask, headmask, e_ind


def _gla_kernel(q_ref, k_ref, v_ref, la_ref, sr_ref, go_ref, lmask_ref, hmask_ref, eind_ref, eindt_ref,
                y_ref, st_scr, *, chunks):
    c = GLA_CHUNK
    nk = GLA_HEADS * GLA_DK
    nv = GLA_HEADS * GLA_DV
    nsub = c // GLA_SUB

    @pl.when(pl.program_id(1) == 0)
    def _():
        st_scr[...] = jnp.zeros_like(st_scr)

    def chunk(ci, carry):
        r0 = pl.multiple_of(ci * c, c)
        q = q_ref[pl.ds(r0, c), :]
        k = k_ref[pl.ds(r0, c), :]
        v = v_ref[pl.ds(r0, c), :]
        a = la_ref[pl.ds(r0, c), :]
        row = lax.broadcasted_iota(jnp.int32, (c, nk), 0)
        b = a
        shift = 1
        while shift < c:
            b = b + jnp.where(row >= shift, pltpu.roll(b, shift, axis=0), 0.0)
            shift *= 2
        parts = [b]
        for lvl in GLA_LEVELS:
            ref = b[lvl - 1:lvl]
            for start in range(2 * lvl, c, 2 * lvl):
                ref = jnp.where(row >= start, b[start + lvl - 1:start + lvl], ref)
            parts.append(jnp.where((row & lvl) != 0, b - ref, ref - b))
        parts.append(b[c - 1:c] - b)
        f = jnp.exp(jnp.concatenate(parts, axis=0))
        v_bf = v.astype(BF16)

        amat = jnp.zeros((GLA_HEADS * c, c), F32)
        hmask = hmask_ref[...]
        for li in range(len(GLA_LEVELS)):
            fl = f[(li + 1) * c:(li + 2) * c]
            qt = q * fl
            kt = (k * fl).astype(BF16)
            qs = (jnp.concatenate([qt] * GLA_HEADS, axis=0) * hmask).astype(BF16)
            sc = lax.dot_general(qs, kt, (((1,), (1,)), ((), ())), preferred_element_type=F32)
            amat = amat + sc * lmask_ref[li]
        a_bf = amat.astype(BF16)
        o = jnp.concatenate(
            [jnp.dot(a_bf[hh * c:(hh + 1) * c], v_bf[:, hh * GLA_DV:(hh + 1) * GLA_DV], preferred_element_type=F32)
             for hh in range(GLA_HEADS)], axis=1)

        qb = (q * f[0:c]).astype(BF16)
        st = st_scr[...]
        o = o + lax.dot_general(qb, st.astype(BF16), (((1,), (1,)), ((), ())), preferred_element_type=F32)

        q3 = q.reshape(nsub, GLA_SUB, nk)
        k3 = k.reshape(nsub, GLA_SUB, nk)
        b3 = b.reshape(nsub, GLA_SUB, nk)
        v3 = v.reshape(nsub, GLA_SUB, nv)
        tt = lax.broadcasted_iota(jnp.int32, (nsub, GLA_SUB, nk), 1)
        od = jnp.zeros((nsub, GLA_SUB, nv), F32)
        eind = eind_ref[...]
        for s in range(GLA_SUB):
            w = jnp.exp(jnp.minimum(b3 - b3[:, s:s + 1, :], 0.0))
            x = jnp.where(tt >= s, q3 * k3[:, s:s + 1, :] * w, 0.0)
            rr = jnp.dot(x.reshape(c, nk).astype(BF16), eind, preferred_element_type=F32)
            od = od + rr.reshape(nsub, GLA_SUB, nv) * v3[:, s:s + 1, :]
        o = o + od.reshape(c, nv)

        kend = (k * f[(len(GLA_LEVELS) + 1) * c:(len(GLA_LEVELS) + 2) * c]).astype(BF16)
        upd = lax.dot_general(v_bf, kend, (((0,), (0,)), ((), ())), preferred_element_type=F32)
        st_scr[...] = st * f[c - 1:c] + upd * eindt_ref[...]

        go = go_ref[...]
        sr = sr_ref[pl.ds(r0, c), :]
        ys = []
        for hh in range(GLA_HEADS):
            oh = o[:, hh * GLA_DV:(hh + 1) * GLA_DV]
            ys.append(oh * _inv_rms(oh, GLA_DV) * go)
        y_ref[pl.ds(r0, c), :] = (jnp.concatenate(ys, axis=1) * sr).astype(y_ref.dtype)
        return carry

    lax.fori_loop(0, chunks, chunk, 0, unroll=True)


def _gla(q, k, v, la, sr, go, layer, *, batch, seq, tb):
    nk = GLA_HEADS * GLA_DK
    nv = GLA_HEADS * GLA_DV
    nb = seq // tb
    lmask, headmask, e_ind = _gla_constants()
    consts = [jnp.asarray(lmask, F32), jnp.asarray(headmask, F32),
              jnp.asarray(e_ind, BF16), jnp.asarray(e_ind.T, F32)]
    row = lambda b, i: (b * nb + i, 0)
    return pl.pallas_call(
        functools.partial(_gla_kernel, chunks=tb // GLA_CHUNK),
        grid=(batch, nb),
        in_specs=[pl.BlockSpec((tb, nk), row), pl.BlockSpec((tb, nk), row), pl.BlockSpec((tb, nv), row),
                  pl.BlockSpec((tb, nk), row), pl.BlockSpec((tb, nv), row), _layer_spec(go, layer)]
                 + [_const_spec(cst.shape) for cst in consts],
        out_specs=pl.BlockSpec((tb, nv), row),
        out_shape=jax.ShapeDtypeStruct((batch * seq, nv), BF16),
        scratch_shapes=[pltpu.VMEM((nv, nk), F32)],
        compiler_params=_cparams(("parallel", "arbitrary"), VMEM_SCAN),
        name="gla_scan",
    )(q, k, v, la, sr, go, *consts)


def _dswa_proj_kernel(x_ref, g_ref, w_ref, gq_ref, gk_ref, *refs, tm):
    ng = len(DSWA_GROUPS)
    out_refs = refs[:3 * ng]
    q_scr, k_scr, v_scr = refs[3 * ng:]
    n = DSWA_HEADS * DSWA_HEAD_DIM
    gq = gq_ref[...]
    gk = gk_ref[...]
    scale = DSWA_HEAD_DIM ** -0.5
    for part in range(ROW_PARTS):
        rows = slice(part * tm // ROW_PARTS, (part + 1) * tm // ROW_PARTS)
        xf = x_ref[rows, :]
        h = (xf * _inv_rms(xf, xf.shape[-1]) * g_ref[...]).astype(BF16)
        u = jnp.dot(h, w_ref[...], preferred_element_type=F32)
        for hh in range(DSWA_HEADS):
            sl = slice(hh * DSWA_HEAD_DIM, (hh + 1) * DSWA_HEAD_DIM)
            qh = u[:, sl]
            q_scr[hh, rows, :] = qh * _inv_rms(qh, DSWA_HEAD_DIM) * gq * scale
            kh = u[:, n + hh * DSWA_HEAD_DIM:n + (hh + 1) * DSWA_HEAD_DIM]
            k_scr[hh, rows, :] = kh * _inv_rms(kh, DSWA_HEAD_DIM) * gk
            v_scr[hh, rows, :] = u[:, 2 * n + hh * DSWA_HEAD_DIM:2 * n + (hh + 1) * DSWA_HEAD_DIM]
    for gi, (_, dil) in enumerate(DSWA_GROUPS):
        for src, dst in zip((q_scr, k_scr, v_scr), out_refs[3 * gi:3 * gi + 3]):
            for hh in range(DSWA_HEADS_PER_GROUP):
                head = gi * DSWA_HEADS_PER_GROUP + hh
                cols = slice(hh * DSWA_HEAD_DIM, (hh + 1) * DSWA_HEAD_DIM)
                for r in range(dil):
                    rows = pl.ds(r, tm // dil, stride=dil) if dil > 1 else slice(None)
                    dst[r, :, cols] = src[head, rows, :].astype(BF16)


def _dswa_proj(x, g, w, gq, gk, layer, *, batch, seq, tm):
    t, d = x.shape
    n = DSWA_HEADS * DSWA_HEAD_DIM
    gw = DSWA_HEADS_PER_GROUP * DSWA_HEAD_DIM
    nblk = seq // tm
    out_specs, out_shape = [], []
    for _, dil in DSWA_GROUPS:
        assert tm % (BF16_ROWS * dil) == 0
        out_specs += [pl.BlockSpec((None, dil, tm // dil, gw), lambda b, i: (b, 0, i, 0))] * 3
        out_shape += [jax.ShapeDtypeStruct((batch, dil, seq // dil, gw), BF16)] * 3
    return pl.pallas_call(
        functools.partial(_dswa_proj_kernel, tm=tm),
        grid=(batch, nblk),
        in_specs=[pl.BlockSpec((tm, d), lambda b, i: (b * nblk + i, 0))]
                 + [_layer_spec(p, layer) for p in (g, w, gq, gk)],
        out_specs=out_specs,
        out_shape=out_shape,
        scratch_shapes=[pltpu.VMEM((DSWA_HEADS, tm, DSWA_HEAD_DIM), F32)] * 3,
        compiler_params=_cparams(("parallel", "parallel"), VMEM_RESIDENT),
        name="dswa_proj",
    )(x, g, w, gq, gk)


def _dswa_kernel(slope_ref, q_ref, kc_ref, kp_ref, vc_ref, vp_ref, o_ref, l_ref, *, dilation, group, nb):
    w = DSWA_W
    ib = pl.program_id(2)
    i = lax.broadcasted_iota(jnp.int32, (w, 2 * w), 0)
    j = lax.broadcasted_iota(jnp.int32, (w, 2 * w), 1)
    steps = w + i - j
    dist = (steps * dilation).astype(F32)
    in_window = (steps >= 0) & (steps <= w)
    for hh in range(DSWA_HEADS_PER_GROUP):
        cols = slice(hh * DSWA_HEAD_DIM, (hh + 1) * DSWA_HEAD_DIM)
        bias_all = jnp.where(in_window, -slope_ref[group * DSWA_HEADS_PER_GROUP + hh] * dist, NEG_INF)
        bias_first = jnp.where(j >= w, bias_all, NEG_INF)
        for jb in range(nb):
            rows = slice(jb * w, (jb + 1) * w)
            if jb == 0:
                kprev, vprev = kp_ref[:, cols], vp_ref[:, cols]
                bias_j = jnp.where(ib > 0, bias_all, bias_first)
            else:
                prows = slice((jb - 1) * w, jb * w)
                kprev, vprev = kc_ref[prows, cols], vc_ref[prows, cols]
                bias_j = bias_all
            keys = jnp.concatenate([kprev, kc_ref[rows, cols]], axis=0)
            vals = jnp.concatenate([vprev, vc_ref[rows, cols]], axis=0)
            sc = lax.dot_general(q_ref[rows, cols], keys, (((1,), (1,)), ((), ())), preferred_element_type=F32)
            sc = jnp.where(bias_j > 0.5 * NEG_INF, sc + bias_j, NEG_INF)
            m = jnp.max(sc, axis=-1, keepdims=True)
            p = jnp.exp(sc - m)
            lsum = jnp.sum(p, axis=-1, keepdims=True)
            out = jnp.dot(p.astype(BF16), vals, preferred_element_type=F32) / lsum
            o_ref[rows, cols] = out.astype(o_ref.dtype)
            l_ref[rows, hh * LSE_LANES:(hh + 1) * LSE_LANES] = jnp.broadcast_to(m + jnp.log(lsum), (w, LSE_LANES))


def _dswa_group(slopes, qg, kg, vg, *, group):
    window, dilation = DSWA_GROUPS[group]
    assert window // dilation == DSWA_W
    batch, _, sd, gw = qg.shape
    tq = min(DSWA_MAX_Q, sd)
    nb = tq // DSWA_W
    cur = pl.BlockSpec((None, None, tq, gw), lambda b, r, i: (b, r, i, 0))
    prev = pl.BlockSpec((None, None, DSWA_W, gw), lambda b, r, i: (b, r, jnp.maximum(i * nb - 1, 0), 0))
    lse = pl.BlockSpec((None, None, tq, LANE), lambda b, r, i: (b, r, i, 0))
    return pl.pallas_call(
        functools.partial(_dswa_kernel, dilation=dilation, group=group, nb=nb),
        grid=(batch, dilation, sd // tq),
        in_specs=[pl.BlockSpec(memory_space=pltpu.SMEM), cur, cur, prev, cur, prev],
        out_specs=[cur, lse],
        out_shape=[jax.ShapeDtypeStruct(qg.shape, BF16), jax.ShapeDtypeStruct(qg.shape[:3] + (LANE,), F32)],
        compiler_params=_cparams(("parallel", "parallel", "arbitrary"), VMEM_SMALL),
        name="dswa_g%d" % group,
    )(slopes, qg, kg, kg, vg, vg)


def _dswa_combine_kernel(*refs, tm):
    ng = len(DSWA_GROUPS)
    in_refs, y_ref, scr = refs[:2 * ng], refs[2 * ng], refs[2 * ng + 1]
    nslot = DSWA_HEADS_PER_GROUP + 1

    def token_order(slot, dil, piece):
        if dil == 1:
            return piece(0)
        for r in range(dil):
            scr[slot, pl.ds(r, tm // dil, stride=dil), :] = piece(r)
        return scr[slot]

    outs, lses = [], []
    for gi, (_, dil) in enumerate(DSWA_GROUPS):
        o_src, l_src = in_refs[2 * gi], in_refs[2 * gi + 1]
        heads = []
        for hh in range(DSWA_HEADS_PER_GROUP):
            cols = slice(hh * DSWA_HEAD_DIM, (hh + 1) * DSWA_HEAD_DIM)
            heads.append(token_order(gi * nslot + hh, dil, lambda r, cols=cols: o_src[r, :, cols].astype(F32)))
        outs.append(jnp.concatenate(heads, axis=1))
        lse = token_order(gi * nslot + DSWA_HEADS_PER_GROUP, dil, lambda r: l_src[r])
        bands = [lse[:, hh * LSE_LANES:(hh + 1) * LSE_LANES] for hh in range(DSWA_HEADS_PER_GROUP)]
        lses.append(jnp.concatenate([b for b in bands for _ in range(DSWA_HEAD_DIM // LSE_LANES)], axis=1))
    mx = functools.reduce(jnp.maximum, lses)
    es = [jnp.exp(l - mx) for l in lses]
    den = functools.reduce(lambda a, b: a + b, es)
    y = functools.reduce(lambda a, b: a + b, [(e / den) * o for e, o in zip(es, outs)])
    y_ref[...] = y.astype(y_ref.dtype)


def _dswa_combine(outs_lses, *, batch, seq, tm):
    gw = DSWA_HEADS_PER_GROUP * DSWA_HEAD_DIM
    nblk = seq // tm
    in_specs = []
    for _, dil in DSWA_GROUPS:
        in_specs += [pl.BlockSpec((None, dil, tm // dil, gw), lambda b, i: (b, 0, i, 0)),
                     pl.BlockSpec((None, dil, tm // dil, LANE), lambda b, i: (b, 0, i, 0))]
    return pl.pallas_call(
        functools.partial(_dswa_combine_kernel, tm=tm),
        grid=(batch, nblk),
        in_specs=in_specs,
        out_specs=pl.BlockSpec((tm, gw), lambda b, i: (b * nblk + i, 0)),
        out_shape=jax.ShapeDtypeStruct((batch * seq, gw), BF16),
        scratch_shapes=[pltpu.VMEM((len(DSWA_GROUPS) * (DSWA_HEADS_PER_GROUP + 1), tm, LANE), F32)],
        compiler_params=_cparams(("parallel", "parallel"), VMEM_SMALL),
        name="dswa_combine",
    )(*outs_lses)


def _merge_kernel(x_ref, g_ref, ya_ref, yb_ref, yc_ref, wga_ref, wgb_ref, wgc_ref, wa_ref, wb_ref, wc_ref, wo_ref,
                  out_ref, h_scr):
    dot = lambda a, b: jnp.dot(a, b, preferred_element_type=F32)

    def chunk(h):
        tn = wo_ref.shape[0]
        halves = []
        for c0 in range(0, tn, tn // 2):
            cols = slice(c0, c0 + tn // 2)
            part = (jax.nn.sigmoid(dot(h, wga_ref[:, cols])) * dot(ya_ref[...], wa_ref[:, cols])
                    + jax.nn.sigmoid(dot(h, wgb_ref[:, cols])) * dot(yb_ref[...], wb_ref[:, cols])
                    + jax.nn.sigmoid(dot(h, wgc_ref[:, cols])) * dot(yc_ref[...], wc_ref[:, cols]))
            halves.append(part.astype(BF16))
        return dot(jnp.concatenate(halves, axis=1), wo_ref[...])

    @pl.when(pl.program_id(1) == 0)
    def _():
        xf = x_ref[...]
        h = (xf * _inv_rms(xf, xf.shape[-1]) * g_ref[...]).astype(BF16)
        h_scr[...] = h
        out_ref[...] = xf + chunk(h)

    @pl.when(pl.program_id(1) > 0)
    def _():
        out_ref[...] += chunk(h_scr[...])


def _merge(x, g, ya, yb, yc, wga, wgb, wgc, wa, wb, wc, wo, layer, *, tm, tn):
    t, d = x.shape
    row = lambda i, j: (i, 0)
    colw = lambda i, j: (layer, 0, j)
    return pl.pallas_call(
        _merge_kernel,
        grid=(t // tm, d // tn),
        in_specs=[pl.BlockSpec((tm, d), row), _layer_spec(g, layer),
                  pl.BlockSpec((tm, ya.shape[1]), row), pl.BlockSpec((tm, yb.shape[1]), row),
                  pl.BlockSpec((tm, yc.shape[1]), row)]
                 + [pl.BlockSpec((None, w.shape[1], tn), colw) for w in (wga, wgb, wgc, wa, wb, wc)]
                 + [pl.BlockSpec((None, tn, d), lambda i, j: (layer, j, 0))],
        out_specs=pl.BlockSpec((tm, d), row),
        out_shape=jax.ShapeDtypeStruct((t, d), F32),
        scratch_shapes=[pltpu.VMEM((tm, d), BF16)],
        compiler_params=_cparams(("parallel", "arbitrary"), VMEM_RESIDENT),
        name="merge",
    )(x, g, ya, yb, yc, wga, wgb, wgc, wa, wb, wc, wo)


def _pad_last(w, n):
    return jnp.pad(w, [(0, 0)] * (w.ndim - 1) + [(0, n - w.shape[-1])])


def _rot_half_cols(w):
    half = w.shape[-1] // 2
    return jnp.concatenate([-w[..., half:], w[..., :half]], axis=-1)


def _swap_half(g):
    half = g.shape[-1] // 2
    return jnp.concatenate([g[..., half:], g[..., :half]], axis=-1)


def _gain3(g):
    rope = g[:, MLA_NOPE:]
    return jnp.stack([g[:, :MLA_NOPE], _pad_last(rope, LANE), _pad_last(_swap_half(rope), LANE)], axis=1)


def _row(g):
    return g[:, None, :]


def kernel(x, positions, ffn1_norm, ffn1_w_gate, ffn1_w_up, ffn1_w_down, mix_norm, w_in, mla_cq_norm, mla_ckv_norm, mla_w_uq, mla_w_ukv, mla_q_norm, mla_k_norm, gla_w_gate2, gla_b_gate2, gla_o_norm, dswa_q_norm, dswa_k_norm, w_branch_a, w_branch_b, w_branch_c, w_out, ffn2_norm, ffn2_w_gate, ffn2_w_up, ffn2_w_down):
    batch, seq, d = x.shape
    depth = w_in.shape[0]
    d_ff = ffn1_w_gate.shape[-1]
    t = batch * seq
    tm, tm_ffn, tf = TOKEN_BLOCK, FFN_TOKEN_BLOCK, FFN_CHUNK
    ff_pad = -(-d_ff // tf) * tf
    assert seq % max(wd for wd, _ in DSWA_GROUPS) == 0 and seq % tm == 0 and t % tm_ffn == 0

    xf = x.reshape(t, d)
    half = MLA_ROPE // 2
    inv_freq = ROPE_THETA ** (-jnp.arange(half, dtype=F32) / half)
    inv_pad = jnp.pad(jnp.concatenate([inv_freq, inv_freq]), (0, LANE - MLA_ROPE)).reshape(1, LANE)
    cos, sin = _rope_tables(positions.reshape(t, 1), inv_pad, tm=tm)
    slopes = 2.0 ** (-ALIBI_MAX_EXP * jnp.arange(1, DSWA_HEADS + 1, dtype=F32) / DSWA_HEADS)

    o_cq, o_ckv, o_kr = 0, MLA_Q_RANK, MLA_Q_RANK + MLA_KV_RANK
    o_gla = o_kr + MLA_ROPE
    n_qkv_b = GLA_HEADS * (2 * GLA_DK + GLA_DV)
    o_glr = o_gla + n_qkv_b
    o_rb = o_glr + GLA_GATE_RANK
    o_c = o_rb + GLA_HEADS * GLA_DV
    o_ga = o_c + 3 * DSWA_HEADS * DSWA_HEAD_DIM
    o_gb, o_gc = o_ga + d, o_ga + 2 * d

    bf = lambda w: w.astype(BF16)

    def ffn_weights(wg, wu, wdn):
        zc = jnp.zeros((depth, d, ff_pad - d_ff), BF16)
        zr = jnp.zeros((depth, ff_pad - d_ff, d), BF16)
        return (jnp.concatenate([bf(wg), zc], axis=2), jnp.concatenate([bf(wu), zc], axis=2),
                jnp.concatenate([bf(wdn), zr], axis=1))

    ffn1 = (_row(ffn1_norm),) + ffn_weights(ffn1_w_gate, ffn1_w_up, ffn1_w_down)
    ffn2 = (_row(ffn2_norm),) + ffn_weights(ffn2_w_gate, ffn2_w_up, ffn2_w_down)
    gmix = _row(mix_norm)
    w_in = bf(w_in)
    w_kr = w_in[:, :, o_kr:o_kr + MLA_ROPE]
    w1 = bf(jnp.concatenate([w_in[:, :, o_cq:o_kr], _pad_last(w_kr, LANE), _pad_last(_rot_half_cols(w_kr), LANE)],
                            axis=-1))
    wq4 = mla_w_uq.reshape(depth, MLA_Q_RANK, MLA_HEADS, MLA_QK)
    wq_rope = wq4[..., MLA_NOPE:]
    wuq = jnp.concatenate([wq4[..., :MLA_NOPE], _pad_last(wq_rope, LANE), _pad_last(_rot_half_cols(wq_rope), LANE)],
                          axis=-1)
    wuq = bf(wuq.reshape(depth, MLA_Q_RANK, MLA_HEADS * 3 * LANE))
    wkv4 = mla_w_ukv.reshape(depth, MLA_KV_RANK, MLA_HEADS, MLA_NOPE + MLA_V)
    wukv = bf(jnp.concatenate([wkv4[..., :MLA_NOPE].reshape(depth, MLA_KV_RANK, -1),
                               wkv4[..., MLA_NOPE:].reshape(depth, MLA_KV_RANK, -1)], axis=-1))
    mla_params = (w1, _row(mla_cq_norm), _row(mla_ckv_norm), wuq, wukv, _gain3(mla_q_norm), _gain3(mla_k_norm))
    w_gla = bf(jnp.concatenate([w_in[:, :, o_gla:o_glr], _pad_last(w_in[:, :, o_glr:o_rb], LANE),
                                w_in[:, :, o_rb:o_c]], axis=-1))
    w2 = bf(jnp.pad(gla_w_gate2, ((0, 0), (0, LANE - GLA_GATE_RANK), (0, 0))))
    gla_params = (w_gla, w2, _row(gla_b_gate2))
    go = _row(gla_o_norm)
    dswa_params = (bf(w_in[:, :, o_c:o_ga]), _row(dswa_q_norm), _row(dswa_k_norm))
    merge_params = (bf(w_in[:, :, o_ga:o_gb]), bf(w_in[:, :, o_gb:o_gc]), bf(w_in[:, :, o_gc:o_gc + d]),
                    bf(w_branch_a), bf(w_branch_b), bf(w_branch_c), bf(w_out))

    for l in range(depth):
        xf = _ffn(xf, *ffn1, l, tm=tm_ffn, tf=tf)

        qa, ka, va = _mla_prep(xf, gmix, cos, sin, *mla_params, l, tm=tm)
        y_a = _flash(qa, ka, va, batch=batch, seq=seq, tq=FLASH_TILE)

        qb, kb, vb, la, sr = _gla_proj(xf, gmix, *gla_params, l, tm=tm)
        y_b = _gla(qb, kb, vb, la, sr, go, l, batch=batch, seq=seq, tb=GLA_TOKENS)

        qkv_c = _dswa_proj(xf, gmix, *dswa_params, l, batch=batch, seq=seq, tm=tm)
        outs_lses = []
        for gi in range(len(DSWA_GROUPS)):
            outs_lses += _dswa_group(slopes, *qkv_c[3 * gi:3 * gi + 3], group=gi)
        y_c = _dswa_combine(outs_lses, batch=batch, seq=seq, tm=tm)

        xf = _merge(xf, gmix, y_a, y_b, y_c, *merge_params, l, tm=tm, tn=MERGE_CHUNK)
        xf = _ffn(xf, *ffn2, l, tm=tm_ffn, tf=tf)
    return xf.reshape(batch, seq, d)
```
